```python
import math
import jax, jax.numpy as jnp
from jax import lax
import numpy as np

D_MODEL = 1024
BATCH = 16
SEQ = 4096
DEPTH = 4

GRID_W = 64
CTX_LEN = 256
Q_BLOCK = 128
ROPE_BASE = 10000.0
LN_EPS = 1e-6

MLA_HEADS = 6
MLA_NOPE = 64
MLA_ROPE = 32
MLA_V = 64
MLA_Q_RANK = 256
MLA_KV_RANK = 128

DIFF_HEADS = 4
DIFF_DIM = 48
DIFF_V = 2 * DIFF_DIM

HY_CH = 256
HY_ORDER = 2
HY_BANDS = 16
HY_EMB = 1 + 2 * HY_BANDS
HY_FILTER_HIDDEN = 64
HY_TARGET = 1e-2
HY_FAST_DECAY_PCT = 0.3
HY_SLOW_DECAY_PCT = 1.5
HY_MIN_RATE = -math.log(HY_TARGET) / HY_SLOW_DECAY_PCT
HY_MAX_RATE = -math.log(HY_TARGET) / HY_FAST_DECAY_PCT

D_FF = 4 * D_MODEL
N_MOD = 6

D_MIX = MLA_HEADS * MLA_V + DIFF_HEADS * DIFF_V + HY_CH
IN_MLA = MLA_Q_RANK + MLA_KV_RANK + MLA_ROPE
IN_DIFF = 2 * DIFF_HEADS * 2 * DIFF_DIM + DIFF_HEADS * DIFF_V
IN_HY = (HY_ORDER + 1) * HY_CH
D_IN = IN_MLA + IN_DIFF + IN_HY

kernel_name = 'hybrid_mla_diffattn_hyena_dit'

F32 = jnp.float32


def _layer_norm(x):
    xf = x.astype(F32)
    mu = jnp.mean(xf, -1, keepdims=True)
    var = jnp.mean(jnp.square(xf - mu), -1, keepdims=True)
    return ((xf - mu) * lax.rsqrt(var + LN_EPS)).astype(x.dtype)


def _rms_norm(x, g):
    xf = x.astype(F32)
    ms = jnp.mean(jnp.square(xf), -1, keepdims=True)
    return (xf * lax.rsqrt(ms + LN_EPS)).astype(x.dtype) * g


def _modulate(x, shift, scale):
    return _layer_norm(x) * (1 + scale) + shift


def _post_norm(x, g, b):
    return _layer_norm(x) * g + b


def _rope_1d(x, pos):
    h = x.shape[-1]
    inv = ROPE_BASE ** (-(jnp.arange(h // 2, dtype=F32) * (2.0 / h)))
    ang = pos.astype(F32)[:, None] * inv[None, :]
    cos = jnp.cos(ang).astype(x.dtype)
    sin = jnp.sin(ang).astype(x.dtype)
    x1, x2 = x[..., : h // 2], x[..., h // 2:]
    return jnp.concatenate([x1 * cos - x2 * sin, x2 * cos + x1 * sin], -1)


def _rope_2d(x, row, col):
    d = x.shape[-1]
    return jnp.concatenate([_rope_1d(x[..., : d // 2], row), _rope_1d(x[..., d // 2:], col)], -1)


def _merge_heads(o):
    b, h, l, d = o.shape
    return o.transpose(0, 2, 1, 3).reshape(b, l, h * d)


def _blocked_attention(q, k, v, scale):
    b, h, l, dk = q.shape
    nb = l // Q_BLOCK
    qb = q.reshape(b, h, nb, Q_BLOCK, dk).transpose(2, 0, 1, 3, 4)

    def one(qi):
        s = jnp.einsum('bhqd,bhkd->bhqk', qi, k).astype(F32) * scale
        p = jax.nn.softmax(s, axis=-1).astype(v.dtype)
        return jnp.einsum('bhqk,bhkd->bhqd', p, v)

    o = lax.map(one, qb)
    return o.transpose(1, 2, 0, 3, 4).reshape(b, h, l, v.shape[-1])


def _blocked_diff_attention(q, k, v, lam, scale):
    b, h, m, l, d = q.shape
    nb = l // Q_BLOCK
    qb = q.reshape(b, h, m, nb, Q_BLOCK, d).transpose(3, 0, 1, 2, 4, 5)

    def one(qi):
        s = jnp.einsum('bhmqd,bhmkd->bhmqk', qi, k).astype(F32) * scale
        p = jax.nn.softmax(s, axis=-1)
        a = (p[:, :, 0] - lam * p[:, :, 1]).astype(v.dtype)
        return jnp.einsum('bhqk,bhkd->bhqd', a, v)

    o = lax.map(one, qb)
    return o.transpose(1, 2, 0, 3, 4).reshape(b, h, l, v.shape[-1])


def _mla_project(p, q_norm, w_q_up, kv_norm, w_kv_up):
    b, l, _ = p.shape
    cq = p[..., :MLA_Q_RANK]
    ckv = p[..., MLA_Q_RANK:MLA_Q_RANK + MLA_KV_RANK]
    k_rope = p[..., MLA_Q_RANK + MLA_KV_RANK:IN_MLA]
    q = (_rms_norm(cq, q_norm) @ w_q_up).reshape(b, l, MLA_HEADS, MLA_NOPE + MLA_ROPE).transpose(0, 2, 1, 3)
    kv = (_rms_norm(ckv, kv_norm) @ w_kv_up).reshape(b, l, MLA_HEADS, MLA_NOPE + MLA_V).transpose(0, 2, 1, 3)
    return q, kv[..., :MLA_NOPE], kv[..., MLA_NOPE:], k_rope


def _mla_keys(k_nope, k_rope):
    b, h, l, _ = k_nope.shape
    return jnp.concatenate([k_nope, jnp.broadcast_to(k_rope[:, None], (b, h, l, MLA_ROPE))], -1)


def _mla(px, pc, row, col, need_ctx, q_norm, w_q_up, kv_norm, w_kv_up):
    qx, knx, vx, krx = _mla_project(px, q_norm, w_q_up, kv_norm, w_kv_up)
    qc, knc, vc, krc = _mla_project(pc, q_norm, w_q_up, kv_norm, w_kv_up)
    qx = jnp.concatenate([qx[..., :MLA_NOPE], _rope_2d(qx[..., MLA_NOPE:], row, col)], -1)
    kx = _mla_keys(knx, _rope_2d(krx, row, col))
    kc = _mla_keys(knc, krc)
    scale = (MLA_NOPE + MLA_ROPE) ** -0.5
    ox = _merge_heads(_blocked_attention(qx, jnp.concatenate([kc, kx], 2), jnp.concatenate([vc, vx], 2), scale))
    oc = _merge_heads(_blocked_attention(qc, kc, vc, scale)) if need_ctx else None
    return ox, oc


def _diff_project(p):
    b, l, _ = p.shape
    n = DIFF_HEADS * 2 * DIFF_DIM
    q = p[..., :n].reshape(b, l, DIFF_HEADS, 2, DIFF_DIM).transpose(0, 2, 3, 1, 4)
    k = p[..., n:2 * n].reshape(b, l, DIFF_HEADS, 2, DIFF_DIM).transpose(0, 2, 3, 1, 4)
    v = p[..., 2 * n:].reshape(b, l, DIFF_HEADS, DIFF_V).transpose(0, 2, 1, 3)
    return q, k, v


def _diff(px, pc, row, col, lam_init, need_ctx, lam_p, subln):
    qx, kx, vx = _diff_project(px)
    qc, kc, vc = _diff_project(pc)
    qx = _rope_2d(qx, row, col)
    kx = _rope_2d(kx, row, col)
    lp = lam_p.astype(F32)
    lam = jnp.exp(jnp.sum(lp[0] * lp[1])) - jnp.exp(jnp.sum(lp[2] * lp[3])) + lam_init
    scale = DIFF_DIM ** -0.5

    def finish(o):
        return _merge_heads(_rms_norm(o, subln) * (1.0 - lam_init))

    ox = finish(_blocked_diff_attention(qx, jnp.concatenate([kc, kx], 3), jnp.concatenate([vc, vx], 2), lam, scale))
    oc = finish(_blocked_diff_attention(qc, kc, vc, lam, scale)) if need_ctx else None
    return ox, oc


def _short_conv(u, w, b):
    l = u.shape[1]
    up = jnp.pad(u, ((0, 0), (1, 1), (0, 0)))
    return up[:, :l] * w[0] + up[:, 1:l + 1] * w[1] + up[:, 2:] * w[2] + b


def _hyena_filters(l, fw1, fb1, fw2, fb2, fw3, fb3):
    t = jnp.arange(l, dtype=F32)
    bands = jnp.arange(1, HY_BANDS + 1, dtype=F32)
    ang = (2.0 * math.pi / l) * t[:, None] * bands[None, :]
    emb = jnp.concatenate([(t / l)[:, None], jnp.cos(ang), jnp.sin(ang)], -1)
    h = jnp.sin(emb @ fw1.astype(F32) + fb1.astype(F32))
    h = jnp.sin(h @ fw2.astype(F32) + fb2.astype(F32))
    h = (h @ fw3.astype(F32) + fb3.astype(F32)).reshape(l, HY_ORDER, 2, HY_CH)
    rates = jnp.linspace(HY_MIN_RATE, HY_MAX_RATE, HY_CH, dtype=F32)
    window = jnp.exp(-(t / l)[:, None] * rates[None, :])
    h = h * window[:, None, None, :]
    fwd, bwd = h[:, :, 0], h[:, :, 1]
    kern = jnp.concatenate([fwd, jnp.zeros((1, HY_ORDER, HY_CH), F32), bwd[1:][::-1]], 0)
    kern = kern / jnp.sum(jnp.abs(kern), axis=0, keepdims=True)
    return jnp.fft.rfft(kern, axis=0)


def _fftconv(z, kf, bias):
    l = z.shape[1]
    zf = z.astype(F32)
    y = jnp.fft.irfft(jnp.fft.rfft(zf, n=2 * l, axis=1) * kf[None], n=2 * l, axis=1)[:, :l]
    return (y + zf * bias.astype(F32)).astype(z.dtype)


def _hyena(u, conv_w, conv_b, fw1, fb1, fw2, fb2, fw3, fb3, hbias):
    l = u.shape[1]
    u = _short_conv(u, conv_w, conv_b)
    v, x1, x2 = jnp.split(u, 3, axis=-1)
    kf = _hyena_filters(l, fw1, fb1, fw2, fb2, fw3, fb3)
    z = x1 * _fftconv(v, kf[:, 0], hbias[0])
    return x2 * _fftconv(z, kf[:, 1], hbias[1])


def _mixer(px, pc, row, col, lam_init, need_ctx, q_norm, w_q_up, kv_norm, w_kv_up, lam_p, subln,
           conv_w, conv_b, fw1, fb1, fw2, fb2, fw3, fb3, hbias):
    s1 = IN_MLA
    s2 = IN_MLA + IN_DIFF
    mla_x, mla_c = _mla(px[..., :s1], pc[..., :s1], row, col, need_ctx, q_norm, w_q_up, kv_norm, w_kv_up)
    dif_x, dif_c = _diff(px[..., s1:s2], pc[..., s1:s2], row, col, lam_init, need_ctx, lam_p, subln)
    hy_x = _hyena(px[..., s2:], conv_w, conv_b, fw1, fb1, fw2, fb2, fw3, fb3, hbias)
    ox = jnp.concatenate([mla_x, dif_x, hy_x], -1)
    oc = None
    if need_ctx:
        hy_c = _hyena(pc[..., s2:], conv_w, conv_b, fw1, fb1, fw2, fb2, fw3, fb3, hbias)
        oc = jnp.concatenate([mla_c, dif_c, hy_c], -1)
    return ox, oc


def _ffn(h, w1, b1, w2, b2):
    return jnp.square(jax.nn.relu(h @ w1 + b1)) @ w2 + b2


def setup_inputs(seed: int = 0) -> dict:
    key = jax.random.key(seed)
    ks = iter(jax.random.split(key, 40))
    L = DEPTH
    beta = (8.0 * DEPTH) ** -0.25

    def nrm(shape, scale):
        return jax.random.normal(next(ks), shape, F32) * scale

    def gain(shape):
        return 1.0 + nrm(shape, 0.02)

    return {
        'x': nrm((BATCH, SEQ, D_MODEL), 1.0),
        'c': nrm((BATCH, D_MODEL), 1.0),
        'ctx': nrm((BATCH, CTX_LEN, D_MODEL), 1.0),
        'c_ctx': nrm((D_MODEL,), 1.0),
        'w_mod': nrm((L, D_MODEL, N_MOD * D_MODEL), D_MODEL ** -0.5),
        'b_mod': nrm((L, N_MOD * D_MODEL), 0.02),
        'w_in': nrm((L, D_MODEL, D_IN), D_MODEL ** -0.5),
        'mla_q_norm': gain((L, MLA_Q_RANK)),
        'w_q_up': nrm((L, MLA_Q_RANK, MLA_HEADS * (MLA_NOPE + MLA_ROPE)), MLA_Q_RANK ** -0.5),
        'mla_kv_norm': gain((L, MLA_KV_RANK)),
        'w_kv_up': nrm((L, MLA_KV_RANK, MLA_HEADS * (MLA_NOPE + MLA_V)), MLA_KV_RANK ** -0.5),
        'diff_lambda': nrm((L, 4, DIFF_DIM), 0.1),
        'diff_subln': gain((L, DIFF_V)),
        'hy_conv_w': nrm((L, 3, IN_HY), 3 ** -0.5),
        'hy_conv_b': nrm((L, IN_HY), 0.02),
        'hy_fw1': nrm((L, HY_EMB, HY_FILTER_HIDDEN), HY_EMB ** -0.5),
        'hy_fb1': nrm((L, HY_FILTER_HIDDEN), 0.02),
        'hy_fw2': nrm((L, HY_FILTER_HIDDEN, HY_FILTER_HIDDEN), HY_FILTER_HIDDEN ** -0.5),
        'hy_fb2': nrm((L, HY_FILTER_HIDDEN), 0.02),
        'hy_fw3': nrm((L, HY_FILTER_HIDDEN, HY_ORDER * 2 * HY_CH), HY_FILTER_HIDDEN ** -0.5),
        'hy_fb3': nrm((L, HY_ORDER * 2 * HY_CH), 0.02),
        'hy_bias': nrm((L, HY_ORDER, HY_CH), 0.1),
        'w_out': nrm((L, D_MIX, D_MODEL), beta * D_MIX ** -0.5),
        'b_out': nrm((L, D_MODEL), 0.02),
        'ln1_g': gain((L, D_MODEL)),
        'ln1_b': nrm((L, D_MODEL), 0.02),
        'w_ff1': nrm((L, D_MODEL, D_FF), D_MODEL ** -0.5),
        'b_ff1': nrm((L, D_FF), 0.02),
        'w_ff2': nrm((L, D_FF, D_MODEL), beta * D_FF ** -0.5),
        'b_ff2': nrm((L, D_MODEL), 0.02),
        'ln2_g': gain((L, D_MODEL)),
        'ln2_b': nrm((L, D_MODEL), 0.02),
    }


def reference(x, c, ctx, c_ctx, w_mod, b_mod, w_in, mla_q_norm, w_q_up, mla_kv_norm, w_kv_up,
              diff_lambda, diff_subln, hy_conv_w, hy_conv_b, hy_fw1, hy_fb1, hy_fw2, hy_fb2, hy_fw3, hy_fb3,
              hy_bias, w_out, b_out, ln1_g, ln1_b, w_ff1, b_ff1, w_ff2, b_ff2, ln2_g, ln2_b):
    seq = x.shape[1]
    rows = seq // GRID_W
    row = jnp.repeat(jnp.arange(rows), GRID_W)
    col = jnp.tile(jnp.arange(GRID_W), rows)
    alpha = (2.0 * DEPTH) ** 0.25
    silu_c = jax.nn.silu(c)
    silu_cc = jax.nn.silu(c_ctx)
    for layer in range(DEPTH):
        need_ctx = layer < DEPTH - 1
        lam_init = 0.8 - 0.6 * math.exp(-0.3 * layer)
        mod = silu_c @ w_mod[layer] + b_mod[layer]
        mod_c = silu_cc @ w_mod[layer] + b_mod[layer]
        sh1, sc1, g1, sh2, sc2, g2 = jnp.split(mod[:, None, :], N_MOD, axis=-1)
        csh1, csc1, cg1, csh2, csc2, cg2 = jnp.split(mod_c, N_MOD, axis=-1)

        px = _modulate(x, sh1, sc1) @ w_in[layer]
        pc = _modulate(ctx, csh1, csc1) @ w_in[layer]
        ox, oc = _mixer(px, pc, row, col, lam_init, need_ctx,
                        mla_q_norm[layer], w_q_up[layer], mla_kv_norm[layer], w_kv_up[layer],
                        diff_lambda[layer], diff_subln[layer],
                        hy_conv_w[layer], hy_conv_b[layer], hy_fw1[layer], hy_fb1[layer],
                        hy_fw2[layer], hy_fb2[layer], hy_fw3[layer], hy_fb3[layer], hy_bias[layer])
        x = _post_norm(alpha * x + g1 * (ox @ w_out[layer] + b_out[layer]), ln1_g[layer], ln1_b[layer])
        x = _post_norm(alpha * x + g2 * _ffn(_modulate(x, sh2, sc2), w_ff1[layer], b_ff1[layer], w_ff2[layer], b_ff2[layer]),
                       ln2_g[layer], ln2_b[layer])
        if need_ctx:
            ctx = _post_norm(alpha * ctx + cg1 * (oc @ w_out[layer] + b_out[layer]), ln1_g[layer], ln1_b[layer])
            ctx = _post_norm(alpha * ctx + cg2 * _ffn(_modulate(ctx, csh2, csc2), w_ff1[layer], b_ff1[layer], w_ff2[layer], b_ff2[layer]),
                             ln2_g[layer], ln2_b[layer])
    return x
```

```python
import functools
import math

import jax
import jax.numpy as jnp
from jax import lax
from jax.experimental import pallas as pl
from jax.experimental.pallas import tpu as pltpu

F32 = jnp.float32
BF16 = jnp.bfloat16
HI = lax.Precision.HIGHEST

GRID_W = 64
ROPE_BASE = 10000.0
LN_EPS = 1e-6
MLA_HEADS, MLA_NOPE, MLA_ROPE, MLA_V = 6, 64, 32, 64
MLA_Q_RANK, MLA_KV_RANK = 256, 128
DIFF_HEADS, DIFF_DIM = 4, 48
DIFF_V = 2 * DIFF_DIM
HY_CH, HY_ORDER, HY_BANDS = 256, 2, 16
HY_EMB = 1 + 2 * HY_BANDS
HY_TARGET, HY_FAST_DECAY_PCT, HY_SLOW_DECAY_PCT = 1e-2, 0.3, 1.5
HY_MIN_RATE = -math.log(HY_TARGET) / HY_SLOW_DECAY_PCT
HY_MAX_RATE = -math.log(HY_TARGET) / HY_FAST_DECAY_PCT
N_MOD = 6

LANE = 128
VMEM_LIMIT = 56 * 1024 * 1024

IN_MLA = MLA_Q_RANK + MLA_KV_RANK + MLA_ROPE
N_DQ = DIFF_HEADS * 2 * DIFF_DIM
IN_DIFF = 2 * N_DQ + DIFF_HEADS * DIFF_V
W_QM = MLA_HEADS * LANE
W_VM = MLA_HEADS * MLA_V
W_D = DIFF_HEADS * LANE
W_HY = (HY_ORDER + 1) * HY_CH
C_CQ, C_CKV, C_KR = 0, MLA_Q_RANK, MLA_Q_RANK + MLA_KV_RANK
C_DQ = C_KR + LANE
C_DK = C_DQ + W_D
C_DV = C_DK + W_D
C_HY = C_DV + W_D
W_IN = C_HY + W_HY


def _cparams(sem):
    return pltpu.CompilerParams(dimension_semantics=sem, vmem_limit_bytes=VMEM_LIMIT)


def _ln(x):
    mu = jnp.mean(x, -1, keepdims=True)
    xc = x - mu
    var = jnp.mean(xc * xc, -1, keepdims=True)
    return xc * lax.rsqrt(var + LN_EPS)


def _bdot(a, b):
    return jnp.dot(a, b, preferred_element_type=F32)


def _mod_kernel(c_ref, w_ref, b_ref, o_ref):
    c = c_ref[...]
    s = c / (1.0 + jnp.exp(-c))
    o_ref[...] = jnp.dot(s, w_ref[...], preferred_element_type=F32, precision=HI) + b_ref[...]


def _modulation(cc, w_mod, b_mod):
    depth, d, n = w_mod.shape
    r = cc.shape[0]
    tn = 1024
    return pl.pallas_call(
        _mod_kernel,
        grid=(depth, n // tn),
        in_specs=[pl.BlockSpec((r, d), lambda l, j: (0, 0)),
                  pl.BlockSpec((None, d, tn), lambda l, j: (l, 0, j)),
                  pl.BlockSpec((None, 1, tn), lambda l, j: (l, 0, j))],
        out_specs=pl.BlockSpec((None, r, tn), lambda l, j: (l, 0, j)),
        out_shape=jax.ShapeDtypeStruct((depth, r, n), F32),
        compiler_params=_cparams(("arbitrary", "arbitrary")),
        name="modulation",
    )(cc, w_mod, b_mod.reshape(depth, 1, n))


def _rope(x, c, sa, sb, half):
    return x * c + pltpu.roll(x, LANE - half, 1) * sa + pltpu.roll(x, half, 1) * sb


def _inproj_kernel(x_ref, sh_ref, sc_ref, win_ref, gq_ref, wq_ref, gkv_ref, wkv_ref, tab_ref,
                   qm_ref, km_ref, vm_ref, dq_ref, dk_ref, dv_ref, hy_ref):
    x = x_ref[0]
    h = _ln(x) * (1.0 + sc_ref[0]) + sh_ref[0]
    p = _bdot(h.astype(BF16), win_ref[...])
    tab = tab_ref[...]
    cm, sam, sbm = tab[:, 0:LANE], tab[:, LANE:2 * LANE], tab[:, 2 * LANE:3 * LANE]
    cd, sad, sbd = tab[:, 3 * LANE:4 * LANE], tab[:, 4 * LANE:5 * LANE], tab[:, 5 * LANE:6 * LANE]

    cq = p[:, C_CQ:C_CQ + MLA_Q_RANK]
    qn = cq * lax.rsqrt(jnp.mean(cq * cq, -1, keepdims=True) + LN_EPS) * gq_ref[...]
    q = _bdot(qn.astype(BF16), wq_ref[...])
    ckv = p[:, C_CKV:C_CKV + MLA_KV_RANK]
    kvn = ckv * lax.rsqrt(jnp.mean(ckv * ckv, -1, keepdims=True) + LN_EPS) * gkv_ref[...]
    kv = _bdot(kvn.astype(BF16), wkv_ref[...])
    kr = _rope(p[:, C_KR:C_KR + LANE], cm, sam, sbm, MLA_ROPE // 4)
    scale_m = (MLA_NOPE + MLA_ROPE) ** -0.5
    for hd in range(MLA_HEADS):
        sl = slice(hd * LANE, (hd + 1) * LANE)
        qm_ref[0, :, sl] = (_rope(q[:, sl], cm, sam, sbm, MLA_ROPE // 4) * scale_m).astype(BF16)
        km_ref[0, :, sl] = (kv[:, sl] + kr).astype(BF16)
    vm_ref[0] = kv[:, W_QM:W_QM + W_VM].astype(BF16)
    scale_d = DIFF_DIM ** -0.5
    for hd in range(DIFF_HEADS):
        sl = slice(hd * LANE, (hd + 1) * LANE)
        dq = p[:, C_DQ + hd * LANE:C_DQ + (hd + 1) * LANE]
        dk = p[:, C_DK + hd * LANE:C_DK + (hd + 1) * LANE]
        dq_ref[0, :, sl] = (_rope(dq, cd, sad, sbd, DIFF_DIM // 4) * scale_d).astype(BF16)
        dk_ref[0, :, sl] = _rope(dk, cd, sad, sbd, DIFF_DIM // 4).astype(BF16)
    dv_ref[0] = p[:, C_DV:C_DV + W_D].astype(BF16)
    hy_ref[0] = p[:, C_HY:C_HY + W_HY].astype(BF16)


def _in_proj(x, sh, sc, win, gq, wq, gkv, wkv, tab, tm):
    b, t, d = x.shape
    per_batch = sh.shape[0] == b and b > 1
    mod_map = (lambda i, bb: (bb, 0, 0)) if per_batch else (lambda i, bb: (0, 0, 0))
    const = lambda i, bb: (0, 0)
    widths = (W_QM, W_QM, W_VM, W_D, W_D, W_D, W_HY)
    return pl.pallas_call(
        _inproj_kernel,
        grid=(t // tm, b),
        in_specs=[pl.BlockSpec((1, tm, d), lambda i, bb: (bb, i, 0)),
                  pl.BlockSpec((1, 1, d), mod_map),
                  pl.BlockSpec((1, 1, d), mod_map),
                  pl.BlockSpec(win.shape, const),
                  pl.BlockSpec(gq.shape, const),
                  pl.BlockSpec(wq.shape, const),
                  pl.BlockSpec(gkv.shape, const),
                  pl.BlockSpec(wkv.shape, const),
                  pl.BlockSpec((tm, 6 * LANE), lambda i, bb: (i, 0))],
        out_specs=[pl.BlockSpec((1, tm, w), lambda i, bb: (bb, i, 0)) for w in widths],
        out_shape=[jax.ShapeDtypeStruct((b, t, w), BF16) for w in widths],
        compiler_params=_cparams(("arbitrary", "arbitrary")),
        name="in_proj",
    )(x, sh, sc, win, gq, wq, gkv, wkv, tab)


NEG_BIG = -1e30


def _rows(j, tk):
    if isinstance(j, int):
        return pl.ds(j * tk, tk)
    return pl.ds(pl.multiple_of(j * tk, tk), tk)


def _score_pass(q, k_ref, ksl, s_ref, n, tk, m):
    def body(j, m):
        k = k_ref[0, _rows(j, tk), ksl]
        s = lax.dot_general(q, k, (((1,), (1,)), ((), ())), preferred_element_type=F32)
        s_ref[j] = s
        return jnp.maximum(m, jnp.max(s, axis=-1, keepdims=True))
    if n == tk:
        return body(0, m)
    return lax.fori_loop(0, n // tk, body, m)


def _pv_pass(s_ref, v_ref, vsl, n, tk, m, l, acc):
    def body(j, carry):
        l, acc = carry
        p = jnp.exp(s_ref[j] - m)
        l = l + jnp.sum(p, axis=-1, keepdims=True)
        v = v_ref[0, _rows(j, tk), vsl]
        return l, acc + _bdot(p.astype(BF16), v)
    if n == tk:
        return body(0, (l, acc))
    return lax.fori_loop(0, n // tk, body, (l, acc))


def _softmax_pv(q, parts, ksl, vsl, s_refs):
    tq = q.shape[0]
    m = jnp.full((tq, 1), NEG_BIG, F32)
    for (k_ref, _, n, tk), s_ref in zip(parts, s_refs):
        m = _score_pass(q, k_ref, ksl, s_ref, n, tk, m)
    l = jnp.zeros((tq, 1), F32)
    acc = jnp.zeros((tq, LANE), F32)
    for (_, v_ref, n, tk), s_ref in zip(parts, s_refs):
        l, acc = _pv_pass(s_ref, v_ref, vsl, n, tk, m, l, acc)
    return acc * (1.0 / l)


def _mla_attn_kernel(*refs, part_shapes):
    np_ = len(part_shapes)
    q_ref = refs[0]
    k_refs = refs[1:1 + np_]
    v_refs = refs[1 + np_:1 + 2 * np_]
    o_ref = refs[1 + 2 * np_]
    s_refs = refs[2 + 2 * np_:]
    parts = [(k_refs[i], v_refs[i]) + part_shapes[i] for i in range(np_)]
    outs = []
    for hh in range(2):
        sl = slice(hh * LANE, (hh + 1) * LANE)
        outs.append(_softmax_pv(q_ref[0, :, sl], parts, sl, slice(0, LANE), s_refs))
    lane = lax.broadcasted_iota(jnp.int32, outs[0].shape, 1)
    o_ref[0] = jnp.where(lane < MLA_V, outs[0], outs[1]).astype(BF16)


def _chunk(n):
    return n if n <= 512 else 512


def _mla_attention(q, ks, vs, tq):
    b, t, _ = q.shape
    part_shapes = tuple((k.shape[1], _chunk(k.shape[1])) for k in ks)
    in_specs = [pl.BlockSpec((1, tq, 2 * LANE), lambda bb, pr, i: (bb, i, pr))]
    in_specs += [pl.BlockSpec((1, k.shape[1], 2 * LANE), lambda bb, pr, i: (bb, 0, pr)) for k in ks]
    in_specs += [pl.BlockSpec((1, v.shape[1], LANE), lambda bb, pr, i: (bb, 0, pr)) for v in vs]
    scratch = [pltpu.VMEM((n // tk, tq, tk), F32) for n, tk in part_shapes]
    return pl.pallas_call(
        functools.partial(_mla_attn_kernel, part_shapes=part_shapes),
        grid=(b, MLA_HEADS // 2, t // tq),
        in_specs=in_specs,
        out_specs=pl.BlockSpec((1, tq, LANE), lambda bb, pr, i: (bb, i, pr)),
        out_shape=jax.ShapeDtypeStruct((b, t, W_VM), BF16),
        scratch_shapes=scratch,
        compiler_params=_cparams(("arbitrary", "arbitrary", "arbitrary")),
        name="mla_attention",
    )(q, *ks, *vs)


def _diff_attn_kernel(*refs, part_shapes, lam_init):
    np_ = len(part_shapes)
    q_ref = refs[0]
    k_refs = refs[1:1 + np_]
    v_refs = refs[1 + np_:1 + 2 * np_]
    lam_ref, g_ref, o_ref = refs[1 + 2 * np_:4 + 2 * np_]
    s_refs = refs[4 + 2 * np_:]
    parts = [(k_refs[i], v_refs[i]) + part_shapes[i] for i in range(np_)]
    q = q_ref[0]
    lane = lax.broadcasted_iota(jnp.int32, q.shape, 1)
    zero = jnp.zeros_like(q)
    full = slice(0, LANE)
    o1 = _softmax_pv(jnp.where(lane < LANE // 2, q, zero), parts, full, full, s_refs)
    o2 = _softmax_pv(jnp.where(lane >= LANE // 2, q, zero), parts, full, full, s_refs)
    lp = lam_ref[...]
    lam = (jnp.exp(jnp.sum(lp[0:1] * lp[1:2], axis=-1, keepdims=True))
           - jnp.exp(jnp.sum(lp[2:3] * lp[3:4], axis=-1, keepdims=True)) + lam_init)
    o = o1 - lam * o2
    ms = jnp.sum(o * o, axis=-1, keepdims=True) * (1.0 / DIFF_V)
    o_ref[0] = (o * lax.rsqrt(ms + LN_EPS) * g_ref[...] * (1.0 - lam_init)).astype(BF16)


def _diff_attention(q, ks, vs, lam_p, subln, lam_init, tq):
    b, t, _ = q.shape
    part_shapes = tuple((k.shape[1], _chunk(k.shape[1])) for k in ks)
    hmap = lambda bb, hd, i: (bb, 0, hd)
    in_specs = [pl.BlockSpec((1, tq, LANE), lambda bb, hd, i: (bb, i, hd))]
    in_specs += [pl.BlockSpec((1, k.shape[1], LANE), hmap) for k in ks]
    in_specs += [pl.BlockSpec((1, v.shape[1], LANE), hmap) for v in vs]
    in_specs += [pl.BlockSpec(lam_p.shape, lambda bb, hd, i: (0, 0)),
                 pl.BlockSpec(subln.shape, lambda bb, hd, i: (0, 0))]
    scratch = [pltpu.VMEM((n // tk, tq, tk), F32) for n, tk in part_shapes]
    return pl.pallas_call(
        functools.partial(_diff_attn_kernel, part_shapes=part_shapes, lam_init=lam_init),
        grid=(b, DIFF_HEADS, t // tq),
        in_specs=in_specs,
        out_specs=pl.BlockSpec((1, tq, LANE), lambda bb, hd, i: (bb, i, hd)),
        out_shape=jax.ShapeDtypeStruct((b, t, W_D), BF16),
        scratch_shapes=scratch,
        compiler_params=_cparams(("arbitrary", "arbitrary", "arbitrary")),
        name="diff_attention",
    )(q, *ks, *vs, lam_p, subln)


def _hyfilt_kernel(emb_ref, win_ref, fw1_ref, fb1_ref, fw2_ref, fb2_ref, fwf_ref, fbf_ref, fwb_ref, fbb_ref,
                   sh_ref, sl_ref, dh_ref, dl_ref):
    h = jnp.sin(jnp.dot(emb_ref[...], fw1_ref[...], preferred_element_type=F32, precision=HI) + fb1_ref[...])
    h = jnp.sin(jnp.dot(h, fw2_ref[...], preferred_element_type=F32, precision=HI) + fb2_ref[...])
    w = win_ref[...]
    fwd = (jnp.dot(h, fwf_ref[...], preferred_element_type=F32, precision=HI) + fbf_ref[...]) * w
    bwd = (jnp.dot(h, fwb_ref[...], preferred_element_type=F32, precision=HI) + fbb_ref[...]) * w
    row = lax.broadcasted_iota(jnp.int32, bwd.shape, 0)
    bwd = jnp.where(row == 0, 0.0, bwd)
    norm = jnp.sum(jnp.abs(fwd), axis=0, keepdims=True) + jnp.sum(jnp.abs(bwd), axis=0, keepdims=True)
    inv = 1.0 / norm
    dsum = (fwd + bwd) * inv
    ddif = (fwd - bwd) * inv
    sh = dsum.astype(BF16)
    dh = ddif.astype(BF16)
    sh_ref[...] = sh
    sl_ref[...] = (dsum - sh.astype(F32)).astype(BF16)
    dh_ref[...] = dh
    dl_ref[...] = (ddif - dh.astype(F32)).astype(BF16)


def _hy_filters(emb, win, fw1, fb1, fw2, fb2, fw3, fb3):
    l = emb.shape[0]
    hid = fw2.shape[0]
    nblk = HY_CH // LANE
    ncol = HY_ORDER * HY_CH
    const = lambda g: (0, 0)
    fcol = lambda g: (0, (g // nblk) * 2 * nblk + g % nblk)
    bcol = lambda g: (0, (g // nblk) * 2 * nblk + nblk + g % nblk)
    out = pl.BlockSpec((l, LANE), lambda g: (0, g))
    return pl.pallas_call(
        _hyfilt_kernel,
        grid=(HY_ORDER * nblk,),
        in_specs=[pl.BlockSpec(emb.shape, const),
                  pl.BlockSpec((l, LANE), lambda g: (0, g % nblk)),
                  pl.BlockSpec(fw1.shape, const), pl.BlockSpec(fb1.shape, const),
                  pl.BlockSpec(fw2.shape, const), pl.BlockSpec(fb2.shape, const),
                  pl.BlockSpec((hid, LANE), fcol), pl.BlockSpec((1, LANE), fcol),
                  pl.BlockSpec((hid, LANE), bcol), pl.BlockSpec((1, LANE), bcol)],
        out_specs=[out, out, out, out],
        out_shape=[jax.ShapeDtypeStruct((l, ncol), BF16)] * 4,
        compiler_params=_cparams(("arbitrary",)),
        name="hyena_filters",
    )(emb, win, fw1, fb1, fw2, fb2, fw3, fb3, fw3, fb3)


def _hyspec_kernel(f_ref, sh_ref, sl_ref, dh_ref, dl_ref, o_ref):
    tf = f_ref.shape[0] // 2
    fc = f_ref[0:tf, :]
    fs = f_ref[tf:2 * tf, :]
    o_ref[0:tf, :] = _bdot(fc, sh_ref[...]) + _bdot(fc, sl_ref[...])
    o_ref[tf:2 * tf, :] = _bdot(fs, dh_ref[...]) + _bdot(fs, dl_ref[...])


def _hy_spectrum(fmat, sh, sl, dh, dl, tf):
    rows, l = fmat.shape
    ncol = sh.shape[1]
    data = pl.BlockSpec((l, HY_CH), lambda j, o: (0, o))
    return pl.pallas_call(
        _hyspec_kernel,
        grid=(rows // (2 * tf), ncol // HY_CH),
        in_specs=[pl.BlockSpec((2 * tf, l), lambda j, o: (j, 0))] + [data] * 4,
        out_specs=pl.BlockSpec((2 * tf, HY_CH), lambda j, o: (j, o)),
        out_shape=jax.ShapeDtypeStruct((rows, ncol), F32),
        compiler_params=_cparams(("arbitrary", "arbitrary")),
        name="hyena_filter_spectrum",
    )(fmat, sh, sl, dh, dl)


def _shortconv_kernel(u_ref, w_ref, b_ref, o_ref):
    u = u_ref[0].astype(F32)
    l = u.shape[0]
    row = lax.broadcasted_iota(jnp.int32, u.shape, 0)
    up = jnp.where(row == 0, 0.0, pltpu.roll(u, 1, 0))
    un = jnp.where(row == l - 1, 0.0, pltpu.roll(u, l - 1, 0))
    w = w_ref[...]
    o_ref[0] = (up * w[0:1] + u * w[1:2] + un * w[2:3] + b_ref[...]).astype(BF16)


def _short_conv(u, w, bias):
    b, l, _ = u.shape
    g = HY_ORDER + 1
    return pl.pallas_call(
        _shortconv_kernel,
        grid=(b, g),
        in_specs=[pl.BlockSpec((1, l, HY_CH), lambda bb, gg: (bb, 0, gg)),
                  pl.BlockSpec((3, HY_CH), lambda bb, gg: (0, gg)),
                  pl.BlockSpec((1, HY_CH), lambda bb, gg: (0, gg))],
        out_specs=pl.BlockSpec((None, 1, l, HY_CH), lambda bb, gg: (gg, bb, 0, 0)),
        out_shape=jax.ShapeDtypeStruct((g, b, l, HY_CH), BF16),
        compiler_params=_cparams(("arbitrary", "arbitrary")),
        name="hyena_short_conv",
    )(u, w, bias)


def _hyfwd_kernel(f_ref, z_ref, k_ref, y_ref):
    tf = f_ref.shape[0] // 2
    s = _bdot(f_ref[...], z_ref[0])
    sr, si = s[0:tf], s[tf:2 * tf]
    kr, ki = k_ref[0:tf, :], k_ref[tf:2 * tf, :]
    y_ref[0, 0:tf, :] = (sr * kr - si * ki).astype(BF16)
    y_ref[0, tf:2 * tf, :] = (sr * ki + si * kr).astype(BF16)


def _hy_forward(fmat, z, kspec, order, tf):
    rows, l = fmat.shape
    b = z.shape[0]
    return pl.pallas_call(
        _hyfwd_kernel,
        grid=(rows // (2 * tf), b),
        in_specs=[pl.BlockSpec((2 * tf, l), lambda j, bb: (j, 0)),
                  pl.BlockSpec((1, l, HY_CH), lambda j, bb: (bb, 0, 0)),
                  pl.BlockSpec((2 * tf, HY_CH), lambda j, bb: (j, order))],
        out_specs=pl.BlockSpec((1, 2 * tf, HY_CH), lambda j, bb: (bb, j, 0)),
        out_shape=jax.ShapeDtypeStruct((b, rows, HY_CH), BF16),
        compiler_params=_cparams(("arbitrary", "arbitrary")),
        name="hyena_dft_forward",
    )(fmat, z, kspec)


def _hyinv_kernel(g_ref, y_ref, z_ref, x_ref, b_ref, o_ref):
    conv = _bdot(g_ref[...], y_ref[0])
    z = z_ref[0].astype(F32)
    o_ref[0] = (x_ref[0].astype(F32) * (conv + z * b_ref[...])).astype(BF16)


def _hy_inverse(gmat, y, z, gate, bias, tt):
    l, rows = gmat.shape
    b = y.shape[0]
    return pl.pallas_call(
        _hyinv_kernel,
        grid=(l // tt, b),
        in_specs=[pl.BlockSpec((tt, rows), lambda i, bb: (i, 0)),
                  pl.BlockSpec((1, rows, HY_CH), lambda i, bb: (bb, 0, 0)),
                  pl.BlockSpec((1, tt, HY_CH), lambda i, bb: (bb, i, 0)),
                  pl.BlockSpec((1, tt, HY_CH), lambda i, bb: (bb, i, 0)),
                  pl.BlockSpec((1, HY_CH), lambda i, bb: (0, 0))],
        out_specs=pl.BlockSpec((1, tt, HY_CH), lambda i, bb: (bb, i, 0)),
        out_shape=jax.ShapeDtypeStruct((b, l, HY_CH), BF16),
        compiler_params=_cparams(("arbitrary", "arbitrary")),
        name="hyena_dft_inverse",
    )(gmat, y, z, gate, bias)


def _dft_tables(l, tf):
    k = jnp.arange(l, dtype=jnp.int32)
    n = jnp.arange(l, dtype=jnp.int32)
    ph = ((2 * k + 1)[:, None] * n[None, :]) % (4 * l)
    ang = ph.astype(F32) * (2.0 * math.pi / (4 * l))
    c = jnp.cos(ang).reshape(l // tf, 1, tf, l)
    s = (-jnp.sin(ang)).reshape(l // tf, 1, tf, l)
    f = jnp.concatenate([c, s], axis=1).reshape(2 * l, l)
    return f.astype(BF16), (f.T * (1.0 / l)).astype(BF16)


def _hy_consts(l):
    t = jnp.arange(l, dtype=F32)
    bands = jnp.arange(1, HY_BANDS + 1, dtype=F32)
    ang = (2.0 * math.pi / l) * t[:, None] * bands[None, :]
    emb = jnp.concatenate([(t / l)[:, None], jnp.cos(ang), jnp.sin(ang)], -1)
    emb = jnp.pad(emb, ((0, 0), (0, LANE - HY_EMB)))
    rates = jnp.linspace(HY_MIN_RATE, HY_MAX_RATE, HY_CH, dtype=F32)
    win = jnp.exp(-(t / l)[:, None] * rates[None, :])
    return emb, win


def _hyena(u, consts, conv_w, conv_b, fw1, fb1, fw2, fb2, fw3, fb3, hbias):
    emb, win, fmat, gmat, tf = consts
    sh, sl, dh, dl = _hy_filters(emb, win, fw1, fb1, fw2, fb2, fw3, fb3)
    kspec = _hy_spectrum(fmat, sh, sl, dh, dl, tf)
    vxx = _short_conv(u, conv_w, conv_b)
    v, x1, x2 = vxx[0], vxx[1], vxx[2]
    z = _hy_inverse(gmat, _hy_forward(fmat, v, kspec, 0, tf), v, x1, hbias[0:1], tf)
    return _hy_inverse(gmat, _hy_forward(fmat, z, kspec, 1, tf), z, x2, hbias[1:2], tf)


def _outproj_kernel(om_ref, od_ref, oh_ref, wm_ref, wd_ref, wh_ref, bo_ref, x_ref, g_ref, lg_ref, lb_ref,
                    o_ref, *, alpha):
    y = _bdot(om_ref[0], wm_ref[...]) + _bdot(od_ref[0], wd_ref[...]) + _bdot(oh_ref[0], wh_ref[...])
    z = alpha * x_ref[0] + g_ref[0] * (y + bo_ref[...])
    o_ref[0] = _ln(z) * lg_ref[...] + lb_ref[...]


def _out_proj(om, od, oh, wm, wd, wh, bo, x, gate, lg, lb, alpha, tm):
    b, t, d = x.shape
    per_batch = gate.shape[0] == b and b > 1
    gmap = (lambda bb, i: (bb, 0, 0)) if per_batch else (lambda bb, i: (0, 0, 0))
    const = lambda bb, i: (0, 0)
    row = lambda bb, i: (bb, i, 0)
    return pl.pallas_call(
        functools.partial(_outproj_kernel, alpha=alpha),
        grid=(b, t // tm),
        in_specs=[pl.BlockSpec((1, tm, om.shape[2]), row), pl.BlockSpec((1, tm, od.shape[2]), row),
                  pl.BlockSpec((1, tm, oh.shape[2]), row),
                  pl.BlockSpec(wm.shape, const), pl.BlockSpec(wd.shape, const), pl.BlockSpec(wh.shape, const),
                  pl.BlockSpec((1, d), const), pl.BlockSpec((1, tm, d), row), pl.BlockSpec((1, 1, d), gmap),
                  pl.BlockSpec((1, d), const), pl.BlockSpec((1, d), const)],
        out_specs=pl.BlockSpec((1, tm, d), row),
        out_shape=jax.ShapeDtypeStruct((b, t, d), F32),
        compiler_params=_cparams(("arbitrary", "arbitrary")),
        name="out_proj",
    )(om, od, oh, wm, wd, wh, bo, x, gate, lg, lb)


def _ffn_kernel(x_ref, sh_ref, sc_ref, g_ref, w1_ref, b1_ref, w2_ref, b2_ref, lg_ref, lb_ref, o_ref,
                h_scr, acc_scr, *, alpha):
    j = pl.program_id(2)

    @pl.when(j == 0)
    def _():
        h_scr[...] = (_ln(x_ref[0]) * (1.0 + sc_ref[0]) + sh_ref[0]).astype(BF16)
        acc_scr[...] = jnp.zeros_like(acc_scr)

    a = jnp.maximum(_bdot(h_scr[...], w1_ref[...]) + b1_ref[...], 0.0)
    acc_scr[...] += _bdot((a * a).astype(BF16), w2_ref[...])

    @pl.when(j == pl.num_programs(2) - 1)
    def _():
        z = alpha * x_ref[0] + g_ref[0] * (acc_scr[...] + b2_ref[...])
        o_ref[0] = _ln(z) * lg_ref[...] + lb_ref[...]


def _ffn(x, sh, sc, gate, w1, b1, w2, b2, lg, lb, alpha, tm, tf):
    b, t, d = x.shape
    dff = w1.shape[1]
    per_batch = gate.shape[0] == b and b > 1
    gmap = (lambda bb, i, j: (bb, 0, 0)) if per_batch else (lambda bb, i, j: (0, 0, 0))
    const = lambda bb, i, j: (0, 0)
    row = lambda bb, i, j: (bb, i, 0)
    return pl.pallas_call(
        functools.partial(_ffn_kernel, alpha=alpha),
        grid=(b, t // tm, dff // tf),
        in_specs=[pl.BlockSpec((1, tm, d), row),
                  pl.BlockSpec((1, 1, d), gmap), pl.BlockSpec((1, 1, d), gmap), pl.BlockSpec((1, 1, d), gmap),
                  pl.BlockSpec((d, tf), lambda bb, i, j: (0, j)), pl.BlockSpec((1, tf), lambda bb, i, j: (0, j)),
                  pl.BlockSpec((tf, d), lambda bb, i, j: (j, 0)), pl.BlockSpec((1, d), const),
                  pl.BlockSpec((1, d), const), pl.BlockSpec((1, d), const)],
        out_specs=pl.BlockSpec((1, tm, d), row),
        out_shape=jax.ShapeDtypeStruct((b, t, d), F32),
        scratch_shapes=[pltpu.VMEM((tm, d), BF16), pltpu.VMEM((tm, d), F32)],
        compiler_params=_cparams(("arbitrary", "arbitrary", "arbitrary")),
        name="ffn",
    )(x, sh, sc, gate, w1, b1, w2, b2, lg, lb)


def _rope_tables(t_len, roped):
    if not roped:
        one = jnp.ones((t_len, LANE), F32)
        zero = jnp.zeros((t_len, LANE), F32)
        return jnp.concatenate([one, zero, zero, one, zero, zero], axis=1)
    pos_t = jnp.arange(t_len)
    row = (pos_t // GRID_W).astype(F32)[:, None]
    col = (pos_t % GRID_W).astype(F32)[:, None]

    def tile(starts, dims):
        h = dims // 2
        lane = jnp.arange(LANE)
        c = jnp.ones((t_len, LANE), F32)
        sa = jnp.zeros((t_len, LANE), F32)
        sb = jnp.zeros((t_len, LANE), F32)
        inv = ROPE_BASE ** (-(jnp.arange(h // 2, dtype=F32) * (2.0 / h)))
        for st in starts:
            for axis_i, pos in enumerate((row, col)):
                ang = pos * inv[None, :]
                cos, sin = jnp.cos(ang), jnp.sin(ang)
                lo = st + axis_i * h
                c = c.at[:, lo:lo + h // 2].set(cos).at[:, lo + h // 2:lo + h].set(cos)
                sa = sa.at[:, lo:lo + h // 2].set(-sin)
                sb = sb.at[:, lo + h // 2:lo + h].set(sin)
        del lane
        return [c, sa, sb]

    return jnp.concatenate(tile([MLA_NOPE], MLA_ROPE) + tile([0, LANE // 2], DIFF_DIM), axis=1)


def _prep_weights(w_in, w_q_up, w_kv_up, w_out, diff_lambda, diff_subln):
    depth, d, _ = w_in.shape
    s1, s2 = IN_MLA, IN_MLA + IN_DIFF
    pad_last = lambda a, n: jnp.pad(a, [(0, 0)] * (a.ndim - 1) + [(0, n - a.shape[-1])])
    kr = jnp.pad(w_in[..., C_KR:s1], ((0, 0), (0, 0), (MLA_NOPE, LANE - MLA_NOPE - MLA_ROPE)))
    dq = pad_last(w_in[..., s1:s1 + N_DQ].reshape(depth, d, DIFF_HEADS, 2, DIFF_DIM), LANE // 2)
    dk = pad_last(w_in[..., s1 + N_DQ:s1 + 2 * N_DQ].reshape(depth, d, DIFF_HEADS, 2, DIFF_DIM), LANE // 2)
    dv = pad_last(w_in[..., s1 + 2 * N_DQ:s2].reshape(depth, d, DIFF_HEADS, DIFF_V), LANE)
    win = jnp.concatenate([w_in[..., :C_KR], kr, dq.reshape(depth, d, W_D), dk.reshape(depth, d, W_D),
                           dv.reshape(depth, d, W_D), w_in[..., s2:]], axis=-1).astype(BF16)
    wq = pad_last(w_q_up.reshape(depth, MLA_Q_RANK, MLA_HEADS, MLA_NOPE + MLA_ROPE), LANE)
    wq = wq.reshape(depth, MLA_Q_RANK, W_QM).astype(BF16)
    kv = w_kv_up.reshape(depth, MLA_KV_RANK, MLA_HEADS, MLA_NOPE + MLA_V)
    wk = pad_last(kv[..., :MLA_NOPE], LANE).reshape(depth, MLA_KV_RANK, W_QM)
    wv = kv[..., MLA_NOPE:].reshape(depth, MLA_KV_RANK, W_VM)
    wkv = jnp.concatenate([wk, wv], axis=-1).astype(BF16)
    wo_m = w_out[:, :W_VM].astype(BF16)
    wo_d = w_out[:, W_VM:W_VM + DIFF_HEADS * DIFF_V].reshape(depth, DIFF_HEADS, DIFF_V, -1)
    wo_d = jnp.pad(wo_d, ((0, 0), (0, 0), (0, LANE - DIFF_V), (0, 0))).reshape(depth, W_D, -1).astype(BF16)
    wo_h = w_out[:, W_VM + DIFF_HEADS * DIFF_V:].astype(BF16)
    lam = pad_last(diff_lambda.astype(F32), LANE)
    subln = pad_last(diff_subln.astype(F32), LANE)[:, None, :]
    return win, wq, wkv, wo_m, wo_d, wo_h, lam, subln


def kernel(x, c, ctx, c_ctx, w_mod, b_mod, w_in, mla_q_norm, w_q_up, mla_kv_norm, w_kv_up, diff_lambda, diff_subln, hy_conv_w, hy_conv_b, hy_fw1, hy_fb1, hy_fw2, hy_fb2, hy_fw3, hy_fb3, hy_bias, w_out, b_out, ln1_g, ln1_b, w_ff1, b_ff1, w_ff2, b_ff2, ln2_g, ln2_b):
    bsz, seq, d = x.shape
    n_ctx = ctx.shape[1]
    depth = w_in.shape[0]
    alpha = (2.0 * depth) ** 0.25

    rpad = -(bsz + 1) % 8
    cc = jnp.concatenate([c, c_ctx[None, :], jnp.zeros((rpad, d), F32)], axis=0)
    mod_all = _modulation(cc, w_mod, b_mod)

    win, wq, wkv, wo_m, wo_d, wo_h, lam_p, subln = _prep_weights(w_in, w_q_up, w_kv_up, w_out, diff_lambda, diff_subln)
    w1 = w_ff1.astype(BF16)
    w2 = w_ff2.astype(BF16)
    fw1 = jnp.pad(hy_fw1, ((0, 0), (0, LANE - HY_EMB), (0, 0)))
    tab_x = _rope_tables(seq, True)
    tab_c = _rope_tables(n_ctx, False)
    tf_x, tf_c = min(512, seq), min(512, n_ctx)
    hyc_x = _hy_consts(seq) + _dft_tables(seq, tf_x) + (tf_x,)
    hyc_c = _hy_consts(n_ctx) + _dft_tables(n_ctx, tf_c) + (tf_c,)
    tm_x, tm_c = min(512, seq), min(512, n_ctx)
    tq_x, tq_c = min(256, seq), min(256, n_ctx)
    tff = min(1024, w1.shape[2])
    row2 = lambda a: a.reshape(1, -1)

    for layer in range(depth):
        need_ctx = layer < depth - 1
        lam_init = 0.8 - 0.6 * math.exp(-0.3 * layer)
        mod = mod_all[layer, :bsz].reshape(bsz, 1, N_MOD, d)
        modc = mod_all[layer, bsz:bsz + 1].reshape(1, 1, N_MOD, d)
        sh1, sc1, g1, sh2, sc2, g2 = [mod[:, :, i] for i in range(N_MOD)]
        csh1, csc1, cg1, csh2, csc2, cg2 = [modc[:, :, i] for i in range(N_MOD)]
        gq, gkv = row2(mla_q_norm[layer]), row2(mla_kv_norm[layer])
        hy_args = (hy_conv_w[layer], row2(hy_conv_b[layer]), fw1[layer], row2(hy_fb1[layer]), hy_fw2[layer],
                   row2(hy_fb2[layer]), hy_fw3[layer], row2(hy_fb3[layer]), hy_bias[layer])

        qm, km, vm, dq, dk, dv, uh = _in_proj(x, sh1, sc1, win[layer], gq, wq[layer], gkv, wkv[layer], tab_x, tm_x)
        cqm, ckm, cvm, cdq, cdk, cdv, cuh = _in_proj(ctx, csh1, csc1, win[layer], gq, wq[layer], gkv, wkv[layer],
                                                     tab_c, tm_c)
        om = _mla_attention(qm, [ckm, km], [cvm, vm], tq_x)
        od = _diff_attention(dq, [cdk, dk], [cdv, dv], lam_p[layer], subln[layer], lam_init, tq_x)
        oh = _hyena(uh, hyc_x, *hy_args)
        x = _out_proj(om, od, oh, wo_m[layer], wo_d[layer], wo_h[layer], row2(b_out[layer]), x, g1,
                      row2(ln1_g[layer]), row2(ln1_b[layer]), alpha, tm_x)
        x = _ffn(x, sh2, sc2, g2, w1[layer], row2(b_ff1[layer]), w2[layer], row2(b_ff2[layer]),
                 row2(ln2_g[layer]), row2(ln2_b[layer]), alpha, tm_x, tff)
        if need_ctx:
            com = _mla_attention(cqm, [ckm], [cvm], tq_c)
            cod = _diff_attention(cdq, [cdk], [cdv], lam_p[layer], subln[layer], lam_init, tq_c)
            coh = _hyena(cuh, hyc_c, *hy_args)
            ctx = _out_proj(com, cod, coh, wo_m[layer], wo_d[layer], wo_h[layer], row2(b_out[layer]), ctx, cg1,
                            row2(ln1_g[layer]), row2(ln1_b[layer]), alpha, tm_c)
            ctx = _ffn(ctx, csh2, csc2, cg2, w1[layer], row2(b_ff1[layer]), w2[layer], row2(b_ff2[layer]),
                       row2(ln2_g[layer]), row2(ln2_b[layer]), alpha, tm_c, tff)
    return x
```

```python
import functools
import math

import jax
import jax.numpy as jnp
from jax import lax
from jax.experimental import pallas as pl
from jax.experimental.pallas import tpu as pltpu

F32 = jnp.float32
BF16 = jnp.bfloat16
HI = lax.Precision.HIGHEST

GRID_W = 64
ROPE_BASE = 10000.0
LN_EPS = 1e-6
MLA_HEADS, MLA_NOPE, MLA_ROPE, MLA_V = 6, 64, 32, 64
MLA_Q_RANK, MLA_KV_RANK = 256, 128
DIFF_HEADS, DIFF_DIM = 4, 48
DIFF_V = 2 * DIFF_DIM
HY_CH, HY_ORDER, HY_BANDS = 256, 2, 16
HY_EMB = 1 + 2 * HY_BANDS
HY_TARGET, HY_FAST_DECAY_PCT, HY_SLOW_DECAY_PCT = 1e-2, 0.3, 1.5
HY_MIN_RATE = -math.log(HY_TARGET) / HY_SLOW_DECAY_PCT
HY_MAX_RATE = -math.log(HY_TARGET) / HY_FAST_DECAY_PCT
N_MOD = 6

LOG2E = 1.0 / math.log(2.0)
LANE = 128
VMEM_LIMIT = 56 * 1024 * 1024

IN_MLA = MLA_Q_RANK + MLA_KV_RANK + MLA_ROPE
N_DQ = DIFF_HEADS * 2 * DIFF_DIM
IN_DIFF = 2 * N_DQ + DIFF_HEADS * DIFF_V
W_QM = MLA_HEADS * LANE
W_VM = MLA_HEADS * MLA_V
W_D = DIFF_HEADS * LANE
W_HY = (HY_ORDER + 1) * HY_CH
C_CQ, C_CKV, C_KR = 0, MLA_Q_RANK, MLA_Q_RANK + MLA_KV_RANK
C_DQ = C_KR + LANE
C_DK = C_DQ + W_D
C_DV = C_DK + W_D
C_HY = C_DV + W_D
W_IN = C_HY + W_HY


def _cparams(sem):
    return pltpu.CompilerParams(dimension_semantics=sem, vmem_limit_bytes=VMEM_LIMIT)


def _ln(x):
    mu = jnp.mean(x, -1, keepdims=True)
    xc = x - mu
    var = jnp.mean(xc * xc, -1, keepdims=True)
    return xc * lax.rsqrt(var + LN_EPS)


def _bdot(a, b):
    return jnp.dot(a, b, preferred_element_type=F32)


def _mod_kernel(c_ref, w_ref, b_ref, o_ref):
    c = c_ref[...]
    s = c / (1.0 + jnp.exp(-c))
    o_ref[...] = jnp.dot(s, w_ref[...], preferred_element_type=F32, precision=HI) + b_ref[...]


def _modulation(cc, w_mod, b_mod):
    depth, d, n = w_mod.shape
    r = cc.shape[0]
    tn = 1024
    return pl.pallas_call(
        _mod_kernel,
        grid=(depth, n // tn),
        in_specs=[pl.BlockSpec((r, d), lambda l, j: (0, 0)),
                  pl.BlockSpec((None, d, tn), lambda l, j: (l, 0, j)),
                  pl.BlockSpec((None, 1, tn), lambda l, j: (l, 0, j))],
        out_specs=pl.BlockSpec((None, r, tn), lambda l, j: (l, 0, j)),
        out_shape=jax.ShapeDtypeStruct((depth, r, n), F32),
        compiler_params=_cparams(("arbitrary", "arbitrary")),
        name="modulation",
    )(cc, w_mod, b_mod.reshape(depth, 1, n))


def _rope(x, c, sa, sb, half):
    return x * c + pltpu.roll(x, LANE - half, 1) * sa + pltpu.roll(x, half, 1) * sb


def _inproj_kernel(x_ref, sh_ref, sc_ref, win_ref, gq_ref, wq_ref, gkv_ref, wkv_ref, tab_ref,
                   qm_ref, km_ref, vm_ref, dq_ref, dk_ref, dv_ref, hy_ref):
    x = x_ref[0]
    h = _ln(x) * (1.0 + sc_ref[0]) + sh_ref[0]
    p = _bdot(h.astype(BF16), win_ref[...])
    tab = tab_ref[...]
    cm, sam, sbm = tab[:, 0:LANE], tab[:, LANE:2 * LANE], tab[:, 2 * LANE:3 * LANE]
    cd, sad, sbd = tab[:, 3 * LANE:4 * LANE], tab[:, 4 * LANE:5 * LANE], tab[:, 5 * LANE:6 * LANE]

    cq = p[:, C_CQ:C_CQ + MLA_Q_RANK]
    qn = cq * lax.rsqrt(jnp.mean(cq * cq, -1, keepdims=True) + LN_EPS) * gq_ref[...]
    q = _bdot(qn.astype(BF16), wq_ref[...])
    ckv = p[:, C_CKV:C_CKV + MLA_KV_RANK]
    kvn = ckv * lax.rsqrt(jnp.mean(ckv * ckv, -1, keepdims=True) + LN_EPS) * gkv_ref[...]
    kv = _bdot(kvn.astype(BF16), wkv_ref[...])
    kr = _rope(p[:, C_KR:C_KR + LANE], cm, sam, sbm, MLA_ROPE // 4)
    scale_m = LOG2E * (MLA_NOPE + MLA_ROPE) ** -0.5
    for hd in range(MLA_HEADS):
        sl = slice(hd * LANE, (hd + 1) * LANE)
        qm_ref[0, :, sl] = (_rope(q[:, sl], cm, sam, sbm, MLA_ROPE // 4) * scale_m).astype(BF16)
        km_ref[0, :, sl] = (kv[:, sl] + kr).astype(BF16)
    vm_ref[0] = kv[:, W_QM:W_QM + W_VM].astype(BF16)
    scale_d = LOG2E * DIFF_DIM ** -0.5
    for hd in range(DIFF_HEADS):
        sl = slice(hd * LANE, (hd + 1) * LANE)
        dq = p[:, C_DQ + hd * LANE:C_DQ + (hd + 1) * LANE]
        dk = p[:, C_DK + hd * LANE:C_DK + (hd + 1) * LANE]
        dq_ref[0, :, sl] = (_rope(dq, cd, sad, sbd, DIFF_DIM // 4) * scale_d).astype(BF16)
        dk_ref[0, :, sl] = _rope(dk, cd, sad, sbd, DIFF_DIM // 4).astype(BF16)
    dv_ref[0] = p[:, C_DV:C_DV + W_D].astype(BF16)
    hy_ref[0] = p[:, C_HY:C_HY + W_HY].astype(BF16)


def _in_proj(x, sh, sc, win, gq, wq, gkv, wkv, tab, tm):
    b, t, d = x.shape
    per_batch = sh.shape[0] == b and b > 1
    mod_map = (lambda i, bb: (bb, 0, 0)) if per_batch else (lambda i, bb: (0, 0, 0))
    const = lambda i, bb: (0, 0)
    widths = (W_QM, W_QM, W_VM, W_D, W_D, W_D, W_HY)
    return pl.pallas_call(
        _inproj_kernel,
        grid=(t // tm, b),
        in_specs=[pl.BlockSpec((1, tm, d), lambda i, bb: (bb, i, 0)),
                  pl.BlockSpec((1, 1, d), mod_map),
                  pl.BlockSpec((1, 1, d), mod_map),
                  pl.BlockSpec(win.shape, const),
                  pl.BlockSpec(gq.shape, const),
                  pl.BlockSpec(wq.shape, const),
                  pl.BlockSpec(gkv.shape, const),
                  pl.BlockSpec(wkv.shape, const),
                  pl.BlockSpec((tm, 6 * LANE), lambda i, bb: (i, 0))],
        out_specs=[pl.BlockSpec((1, tm, w), lambda i, bb: (bb, i, 0)) for w in widths],
        out_shape=[jax.ShapeDtypeStruct((b, t, w), BF16) for w in widths],
        compiler_params=_cparams(("arbitrary", "arbitrary")),
        name="in_proj",
    )(x, sh, sc, win, gq, wq, gkv, wkv, tab)


NEG_BIG = -1e30


def _lane_fold(x, op):
    r = x[:, 0:LANE]
    for i in range(1, x.shape[1] // LANE):
        r = op(r, x[:, i * LANE:(i + 1) * LANE])
    return r


def _score_pass(q, parts, ksl, s_refs):
    mx = jnp.full((q.shape[0], LANE), NEG_BIG, F32)
    for (k_ref, _, n, tk), s_ref in zip(parts, s_refs):
        for j in range(n // tk):
            k = k_ref[0, j * tk:(j + 1) * tk, ksl]
            s = lax.dot_general(q, k, (((1,), (1,)), ((), ())), preferred_element_type=F32)
            s_ref[j] = s
            mx = jnp.maximum(mx, _lane_fold(s, jnp.maximum))
    return jnp.max(mx, axis=-1, keepdims=True)


def _softmax_pv(q, parts, ksl, vsl, s_refs):
    m = _score_pass(q, parts, ksl, s_refs)
    ls = jnp.zeros((q.shape[0], LANE), F32)
    acc = jnp.zeros((q.shape[0], LANE), F32)
    for (_, v_ref, n, tk), s_ref in zip(parts, s_refs):
        for j in range(n // tk):
            p = jnp.exp2(s_ref[j] - m)
            ls = ls + _lane_fold(p, jnp.add)
            acc = acc + _bdot(p.astype(BF16), v_ref[0, j * tk:(j + 1) * tk, vsl])
    return acc * (1.0 / jnp.sum(ls, axis=-1, keepdims=True))


def _exp_pass(parts, s_refs, m):
    ls = jnp.zeros((m.shape[0], LANE), F32)
    for (_, _, n, tk), s_ref in zip(parts, s_refs):
        for j in range(n // tk):
            p = jnp.exp2(s_ref[j] - m)
            s_ref[j] = p
            ls = ls + _lane_fold(p, jnp.add)
    return jnp.sum(ls, axis=-1, keepdims=True)


def _mla_attn_kernel(*refs, part_shapes):
    np_ = len(part_shapes)
    q_ref = refs[0]
    k_refs = refs[1:1 + np_]
    v_refs = refs[1 + np_:1 + 2 * np_]
    o_ref = refs[1 + 2 * np_]
    s_refs = refs[2 + 2 * np_:]
    parts = [(k_refs[i], v_refs[i]) + part_shapes[i] for i in range(np_)]
    outs = []
    for hh in range(2):
        sl = slice(hh * LANE, (hh + 1) * LANE)
        outs.append(_softmax_pv(q_ref[0, :, sl], parts, sl, slice(0, LANE), s_refs))
    lane = lax.broadcasted_iota(jnp.int32, outs[0].shape, 1)
    o_ref[0] = jnp.where(lane < MLA_V, outs[0], outs[1]).astype(BF16)


def _chunk(n):
    return n if n <= 512 else 512


def _mla_attention(q, ks, vs, tq):
    b, t, _ = q.shape
    part_shapes = tuple((k.shape[1], _chunk(k.shape[1])) for k in ks)
    in_specs = [pl.BlockSpec((1, tq, 2 * LANE), lambda bb, pr, i: (bb, i, pr))]
    in_specs += [pl.BlockSpec((1, k.shape[1], 2 * LANE), lambda bb, pr, i: (bb, 0, pr)) for k in ks]
    in_specs += [pl.BlockSpec((1, v.shape[1], LANE), lambda bb, pr, i: (bb, 0, pr)) for v in vs]
    scratch = [pltpu.VMEM((n // tk, tq, tk), F32) for n, tk in part_shapes]
    return pl.pallas_call(
        functools.partial(_mla_attn_kernel, part_shapes=part_shapes),
        grid=(b, MLA_HEADS // 2, t // tq),
        in_specs=in_specs,
        out_specs=pl.BlockSpec((1, tq, LANE), lambda bb, pr, i: (bb, i, pr)),
        out_shape=jax.ShapeDtypeStruct((b, t, W_VM), BF16),
        scratch_shapes=scratch,
        compiler_params=_cparams(("arbitrary", "arbitrary", "arbitrary")),
        name="mla_attention",
    )(q, *ks, *vs)


def _diff_attn_kernel(*refs, part_shapes, lam_init):
    np_ = len(part_shapes)
    q_ref = refs[0]
    k_refs = refs[1:1 + np_]
    v_refs = refs[1 + np_:1 + 2 * np_]
    lam_ref, g_ref, o_ref = refs[1 + 2 * np_:4 + 2 * np_]
    s1_refs = refs[4 + 2 * np_:4 + 3 * np_]
    s2_refs = refs[4 + 3 * np_:]
    parts = [(k_refs[i], v_refs[i]) + part_shapes[i] for i in range(np_)]
    q = q_ref[0]
    lane = lax.broadcasted_iota(jnp.int32, q.shape, 1)
    zero = jnp.zeros_like(q)
    full = slice(0, LANE)
    m1 = _score_pass(jnp.where(lane < LANE // 2, q, zero), parts, full, s1_refs)
    m2 = _score_pass(jnp.where(lane >= LANE // 2, q, zero), parts, full, s2_refs)
    l1 = _exp_pass(parts, s1_refs, m1)
    l2 = _exp_pass(parts, s2_refs, m2)
    lp = lam_ref[...]
    lam = (jnp.exp(jnp.sum(lp[0:1] * lp[1:2], axis=-1, keepdims=True))
           - jnp.exp(jnp.sum(lp[2:3] * lp[3:4], axis=-1, keepdims=True)) + lam_init)
    r1 = 1.0 / l1
    r2 = lam / l2
    o = jnp.zeros((q.shape[0], LANE), F32)
    for (_, v_ref, n, tk), s1_ref, s2_ref in zip(parts, s1_refs, s2_refs):
        for j in range(n // tk):
            a = s1_ref[j] * r1 - s2_ref[j] * r2
            o = o + _bdot(a.astype(BF16), v_ref[0, j * tk:(j + 1) * tk, :])
    ms = jnp.sum(o * o, axis=-1, keepdims=True) * (1.0 / DIFF_V)
    o_ref[0] = (o * lax.rsqrt(ms + LN_EPS) * g_ref[...] * (1.0 - lam_init)).astype(BF16)


def _diff_attention(q, ks, vs, lam_p, subln, lam_init, tq):
    b, t, _ = q.shape
    part_shapes = tuple((k.shape[1], _chunk(k.shape[1])) for k in ks)
    hmap = lambda bb, hd, i: (bb, 0, hd)
    in_specs = [pl.BlockSpec((1, tq, LANE), lambda bb, hd, i: (bb, i, hd))]
    in_specs += [pl.BlockSpec((1, k.shape[1], LANE), hmap) for k in ks]
    in_specs += [pl.BlockSpec((1, v.shape[1], LANE), hmap) for v in vs]
    in_specs += [pl.BlockSpec(lam_p.shape, lambda bb, hd, i: (0, 0)),
                 pl.BlockSpec(subln.shape, lambda bb, hd, i: (0, 0))]
    scratch = [pltpu.VMEM((n // tk, tq, tk), F32) for n, tk in part_shapes] * 2
    return pl.pallas_call(
        functools.partial(_diff_attn_kernel, part_shapes=part_shapes, lam_init=lam_init),
        grid=(b, DIFF_HEADS, t // tq),
        in_specs=in_specs,
        out_specs=pl.BlockSpec((1, tq, LANE), lambda bb, hd, i: (bb, i, hd)),
        out_shape=jax.ShapeDtypeStruct((b, t, W_D), BF16),
        scratch_shapes=scratch,
        compiler_params=_cparams(("arbitrary", "arbitrary", "arbitrary")),
        name="diff_attention",
    )(q, *ks, *vs, lam_p, subln)


def _hyfilt_kernel(emb_ref, win_ref, fw1_ref, fb1_ref, fw2_ref, fb2_ref, fwf_ref, fbf_ref, fwb_ref, fbb_ref,
                   sh_ref, sl_ref, dh_ref, dl_ref):
    h = jnp.sin(jnp.dot(emb_ref[...], fw1_ref[...], preferred_element_type=F32, precision=HI) + fb1_ref[...])
    h = jnp.sin(jnp.dot(h, fw2_ref[...], preferred_element_type=F32, precision=HI) + fb2_ref[...])
    w = win_ref[...]
    fwd = (jnp.dot(h, fwf_ref[...], preferred_element_type=F32, precision=HI) + fbf_ref[...]) * w
    bwd = (jnp.dot(h, fwb_ref[...], preferred_element_type=F32, precision=HI) + fbb_ref[...]) * w
    row = lax.broadcasted_iota(jnp.int32, bwd.shape, 0)
    bwd = jnp.where(row == 0, 0.0, bwd)
    norm = jnp.sum(jnp.abs(fwd), axis=0, keepdims=True) + jnp.sum(jnp.abs(bwd), axis=0, keepdims=True)
    inv = 1.0 / norm
    dsum = (fwd + bwd) * inv
    ddif = (fwd - bwd) * inv
    sh = dsum.astype(BF16)
    dh = ddif.astype(BF16)
    sh_ref[...] = sh
    sl_ref[...] = (dsum - sh.astype(F32)).astype(BF16)
    dh_ref[...] = dh
    dl_ref[...] = (ddif - dh.astype(F32)).astype(BF16)


def _hy_filters(emb, win, fw1, fb1, fw2, fb2, fw3, fb3):
    l = emb.shape[0]
    hid = fw2.shape[0]
    nblk = HY_CH // LANE
    ncol = HY_ORDER * HY_CH
    const = lambda g: (0, 0)
    fcol = lambda g: (0, (g // nblk) * 2 * nblk + g % nblk)
    bcol = lambda g: (0, (g // nblk) * 2 * nblk + nblk + g % nblk)
    out = pl.BlockSpec((l, LANE), lambda g: (0, g))
    return pl.pallas_call(
        _hyfilt_kernel,
        grid=(HY_ORDER * nblk,),
        in_specs=[pl.BlockSpec(emb.shape, const),
                  pl.BlockSpec((l, LANE), lambda g: (0, g % nblk)),
                  pl.BlockSpec(fw1.shape, const), pl.BlockSpec(fb1.shape, const),
                  pl.BlockSpec(fw2.shape, const), pl.BlockSpec(fb2.shape, const),
                  pl.BlockSpec((hid, LANE), fcol), pl.BlockSpec((1, LANE), fcol),
                  pl.BlockSpec((hid, LANE), bcol), pl.BlockSpec((1, LANE), bcol)],
        out_specs=[out, out, out, out],
        out_shape=[jax.ShapeDtypeStruct((l, ncol), BF16)] * 4,
        compiler_params=_cparams(("arbitrary",)),
        name="hyena_filters",
    )(emb, win, fw1, fb1, fw2, fb2, fw3, fb3, fw3, fb3)


def _hyspec_kernel(f_ref, sh_ref, sl_ref, dh_ref, dl_ref, o_ref):
    tf = f_ref.shape[0] // 2
    fc = f_ref[0:tf, :]
    fs = f_ref[tf:2 * tf, :]
    o_ref[0:tf, :] = _bdot(fc, sh_ref[...]) + _bdot(fc, sl_ref[...])
    o_ref[tf:2 * tf, :] = _bdot(fs, dh_ref[...]) + _bdot(fs, dl_ref[...])


def _hy_spectrum(fmat, sh, sl, dh, dl, tf):
    rows, l = fmat.shape
    ncol = sh.shape[1]
    data = pl.BlockSpec((l, HY_CH), lambda j, o: (0, o))
    return pl.pallas_call(
        _hyspec_kernel,
        grid=(rows // (2 * tf), ncol // HY_CH),
        in_specs=[pl.BlockSpec((2 * tf, l), lambda j, o: (j, 0))] + [data] * 4,
        out_specs=pl.BlockSpec((2 * tf, HY_CH), lambda j, o: (j, o)),
        out_shape=jax.ShapeDtypeStruct((rows, ncol), F32),
        compiler_params=_cparams(("arbitrary", "arbitrary")),
        name="hyena_filter_spectrum",
    )(fmat, sh, sl, dh, dl)


def _shortconv_kernel(u_ref, w_ref, b_ref, o_ref):
    u = u_ref[0].astype(F32)
    l = u.shape[0]
    row = lax.broadcasted_iota(jnp.int32, u.shape, 0)
    up = jnp.where(row == 0, 0.0, pltpu.roll(u, 1, 0))
    un = jnp.where(row == l - 1, 0.0, pltpu.roll(u, l - 1, 0))
    w = w_ref[...]
    o_ref[0] = (up * w[0:1] + u * w[1:2] + un * w[2:3] + b_ref[...]).astype(BF16)


def _short_conv(u, w, bias):
    b, l, _ = u.shape
    g = HY_ORDER + 1
    return pl.pallas_call(
        _shortconv_kernel,
        grid=(b, g),
        in_specs=[pl.BlockSpec((1, l, HY_CH), lambda bb, gg: (bb, 0, gg)),
                  pl.BlockSpec((3, HY_CH), lambda bb, gg: (0, gg)),
                  pl.BlockSpec((1, HY_CH), lambda bb, gg: (0, gg))],
        out_specs=pl.BlockSpec((None, 1, l, HY_CH), lambda bb, gg: (gg, bb, 0, 0)),
        out_shape=jax.ShapeDtypeStruct((g, b, l, HY_CH), BF16),
        compiler_params=_cparams(("arbitrary", "arbitrary")),
        name="hyena_short_conv",
    )(u, w, bias)


def _hyfwd_kernel(f_ref, z_ref, k_ref, y_ref):
    tf = f_ref.shape[0] // 2
    s = _bdot(f_ref[...], z_ref[0])
    sr, si = s[0:tf], s[tf:2 * tf]
    kr, ki = k_ref[0:tf, :], k_ref[tf:2 * tf, :]
    y_ref[0, 0:tf, :] = (sr * kr - si * ki).astype(BF16)
    y_ref[0, tf:2 * tf, :] = (sr * ki + si * kr).astype(BF16)


def _hy_forward(fmat, z, kspec, order, tf):
    rows, l = fmat.shape
    b = z.shape[0]
    return pl.pallas_call(
        _hyfwd_kernel,
        grid=(rows // (2 * tf), b),
        in_specs=[pl.BlockSpec((2 * tf, l), lambda j, bb: (j, 0)),
                  pl.BlockSpec((1, l, HY_CH), lambda j, bb: (bb, 0, 0)),
                  pl.BlockSpec((2 * tf, HY_CH), lambda j, bb: (j, order))],
        out_specs=pl.BlockSpec((1, 2 * tf, HY_CH), lambda j, bb: (bb, j, 0)),
        out_shape=jax.ShapeDtypeStruct((b, rows, HY_CH), BF16),
        compiler_params=_cparams(("arbitrary", "arbitrary")),
        name="hyena_dft_forward",
    )(fmat, z, kspec)


def _hyinv_kernel(g_ref, y_ref, z_ref, x_ref, b_ref, o_ref):
    conv = _bdot(g_ref[...], y_ref[0])
    z = z_ref[0].astype(F32)
    o_ref[0] = (x_ref[0].astype(F32) * (conv + z * b_ref[...])).astype(BF16)


def _hy_inverse(gmat, y, z, gate, bias, tt):
    l, rows = gmat.shape
    b = y.shape[0]
    return pl.pallas_call(
        _hyinv_kernel,
        grid=(l // tt, b),
        in_specs=[pl.BlockSpec((tt, rows), lambda i, bb: (i, 0)),
                  pl.BlockSpec((1, rows, HY_CH), lambda i, bb: (bb, 0, 0)),
                  pl.BlockSpec((1, tt, HY_CH), lambda i, bb: (bb, i, 0)),
                  pl.BlockSpec((1, tt, HY_CH), lambda i, bb: (bb, i, 0)),
                  pl.BlockSpec((1, HY_CH), lambda i, bb: (0, 0))],
        out_specs=pl.BlockSpec((1, tt, HY_CH), lambda i, bb: (bb, i, 0)),
        out_shape=jax.ShapeDtypeStruct((b, l, HY_CH), BF16),
        compiler_params=_cparams(("arbitrary", "arbitrary")),
        name="hyena_dft_inverse",
    )(gmat, y, z, gate, bias)


def _dft_tables(l, tf):
    k = jnp.arange(l, dtype=jnp.int32)
    n = jnp.arange(l, dtype=jnp.int32)
    ph = ((2 * k + 1)[:, None] * n[None, :]) % (4 * l)
    ang = ph.astype(F32) * (2.0 * math.pi / (4 * l))
    c = jnp.cos(ang).reshape(l // tf, 1, tf, l)
    s = (-jnp.sin(ang)).reshape(l // tf, 1, tf, l)
    f = jnp.concatenate([c, s], axis=1).reshape(2 * l, l)
    return f.astype(BF16), (f.T * (1.0 / l)).astype(BF16)


def _hy_consts(l):
    t = jnp.arange(l, dtype=F32)
    bands = jnp.arange(1, HY_BANDS + 1, dtype=F32)
    ang = (2.0 * math.pi / l) * t[:, None] * bands[None, :]
    emb = jnp.concatenate([(t / l)[:, None], jnp.cos(ang), jnp.sin(ang)], -1)
    emb = jnp.pad(emb, ((0, 0), (0, LANE - HY_EMB)))
    rates = jnp.linspace(HY_MIN_RATE, HY_MAX_RATE, HY_CH, dtype=F32)
    win = jnp.exp(-(t / l)[:, None] * rates[None, :])
    return emb, win


def _hyena(u, consts, conv_w, conv_b, fw1, fb1, fw2, fb2, fw3, fb3, hbias):
    emb, win, fmat, gmat, tf = consts
    sh, sl, dh, dl = _hy_filters(emb, win, fw1, fb1, fw2, fb2, fw3, fb3)
    kspec = _hy_spectrum(fmat, sh, sl, dh, dl, tf)
    vxx = _short_conv(u, conv_w, conv_b)
    v, x1, x2 = vxx[0], vxx[1], vxx[2]
    z = _hy_inverse(gmat, _hy_forward(fmat, v, kspec, 0, tf), v, x1, hbias[0:1], tf)
    return _hy_inverse(gmat, _hy_forward(fmat, z, kspec, 1, tf), z, x2, hbias[1:2], tf)


def _outproj_kernel(om_ref, od_ref, oh_ref, wm_ref, wd_ref, wh_ref, bo_ref, x_ref, g_ref, lg_ref, lb_ref,
                    o_ref, *, alpha):
    y = _bdot(om_ref[0], wm_ref[...]) + _bdot(od_ref[0], wd_ref[...]) + _bdot(oh_ref[0], wh_ref[...])
    z = alpha * x_ref[0] + g_ref[0] * (y + bo_ref[...])
    o_ref[0] = _ln(z) * lg_ref[...] + lb_ref[...]


def _out_proj(om, od, oh, wm, wd, wh, bo, x, gate, lg, lb, alpha, tm):
    b, t, d = x.shape
    per_batch = gate.shape[0] == b and b > 1
    gmap = (lambda bb, i: (bb, 0, 0)) if per_batch else (lambda bb, i: (0, 0, 0))
    const = lambda bb, i: (0, 0)
    row = lambda bb, i: (bb, i, 0)
    return pl.pallas_call(
        functools.partial(_outproj_kernel, alpha=alpha),
        grid=(b, t // tm),
        in_specs=[pl.BlockSpec((1, tm, om.shape[2]), row), pl.BlockSpec((1, tm, od.shape[2]), row),
                  pl.BlockSpec((1, tm, oh.shape[2]), row),
                  pl.BlockSpec(wm.shape, const), pl.BlockSpec(wd.shape, const), pl.BlockSpec(wh.shape, const),
                  pl.BlockSpec((1, d), const), pl.BlockSpec((1, tm, d), row), pl.BlockSpec((1, 1, d), gmap),
                  pl.BlockSpec((1, d), const), pl.BlockSpec((1, d), const)],
        out_specs=pl.BlockSpec((1, tm, d), row),
        out_shape=jax.ShapeDtypeStruct((b, t, d), F32),
        compiler_params=_cparams(("arbitrary", "arbitrary")),
        name="out_proj",
    )(om, od, oh, wm, wd, wh, bo, x, gate, lg, lb)


def _ffn_kernel(x_ref, sh_ref, sc_ref, g_ref, w1_ref, b1_ref, w2_ref, b2_ref, lg_ref, lb_ref, o_ref,
                h_scr, acc_scr, *, alpha):
    j = pl.program_id(2)

    @pl.when(j == 0)
    def _():
        h_scr[...] = (_ln(x_ref[0]) * (1.0 + sc_ref[0]) + sh_ref[0]).astype(BF16)
        acc_scr[...] = jnp.zeros_like(acc_scr)

    a = jnp.maximum(_bdot(h_scr[...], w1_ref[...]) + b1_ref[...], 0.0)
    acc_scr[...] += _bdot((a * a).astype(BF16), w2_ref[...])

    @pl.when(j == pl.num_programs(2) - 1)
    def _():
        z = alpha * x_ref[0] + g_ref[0] * (acc_scr[...] + b2_ref[...])
        o_ref[0] = _ln(z) * lg_ref[...] + lb_ref[...]


def _ffn(x, sh, sc, gate, w1, b1, w2, b2, lg, lb, alpha, tm, tf):
    b, t, d = x.shape
    dff = w1.shape[1]
    per_batch = gate.shape[0] == b and b > 1
    gmap = (lambda bb, i, j: (bb, 0, 0)) if per_batch else (lambda bb, i, j: (0, 0, 0))
    const = lambda bb, i, j: (0, 0)
    row = lambda bb, i, j: (bb, i, 0)
    return pl.pallas_call(
        functools.partial(_ffn_kernel, alpha=alpha),
        grid=(b, t // tm, dff // tf),
        in_specs=[pl.BlockSpec((1, tm, d), row),
                  pl.BlockSpec((1, 1, d), gmap), pl.BlockSpec((1, 1, d), gmap), pl.BlockSpec((1, 1, d), gmap),
                  pl.BlockSpec((d, tf), lambda bb, i, j: (0, j)), pl.BlockSpec((1, tf), lambda bb, i, j: (0, j)),
                  pl.BlockSpec((tf, d), lambda bb, i, j: (j, 0)), pl.BlockSpec((1, d), const),
                  pl.BlockSpec((1, d), const), pl.BlockSpec((1, d), const)],
        out_specs=pl.BlockSpec((1, tm, d), row),
        out_shape=jax.ShapeDtypeStruct((b, t, d), F32),
        scratch_shapes=[pltpu.VMEM((tm, d), BF16), pltpu.VMEM((tm, d), F32)],
        compiler_params=_cparams(("arbitrary", "arbitrary", "arbitrary")),
        name="ffn",
    )(x, sh, sc, gate, w1, b1, w2, b2, lg, lb)


def _rope_tables(t_len, roped):
    if not roped:
        one = jnp.ones((t_len, LANE), F32)
        zero = jnp.zeros((t_len, LANE), F32)
        return jnp.concatenate([one, zero, zero, one, zero, zero], axis=1)
    pos_t = jnp.arange(t_len)
    row = (pos_t // GRID_W).astype(F32)[:, None]
    col = (pos_t % GRID_W).astype(F32)[:, None]

    def tile(starts, dims):
        h = dims // 2
        lane = jnp.arange(LANE)
        c = jnp.ones((t_len, LANE), F32)
        sa = jnp.zeros((t_len, LANE), F32)
        sb = jnp.zeros((t_len, LANE), F32)
        inv = ROPE_BASE ** (-(jnp.arange(h // 2, dtype=F32) * (2.0 / h)))
        for st in starts:
            for axis_i, pos in enumerate((row, col)):
                ang = pos * inv[None, :]
                cos, sin = jnp.cos(ang), jnp.sin(ang)
                lo = st + axis_i * h
                c = c.at[:, lo:lo + h // 2].set(cos).at[:, lo + h // 2:lo + h].set(cos)
                sa = sa.at[:, lo:lo + h // 2].set(-sin)
                sb = sb.at[:, lo + h // 2:lo + h].set(sin)
        del lane
        return [c, sa, sb]

    return jnp.concatenate(tile([MLA_NOPE], MLA_ROPE) + tile([0, LANE // 2], DIFF_DIM), axis=1)


def _prep_weights(w_in, w_q_up, w_kv_up, w_out, diff_lambda, diff_subln):
    depth, d, _ = w_in.shape
    s1, s2 = IN_MLA, IN_MLA + IN_DIFF
    pad_last = lambda a, n: jnp.pad(a, [(0, 0)] * (a.ndim - 1) + [(0, n - a.shape[-1])])
    kr = jnp.pad(w_in[..., C_KR:s1], ((0, 0), (0, 0), (MLA_NOPE, LANE - MLA_NOPE - MLA_ROPE)))
    dq = pad_last(w_in[..., s1:s1 + N_DQ].reshape(depth, d, DIFF_HEADS, 2, DIFF_DIM), LANE // 2)
    dk = pad_last(w_in[..., s1 + N_DQ:s1 + 2 * N_DQ].reshape(depth, d, DIFF_HEADS, 2, DIFF_DIM), LANE // 2)
    dv = pad_last(w_in[..., s1 + 2 * N_DQ:s2].reshape(depth, d, DIFF_HEADS, DIFF_V), LANE)
    win = jnp.concatenate([w_in[..., :C_KR], kr, dq.reshape(depth, d, W_D), dk.reshape(depth, d, W_D),
                           dv.reshape(depth, d, W_D), w_in[..., s2:]], axis=-1).astype(BF16)
    wq = pad_last(w_q_up.reshape(depth, MLA_Q_RANK, MLA_HEADS, MLA_NOPE + MLA_ROPE), LANE)
    wq = wq.reshape(depth, MLA_Q_RANK, W_QM).astype(BF16)
    kv = w_kv_up.reshape(depth, MLA_KV_RANK, MLA_HEADS, MLA_NOPE + MLA_V)
    wk = pad_last(kv[..., :MLA_NOPE], LANE).reshape(depth, MLA_KV_RANK, W_QM)
    wv = kv[..., MLA_NOPE:].reshape(depth, MLA_KV_RANK, W_VM)
    wkv = jnp.concatenate([wk, wv], axis=-1).astype(BF16)
    wo_m = w_out[:, :W_VM].astype(BF16)
    wo_d = w_out[:, W_VM:W_VM + DIFF_HEADS * DIFF_V].reshape(depth, DIFF_HEADS, DIFF_V, -1)
    wo_d = jnp.pad(wo_d, ((0, 0), (0, 0), (0, LANE - DIFF_V), (0, 0))).reshape(depth, W_D, -1).astype(BF16)
    wo_h = w_out[:, W_VM + DIFF_HEADS * DIFF_V:].astype(BF16)
    lam = pad_last(diff_lambda.astype(F32), LANE)
    subln = pad_last(diff_subln.astype(F32), LANE)[:, None, :]
    return win, wq, wkv, wo_m, wo_d, wo_h, lam, subln


def kernel(x, c, ctx, c_ctx, w_mod, b_mod, w_in, mla_q_norm, w_q_up, mla_kv_norm, w_kv_up, diff_lambda, diff_subln, hy_conv_w, hy_conv_b, hy_fw1, hy_fb1, hy_fw2, hy_fb2, hy_fw3, hy_fb3, hy_bias, w_out, b_out, ln1_g, ln1_b, w_ff1, b_ff1, w_ff2, b_ff2, ln2_g, ln2_b):
    bsz, seq, d = x.shape
    n_ctx = ctx.shape[1]
    depth = w_in.shape[0]
    alpha = (2.0 * depth) ** 0.25

    rpad = -(bsz + 1) % 8
    cc = jnp.concatenate([c, c_ctx[None, :], jnp.zeros((rpad, d), F32)], axis=0)
    mod_all = _modulation(cc, w_mod, b_mod)

    win, wq, wkv, wo_m, wo_d, wo_h, lam_p, subln = _prep_weights(w_in, w_q_up, w_kv_up, w_out, diff_lambda, diff_subln)
    w1 = w_ff1.astype(BF16)
    w2 = w_ff2.astype(BF16)
    fw1 = jnp.pad(hy_fw1, ((0, 0), (0, LANE - HY_EMB), (0, 0)))
    tab_x = _rope_tables(seq, True)
    tab_c = _rope_tables(n_ctx, False)
    tf_x, tf_c = min(512, seq), min(512, n_ctx)
    hyc_x = _hy_consts(seq) + _dft_tables(seq, tf_x) + (tf_x,)
    hyc_c = _hy_consts(n_ctx) + _dft_tables(n_ctx, tf_c) + (tf_c,)
    tm_x, tm_c = min(512, seq), min(512, n_ctx)
    tq_x, tq_c = min(256, seq), min(256, n_ctx)
    tff = min(1024, w1.shape[2])
    row2 = lambda a: a.reshape(1, -1)

    for layer in range(depth):
        need_ctx = layer < depth - 1
        lam_init = 0.8 - 0.6 * math.exp(-0.3 * layer)
        mod = mod_all[layer, :bsz].reshape(bsz, 1, N_MOD, d)
        modc = mod_all[layer, bsz:bsz + 1].reshape(1, 1, N_MOD, d)
        sh1, sc1, g1, sh2, sc2, g2 = [mod[:, :, i] for i in range(N_MOD)]
        csh1, csc1, cg1, csh2, csc2, cg2 = [modc[:, :, i] for i in range(N_MOD)]
        gq, gkv = row2(mla_q_norm[layer]), row2(mla_kv_norm[layer])
        hy_args = (hy_conv_w[layer], row2(hy_conv_b[layer]), fw1[layer], row2(hy_fb1[layer]), hy_fw2[layer],
                   row2(hy_fb2[layer]), hy_fw3[layer], row2(hy_fb3[layer]), hy_bias[layer])

        qm, km, vm, dq, dk, dv, uh = _in_proj(x, sh1, sc1, win[layer], gq, wq[layer], gkv, wkv[layer], tab_x, tm_x)
        cqm, ckm, cvm, cdq, cdk, cdv, cuh = _in_proj(ctx, csh1, csc1, win[layer], gq, wq[layer], gkv, wkv[layer],
                                                     tab_c, tm_c)
        om = _mla_attention(qm, [ckm, km], [cvm, vm], tq_x)
        od = _diff_attention(dq, [cdk, dk], [cdv, dv], lam_p[layer], subln[layer], lam_init, tq_x)
        oh = _hyena(uh, hyc_x, *hy_args)
        x = _out_proj(om, od, oh, wo_m[layer], wo_d[layer], wo_h[layer], row2(b_out[layer]), x, g1,
                      row2(ln1_g[layer]), row2(ln1_b[layer]), alpha, tm_x)
        x = _ffn(x, sh2, sc2, g2, w1[layer], row2(b_ff1[layer]), w2[layer], row2(b_ff2[layer]),
                 row2(ln2_g[layer]), row2(ln2_b[layer]), alpha, tm_x, tff)
        if need_ctx:
            com = _mla_attention(cqm, [ckm], [cvm], tq_c)
            cod = _diff_attention(cdq, [cdk], [cdv], lam_p[layer], subln[layer], lam_init, tq_c)
            coh = _hyena(cuh, hyc_c, *hy_args)
            ctx = _out_proj(com, cod, coh, wo_m[layer], wo_d[layer], wo_h[layer], row2(b_out[layer]), ctx, cg1,
                            row2(ln1_g[layer]), row2(ln1_b[layer]), alpha, tm_c)
            ctx = _ffn(ctx, csh2, csc2, cg2, w1[layer], row2(b_ff1[layer]), w2[layer], row2(b_ff2[layer]),
                       row2(ln2_g[layer]), row2(ln2_b[layer]), alpha, tm_c, tff)
    return x
```

```python
import functools
import math

import jax
import jax.numpy as jnp
from jax import lax
from jax.experimental import pallas as pl
from jax.experimental.pallas import tpu as pltpu

F32 = jnp.float32
BF16 = jnp.bfloat16
HI = lax.Precision.HIGHEST

GRID_W = 64
ROPE_BASE = 10000.0
LN_EPS = 1e-6
MLA_HEADS, MLA_NOPE, MLA_ROPE, MLA_V = 6, 64, 32, 64
MLA_Q_RANK, MLA_KV_RANK = 256, 128
DIFF_HEADS, DIFF_DIM = 4, 48
DIFF_V = 2 * DIFF_DIM
HY_CH, HY_ORDER, HY_BANDS = 256, 2, 16
HY_EMB = 1 + 2 * HY_BANDS
HY_TARGET, HY_FAST_DECAY_PCT, HY_SLOW_DECAY_PCT = 1e-2, 0.3, 1.5
HY_MIN_RATE = -math.log(HY_TARGET) / HY_SLOW_DECAY_PCT
HY_MAX_RATE = -math.log(HY_TARGET) / HY_FAST_DECAY_PCT
N_MOD = 6

LOG2E = 1.0 / math.log(2.0)
LANE = 128
VMEM_LIMIT = 56 * 1024 * 1024

IN_MLA = MLA_Q_RANK + MLA_KV_RANK + MLA_ROPE
N_DQ = DIFF_HEADS * 2 * DIFF_DIM
IN_DIFF = 2 * N_DQ + DIFF_HEADS * DIFF_V
W_QM = MLA_HEADS * LANE
W_VM = MLA_HEADS * MLA_V
W_D = DIFF_HEADS * LANE
W_HY = (HY_ORDER + 1) * HY_CH
C_CQ, C_CKV, C_KR = 0, MLA_Q_RANK, MLA_Q_RANK + MLA_KV_RANK
C_DQ = C_KR + LANE
C_DK = C_DQ + W_D
C_DV = C_DK + W_D
C_HY = C_DV + W_D
W_IN = C_HY + W_HY


def _cparams(sem):
    return pltpu.CompilerParams(dimension_semantics=sem, vmem_limit_bytes=VMEM_LIMIT)


def _ln(x):
    mu = jnp.mean(x, -1, keepdims=True)
    xc = x - mu
    var = jnp.mean(xc * xc, -1, keepdims=True)
    return xc * lax.rsqrt(var + LN_EPS)


def _bdot(a, b):
    return jnp.dot(a, b, preferred_element_type=F32)


def _mod_kernel(c_ref, w_ref, b_ref, o_ref):
    c = c_ref[...]
    s = c / (1.0 + jnp.exp(-c))
    o_ref[...] = jnp.dot(s, w_ref[...], preferred_element_type=F32, precision=HI) + b_ref[...]


def _modulation(cc, w_mod, b_mod):
    depth, d, n = w_mod.shape
    r = cc.shape[0]
    tn = 1024
    return pl.pallas_call(
        _mod_kernel,
        grid=(depth, n // tn),
        in_specs=[pl.BlockSpec((r, d), lambda l, j: (0, 0)),
                  pl.BlockSpec((None, d, tn), lambda l, j: (l, 0, j)),
                  pl.BlockSpec((None, 1, tn), lambda l, j: (l, 0, j))],
        out_specs=pl.BlockSpec((None, r, tn), lambda l, j: (l, 0, j)),
        out_shape=jax.ShapeDtypeStruct((depth, r, n), F32),
        compiler_params=_cparams(("arbitrary", "arbitrary")),
        name="modulation",
    )(cc, w_mod, b_mod.reshape(depth, 1, n))


def _rope(x, c, sa, sb, half):
    return x * c + pltpu.roll(x, LANE - half, 1) * sa + pltpu.roll(x, half, 1) * sb


def _inproj_kernel(x_ref, sh_ref, sc_ref, win_ref, gq_ref, wq_ref, gkv_ref, wkv_ref, tab_ref,
                   qm_ref, km_ref, vm_ref, dq_ref, dk_ref, dv_ref, hy_ref):
    x = x_ref[0]
    h = _ln(x) * (1.0 + sc_ref[0]) + sh_ref[0]
    p = _bdot(h.astype(BF16), win_ref[...])
    tab = tab_ref[...]
    cm, sam, sbm = tab[:, 0:LANE], tab[:, LANE:2 * LANE], tab[:, 2 * LANE:3 * LANE]
    cd, sad, sbd = tab[:, 3 * LANE:4 * LANE], tab[:, 4 * LANE:5 * LANE], tab[:, 5 * LANE:6 * LANE]

    cq = p[:, C_CQ:C_CQ + MLA_Q_RANK]
    qn = cq * lax.rsqrt(jnp.mean(cq * cq, -1, keepdims=True) + LN_EPS) * gq_ref[...]
    q = _bdot(qn.astype(BF16), wq_ref[...])
    ckv = p[:, C_CKV:C_CKV + MLA_KV_RANK]
    kvn = ckv * lax.rsqrt(jnp.mean(ckv * ckv, -1, keepdims=True) + LN_EPS) * gkv_ref[...]
    kv = _bdot(kvn.astype(BF16), wkv_ref[...])
    kr = _rope(p[:, C_KR:C_KR + LANE], cm, sam, sbm, MLA_ROPE // 4)
    scale_m = LOG2E * (MLA_NOPE + MLA_ROPE) ** -0.5
    for hd in range(MLA_HEADS):
        sl = slice(hd * LANE, (hd + 1) * LANE)
        qm_ref[0, :, sl] = (_rope(q[:, sl], cm, sam, sbm, MLA_ROPE // 4) * scale_m).astype(BF16)
        km_ref[0, :, sl] = (kv[:, sl] + kr).astype(BF16)
    vm_ref[0] = kv[:, W_QM:W_QM + W_VM].astype(BF16)
    scale_d = LOG2E * DIFF_DIM ** -0.5
    for hd in range(DIFF_HEADS):
        sl = slice(hd * LANE, (hd + 1) * LANE)
        dq = p[:, C_DQ + hd * LANE:C_DQ + (hd + 1) * LANE]
        dk = p[:, C_DK + hd * LANE:C_DK + (hd + 1) * LANE]
        dq_ref[0, :, sl] = (_rope(dq, cd, sad, sbd, DIFF_DIM // 4) * scale_d).astype(BF16)
        dk_ref[0, :, sl] = _rope(dk, cd, sad, sbd, DIFF_DIM // 4).astype(BF16)
    dv_ref[0] = p[:, C_DV:C_DV + W_D].astype(BF16)
    hy_ref[0] = p[:, C_HY:C_HY + W_HY].astype(BF16)


def _in_proj(x, sh, sc, win, gq, wq, gkv, wkv, tab, tm):
    b, t, d = x.shape
    per_batch = sh.shape[0] == b and b > 1
    mod_map = (lambda i, bb: (bb, 0, 0)) if per_batch else (lambda i, bb: (0, 0, 0))
    const = lambda i, bb: (0, 0)
    widths = (W_QM, W_QM, W_VM, W_D, W_D, W_D, W_HY)
    return pl.pallas_call(
        _inproj_kernel,
        grid=(t // tm, b),
        in_specs=[pl.BlockSpec((1, tm, d), lambda i, bb: (bb, i, 0)),
                  pl.BlockSpec((1, 1, d), mod_map),
                  pl.BlockSpec((1, 1, d), mod_map),
                  pl.BlockSpec(win.shape, const),
                  pl.BlockSpec(gq.shape, const),
                  pl.BlockSpec(wq.shape, const),
                  pl.BlockSpec(gkv.shape, const),
                  pl.BlockSpec(wkv.shape, const),
                  pl.BlockSpec((tm, 6 * LANE), lambda i, bb: (i, 0))],
        out_specs=[pl.BlockSpec((1, tm, w), lambda i, bb: (bb, i, 0)) for w in widths],
        out_shape=[jax.ShapeDtypeStruct((b, t, w), BF16) for w in widths],
        compiler_params=_cparams(("arbitrary", "arbitrary")),
        name="in_proj",
    )(x, sh, sc, win, gq, wq, gkv, wkv, tab)


NEG_BIG = -1e30


def _lane_fold(x, op):
    r = x[:, 0:LANE]
    for i in range(1, x.shape[1] // LANE):
        r = op(r, x[:, i * LANE:(i + 1) * LANE])
    return r


def _score_pass(q, parts, ksl, s_refs):
    mx = jnp.full((q.shape[0], LANE), NEG_BIG, F32)
    for (k_ref, _, n, tk), s_ref in zip(parts, s_refs):
        for j in range(n // tk):
            k = k_ref[0, j * tk:(j + 1) * tk, ksl]
            s = lax.dot_general(q, k, (((1,), (1,)), ((), ())), preferred_element_type=F32)
            s_ref[j] = s
            mx = jnp.maximum(mx, _lane_fold(s, jnp.maximum))
    return jnp.max(mx, axis=-1, keepdims=True)


def _softmax_pv(q, parts, ksl, vsl, s_refs):
    m = _score_pass(q, parts, ksl, s_refs)
    ls = jnp.zeros((q.shape[0], LANE), F32)
    acc = jnp.zeros((q.shape[0], LANE), F32)
    for (_, v_ref, n, tk), s_ref in zip(parts, s_refs):
        for j in range(n // tk):
            p = jnp.exp2(s_ref[j] - m)
            ls = ls + _lane_fold(p, jnp.add)
            acc = acc + _bdot(p.astype(BF16), v_ref[0, j * tk:(j + 1) * tk, vsl])
    return acc * (1.0 / jnp.sum(ls, axis=-1, keepdims=True))


def _mla_attn_kernel(*refs, part_shapes, n_pairs):
    np_ = len(part_shapes)
    q_ref = refs[0]
    k_refs = refs[1:1 + np_]
    v_refs = refs[1 + np_:1 + 2 * np_]
    o_ref = refs[1 + 2 * np_]
    s_sets = (refs[2 + 2 * np_:2 + 3 * np_], refs[2 + 3 * np_:])
    parts = [(k_refs[i], v_refs[i]) + part_shapes[i] for i in range(np_)]
    for pr in range(n_pairs):
        outs = []
        for hh in range(2):
            hd = 2 * pr + hh
            sl = slice(hd * LANE, (hd + 1) * LANE)
            outs.append(_softmax_pv(q_ref[0, :, sl], parts, sl, slice(pr * LANE, (pr + 1) * LANE), s_sets[hh]))
        lane = lax.broadcasted_iota(jnp.int32, outs[0].shape, 1)
        o_ref[0, :, pr * LANE:(pr + 1) * LANE] = jnp.where(lane < MLA_V, outs[0], outs[1]).astype(BF16)


def _chunk(n):
    return n if n <= 512 else 512


def _mla_attention(q, ks, vs, tq, n_pairs):
    b, t, _ = q.shape
    part_shapes = tuple((k.shape[1], _chunk(k.shape[1])) for k in ks)
    in_specs = [pl.BlockSpec((1, tq, 2 * n_pairs * LANE), lambda bb, g, i: (bb, i, g))]
    in_specs += [pl.BlockSpec((1, k.shape[1], 2 * n_pairs * LANE), lambda bb, g, i: (bb, 0, g)) for k in ks]
    in_specs += [pl.BlockSpec((1, v.shape[1], n_pairs * LANE), lambda bb, g, i: (bb, 0, g)) for v in vs]
    scratch = [pltpu.VMEM((n // tk, tq, tk), F32) for n, tk in part_shapes] * 2
    return pl.pallas_call(
        functools.partial(_mla_attn_kernel, part_shapes=part_shapes, n_pairs=n_pairs),
        grid=(b, MLA_HEADS // (2 * n_pairs), t // tq),
        in_specs=in_specs,
        out_specs=pl.BlockSpec((1, tq, n_pairs * LANE), lambda bb, g, i: (bb, i, g)),
        out_shape=jax.ShapeDtypeStruct((b, t, W_VM), BF16),
        scratch_shapes=scratch,
        compiler_params=_cparams(("arbitrary", "arbitrary", "arbitrary")),
        name="mla_attention",
    )(q, *ks, *vs)


def _diff_attn_kernel(*refs, part_shapes, lam_init, n_heads):
    np_ = len(part_shapes)
    q_ref = refs[0]
    k_refs = refs[1:1 + np_]
    v_refs = refs[1 + np_:1 + 2 * np_]
    lam_ref, g_ref, o_ref = refs[1 + 2 * np_:4 + 2 * np_]
    s_sets = (refs[4 + 2 * np_:4 + 3 * np_], refs[4 + 3 * np_:])
    parts = [(k_refs[i], v_refs[i]) + part_shapes[i] for i in range(np_)]
    lp = lam_ref[...]
    lam = (jnp.exp(jnp.sum(lp[0:1] * lp[1:2], axis=-1, keepdims=True))
           - jnp.exp(jnp.sum(lp[2:3] * lp[3:4], axis=-1, keepdims=True)) + lam_init)
    for hd in range(n_heads):
        sl = slice(hd * LANE, (hd + 1) * LANE)
        q = q_ref[0, :, sl]
        lane = lax.broadcasted_iota(jnp.int32, q.shape, 1)
        zero = jnp.zeros_like(q)
        o1 = _softmax_pv(jnp.where(lane < LANE // 2, q, zero), parts, sl, sl, s_sets[0])
        o2 = _softmax_pv(jnp.where(lane >= LANE // 2, q, zero), parts, sl, sl, s_sets[1])
        o = o1 - lam * o2
        ms = jnp.sum(o * o, axis=-1, keepdims=True) * (1.0 / DIFF_V)
        o_ref[0, :, sl] = (o * lax.rsqrt(ms + LN_EPS) * g_ref[...] * (1.0 - lam_init)).astype(BF16)


def _diff_attention(q, ks, vs, lam_p, subln, lam_init, tq, n_heads):
    b, t, _ = q.shape
    part_shapes = tuple((k.shape[1], _chunk(k.shape[1])) for k in ks)
    hmap = lambda bb, g, i: (bb, 0, g)
    in_specs = [pl.BlockSpec((1, tq, n_heads * LANE), lambda bb, g, i: (bb, i, g))]
    in_specs += [pl.BlockSpec((1, k.shape[1], n_heads * LANE), hmap) for k in ks]
    in_specs += [pl.BlockSpec((1, v.shape[1], n_heads * LANE), hmap) for v in vs]
    in_specs += [pl.BlockSpec(lam_p.shape, lambda bb, g, i: (0, 0)),
                 pl.BlockSpec(subln.shape, lambda bb, g, i: (0, 0))]
    scratch = [pltpu.VMEM((n // tk, tq, tk), F32) for n, tk in part_shapes] * 2
    return pl.pallas_call(
        functools.partial(_diff_attn_kernel, part_shapes=part_shapes, lam_init=lam_init, n_heads=n_heads),
        grid=(b, DIFF_HEADS // n_heads, t // tq),
        in_specs=in_specs,
        out_specs=pl.BlockSpec((1, tq, n_heads * LANE), lambda bb, g, i: (bb, i, g)),
        out_shape=jax.ShapeDtypeStruct((b, t, W_D), BF16),
        scratch_shapes=scratch,
        compiler_params=_cparams(("arbitrary", "arbitrary", "arbitrary")),
        name="diff_attention",
    )(q, *ks, *vs, lam_p, subln)


def _hyfilt_kernel(emb_ref, win_ref, fw1_ref, fb1_ref, fw2_ref, fb2_ref, fwf_ref, fbf_ref, fwb_ref, fbb_ref,
                   sh_ref, sl_ref, dh_ref, dl_ref):
    h = jnp.sin(jnp.dot(emb_ref[...], fw1_ref[...], preferred_element_type=F32, precision=HI) + fb1_ref[...])
    h = jnp.sin(jnp.dot(h, fw2_ref[...], preferred_element_type=F32, precision=HI) + fb2_ref[...])
    w = win_ref[...]
    fwd = (jnp.dot(h, fwf_ref[...], preferred_element_type=F32, precision=HI) + fbf_ref[...]) * w
    bwd = (jnp.dot(h, fwb_ref[...], preferred_element_type=F32, precision=HI) + fbb_ref[...]) * w
    row = lax.broadcasted_iota(jnp.int32, bwd.shape, 0)
    bwd = jnp.where(row == 0, 0.0, bwd)
    norm = jnp.sum(jnp.abs(fwd), axis=0, keepdims=True) + jnp.sum(jnp.abs(bwd), axis=0, keepdims=True)
    inv = 1.0 / norm
    dsum = (fwd + bwd) * inv
    ddif = (fwd - bwd) * inv
    sh = dsum.astype(BF16)
    dh = ddif.astype(BF16)
    sh_ref[...] = sh
    sl_ref[...] = (dsum - sh.astype(F32)).astype(BF16)
    dh_ref[...] = dh
    dl_ref[...] = (ddif - dh.astype(F32)).astype(BF16)


def _hy_filters(emb, win, fw1, fb1, fw2, fb2, fw3, fb3):
    l = emb.shape[0]
    hid = fw2.shape[0]
    nblk = HY_CH // LANE
    ncol = HY_ORDER * HY_CH
    const = lambda g: (0, 0)
    fcol = lambda g: (0, (g // nblk) * 2 * nblk + g % nblk)
    bcol = lambda g: (0, (g // nblk) * 2 * nblk + nblk + g % nblk)
    out = pl.BlockSpec((l, LANE), lambda g: (0, g))
    return pl.pallas_call(
        _hyfilt_kernel,
        grid=(HY_ORDER * nblk,),
        in_specs=[pl.BlockSpec(emb.shape, const),
                  pl.BlockSpec((l, LANE), lambda g: (0, g % nblk)),
                  pl.BlockSpec(fw1.shape, const), pl.BlockSpec(fb1.shape, const),
                  pl.BlockSpec(fw2.shape, const), pl.BlockSpec(fb2.shape, const),
                  pl.BlockSpec((hid, LANE), fcol), pl.BlockSpec((1, LANE), fcol),
                  pl.BlockSpec((hid, LANE), bcol), pl.BlockSpec((1, LANE), bcol)],
        out_specs=[out, out, out, out],
        out_shape=[jax.ShapeDtypeStruct((l, ncol), BF16)] * 4,
        compiler_params=_cparams(("arbitrary",)),
        name="hyena_filters",
    )(emb, win, fw1, fb1, fw2, fb2, fw3, fb3, fw3, fb3)


def _hyspec_kernel(f_ref, sh_ref, sl_ref, dh_ref, dl_ref, o_ref):
    tf = f_ref.shape[0] // 2
    fc = f_ref[0:tf, :]
    fs = f_ref[tf:2 * tf, :]
    o_ref[0:tf, :] = _bdot(fc, sh_ref[...]) + _bdot(fc, sl_ref[...])
    o_ref[tf:2 * tf, :] = _bdot(fs, dh_ref[...]) + _bdot(fs, dl_ref[...])


def _hy_spectrum(fmat, sh, sl, dh, dl, tf):
    rows, l = fmat.shape
    ncol = sh.shape[1]
    data = pl.BlockSpec((l, HY_CH), lambda j, o: (0, o))
    return pl.pallas_call(
        _hyspec_kernel,
        grid=(rows // (2 * tf), ncol // HY_CH),
        in_specs=[pl.BlockSpec((2 * tf, l), lambda j, o: (j, 0))] + [data] * 4,
        out_specs=pl.BlockSpec((2 * tf, HY_CH), lambda j, o: (j, o)),
        out_shape=jax.ShapeDtypeStruct((rows, ncol), F32),
        compiler_params=_cparams(("arbitrary", "arbitrary")),
        name="hyena_filter_spectrum",
    )(fmat, sh, sl, dh, dl)


def _shortconv_kernel(u_ref, w_ref, b_ref, o_ref):
    u = u_ref[0].astype(F32)
    l = u.shape[0]
    row = lax.broadcasted_iota(jnp.int32, u.shape, 0)
    up = jnp.where(row == 0, 0.0, pltpu.roll(u, 1, 0))
    un = jnp.where(row == l - 1, 0.0, pltpu.roll(u, l - 1, 0))
    w = w_ref[...]
    o_ref[0] = (up * w[0:1] + u * w[1:2] + un * w[2:3] + b_ref[...]).astype(BF16)


def _short_conv(u, w, bias):
    b, l, _ = u.shape
    g = HY_ORDER + 1
    return pl.pallas_call(
        _shortconv_kernel,
        grid=(b, g),
        in_specs=[pl.BlockSpec((1, l, HY_CH), lambda bb, gg: (bb, 0, gg)),
                  pl.BlockSpec((3, HY_CH), lambda bb, gg: (0, gg)),
                  pl.BlockSpec((1, HY_CH), lambda bb, gg: (0, gg))],
        out_specs=pl.BlockSpec((None, 1, l, HY_CH), lambda bb, gg: (gg, bb, 0, 0)),
        out_shape=jax.ShapeDtypeStruct((g, b, l, HY_CH), BF16),
        compiler_params=_cparams(("arbitrary", "arbitrary")),
        name="hyena_short_conv",
    )(u, w, bias)


def _hyfwd_kernel(f_ref, z_ref, k_ref, y_ref):
    tf = f_ref.shape[0] // 2
    s = _bdot(f_ref[...], z_ref[0])
    sr, si = s[0:tf], s[tf:2 * tf]
    kr, ki = k_ref[0:tf, :], k_ref[tf:2 * tf, :]
    y_ref[0, 0:tf, :] = (sr * kr - si * ki).astype(BF16)
    y_ref[0, tf:2 * tf, :] = (sr * ki + si * kr).astype(BF16)


def _hy_forward(fmat, z, kspec, order, tf):
    rows, l = fmat.shape
    b = z.shape[0]
    return pl.pallas_call(
        _hyfwd_kernel,
        grid=(rows // (2 * tf), b),
        in_specs=[pl.BlockSpec((2 * tf, l), lambda j, bb: (j, 0)),
                  pl.BlockSpec((1, l, HY_CH), lambda j, bb: (bb, 0, 0)),
                  pl.BlockSpec((2 * tf, HY_CH), lambda j, bb: (j, order))],
        out_specs=pl.BlockSpec((1, 2 * tf, HY_CH), lambda j, bb: (bb, j, 0)),
        out_shape=jax.ShapeDtypeStruct((b, rows, HY_CH), BF16),
        compiler_params=_cparams(("arbitrary", "arbitrary")),
        name="hyena_dft_forward",
    )(fmat, z, kspec)


def _hyinv_kernel(g_ref, y_ref, z_ref, x_ref, b_ref, o_ref):
    conv = _bdot(g_ref[...], y_ref[0])
    z = z_ref[0].astype(F32)
    o_ref[0] = (x_ref[0].astype(F32) * (conv + z * b_ref[...])).astype(BF16)


def _hy_inverse(gmat, y, z, gate, bias, tt):
    l, rows = gmat.shape
    b = y.shape[0]
    return pl.pallas_call(
        _hyinv_kernel,
        grid=(l // tt, b),
        in_specs=[pl.BlockSpec((tt, rows), lambda i, bb: (i, 0)),
                  pl.BlockSpec((1, rows, HY_CH), lambda i, bb: (bb, 0, 0)),
                  pl.BlockSpec((1, tt, HY_CH), lambda i, bb: (bb, i, 0)),
                  pl.BlockSpec((1, tt, HY_CH), lambda i, bb: (bb, i, 0)),
                  pl.BlockSpec((1, HY_CH), lambda i, bb: (0, 0))],
        out_specs=pl.BlockSpec((1, tt, HY_CH), lambda i, bb: (bb, i, 0)),
        out_shape=jax.ShapeDtypeStruct((b, l, HY_CH), BF16),
        compiler_params=_cparams(("arbitrary", "arbitrary")),
        name="hyena_dft_inverse",
    )(gmat, y, z, gate, bias)


def _dft_tables(l, tf):
    k = jnp.arange(l, dtype=jnp.int32)
    n = jnp.arange(l, dtype=jnp.int32)
    ph = ((2 * k + 1)[:, None] * n[None, :]) % (4 * l)
    ang = ph.astype(F32) * (2.0 * math.pi / (4 * l))
    c = jnp.cos(ang).reshape(l // tf, 1, tf, l)
    s = (-jnp.sin(ang)).reshape(l // tf, 1, tf, l)
    f = jnp.concatenate([c, s], axis=1).reshape(2 * l, l)
    return f.astype(BF16), (f.T * (1.0 / l)).astype(BF16)


def _hy_consts(l):
    t = jnp.arange(l, dtype=F32)
    bands = jnp.arange(1, HY_BANDS + 1, dtype=F32)
    ang = (2.0 * math.pi / l) * t[:, None] * bands[None, :]
    emb = jnp.concatenate([(t / l)[:, None], jnp.cos(ang), jnp.sin(ang)], -1)
    emb = jnp.pad(emb, ((0, 0), (0, LANE - HY_EMB)))
    rates = jnp.linspace(HY_MIN_RATE, HY_MAX_RATE, HY_CH, dtype=F32)
    win = jnp.exp(-(t / l)[:, None] * rates[None, :])
    return emb, win


HY_N1 = 16
HY_J = 16
HY_GROUPS = 4
HY_KG = 4


def _hy2_tables(l):
    n1h, jj = HY_N1, HY_J
    n2, k1n = l // n1h, 2 * n1h
    k2n = n2 // 2
    k1 = jnp.arange(k1n, dtype=jnp.int32)
    ph1 = ((2 * k1 + 1)[:, None] * jnp.arange(n1h, dtype=jnp.int32)[None, :]) % (4 * n1h)
    a1 = ph1.astype(F32) * (2.0 * math.pi / (4 * n1h))
    f1 = jnp.stack([jnp.cos(a1), -jnp.sin(a1)], axis=1)
    f1big = jnp.einsum('krn,ab->kranb', f1, jnp.eye(jj, dtype=F32)).reshape(k1n * 2 * jj, n1h * jj)
    kk = k1[:, None] + k1n * jnp.arange(k2n, dtype=jnp.int32)[None, :]
    ph2 = ((2 * kk + 1)[:, :, None] * jnp.arange(n2, dtype=jnp.int32)[None, None, :]) % (4 * l)
    a2 = ph2.astype(F32) * (2.0 * math.pi / (4 * l))
    mr, mi = jnp.cos(a2), -jnp.sin(a2)
    f2t = jnp.concatenate([jnp.concatenate([mr, -mi], axis=2), jnp.concatenate([mi, mr], axis=2)], axis=1)
    ph = ((2 * kk + 1)[:, :, None] * jnp.arange(l, dtype=jnp.int32)[None, None, :]) % (4 * l)
    a = ph.astype(F32) * (2.0 * math.pi / (4 * l))
    fperm = jnp.stack([jnp.cos(a), -jnp.sin(a)], axis=1).reshape(2 * l, l)
    return (f1big.astype(BF16), (f1big.T * (1.0 / l)).astype(BF16), f2t.astype(BF16),
            jnp.swapaxes(f2t, 1, 2).astype(BF16), fperm.astype(BF16))


def _hy2_s1_kernel(f_ref, z_ref, o_ref):
    n1h, _, c = z_ref.shape[1:]
    k1n = o_ref.shape[1]
    for g in range(HY_GROUPS):
        cols = slice(g * HY_J, (g + 1) * HY_J)
        z = z_ref[0, :, cols, :].reshape(n1h * HY_J, c)
        a = _bdot(f_ref[...], z)
        o_ref[0, :, :, cols, :] = a.astype(BF16).reshape(k1n, 2, HY_J, c)


def _hy2_stage1(f1big, vxx, which, z5):
    src = vxx if z5 is None else z5
    b, n1h, n2, c = src.shape[-4:]
    k1n = 2 * n1h
    w = HY_J * HY_GROUPS
    if z5 is None:
        zspec = pl.BlockSpec((None, 1, n1h, w, c), lambda bb, g: (which, bb, 0, g, 0))
    else:
        zspec = pl.BlockSpec((1, n1h, w, c), lambda bb, g: (bb, 0, g, 0))
    return pl.pallas_call(
        _hy2_s1_kernel,
        grid=(b, n2 // w),
        in_specs=[pl.BlockSpec(f1big.shape, lambda bb, g: (0, 0)), zspec],
        out_specs=pl.BlockSpec((1, k1n, 2, w, c), lambda bb, g: (bb, 0, 0, g, 0)),
        out_shape=jax.ShapeDtypeStruct((b, k1n, 2, n2, c), BF16),
        compiler_params=_cparams(("arbitrary", "arbitrary")),
        name="hyena_stage1",
    )(f1big, src)


def _hy2_s2_kernel(f_ref, g_ref, a_ref, k_ref, o_ref):
    for i in range(HY_KG):
        y = _bdot(f_ref[i], a_ref[0, i])
        h = y.shape[0] // 2
        yr, yi = y[:h], y[h:]
        kr, ki = k_ref[i, 0:h, :], k_ref[i, h:2 * h, :]
        p = jnp.concatenate([yr * kr - yi * ki, yr * ki + yi * kr], axis=0).astype(BF16)
        o_ref[0, i] = _bdot(g_ref[i], p).astype(BF16)


def _hy2_stage2(f2t, g2t, a5, kspec, order):
    b, k1n, _, n2, c = a5.shape
    a4 = a5.reshape(b, k1n, 2 * n2, c)
    k3 = kspec.reshape(k1n, n2, kspec.shape[1])
    out = pl.pallas_call(
        _hy2_s2_kernel,
        grid=(k1n // HY_KG, b),
        in_specs=[pl.BlockSpec((HY_KG, n2, 2 * n2), lambda g, bb: (g, 0, 0)),
                  pl.BlockSpec((HY_KG, 2 * n2, n2), lambda g, bb: (g, 0, 0)),
                  pl.BlockSpec((1, HY_KG, 2 * n2, c), lambda g, bb: (bb, g, 0, 0)),
                  pl.BlockSpec((HY_KG, n2, c), lambda g, bb: (g, 0, order))],
        out_specs=pl.BlockSpec((1, HY_KG, 2 * n2, c), lambda g, bb: (bb, g, 0, 0)),
        out_shape=jax.ShapeDtypeStruct(a4.shape, BF16),
        compiler_params=_cparams(("arbitrary", "arbitrary")),
        name="hyena_stage2",
    )(f2t, g2t, a4, k3)
    return out.reshape(a5.shape)


def _hy2_s3_kernel(g_ref, c_ref, z_ref, x_ref, b_ref, o_ref):
    k1n = c_ref.shape[1]
    n1h, _, c = z_ref.shape[1:]
    for g in range(HY_GROUPS):
        cols = slice(g * HY_J, (g + 1) * HY_J)
        cc = c_ref[0, :, :, cols, :].reshape(k1n * 2 * HY_J, c)
        conv = _bdot(g_ref[...], cc)
        z = z_ref[0, :, cols, :].reshape(n1h * HY_J, c).astype(F32)
        x = x_ref[0, :, cols, :].reshape(n1h * HY_J, c).astype(F32)
        o_ref[0, :, cols, :] = (x * (conv + z * b_ref[...])).astype(BF16).reshape(n1h, HY_J, c)


def _hy2_stage3(g1big, c5, vxx, zwhich, z5, xwhich, bias):
    b, k1n, _, n2, c = c5.shape
    n1h = k1n // 2
    w = HY_J * HY_GROUPS
    sel = lambda which: pl.BlockSpec((None, 1, n1h, w, c), lambda bb, g: (which, bb, 0, g, 0))
    row = pl.BlockSpec((1, n1h, w, c), lambda bb, g: (bb, 0, g, 0))
    return pl.pallas_call(
        _hy2_s3_kernel,
        grid=(b, n2 // w),
        in_specs=[pl.BlockSpec(g1big.shape, lambda bb, g: (0, 0)),
                  pl.BlockSpec((1, k1n, 2, w, c), lambda bb, g: (bb, 0, 0, g, 0)),
                  sel(zwhich) if z5 is None else row, sel(xwhich),
                  pl.BlockSpec((1, c), lambda bb, g: (0, 0))],
        out_specs=row,
        out_shape=jax.ShapeDtypeStruct((b, n1h, n2, c), BF16),
        compiler_params=_cparams(("arbitrary", "arbitrary")),
        name="hyena_stage3",
    )(g1big, c5, vxx if z5 is None else z5, vxx, bias)


def _hyena(u, consts, conv_w, conv_b, fw1, fb1, fw2, fb2, fw3, fb3, hbias):
    emb, win, tabs = consts
    sh, sl, dh, dl = _hy_filters(emb, win, fw1, fb1, fw2, fb2, fw3, fb3)
    vxx = _short_conv(u, conv_w, conv_b)
    if len(tabs) == 3:
        fmat, gmat, tf = tabs
        kspec = _hy_spectrum(fmat, sh, sl, dh, dl, tf)
        v, x1, x2 = vxx[0], vxx[1], vxx[2]
        z = _hy_inverse(gmat, _hy_forward(fmat, v, kspec, 0, tf), v, x1, hbias[0:1], tf)
        return _hy_inverse(gmat, _hy_forward(fmat, z, kspec, 1, tf), z, x2, hbias[1:2], tf)
    f1big, g1big, f2t, g2t, fperm = tabs
    g, b, l, c = vxx.shape
    kspec = _hy_spectrum(fperm, sh, sl, dh, dl, l // (2 * HY_N1))
    vxx5 = vxx.reshape(g, b, HY_N1, l // HY_N1, c)
    c5 = _hy2_stage2(f2t, g2t, _hy2_stage1(f1big, vxx5, 0, None), kspec, 0)
    z5 = _hy2_stage3(g1big, c5, vxx5, 0, None, 1, hbias[0:1])
    c5 = _hy2_stage2(f2t, g2t, _hy2_stage1(f1big, vxx5, 0, z5), kspec, 1)
    return _hy2_stage3(g1big, c5, vxx5, 0, z5, 2, hbias[1:2]).reshape(b, l, c)


def _outproj_kernel(om_ref, od_ref, oh_ref, wm_ref, wd_ref, wh_ref, bo_ref, x_ref, g_ref, lg_ref, lb_ref,
                    o_ref, *, alpha):
    y = _bdot(om_ref[0], wm_ref[...]) + _bdot(od_ref[0], wd_ref[...]) + _bdot(oh_ref[0], wh_ref[...])
    z = alpha * x_ref[0] + g_ref[0] * (y + bo_ref[...])
    o_ref[0] = _ln(z) * lg_ref[...] + lb_ref[...]


def _out_proj(om, od, oh, wm, wd, wh, bo, x, gate, lg, lb, alpha, tm):
    b, t, d = x.shape
    per_batch = gate.shape[0] == b and b > 1
    gmap = (lambda bb, i: (bb, 0, 0)) if per_batch else (lambda bb, i: (0, 0, 0))
    const = lambda bb, i: (0, 0)
    row = lambda bb, i: (bb, i, 0)
    return pl.pallas_call(
        functools.partial(_outproj_kernel, alpha=alpha),
        grid=(b, t // tm),
        in_specs=[pl.BlockSpec((1, tm, om.shape[2]), row), pl.BlockSpec((1, tm, od.shape[2]), row),
                  pl.BlockSpec((1, tm, oh.shape[2]), row),
                  pl.BlockSpec(wm.shape, const), pl.BlockSpec(wd.shape, const), pl.BlockSpec(wh.shape, const),
                  pl.BlockSpec((1, d), const), pl.BlockSpec((1, tm, d), row), pl.BlockSpec((1, 1, d), gmap),
                  pl.BlockSpec((1, d), const), pl.BlockSpec((1, d), const)],
        out_specs=pl.BlockSpec((1, tm, d), row),
        out_shape=jax.ShapeDtypeStruct((b, t, d), F32),
        compiler_params=_cparams(("arbitrary", "arbitrary")),
        name="out_proj",
    )(om, od, oh, wm, wd, wh, bo, x, gate, lg, lb)


def _ffn_kernel(x_ref, sh_ref, sc_ref, g_ref, w1_ref, b1_ref, w2_ref, b2_ref, lg_ref, lb_ref, o_ref,
                h_scr, acc_scr, *, alpha):
    j = pl.program_id(2)

    @pl.when(j == 0)
    def _():
        h_scr[...] = (_ln(x_ref[0]) * (1.0 + sc_ref[0]) + sh_ref[0]).astype(BF16)
        acc_scr[...] = jnp.zeros_like(acc_scr)

    a = jnp.maximum(_bdot(h_scr[...], w1_ref[...]) + b1_ref[...], 0.0)
    acc_scr[...] += _bdot((a * a).astype(BF16), w2_ref[...])

    @pl.when(j == pl.num_programs(2) - 1)
    def _():
        z = alpha * x_ref[0] + g_ref[0] * (acc_scr[...] + b2_ref[...])
        o_ref[0] = _ln(z) * lg_ref[...] + lb_ref[...]


def _ffn(x, sh, sc, gate, w1, b1, w2, b2, lg, lb, alpha, tm, tf):
    b, t, d = x.shape
    dff = w1.shape[1]
    per_batch = gate.shape[0] == b and b > 1
    gmap = (lambda bb, i, j: (bb, 0, 0)) if per_batch else (lambda bb, i, j: (0, 0, 0))
    const = lambda bb, i, j: (0, 0)
    row = lambda bb, i, j: (bb, i, 0)
    return pl.pallas_call(
        functools.partial(_ffn_kernel, alpha=alpha),
        grid=(b, t // tm, dff // tf),
        in_specs=[pl.BlockSpec((1, tm, d), row),
                  pl.BlockSpec((1, 1, d), gmap), pl.BlockSpec((1, 1, d), gmap), pl.BlockSpec((1, 1, d), gmap),
                  pl.BlockSpec((d, tf), lambda bb, i, j: (0, j)), pl.BlockSpec((1, tf), lambda bb, i, j: (0, j)),
                  pl.BlockSpec((tf, d), lambda bb, i, j: (j, 0)), pl.BlockSpec((1, d), const),
                  pl.BlockSpec((1, d), const), pl.BlockSpec((1, d), const)],
        out_specs=pl.BlockSpec((1, tm, d), row),
        out_shape=jax.ShapeDtypeStruct((b, t, d), F32),
        scratch_shapes=[pltpu.VMEM((tm, d), BF16), pltpu.VMEM((tm, d), F32)],
        compiler_params=_cparams(("arbitrary", "arbitrary", "arbitrary")),
        name="ffn",
    )(x, sh, sc, gate, w1, b1, w2, b2, lg, lb)


def _rope_tables(t_len, roped):
    if not roped:
        one = jnp.ones((t_len, LANE), F32)
        zero = jnp.zeros((t_len, LANE), F32)
        return jnp.concatenate([one, zero, zero, one, zero, zero], axis=1)
    pos_t = jnp.arange(t_len)
    row = (pos_t // GRID_W).astype(F32)[:, None]
    col = (pos_t % GRID_W).astype(F32)[:, None]

    def tile(starts, dims):
        h = dims // 2
        lane = jnp.arange(LANE)
        c = jnp.ones((t_len, LANE), F32)
        sa = jnp.zeros((t_len, LANE), F32)
        sb = jnp.zeros((t_len, LANE), F32)
        inv = ROPE_BASE ** (-(jnp.arange(h // 2, dtype=F32) * (2.0 / h)))
        for st in starts:
            for axis_i, pos in enumerate((row, col)):
                ang = pos * inv[None, :]
                cos, sin = jnp.cos(ang), jnp.sin(ang)
                lo = st + axis_i * h
                c = c.at[:, lo:lo + h // 2].set(cos).at[:, lo + h // 2:lo + h].set(cos)
                sa = sa.at[:, lo:lo + h // 2].set(-sin)
                sb = sb.at[:, lo + h // 2:lo + h].set(sin)
        del lane
        return [c, sa, sb]

    return jnp.concatenate(tile([MLA_NOPE], MLA_ROPE) + tile([0, LANE // 2], DIFF_DIM), axis=1)


def _prep_weights(w_in, w_q_up, w_kv_up, w_out, diff_lambda, diff_subln):
    depth, d, _ = w_in.shape
    s1, s2 = IN_MLA, IN_MLA + IN_DIFF
    pad_last = lambda a, n: jnp.pad(a, [(0, 0)] * (a.ndim - 1) + [(0, n - a.shape[-1])])
    kr = jnp.pad(w_in[..., C_KR:s1], ((0, 0), (0, 0), (MLA_NOPE, LANE - MLA_NOPE - MLA_ROPE)))
    dq = pad_last(w_in[..., s1:s1 + N_DQ].reshape(depth, d, DIFF_HEADS, 2, DIFF_DIM), LANE // 2)
    dk = pad_last(w_in[..., s1 + N_DQ:s1 + 2 * N_DQ].reshape(depth, d, DIFF_HEADS, 2, DIFF_DIM), LANE // 2)
    dv = pad_last(w_in[..., s1 + 2 * N_DQ:s2].reshape(depth, d, DIFF_HEADS, DIFF_V), LANE)
    win = jnp.concatenate([w_in[..., :C_KR], kr, dq.reshape(depth, d, W_D), dk.reshape(depth, d, W_D),
                           dv.reshape(depth, d, W_D), w_in[..., s2:]], axis=-1).astype(BF16)
    wq = pad_last(w_q_up.reshape(depth, MLA_Q_RANK, MLA_HEADS, MLA_NOPE + MLA_ROPE), LANE)
    wq = wq.reshape(depth, MLA_Q_RANK, W_QM).astype(BF16)
    kv = w_kv_up.reshape(depth, MLA_KV_RANK, MLA_HEADS, MLA_NOPE + MLA_V)
    wk = pad_last(kv[..., :MLA_NOPE], LANE).reshape(depth, MLA_KV_RANK, W_QM)
    wv = kv[..., MLA_NOPE:].reshape(depth, MLA_KV_RANK, W_VM)
    wkv = jnp.concatenate([wk, wv], axis=-1).astype(BF16)
    wo_m = w_out[:, :W_VM].astype(BF16)
    wo_d = w_out[:, W_VM:W_VM + DIFF_HEADS * DIFF_V].reshape(depth, DIFF_HEADS, DIFF_V, -1)
    wo_d = jnp.pad(wo_d, ((0, 0), (0, 0), (0, LANE - DIFF_V), (0, 0))).reshape(depth, W_D, -1).astype(BF16)
    wo_h = w_out[:, W_VM + DIFF_HEADS * DIFF_V:].astype(BF16)
    lam = pad_last(diff_lambda.astype(F32), LANE)
    subln = pad_last(diff_subln.astype(F32), LANE)[:, None, :]
    return win, wq, wkv, wo_m, wo_d, wo_h, lam, subln


def kernel(x, c, ctx, c_ctx, w_mod, b_mod, w_in, mla_q_norm, w_q_up, mla_kv_norm, w_kv_up, diff_lambda, diff_subln, hy_conv_w, hy_conv_b, hy_fw1, hy_fb1, hy_fw2, hy_fb2, hy_fw3, hy_fb3, hy_bias, w_out, b_out, ln1_g, ln1_b, w_ff1, b_ff1, w_ff2, b_ff2, ln2_g, ln2_b):
    bsz, seq, d = x.shape
    n_ctx = ctx.shape[1]
    depth = w_in.shape[0]
    alpha = (2.0 * depth) ** 0.25

    rpad = -(bsz + 1) % 8
    cc = jnp.concatenate([c, c_ctx[None, :], jnp.zeros((rpad, d), F32)], axis=0)
    mod_all = _modulation(cc, w_mod, b_mod)

    win, wq, wkv, wo_m, wo_d, wo_h, lam_p, subln = _prep_weights(w_in, w_q_up, w_kv_up, w_out, diff_lambda, diff_subln)
    w1 = w_ff1.astype(BF16)
    w2 = w_ff2.astype(BF16)
    fw1 = jnp.pad(hy_fw1, ((0, 0), (0, LANE - HY_EMB), (0, 0)))
    tab_x = _rope_tables(seq, True)
    tab_c = _rope_tables(n_ctx, False)
    def hy_tables(l):
        if l >= 1024 and l % (HY_N1 * HY_J * HY_GROUPS) == 0:
            return _hy2_tables(l)
        return _dft_tables(l, min(512, l)) + (min(512, l),)

    hyc_x = _hy_consts(seq) + (hy_tables(seq),)
    hyc_c = _hy_consts(n_ctx) + (hy_tables(n_ctx),)
    tm_x, tm_c = min(512, seq), min(512, n_ctx)
    tq_x, tq_c = min(256, seq), min(256, n_ctx)
    tff = min(1024, w1.shape[2])
    row2 = lambda a: a.reshape(1, -1)

    for layer in range(depth):
        need_ctx = layer < depth - 1
        lam_init = 0.8 - 0.6 * math.exp(-0.3 * layer)
        mod = mod_all[layer, :bsz].reshape(bsz, 1, N_MOD, d)
        modc = mod_all[layer, bsz:bsz + 1].reshape(1, 1, N_MOD, d)
        sh1, sc1, g1, sh2, sc2, g2 = [mod[:, :, i] for i in range(N_MOD)]
        csh1, csc1, cg1, csh2, csc2, cg2 = [modc[:, :, i] for i in range(N_MOD)]
        gq, gkv = row2(mla_q_norm[layer]), row2(mla_kv_norm[layer])
        hy_args = (hy_conv_w[layer], row2(hy_conv_b[layer]), fw1[layer], row2(hy_fb1[layer]), hy_fw2[layer],
                   row2(hy_fb2[layer]), hy_fw3[layer], row2(hy_fb3[layer]), hy_bias[layer])

        qm, km, vm, dq, dk, dv, uh = _in_proj(x, sh1, sc1, win[layer], gq, wq[layer], gkv, wkv[layer], tab_x, tm_x)
        cqm, ckm, cvm, cdq, cdk, cdv, cuh = _in_proj(ctx, csh1, csc1, win[layer], gq, wq[layer], gkv, wkv[layer],
                                                     tab_c, tm_c)
        om = _mla_attention(qm, [ckm, km], [cvm, vm], tq_x, MLA_HEADS // 2)
        od = _diff_attention(dq, [cdk, dk], [cdv, dv], lam_p[layer], subln[layer], lam_init, tq_x, DIFF_HEADS)
        oh = _hyena(uh, hyc_x, *hy_args)
        x = _out_proj(om, od, oh, wo_m[layer], wo_d[layer], wo_h[layer], row2(b_out[layer]), x, g1,
                      row2(ln1_g[layer]), row2(ln1_b[layer]), alpha, tm_x)
        x = _ffn(x, sh2, sc2, g2, w1[layer], row2(b_ff1[layer]), w2[layer], row2(b_ff2[layer]),
                 row2(ln2_g[layer]), row2(ln2_b[layer]), alpha, tm_x, tff)
        if need_ctx:
            com = _mla_attention(cqm, [ckm], [cvm], tq_c, MLA_HEADS // 2)
            cod = _diff_attention(cdq, [cdk], [cdv], lam_p[layer], subln[layer], lam_init, tq_c, DIFF_HEADS)
            coh = _hyena(cuh, hyc_c, *hy_args)
            ctx = _out_proj(com, cod, coh, wo_m[layer], wo_d[layer], wo_h[layer], row2(b_out[layer]), ctx, cg1,
                            row2(ln1_g[layer]), row2(ln1_b[layer]), alpha, tm_c)
            ctx = _ffn(ctx, csh2, csc2, cg2, w1[layer], row2(b_ff1[layer]), w2[layer], row2(b_ff2[layer]),
                       row2(ln2_g[layer]), row2(ln2_b[layer]), alpha, tm_c, tff)
    return x
```

```python
import functools
import math

import jax
import numpy as np
import jax.numpy as jnp
from jax import lax
from jax.experimental import pallas as pl
from jax.experimental.pallas import tpu as pltpu

F32 = jnp.float32
BF16 = jnp.bfloat16
HI = lax.Precision.HIGHEST

GRID_W = 64
ROPE_BASE = 10000.0
LN_EPS = 1e-6
MLA_HEADS, MLA_NOPE, MLA_ROPE, MLA_V = 6, 64, 32, 64
MLA_Q_RANK, MLA_KV_RANK = 256, 128
DIFF_HEADS, DIFF_DIM = 4, 48
DIFF_V = 2 * DIFF_DIM
HY_CH, HY_ORDER, HY_BANDS = 256, 2, 16
HY_EMB = 1 + 2 * HY_BANDS
HY_TARGET, HY_FAST_DECAY_PCT, HY_SLOW_DECAY_PCT = 1e-2, 0.3, 1.5
HY_MIN_RATE = -math.log(HY_TARGET) / HY_SLOW_DECAY_PCT
HY_MAX_RATE = -math.log(HY_TARGET) / HY_FAST_DECAY_PCT
N_MOD = 6

LOG2E = 1.0 / math.log(2.0)
LANE = 128
VMEM_LIMIT = 56 * 1024 * 1024

IN_MLA = MLA_Q_RANK + MLA_KV_RANK + MLA_ROPE
N_DQ = DIFF_HEADS * 2 * DIFF_DIM
IN_DIFF = 2 * N_DQ + DIFF_HEADS * DIFF_V
W_QM = MLA_HEADS * LANE
W_VM = MLA_HEADS * MLA_V
W_D = DIFF_HEADS * LANE
W_HY = (HY_ORDER + 1) * HY_CH
C_CQ, C_CKV, C_KR = 0, MLA_Q_RANK, MLA_Q_RANK + MLA_KV_RANK
C_DQ = C_KR + LANE
C_DK = C_DQ + W_D
C_DV = C_DK + W_D
C_HY = C_DV + W_D
W_IN = C_HY + W_HY


def _cparams(sem):
    return pltpu.CompilerParams(dimension_semantics=sem, vmem_limit_bytes=VMEM_LIMIT)


def _ln(x):
    mu = jnp.mean(x, -1, keepdims=True)
    xc = x - mu
    var = jnp.mean(xc * xc, -1, keepdims=True)
    return xc * lax.rsqrt(var + LN_EPS)


def _bdot(a, b):
    return jnp.dot(a, b, preferred_element_type=F32)


def _mod_kernel(c_ref, w_ref, b_ref, o_ref):
    c = c_ref[...]
    s = c / (1.0 + jnp.exp(-c))
    o_ref[...] = jnp.dot(s, w_ref[...], preferred_element_type=F32, precision=HI) + b_ref[...]


def _modulation(cc, w_mod, b_mod):
    depth, d, n = w_mod.shape
    r = cc.shape[0]
    tn = 1024
    return pl.pallas_call(
        _mod_kernel,
        grid=(depth, n // tn),
        in_specs=[pl.BlockSpec((r, d), lambda l, j: (0, 0)),
                  pl.BlockSpec((None, d, tn), lambda l, j: (l, 0, j)),
                  pl.BlockSpec((None, 1, tn), lambda l, j: (l, 0, j))],
        out_specs=pl.BlockSpec((None, r, tn), lambda l, j: (l, 0, j)),
        out_shape=jax.ShapeDtypeStruct((depth, r, n), F32),
        compiler_params=_cparams(("arbitrary", "arbitrary")),
        name="modulation",
    )(cc, w_mod, b_mod.reshape(depth, 1, n))


def _rope(x, c, sa, sb, half):
    return x * c + pltpu.roll(x, LANE - half, 1) * sa + pltpu.roll(x, half, 1) * sb


def _inproj_kernel(x_ref, sh_ref, sc_ref, win_ref, gq_ref, wq_ref, gkv_ref, wkv_ref, tab_ref,
                   qm_ref, km_ref, vm_ref, dq_ref, dk_ref, dv_ref, hy_ref):
    x = x_ref[0]
    h = _ln(x) * (1.0 + sc_ref[0]) + sh_ref[0]
    p = _bdot(h.astype(BF16), win_ref[...])
    tab = tab_ref[...]
    cm, sam, sbm = tab[:, 0:LANE], tab[:, LANE:2 * LANE], tab[:, 2 * LANE:3 * LANE]
    cd, sad, sbd = tab[:, 3 * LANE:4 * LANE], tab[:, 4 * LANE:5 * LANE], tab[:, 5 * LANE:6 * LANE]

    cq = p[:, C_CQ:C_CQ + MLA_Q_RANK]
    qn = cq * lax.rsqrt(jnp.mean(cq * cq, -1, keepdims=True) + LN_EPS) * gq_ref[...]
    q = _bdot(qn.astype(BF16), wq_ref[...])
    ckv = p[:, C_CKV:C_CKV + MLA_KV_RANK]
    kvn = ckv * lax.rsqrt(jnp.mean(ckv * ckv, -1, keepdims=True) + LN_EPS) * gkv_ref[...]
    kv = _bdot(kvn.astype(BF16), wkv_ref[...])
    kr = _rope(p[:, C_KR:C_KR + LANE], cm, sam, sbm, MLA_ROPE // 4)
    scale_m = LOG2E * (MLA_NOPE + MLA_ROPE) ** -0.5
    for hd in range(MLA_HEADS):
        sl = slice(hd * LANE, (hd + 1) * LANE)
        qm_ref[0, :, sl] = (_rope(q[:, sl], cm, sam, sbm, MLA_ROPE // 4) * scale_m).astype(BF16)
        km_ref[0, :, sl] = (kv[:, sl] + kr).astype(BF16)
    vm_ref[0] = kv[:, W_QM:W_QM + W_VM].astype(BF16)
    scale_d = LOG2E * DIFF_DIM ** -0.5
    for hd in range(DIFF_HEADS):
        sl = slice(hd * LANE, (hd + 1) * LANE)
        dq = p[:, C_DQ + hd * LANE:C_DQ + (hd + 1) * LANE]
        dk = p[:, C_DK + hd * LANE:C_DK + (hd + 1) * LANE]
        dq_ref[0, :, sl] = (_rope(dq, cd, sad, sbd, DIFF_DIM // 4) * scale_d).astype(BF16)
        dk_ref[0, :, sl] = _rope(dk, cd, sad, sbd, DIFF_DIM // 4).astype(BF16)
    dv_ref[0] = p[:, C_DV:C_DV + W_D].astype(BF16)
    hy_ref[0] = p[:, C_HY:C_HY + W_HY].astype(BF16)


def _in_proj(x, sh, sc, win, gq, wq, gkv, wkv, tab, tm):
    b, t, d = x.shape
    per_batch = sh.shape[0] == b and b > 1
    mod_map = (lambda i, bb: (bb, 0, 0)) if per_batch else (lambda i, bb: (0, 0, 0))
    const = lambda i, bb: (0, 0)
    widths = (W_QM, W_QM, W_VM, W_D, W_D, W_D, W_HY)
    return pl.pallas_call(
        _inproj_kernel,
        grid=(t // tm, b),
        in_specs=[pl.BlockSpec((1, tm, d), lambda i, bb: (bb, i, 0)),
                  pl.BlockSpec((1, 1, d), mod_map),
                  pl.BlockSpec((1, 1, d), mod_map),
                  pl.BlockSpec(win.shape, const),
                  pl.BlockSpec(gq.shape, const),
                  pl.BlockSpec(wq.shape, const),
                  pl.BlockSpec(gkv.shape, const),
                  pl.BlockSpec(wkv.shape, const),
                  pl.BlockSpec((tm, 6 * LANE), lambda i, bb: (i, 0))],
        out_specs=[pl.BlockSpec((1, tm, w), lambda i, bb: (bb, i, 0)) for w in widths],
        out_shape=[jax.ShapeDtypeStruct((b, t, w), BF16) for w in widths],
        compiler_params=_cparams(("arbitrary", "arbitrary")),
        name="in_proj",
    )(x, sh, sc, win, gq, wq, gkv, wkv, tab)


NEG_BIG = -1e30


def _lane_fold(x, op):
    r = x[:, 0:LANE]
    for i in range(1, x.shape[1] // LANE):
        r = op(r, x[:, i * LANE:(i + 1) * LANE])
    return r


def _score_pass(q, parts, ksl, s_refs):
    mx = jnp.full((q.shape[0], LANE), NEG_BIG, F32)
    for (k_ref, _, n, tk), s_ref in zip(parts, s_refs):
        for j in range(n // tk):
            k = k_ref[0, j * tk:(j + 1) * tk, ksl]
            s = lax.dot_general(q, k, (((1,), (1,)), ((), ())), preferred_element_type=F32)
            s_ref[j] = s
            mx = jnp.maximum(mx, _lane_fold(s, jnp.maximum))
    return jnp.max(mx, axis=-1, keepdims=True)


def _softmax_pv(q, parts, ksl, vsl, s_refs):
    m = _score_pass(q, parts, ksl, s_refs)
    ls = jnp.zeros((q.shape[0], LANE), F32)
    acc = jnp.zeros((q.shape[0], LANE), F32)
    for (_, v_ref, n, tk), s_ref in zip(parts, s_refs):
        for j in range(n // tk):
            p = jnp.exp2(s_ref[j] - m)
            ls = ls + _lane_fold(p, jnp.add)
            acc = acc + _bdot(p.astype(BF16), v_ref[0, j * tk:(j + 1) * tk, vsl])
    return acc * (1.0 / jnp.sum(ls, axis=-1, keepdims=True))


def _mla_attn_kernel(*refs, part_shapes, n_pairs):
    np_ = len(part_shapes)
    q_ref = refs[0]
    k_refs = refs[1:1 + np_]
    v_refs = refs[1 + np_:1 + 2 * np_]
    o_ref = refs[1 + 2 * np_]
    s_sets = (refs[2 + 2 * np_:2 + 3 * np_], refs[2 + 3 * np_:])
    parts = [(k_refs[i], v_refs[i]) + part_shapes[i] for i in range(np_)]
    for pr in range(n_pairs):
        outs = []
        for hh in range(2):
            hd = 2 * pr + hh
            sl = slice(hd * LANE, (hd + 1) * LANE)
            outs.append(_softmax_pv(q_ref[0, :, sl], parts, sl, slice(pr * LANE, (pr + 1) * LANE), s_sets[hh]))
        lane = lax.broadcasted_iota(jnp.int32, outs[0].shape, 1)
        o_ref[0, :, pr * LANE:(pr + 1) * LANE] = jnp.where(lane < MLA_V, outs[0], outs[1]).astype(BF16)


def _chunk(n):
    return n if n <= 512 else 512


def _mla_attention(q, ks, vs, tq, n_pairs):
    b, t, _ = q.shape
    part_shapes = tuple((k.shape[1], _chunk(k.shape[1])) for k in ks)
    in_specs = [pl.BlockSpec((1, tq, 2 * n_pairs * LANE), lambda bb, g, i: (bb, i, g))]
    in_specs += [pl.BlockSpec((1, k.shape[1], 2 * n_pairs * LANE), lambda bb, g, i: (bb, 0, g)) for k in ks]
    in_specs += [pl.BlockSpec((1, v.shape[1], n_pairs * LANE), lambda bb, g, i: (bb, 0, g)) for v in vs]
    scratch = [pltpu.VMEM((n // tk, tq, tk), F32) for n, tk in part_shapes] * 2
    return pl.pallas_call(
        functools.partial(_mla_attn_kernel, part_shapes=part_shapes, n_pairs=n_pairs),
        grid=(b, MLA_HEADS // (2 * n_pairs), t // tq),
        in_specs=in_specs,
        out_specs=pl.BlockSpec((1, tq, n_pairs * LANE), lambda bb, g, i: (bb, i, g)),
        out_shape=jax.ShapeDtypeStruct((b, t, W_VM), BF16),
        scratch_shapes=scratch,
        compiler_params=_cparams(("arbitrary", "arbitrary", "arbitrary")),
        name="mla_attention",
    )(q, *ks, *vs)


def _diff_attn_kernel(*refs, part_shapes, lam_init, n_heads):
    np_ = len(part_shapes)
    q_ref = refs[0]
    k_refs = refs[1:1 + np_]
    v_refs = refs[1 + np_:1 + 2 * np_]
    lam_ref, g_ref, o_ref = refs[1 + 2 * np_:4 + 2 * np_]
    s_sets = (refs[4 + 2 * np_:4 + 3 * np_], refs[4 + 3 * np_:])
    parts = [(k_refs[i], v_refs[i]) + part_shapes[i] for i in range(np_)]
    lp = lam_ref[...]
    lam = (jnp.exp(jnp.sum(lp[0:1] * lp[1:2], axis=-1, keepdims=True))
           - jnp.exp(jnp.sum(lp[2:3] * lp[3:4], axis=-1, keepdims=True)) + lam_init)
    for hd in range(n_heads):
        sl = slice(hd * LANE, (hd + 1) * LANE)
        q = q_ref[0, :, sl]
        lane = lax.broadcasted_iota(jnp.int32, q.shape, 1)
        zero = jnp.zeros_like(q)
        o1 = _softmax_pv(jnp.where(lane < LANE // 2, q, zero), parts, sl, sl, s_sets[0])
        o2 = _softmax_pv(jnp.where(lane >= LANE // 2, q, zero), parts, sl, sl, s_sets[1])
        o = o1 - lam * o2
        ms = jnp.sum(o * o, axis=-1, keepdims=True) * (1.0 / DIFF_V)
        o_ref[0, :, sl] = (o * lax.rsqrt(ms + LN_EPS) * g_ref[...] * (1.0 - lam_init)).astype(BF16)


def _diff_attention(q, ks, vs, lam_p, subln, lam_init, tq, n_heads):
    b, t, _ = q.shape
    part_shapes = tuple((k.shape[1], _chunk(k.shape[1])) for k in ks)
    hmap = lambda bb, g, i: (bb, 0, g)
    in_specs = [pl.BlockSpec((1, tq, n_heads * LANE), lambda bb, g, i: (bb, i, g))]
    in_specs += [pl.BlockSpec((1, k.shape[1], n_heads * LANE), hmap) for k in ks]
    in_specs += [pl.BlockSpec((1, v.shape[1], n_heads * LANE), hmap) for v in vs]
    in_specs += [pl.BlockSpec(lam_p.shape, lambda bb, g, i: (0, 0)),
                 pl.BlockSpec(subln.shape, lambda bb, g, i: (0, 0))]
    scratch = [pltpu.VMEM((n // tk, tq, tk), F32) for n, tk in part_shapes] * 2
    return pl.pallas_call(
        functools.partial(_diff_attn_kernel, part_shapes=part_shapes, lam_init=lam_init, n_heads=n_heads),
        grid=(b, DIFF_HEADS // n_heads, t // tq),
        in_specs=in_specs,
        out_specs=pl.BlockSpec((1, tq, n_heads * LANE), lambda bb, g, i: (bb, i, g)),
        out_shape=jax.ShapeDtypeStruct((b, t, W_D), BF16),
        scratch_shapes=scratch,
        compiler_params=_cparams(("arbitrary", "arbitrary", "arbitrary")),
        name="diff_attention",
    )(q, *ks, *vs, lam_p, subln)


def _hyfilt_kernel(emb_ref, win_ref, fw1_ref, fb1_ref, fw2_ref, fb2_ref, fwf_ref, fbf_ref, fwb_ref, fbb_ref,
                   d_ref):
    h = jnp.sin(jnp.dot(emb_ref[...], fw1_ref[...], preferred_element_type=F32, precision=HI) + fb1_ref[...])
    h = jnp.sin(jnp.dot(h, fw2_ref[...], preferred_element_type=F32, precision=HI) + fb2_ref[...])
    w = win_ref[...]
    fwd = (jnp.dot(h, fwf_ref[...], preferred_element_type=F32, precision=HI) + fbf_ref[...]) * w
    bwd = (jnp.dot(h, fwb_ref[...], preferred_element_type=F32, precision=HI) + fbb_ref[...]) * w
    row = lax.broadcasted_iota(jnp.int32, bwd.shape, 0)
    bwd = jnp.where(row == 0, 0.0, bwd)
    norm = jnp.sum(jnp.abs(fwd), axis=0, keepdims=True) + jnp.sum(jnp.abs(bwd), axis=0, keepdims=True)
    inv = 1.0 / norm
    d_ref[0] = ((fwd + bwd) * inv).astype(BF16)
    d_ref[1] = ((fwd - bwd) * inv).astype(BF16)


def _hy_filters(emb, win, fw1, fb1, fw2, fb2, fw3, fb3):
    l = emb.shape[0]
    hid = fw2.shape[0]
    nblk = HY_CH // LANE
    const = lambda g: (0, 0)
    fcol = lambda g: (0, (g // nblk) * 2 * nblk + g % nblk)
    bcol = lambda g: (0, (g // nblk) * 2 * nblk + nblk + g % nblk)
    return pl.pallas_call(
        _hyfilt_kernel,
        grid=(HY_ORDER * nblk,),
        in_specs=[pl.BlockSpec(emb.shape, const),
                  pl.BlockSpec((l, LANE), lambda g: (0, g % nblk)),
                  pl.BlockSpec(fw1.shape, const), pl.BlockSpec(fb1.shape, const),
                  pl.BlockSpec(fw2.shape, const), pl.BlockSpec(fb2.shape, const),
                  pl.BlockSpec((hid, LANE), fcol), pl.BlockSpec((1, LANE), fcol),
                  pl.BlockSpec((hid, LANE), bcol), pl.BlockSpec((1, LANE), bcol)],
        out_specs=pl.BlockSpec((2, None, l, LANE), lambda g: (0, g // nblk, 0, g % nblk)),
        out_shape=jax.ShapeDtypeStruct((2, HY_ORDER, l, HY_CH), BF16),
        compiler_params=_cparams(("arbitrary",)),
        name="hyena_filters",
    )(emb, win, fw1, fb1, fw2, fb2, fw3, fb3, fw3, fb3)


def _hyspec_kernel(f_ref, s_ref, d_ref, o_ref):
    tf = f_ref.shape[0] // 2
    o_ref[0:tf, :] = _bdot(f_ref[0:tf, :], s_ref[...])
    o_ref[tf:2 * tf, :] = _bdot(f_ref[tf:2 * tf, :], d_ref[...])


def _hy_spectrum(fmat, filt, tf):
    rows, l = fmat.shape
    return pl.pallas_call(
        _hyspec_kernel,
        grid=(rows // (2 * tf), HY_ORDER),
        in_specs=[pl.BlockSpec((2 * tf, l), lambda j, o: (j, 0)),
                  pl.BlockSpec((None, None, l, HY_CH), lambda j, o: (0, o, 0, 0)),
                  pl.BlockSpec((None, None, l, HY_CH), lambda j, o: (1, o, 0, 0))],
        out_specs=pl.BlockSpec((2 * tf, HY_CH), lambda j, o: (j, o)),
        out_shape=jax.ShapeDtypeStruct((rows, HY_ORDER * HY_CH), F32),
        compiler_params=_cparams(("arbitrary", "arbitrary")),
        name="hyena_filter_spectrum",
    )(fmat, filt, filt)


def _shortconv_kernel(u_ref, w_ref, b_ref, o_ref):
    u = u_ref[0].astype(F32)
    l = u.shape[0]
    row = lax.broadcasted_iota(jnp.int32, u.shape, 0)
    up = jnp.where(row == 0, 0.0, pltpu.roll(u, 1, 0))
    un = jnp.where(row == l - 1, 0.0, pltpu.roll(u, l - 1, 0))
    w = w_ref[...]
    o_ref[0] = (up * w[0:1] + u * w[1:2] + un * w[2:3] + b_ref[...]).astype(BF16)


def _short_conv(u, w, bias):
    b, l, _ = u.shape
    g = HY_ORDER + 1
    return pl.pallas_call(
        _shortconv_kernel,
        grid=(b, g),
        in_specs=[pl.BlockSpec((1, l, HY_CH), lambda bb, gg: (bb, 0, gg)),
                  pl.BlockSpec((3, HY_CH), lambda bb, gg: (0, gg)),
                  pl.BlockSpec((1, HY_CH), lambda bb, gg: (0, gg))],
        out_specs=pl.BlockSpec((None, 1, l, HY_CH), lambda bb, gg: (gg, bb, 0, 0)),
        out_shape=jax.ShapeDtypeStruct((g, b, l, HY_CH), BF16),
        compiler_params=_cparams(("arbitrary", "arbitrary")),
        name="hyena_short_conv",
    )(u, w, bias)


def _hyfwd_kernel(f_ref, z_ref, k_ref, y_ref):
    tf = f_ref.shape[0] // 2
    s = _bdot(f_ref[...], z_ref[0])
    sr, si = s[0:tf], s[tf:2 * tf]
    kr, ki = k_ref[0:tf, :], k_ref[tf:2 * tf, :]
    y_ref[0, 0:tf, :] = (sr * kr - si * ki).astype(BF16)
    y_ref[0, tf:2 * tf, :] = (sr * ki + si * kr).astype(BF16)


def _hy_forward(fmat, z, kspec, order, tf):
    rows, l = fmat.shape
    b = z.shape[0]
    return pl.pallas_call(
        _hyfwd_kernel,
        grid=(rows // (2 * tf), b),
        in_specs=[pl.BlockSpec((2 * tf, l), lambda j, bb: (j, 0)),
                  pl.BlockSpec((1, l, HY_CH), lambda j, bb: (bb, 0, 0)),
                  pl.BlockSpec((2 * tf, HY_CH), lambda j, bb: (j, order))],
        out_specs=pl.BlockSpec((1, 2 * tf, HY_CH), lambda j, bb: (bb, j, 0)),
        out_shape=jax.ShapeDtypeStruct((b, rows, HY_CH), BF16),
        compiler_params=_cparams(("arbitrary", "arbitrary")),
        name="hyena_dft_forward",
    )(fmat, z, kspec)


def _hyinv_kernel(g_ref, y_ref, z_ref, x_ref, b_ref, o_ref):
    conv = _bdot(g_ref[...], y_ref[0])
    z = z_ref[0].astype(F32)
    o_ref[0] = (x_ref[0].astype(F32) * (conv + z * b_ref[...])).astype(BF16)


def _hy_inverse(gmat, y, z, gate, bias, tt):
    l, rows = gmat.shape
    b = y.shape[0]
    return pl.pallas_call(
        _hyinv_kernel,
        grid=(l // tt, b),
        in_specs=[pl.BlockSpec((tt, rows), lambda i, bb: (i, 0)),
                  pl.BlockSpec((1, rows, HY_CH), lambda i, bb: (bb, 0, 0)),
                  pl.BlockSpec((1, tt, HY_CH), lambda i, bb: (bb, i, 0)),
                  pl.BlockSpec((1, tt, HY_CH), lambda i, bb: (bb, i, 0)),
                  pl.BlockSpec((1, HY_CH), lambda i, bb: (0, 0))],
        out_specs=pl.BlockSpec((1, tt, HY_CH), lambda i, bb: (bb, i, 0)),
        out_shape=jax.ShapeDtypeStruct((b, l, HY_CH), BF16),
        compiler_params=_cparams(("arbitrary", "arbitrary")),
        name="hyena_dft_inverse",
    )(gmat, y, z, gate, bias)


def _dft_tables(l, tf):
    k = jnp.arange(l, dtype=jnp.int32)
    n = jnp.arange(l, dtype=jnp.int32)
    ph = ((2 * k + 1)[:, None] * n[None, :]) % (4 * l)
    ang = ph.astype(F32) * (2.0 * math.pi / (4 * l))
    c = jnp.cos(ang).reshape(l // tf, 1, tf, l)
    s = (-jnp.sin(ang)).reshape(l // tf, 1, tf, l)
    f = jnp.concatenate([c, s], axis=1).reshape(2 * l, l)
    return f.astype(BF16), (f.T * (1.0 / l)).astype(BF16)


def _hy_consts(l):
    t = jnp.arange(l, dtype=F32)
    bands = jnp.arange(1, HY_BANDS + 1, dtype=F32)
    ang = (2.0 * math.pi / l) * t[:, None] * bands[None, :]
    emb = jnp.concatenate([(t / l)[:, None], jnp.cos(ang), jnp.sin(ang)], -1)
    emb = jnp.pad(emb, ((0, 0), (0, LANE - HY_EMB)))
    rates = jnp.linspace(HY_MIN_RATE, HY_MAX_RATE, HY_CH, dtype=F32)
    win = jnp.exp(-(t / l)[:, None] * rates[None, :])
    return emb, win


HY_N1 = 16
HY_J = 16
HY_GROUPS = 4
HY_KG = 4


def _hy2_tables(l):
    n1h, jj = HY_N1, HY_J
    n2, k1n = l // n1h, 2 * n1h
    k2n = n2 // 2
    k1 = jnp.arange(k1n, dtype=jnp.int32)
    ph1 = ((2 * k1 + 1)[:, None] * jnp.arange(n1h, dtype=jnp.int32)[None, :]) % (4 * n1h)
    a1 = ph1.astype(F32) * (2.0 * math.pi / (4 * n1h))
    f1 = jnp.stack([jnp.cos(a1), -jnp.sin(a1)], axis=1)
    f1big = jnp.einsum('krn,ab->kranb', f1, jnp.eye(jj, dtype=F32)).reshape(k1n * 2 * jj, n1h * jj)
    kk = k1[:, None] + k1n * jnp.arange(k2n, dtype=jnp.int32)[None, :]
    ph2 = ((2 * kk + 1)[:, :, None] * jnp.arange(n2, dtype=jnp.int32)[None, None, :]) % (4 * l)
    a2 = ph2.astype(F32) * (2.0 * math.pi / (4 * l))
    mr, mi = jnp.cos(a2), -jnp.sin(a2)
    f2t = jnp.concatenate([jnp.concatenate([mr, -mi], axis=2), jnp.concatenate([mi, mr], axis=2)], axis=1)
    return (f1big.astype(BF16), (f1big.T * (1.0 / l)).astype(BF16), f2t.astype(BF16),
            jnp.swapaxes(f2t, 1, 2).astype(BF16))


def _hy2_s1_kernel(f_ref, z_ref, o_ref):
    n1h, _, c = z_ref.shape[1:]
    k1n = o_ref.shape[1]
    for g in range(HY_GROUPS):
        cols = slice(g * HY_J, (g + 1) * HY_J)
        z = z_ref[0, :, cols, :].reshape(n1h * HY_J, c)
        a = _bdot(f_ref[...], z)
        o_ref[0, :, :, cols, :] = a.astype(BF16).reshape(k1n, 2, HY_J, c)


def _hy2_stage1(f1big, vxx, which, z5):
    src = vxx if z5 is None else z5
    b, n1h, n2, c = src.shape[-4:]
    k1n = 2 * n1h
    w = HY_J * HY_GROUPS
    if z5 is None:
        zspec = pl.BlockSpec((None, 1, n1h, w, c), lambda bb, g: (which, bb, 0, g, 0))
    else:
        zspec = pl.BlockSpec((1, n1h, w, c), lambda bb, g: (bb, 0, g, 0))
    return pl.pallas_call(
        _hy2_s1_kernel,
        grid=(b, n2 // w),
        in_specs=[pl.BlockSpec(f1big.shape, lambda bb, g: (0, 0)), zspec],
        out_specs=pl.BlockSpec((1, k1n, 2, w, c), lambda bb, g: (bb, 0, 0, g, 0)),
        out_shape=jax.ShapeDtypeStruct((b, k1n, 2, n2, c), BF16),
        compiler_params=_cparams(("arbitrary", "arbitrary")),
        name="hyena_stage1",
    )(f1big, src)


def _hy2_fspec_kernel(f_ref, s_ref, d_ref, o_ref):
    h = o_ref.shape[1] // 2
    for i in range(HY_KG):
        o_ref[i, 0:h, :] = _bdot(f_ref[i], s_ref[0, i])[0:h]
        o_ref[i, h:2 * h, :] = _bdot(f_ref[i], d_ref[0, i])[h:2 * h]


def _hy2_filter_spectrum(f2t, a5):
    nsig, k1n, _, n2, c = a5.shape
    a4 = a5.reshape(nsig, k1n, 2 * n2, c)
    return pl.pallas_call(
        _hy2_fspec_kernel,
        grid=(k1n // HY_KG, HY_ORDER),
        in_specs=[pl.BlockSpec((HY_KG, n2, 2 * n2), lambda g, o: (g, 0, 0)),
                  pl.BlockSpec((1, HY_KG, 2 * n2, c), lambda g, o: (o, g, 0, 0)),
                  pl.BlockSpec((1, HY_KG, 2 * n2, c), lambda g, o: (HY_ORDER + o, g, 0, 0))],
        out_specs=pl.BlockSpec((HY_KG, n2, c), lambda g, o: (g, 0, o)),
        out_shape=jax.ShapeDtypeStruct((k1n, n2, HY_ORDER * c), F32),
        compiler_params=_cparams(("arbitrary", "arbitrary")),
        name="hyena_filter_spectrum2",
    )(f2t, a4, a4)


def _hy2_s2_kernel(f_ref, g_ref, a_ref, k_ref, o_ref):
    for i in range(HY_KG):
        y = _bdot(f_ref[i], a_ref[0, i])
        h = y.shape[0] // 2
        yr, yi = y[:h], y[h:]
        kr, ki = k_ref[i, 0:h, :], k_ref[i, h:2 * h, :]
        p = jnp.concatenate([yr * kr - yi * ki, yr * ki + yi * kr], axis=0).astype(BF16)
        o_ref[0, i] = _bdot(g_ref[i], p).astype(BF16)


def _hy2_stage2(f2t, g2t, a5, kspec, order):
    b, k1n, _, n2, c = a5.shape
    a4 = a5.reshape(b, k1n, 2 * n2, c)
    out = pl.pallas_call(
        _hy2_s2_kernel,
        grid=(k1n // HY_KG, b),
        in_specs=[pl.BlockSpec((HY_KG, n2, 2 * n2), lambda g, bb: (g, 0, 0)),
                  pl.BlockSpec((HY_KG, 2 * n2, n2), lambda g, bb: (g, 0, 0)),
                  pl.BlockSpec((1, HY_KG, 2 * n2, c), lambda g, bb: (bb, g, 0, 0)),
                  pl.BlockSpec((HY_KG, n2, c), lambda g, bb: (g, 0, order))],
        out_specs=pl.BlockSpec((1, HY_KG, 2 * n2, c), lambda g, bb: (bb, g, 0, 0)),
        out_shape=jax.ShapeDtypeStruct(a4.shape, BF16),
        compiler_params=_cparams(("arbitrary", "arbitrary")),
        name="hyena_stage2",
    )(f2t, g2t, a4, kspec)
    return out.reshape(a5.shape)


def _hy2_s3_kernel(g_ref, c_ref, z_ref, x_ref, b_ref, o_ref):
    k1n = c_ref.shape[1]
    n1h, _, c = z_ref.shape[1:]
    for g in range(HY_GROUPS):
        cols = slice(g * HY_J, (g + 1) * HY_J)
        cc = c_ref[0, :, :, cols, :].reshape(k1n * 2 * HY_J, c)
        conv = _bdot(g_ref[...], cc)
        z = z_ref[0, :, cols, :].reshape(n1h * HY_J, c).astype(F32)
        x = x_ref[0, :, cols, :].reshape(n1h * HY_J, c).astype(F32)
        o_ref[0, :, cols, :] = (x * (conv + z * b_ref[...])).astype(BF16).reshape(n1h, HY_J, c)


def _hy2_stage3(g1big, c5, vxx, zwhich, z5, xwhich, bias):
    b, k1n, _, n2, c = c5.shape
    n1h = k1n // 2
    w = HY_J * HY_GROUPS
    sel = lambda which: pl.BlockSpec((None, 1, n1h, w, c), lambda bb, g: (which, bb, 0, g, 0))
    row = pl.BlockSpec((1, n1h, w, c), lambda bb, g: (bb, 0, g, 0))
    return pl.pallas_call(
        _hy2_s3_kernel,
        grid=(b, n2 // w),
        in_specs=[pl.BlockSpec(g1big.shape, lambda bb, g: (0, 0)),
                  pl.BlockSpec((1, k1n, 2, w, c), lambda bb, g: (bb, 0, 0, g, 0)),
                  sel(zwhich) if z5 is None else row, sel(xwhich),
                  pl.BlockSpec((1, c), lambda bb, g: (0, 0))],
        out_specs=row,
        out_shape=jax.ShapeDtypeStruct((b, n1h, n2, c), BF16),
        compiler_params=_cparams(("arbitrary", "arbitrary")),
        name="hyena_stage3",
    )(g1big, c5, vxx if z5 is None else z5, vxx, bias)


def _hyena(u, consts, conv_w, conv_b, fw1, fb1, fw2, fb2, fw3, fb3, hbias):
    emb, win, tabs = consts
    filt = _hy_filters(emb, win, fw1, fb1, fw2, fb2, fw3, fb3)
    vxx = _short_conv(u, conv_w, conv_b)
    g, b, l, c = vxx.shape
    if len(tabs) == 3:
        fmat, gmat, tf = tabs
        kspec = _hy_spectrum(fmat, filt, tf)
        v, x1, x2 = vxx[0], vxx[1], vxx[2]
        z = _hy_inverse(gmat, _hy_forward(fmat, v, kspec, 0, tf), v, x1, hbias[0:1], tf)
        return _hy_inverse(gmat, _hy_forward(fmat, z, kspec, 1, tf), z, x2, hbias[1:2], tf)
    f1big, g1big, f2t, g2t = tabs
    n2 = l // HY_N1
    filt5 = filt.reshape(2 * HY_ORDER, HY_N1, n2, c)
    kspec = _hy2_filter_spectrum(f2t, _hy2_stage1(f1big, None, 0, filt5))
    vxx5 = vxx.reshape(g, b, HY_N1, n2, c)
    c5 = _hy2_stage2(f2t, g2t, _hy2_stage1(f1big, vxx5, 0, None), kspec, 0)
    z5 = _hy2_stage3(g1big, c5, vxx5, 0, None, 1, hbias[0:1])
    c5 = _hy2_stage2(f2t, g2t, _hy2_stage1(f1big, vxx5, 0, z5), kspec, 1)
    return _hy2_stage3(g1big, c5, vxx5, 0, z5, 2, hbias[1:2]).reshape(b, l, c)


def _outproj_kernel(om_ref, od_ref, oh_ref, wm_ref, wd_ref, wh_ref, bo_ref, x_ref, g_ref, lg_ref, lb_ref,
                    o_ref, *, alpha):
    y = _bdot(om_ref[0], wm_ref[...]) + _bdot(od_ref[0], wd_ref[...]) + _bdot(oh_ref[0], wh_ref[...])
    z = alpha * x_ref[0] + g_ref[0] * (y + bo_ref[...])
    o_ref[0] = _ln(z) * lg_ref[...] + lb_ref[...]


def _out_proj(om, od, oh, wm, wd, wh, bo, x, gate, lg, lb, alpha, tm):
    b, t, d = x.shape
    per_batch = gate.shape[0] == b and b > 1
    gmap = (lambda bb, i: (bb, 0, 0)) if per_batch else (lambda bb, i: (0, 0, 0))
    const = lambda bb, i: (0, 0)
    row = lambda bb, i: (bb, i, 0)
    return pl.pallas_call(
        functools.partial(_outproj_kernel, alpha=alpha),
        grid=(b, t // tm),
        in_specs=[pl.BlockSpec((1, tm, om.shape[2]), row), pl.BlockSpec((1, tm, od.shape[2]), row),
                  pl.BlockSpec((1, tm, oh.shape[2]), row),
                  pl.BlockSpec(wm.shape, const), pl.BlockSpec(wd.shape, const), pl.BlockSpec(wh.shape, const),
                  pl.BlockSpec((1, d), const), pl.BlockSpec((1, tm, d), row), pl.BlockSpec((1, 1, d), gmap),
                  pl.BlockSpec((1, d), const), pl.BlockSpec((1, d), const)],
        out_specs=pl.BlockSpec((1, tm, d), row),
        out_shape=jax.ShapeDtypeStruct((b, t, d), F32),
        compiler_params=_cparams(("arbitrary", "arbitrary")),
        name="out_proj",
    )(om, od, oh, wm, wd, wh, bo, x, gate, lg, lb)


def _ffn_kernel(x_ref, sh_ref, sc_ref, g_ref, w1_ref, b1_ref, w2_ref, b2_ref, lg_ref, lb_ref, o_ref,
                h_scr, acc_scr, *, alpha):
    j = pl.program_id(2)

    @pl.when(j == 0)
    def _():
        h_scr[...] = (_ln(x_ref[0]) * (1.0 + sc_ref[0]) + sh_ref[0]).astype(BF16)
        acc_scr[...] = jnp.zeros_like(acc_scr)

    a = jnp.maximum(_bdot(h_scr[...], w1_ref[...]) + b1_ref[...], 0.0)
    acc_scr[...] += _bdot((a * a).astype(BF16), w2_ref[...])

    @pl.when(j == pl.num_programs(2) - 1)
    def _():
        z = alpha * x_ref[0] + g_ref[0] * (acc_scr[...] + b2_ref[...])
        o_ref[0] = _ln(z) * lg_ref[...] + lb_ref[...]


def _ffn(x, sh, sc, gate, w1, b1, w2, b2, lg, lb, alpha, tm, tf):
    b, t, d = x.shape
    dff = w1.shape[1]
    per_batch = gate.shape[0] == b and b > 1
    gmap = (lambda bb, i, j: (bb, 0, 0)) if per_batch else (lambda bb, i, j: (0, 0, 0))
    const = lambda bb, i, j: (0, 0)
    row = lambda bb, i, j: (bb, i, 0)
    return pl.pallas_call(
        functools.partial(_ffn_kernel, alpha=alpha),
        grid=(b, t // tm, dff // tf),
        in_specs=[pl.BlockSpec((1, tm, d), row),
                  pl.BlockSpec((1, 1, d), gmap), pl.BlockSpec((1, 1, d), gmap), pl.BlockSpec((1, 1, d), gmap),
                  pl.BlockSpec((d, tf), lambda bb, i, j: (0, j)), pl.BlockSpec((1, tf), lambda bb, i, j: (0, j)),
                  pl.BlockSpec((tf, d), lambda bb, i, j: (j, 0)), pl.BlockSpec((1, d), const),
                  pl.BlockSpec((1, d), const), pl.BlockSpec((1, d), const)],
        out_specs=pl.BlockSpec((1, tm, d), row),
        out_shape=jax.ShapeDtypeStruct((b, t, d), F32),
        scratch_shapes=[pltpu.VMEM((tm, d), BF16), pltpu.VMEM((tm, d), F32)],
        compiler_params=_cparams(("arbitrary", "arbitrary", "arbitrary")),
        name="ffn",
    )(x, sh, sc, gate, w1, b1, w2, b2, lg, lb)


def _rope_lane_plan(starts, dims):
    h = dims // 2
    inv = ROPE_BASE ** (-(np.arange(h // 2, dtype=np.float64) * (2.0 / h)))
    fr, fc, lo, hi = (np.zeros(LANE, np.float32) for _ in range(4))
    for st in starts:
        for axis_i, f in enumerate((fr, fc)):
            base = st + axis_i * h
            f[base:base + h // 2] = inv
            f[base + h // 2:base + h] = inv
            lo[base:base + h // 2] = 1.0
            hi[base + h // 2:base + h] = 1.0
    return fr, fc, lo, hi


def _rope_tables(t_len, roped):
    pos_t = jnp.arange(t_len)
    scale = 1.0 if roped else 0.0
    row = (pos_t // GRID_W).astype(F32)[:, None] * scale
    col = (pos_t % GRID_W).astype(F32)[:, None] * scale
    out = []
    for starts, dims in (([MLA_NOPE], MLA_ROPE), ([0, LANE // 2], DIFF_DIM)):
        fr, fc, lo, hi = _rope_lane_plan(starts, dims)
        ang = row * fr[None, :] + col * fc[None, :]
        sin = jnp.sin(ang)
        out += [jnp.cos(ang), -sin * lo[None, :], sin * hi[None, :]]
    return jnp.concatenate(out, axis=1)


def _prep_weights(w_in, w_q_up, w_kv_up, w_out, diff_lambda, diff_subln):
    depth, d, _ = w_in.shape
    s1, s2 = IN_MLA, IN_MLA + IN_DIFF
    pad_last = lambda a, n: jnp.pad(a, [(0, 0)] * (a.ndim - 1) + [(0, n - a.shape[-1])])
    kr = jnp.pad(w_in[..., C_KR:s1], ((0, 0), (0, 0), (MLA_NOPE, LANE - MLA_NOPE - MLA_ROPE)))
    dq = pad_last(w_in[..., s1:s1 + N_DQ].reshape(depth, d, DIFF_HEADS, 2, DIFF_DIM), LANE // 2)
    dk = pad_last(w_in[..., s1 + N_DQ:s1 + 2 * N_DQ].reshape(depth, d, DIFF_HEADS, 2, DIFF_DIM), LANE // 2)
    dv = pad_last(w_in[..., s1 + 2 * N_DQ:s2].reshape(depth, d, DIFF_HEADS, DIFF_V), LANE)
    win = jnp.concatenate([w_in[..., :C_KR], kr, dq.reshape(depth, d, W_D), dk.reshape(depth, d, W_D),
                           dv.reshape(depth, d, W_D), w_in[..., s2:]], axis=-1).astype(BF16)
    wq = pad_last(w_q_up.reshape(depth, MLA_Q_RANK, MLA_HEADS, MLA_NOPE + MLA_ROPE), LANE)
    wq = wq.reshape(depth, MLA_Q_RANK, W_QM).astype(BF16)
    kv = w_kv_up.reshape(depth, MLA_KV_RANK, MLA_HEADS, MLA_NOPE + MLA_V)
    wk = pad_last(kv[..., :MLA_NOPE], LANE).reshape(depth, MLA_KV_RANK, W_QM)
    wv = kv[..., MLA_NOPE:].reshape(depth, MLA_KV_RANK, W_VM)
    wkv = jnp.concatenate([wk, wv], axis=-1).astype(BF16)
    wo_m = w_out[:, :W_VM].astype(BF16)
    wo_d = w_out[:, W_VM:W_VM + DIFF_HEADS * DIFF_V].reshape(depth, DIFF_HEADS, DIFF_V, -1)
    wo_d = jnp.pad(wo_d, ((0, 0), (0, 0), (0, LANE - DIFF_V), (0, 0))).reshape(depth, W_D, -1).astype(BF16)
    wo_h = w_out[:, W_VM + DIFF_HEADS * DIFF_V:].astype(BF16)
    lam = pad_last(diff_lambda.astype(F32), LANE)
    subln = pad_last(diff_subln.astype(F32), LANE)[:, None, :]
    return win, wq, wkv, wo_m, wo_d, wo_h, lam, subln


def kernel(x, c, ctx, c_ctx, w_mod, b_mod, w_in, mla_q_norm, w_q_up, mla_kv_norm, w_kv_up, diff_lambda, diff_subln, hy_conv_w, hy_conv_b, hy_fw1, hy_fb1, hy_fw2, hy_fb2, hy_fw3, hy_fb3, hy_bias, w_out, b_out, ln1_g, ln1_b, w_ff1, b_ff1, w_ff2, b_ff2, ln2_g, ln2_b):
    bsz, seq, d = x.shape
    n_ctx = ctx.shape[1]
    depth = w_in.shape[0]
    alpha = (2.0 * depth) ** 0.25

    rpad = -(bsz + 1) % 8
    cc = jnp.concatenate([c, c_ctx[None, :], jnp.zeros((rpad, d), F32)], axis=0)
    mod_all = _modulation(cc, w_mod, b_mod)

    win, wq, wkv, wo_m, wo_d, wo_h, lam_p, subln = _prep_weights(w_in, w_q_up, w_kv_up, w_out, diff_lambda, diff_subln)
    w1 = w_ff1.astype(BF16)
    w2 = w_ff2.astype(BF16)
    fw1 = jnp.pad(hy_fw1, ((0, 0), (0, LANE - HY_EMB), (0, 0)))
    tab_x = _rope_tables(seq, True)
    tab_c = _rope_tables(n_ctx, False)
    def hy_tables(l):
        if l >= 1024 and l % (HY_N1 * HY_J * HY_GROUPS) == 0:
            return _hy2_tables(l)
        return _dft_tables(l, min(512, l)) + (min(512, l),)

    hyc_x = _hy_consts(seq) + (hy_tables(seq),)
    hyc_c = _hy_consts(n_ctx) + (hy_tables(n_ctx),)
    tm_x, tm_c = min(512, seq), min(512, n_ctx)
    tmf_x = min(1024, seq)
    tq_x, tq_c = min(256, seq), min(256, n_ctx)
    tff = min(1024, w1.shape[2])
    row2 = lambda a: a.reshape(1, -1)

    for layer in range(depth):
        need_ctx = layer < depth - 1
        lam_init = 0.8 - 0.6 * math.exp(-0.3 * layer)
        mod = mod_all[layer, :bsz].reshape(bsz, 1, N_MOD, d)
        modc = mod_all[layer, bsz:bsz + 1].reshape(1, 1, N_MOD, d)
        sh1, sc1, g1, sh2, sc2, g2 = [mod[:, :, i] for i in range(N_MOD)]
        csh1, csc1, cg1, csh2, csc2, cg2 = [modc[:, :, i] for i in range(N_MOD)]
        gq, gkv = row2(mla_q_norm[layer]), row2(mla_kv_norm[layer])
        hy_args = (hy_conv_w[layer], row2(hy_conv_b[layer]), fw1[layer], row2(hy_fb1[layer]), hy_fw2[layer],
                   row2(hy_fb2[layer]), hy_fw3[layer], row2(hy_fb3[layer]), hy_bias[layer])

        qm, km, vm, dq, dk, dv, uh = _in_proj(x, sh1, sc1, win[layer], gq, wq[layer], gkv, wkv[layer], tab_x, tm_x)
        cqm, ckm, cvm, cdq, cdk, cdv, cuh = _in_proj(ctx, csh1, csc1, win[layer], gq, wq[layer], gkv, wkv[layer],
                                                     tab_c, tm_c)
        om = _mla_attention(qm, [ckm, km], [cvm, vm], tq_x, MLA_HEADS // 2)
        od = _diff_attention(dq, [cdk, dk], [cdv, dv], lam_p[layer], subln[layer], lam_init, tq_x, DIFF_HEADS)
        oh = _hyena(uh, hyc_x, *hy_args)
        x = _out_proj(om, od, oh, wo_m[layer], wo_d[layer], wo_h[layer], row2(b_out[layer]), x, g1,
                      row2(ln1_g[layer]), row2(ln1_b[layer]), alpha, tm_x)
        x = _ffn(x, sh2, sc2, g2, w1[layer], row2(b_ff1[layer]), w2[layer], row2(b_ff2[layer]),
                 row2(ln2_g[layer]), row2(ln2_b[layer]), alpha, tmf_x, tff)
        if need_ctx:
            com = _mla_attention(cqm, [ckm], [cvm], tq_c, MLA_HEADS // 2)
            cod = _diff_attention(cdq, [cdk], [cdv], lam_p[layer], subln[layer], lam_init, tq_c, DIFF_HEADS)
            coh = _hyena(cuh, hyc_c, *hy_args)
            ctx = _out_proj(com, cod, coh, wo_m[layer], wo_d[layer], wo_h[layer], row2(b_out[layer]), ctx, cg1,
                            row2(ln1_g[layer]), row2(ln1_b[layer]), alpha, tm_c)
            ctx = _ffn(ctx, csh2, csc2, cg2, w1[layer], row2(b_ff1[layer]), w2[layer], row2(b_ff2[layer]),
                       row2(ln2_g[layer]), row2(ln2_b[layer]), alpha, tm_c, tff)
    return x
```

```python
import functools
import math

import jax
import numpy as np
import jax.numpy as jnp
from jax import lax
from jax.experimental import pallas as pl
from jax.experimental.pallas import tpu as pltpu

F32 = jnp.float32
BF16 = jnp.bfloat16
HI = lax.Precision.HIGHEST

GRID_W = 64
ROPE_BASE = 10000.0
LN_EPS = 1e-6
MLA_HEADS, MLA_NOPE, MLA_ROPE, MLA_V = 6, 64, 32, 64
MLA_Q_RANK, MLA_KV_RANK = 256, 128
DIFF_HEADS, DIFF_DIM = 4, 48
DIFF_V = 2 * DIFF_DIM
HY_CH, HY_ORDER, HY_BANDS = 256, 2, 16
HY_EMB = 1 + 2 * HY_BANDS
HY_TARGET, HY_FAST_DECAY_PCT, HY_SLOW_DECAY_PCT = 1e-2, 0.3, 1.5
HY_MIN_RATE = -math.log(HY_TARGET) / HY_SLOW_DECAY_PCT
HY_MAX_RATE = -math.log(HY_TARGET) / HY_FAST_DECAY_PCT
N_MOD = 6

LOG2E = 1.0 / math.log(2.0)
LANE = 128
VMEM_LIMIT = 56 * 1024 * 1024

IN_MLA = MLA_Q_RANK + MLA_KV_RANK + MLA_ROPE
N_DQ = DIFF_HEADS * 2 * DIFF_DIM
IN_DIFF = 2 * N_DQ + DIFF_HEADS * DIFF_V
W_QM = MLA_HEADS * LANE
W_VM = MLA_HEADS * MLA_V
W_D = DIFF_HEADS * LANE
W_HY = (HY_ORDER + 1) * HY_CH
C_CQ, C_CKV, C_KR = 0, MLA_Q_RANK, MLA_Q_RANK + MLA_KV_RANK
C_DQ = C_KR + LANE
C_DK = C_DQ + W_D
C_DV = C_DK + W_D
C_HY = C_DV + W_D
W_IN = C_HY + W_HY


def _cparams(sem):
    return pltpu.CompilerParams(dimension_semantics=sem, vmem_limit_bytes=VMEM_LIMIT)


def _ln(x):
    mu = jnp.mean(x, -1, keepdims=True)
    xc = x - mu
    var = jnp.mean(xc * xc, -1, keepdims=True)
    return xc * lax.rsqrt(var + LN_EPS)


def _bdot(a, b):
    return jnp.dot(a, b, preferred_element_type=F32)


def _mod_kernel(c_ref, w_ref, b_ref, o_ref):
    c = c_ref[...]
    s = c / (1.0 + jnp.exp(-c))
    o_ref[...] = jnp.dot(s, w_ref[...], preferred_element_type=F32, precision=HI) + b_ref[...]


def _modulation(cc, w_mod, b_mod):
    depth, d, n = w_mod.shape
    r = cc.shape[0]
    tn = 1024
    return pl.pallas_call(
        _mod_kernel,
        grid=(depth, n // tn),
        in_specs=[pl.BlockSpec((r, d), lambda l, j: (0, 0)),
                  pl.BlockSpec((None, d, tn), lambda l, j: (l, 0, j)),
                  pl.BlockSpec((None, 1, tn), lambda l, j: (l, 0, j))],
        out_specs=pl.BlockSpec((None, r, tn), lambda l, j: (l, 0, j)),
        out_shape=jax.ShapeDtypeStruct((depth, r, n), F32),
        compiler_params=_cparams(("arbitrary", "arbitrary")),
        name="modulation",
    )(cc, w_mod, b_mod.reshape(depth, 1, n))


def _rope(x, c, sa, sb, half):
    return x * c + pltpu.roll(x, LANE - half, 1) * sa + pltpu.roll(x, half, 1) * sb


def _inproj_kernel(x_ref, sh_ref, sc_ref, win_ref, gq_ref, wq_ref, gkv_ref, wkv_ref, tab_ref,
                   qm_ref, km_ref, vm_ref, dq_ref, dk_ref, dv_ref, hy_ref):
    x = x_ref[0]
    h = _ln(x) * (1.0 + sc_ref[0]) + sh_ref[0]
    p = _bdot(h.astype(BF16), win_ref[...])
    tab = tab_ref[...]
    cm, sam, sbm = tab[:, 0:LANE], tab[:, LANE:2 * LANE], tab[:, 2 * LANE:3 * LANE]
    cd, sad, sbd = tab[:, 3 * LANE:4 * LANE], tab[:, 4 * LANE:5 * LANE], tab[:, 5 * LANE:6 * LANE]

    cq = p[:, C_CQ:C_CQ + MLA_Q_RANK]
    qn = cq * lax.rsqrt(jnp.mean(cq * cq, -1, keepdims=True) + LN_EPS) * gq_ref[...]
    q = _bdot(qn.astype(BF16), wq_ref[...])
    ckv = p[:, C_CKV:C_CKV + MLA_KV_RANK]
    kvn = ckv * lax.rsqrt(jnp.mean(ckv * ckv, -1, keepdims=True) + LN_EPS) * gkv_ref[...]
    kv = _bdot(kvn.astype(BF16), wkv_ref[...])
    kr = _rope(p[:, C_KR:C_KR + LANE], cm, sam, sbm, MLA_ROPE // 4)
    scale_m = LOG2E * (MLA_NOPE + MLA_ROPE) ** -0.5
    for hd in range(MLA_HEADS):
        sl = slice(hd * LANE, (hd + 1) * LANE)
        qm_ref[0, :, sl] = (_rope(q[:, sl], cm, sam, sbm, MLA_ROPE // 4) * scale_m).astype(BF16)
        km_ref[0, :, sl] = (kv[:, sl] + kr).astype(BF16)
    vm_ref[0] = kv[:, W_QM:W_QM + W_VM].astype(BF16)
    scale_d = LOG2E * DIFF_DIM ** -0.5
    for hd in range(DIFF_HEADS):
        sl = slice(hd * LANE, (hd + 1) * LANE)
        dq = p[:, C_DQ + hd * LANE:C_DQ + (hd + 1) * LANE]
        dk = p[:, C_DK + hd * LANE:C_DK + (hd + 1) * LANE]
        dq_ref[0, :, sl] = (_rope(dq, cd, sad, sbd, DIFF_DIM // 4) * scale_d).astype(BF16)
        dk_ref[0, :, sl] = _rope(dk, cd, sad, sbd, DIFF_DIM // 4).astype(BF16)
    dv_ref[0] = p[:, C_DV:C_DV + W_D].astype(BF16)
    hy_ref[0] = p[:, C_HY:C_HY + W_HY].astype(BF16)


def _in_proj(x, sh, sc, win, gq, wq, gkv, wkv, tab, tm):
    b, t, d = x.shape
    per_batch = sh.shape[0] == b and b > 1
    mod_map = (lambda i, bb: (bb, 0, 0)) if per_batch else (lambda i, bb: (0, 0, 0))
    const = lambda i, bb: (0, 0)
    widths = (W_QM, W_QM, W_VM, W_D, W_D, W_D, W_HY)
    return pl.pallas_call(
        _inproj_kernel,
        grid=(t // tm, b),
        in_specs=[pl.BlockSpec((1, tm, d), lambda i, bb: (bb, i, 0)),
                  pl.BlockSpec((1, 1, d), mod_map),
                  pl.BlockSpec((1, 1, d), mod_map),
                  pl.BlockSpec(win.shape, const),
                  pl.BlockSpec(gq.shape, const),
                  pl.BlockSpec(wq.shape, const),
                  pl.BlockSpec(gkv.shape, const),
                  pl.BlockSpec(wkv.shape, const),
                  pl.BlockSpec((tm, 6 * LANE), lambda i, bb: (i, 0))],
        out_specs=[pl.BlockSpec((1, tm, w), lambda i, bb: (bb, i, 0)) for w in widths],
        out_shape=[jax.ShapeDtypeStruct((b, t, w), BF16) for w in widths],
        compiler_params=_cparams(("arbitrary", "arbitrary")),
        name="in_proj",
    )(x, sh, sc, win, gq, wq, gkv, wkv, tab)


NEG_BIG = -1e30


def _lane_fold(x, op):
    r = x[:, 0:LANE]
    for i in range(1, x.shape[1] // LANE):
        r = op(r, x[:, i * LANE:(i + 1) * LANE])
    return r


def _score_pass(q, parts, ksl, s_refs):
    mx = jnp.full((q.shape[0], LANE), NEG_BIG, F32)
    for (k_ref, _, n, tk), s_ref in zip(parts, s_refs):
        for j in range(n // tk):
            k = k_ref[0, j * tk:(j + 1) * tk, ksl]
            s = lax.dot_general(q, k, (((1,), (1,)), ((), ())), preferred_element_type=F32)
            s_ref[j] = s
            mx = jnp.maximum(mx, _lane_fold(s, jnp.maximum))
    return jnp.max(mx, axis=-1, keepdims=True)


def _softmax_pv(q, parts, ksl, vsl, s_refs):
    m = _score_pass(q, parts, ksl, s_refs)
    ls = jnp.zeros((q.shape[0], LANE), F32)
    acc = jnp.zeros((q.shape[0], LANE), F32)
    for (_, v_ref, n, tk), s_ref in zip(parts, s_refs):
        for j in range(n // tk):
            p = jnp.exp2(s_ref[j] - m)
            ls = ls + _lane_fold(p, jnp.add)
            acc = acc + _bdot(p.astype(BF16), v_ref[0, j * tk:(j + 1) * tk, vsl])
    return acc * (1.0 / jnp.sum(ls, axis=-1, keepdims=True))


def _mla_attn_kernel(*refs, part_shapes, n_pairs):
    np_ = len(part_shapes)
    q_ref = refs[0]
    k_refs = refs[1:1 + np_]
    v_refs = refs[1 + np_:1 + 2 * np_]
    o_ref = refs[1 + 2 * np_]
    s_sets = (refs[2 + 2 * np_:2 + 3 * np_], refs[2 + 3 * np_:])
    parts = [(k_refs[i], v_refs[i]) + part_shapes[i] for i in range(np_)]
    for pr in range(n_pairs):
        outs = []
        for hh in range(2):
            hd = 2 * pr + hh
            sl = slice(hd * LANE, (hd + 1) * LANE)
            outs.append(_softmax_pv(q_ref[0, :, sl], parts, sl, slice(pr * LANE, (pr + 1) * LANE), s_sets[hh]))
        lane = lax.broadcasted_iota(jnp.int32, outs[0].shape, 1)
        o_ref[0, :, pr * LANE:(pr + 1) * LANE] = jnp.where(lane < MLA_V, outs[0], outs[1]).astype(BF16)


def _chunk(n):
    return n if n <= 512 else 512


def _mla_attention(q, ks, vs, tq, n_pairs):
    b, t, _ = q.shape
    part_shapes = tuple((k.shape[1], _chunk(k.shape[1])) for k in ks)
    in_specs = [pl.BlockSpec((1, tq, 2 * n_pairs * LANE), lambda bb, g, i: (bb, i, g))]
    in_specs += [pl.BlockSpec((1, k.shape[1], 2 * n_pairs * LANE), lambda bb, g, i: (bb, 0, g)) for k in ks]
    in_specs += [pl.BlockSpec((1, v.shape[1], n_pairs * LANE), lambda bb, g, i: (bb, 0, g)) for v in vs]
    scratch = [pltpu.VMEM((n // tk, tq, tk), F32) for n, tk in part_shapes] * 2
    return pl.pallas_call(
        functools.partial(_mla_attn_kernel, part_shapes=part_shapes, n_pairs=n_pairs),
        grid=(b, MLA_HEADS // (2 * n_pairs), t // tq),
        in_specs=in_specs,
        out_specs=pl.BlockSpec((1, tq, n_pairs * LANE), lambda bb, g, i: (bb, i, g)),
        out_shape=jax.ShapeDtypeStruct((b, t, W_VM), BF16),
        scratch_shapes=scratch,
        compiler_params=_cparams(("arbitrary", "arbitrary", "arbitrary")),
        name="mla_attention",
    )(q, *ks, *vs)


def _diff_attn_kernel(*refs, part_shapes, lam_init, n_heads):
    np_ = len(part_shapes)
    q_ref = refs[0]
    k_refs = refs[1:1 + np_]
    v_refs = refs[1 + np_:1 + 2 * np_]
    lam_ref, g_ref, o_ref = refs[1 + 2 * np_:4 + 2 * np_]
    s_sets = (refs[4 + 2 * np_:4 + 3 * np_], refs[4 + 3 * np_:])
    parts = [(k_refs[i], v_refs[i]) + part_shapes[i] for i in range(np_)]
    lp = lam_ref[...]
    lam = (jnp.exp(jnp.sum(lp[0:1] * lp[1:2], axis=-1, keepdims=True))
           - jnp.exp(jnp.sum(lp[2:3] * lp[3:4], axis=-1, keepdims=True)) + lam_init)
    for hd in range(n_heads):
        sl = slice(hd * LANE, (hd + 1) * LANE)
        q = q_ref[0, :, sl]
        lane = lax.broadcasted_iota(jnp.int32, q.shape, 1)
        zero = jnp.zeros_like(q)
        o1 = _softmax_pv(jnp.where(lane < LANE // 2, q, zero), parts, sl, sl, s_sets[0])
        o2 = _softmax_pv(jnp.where(lane >= LANE // 2, q, zero), parts, sl, sl, s_sets[1])
        o = o1 - lam * o2
        ms = jnp.sum(o * o, axis=-1, keepdims=True) * (1.0 / DIFF_V)
        o_ref[0, :, sl] = (o * lax.rsqrt(ms + LN_EPS) * g_ref[...] * (1.0 - lam_init)).astype(BF16)


def _diff_attention(q, ks, vs, lam_p, subln, lam_init, tq, n_heads):
    b, t, _ = q.shape
    part_shapes = tuple((k.shape[1], _chunk(k.shape[1])) for k in ks)
    hmap = lambda bb, g, i: (bb, 0, g)
    in_specs = [pl.BlockSpec((1, tq, n_heads * LANE), lambda bb, g, i: (bb, i, g))]
    in_specs += [pl.BlockSpec((1, k.shape[1], n_heads * LANE), hmap) for k in ks]
    in_specs += [pl.BlockSpec((1, v.shape[1], n_heads * LANE), hmap) for v in vs]
    in_specs += [pl.BlockSpec(lam_p.shape, lambda bb, g, i: (0, 0)),
                 pl.BlockSpec(subln.shape, lambda bb, g, i: (0, 0))]
    scratch = [pltpu.VMEM((n // tk, tq, tk), F32) for n, tk in part_shapes] * 2
    return pl.pallas_call(
        functools.partial(_diff_attn_kernel, part_shapes=part_shapes, lam_init=lam_init, n_heads=n_heads),
        grid=(b, DIFF_HEADS // n_heads, t // tq),
        in_specs=in_specs,
        out_specs=pl.BlockSpec((1, tq, n_heads * LANE), lambda bb, g, i: (bb, i, g)),
        out_shape=jax.ShapeDtypeStruct((b, t, W_D), BF16),
        scratch_shapes=scratch,
        compiler_params=_cparams(("arbitrary", "arbitrary", "arbitrary")),
        name="diff_attention",
    )(q, *ks, *vs, lam_p, subln)


def _hyfilt_kernel(emb_ref, win_ref, fw1_ref, fb1_ref, fw2_ref, fb2_ref, fwf_ref, fbf_ref, fwb_ref, fbb_ref,
                   d_ref, h_scr):
    @pl.when(pl.program_id(0) == 0)
    def _():
        h1 = jnp.sin(jnp.dot(emb_ref[...], fw1_ref[...], preferred_element_type=F32, precision=HI) + fb1_ref[...])
        h_scr[...] = jnp.sin(jnp.dot(h1, fw2_ref[...], preferred_element_type=F32, precision=HI) + fb2_ref[...])

    h = h_scr[...]
    w = win_ref[...]
    fwd = (jnp.dot(h, fwf_ref[...], preferred_element_type=F32, precision=HI) + fbf_ref[...]) * w
    bwd = (jnp.dot(h, fwb_ref[...], preferred_element_type=F32, precision=HI) + fbb_ref[...]) * w
    row = lax.broadcasted_iota(jnp.int32, bwd.shape, 0)
    bwd = jnp.where(row == 0, 0.0, bwd)
    norm = jnp.sum(jnp.abs(fwd), axis=0, keepdims=True) + jnp.sum(jnp.abs(bwd), axis=0, keepdims=True)
    inv = 1.0 / norm
    d_ref[0] = ((fwd + bwd) * inv).astype(BF16)
    d_ref[1] = ((fwd - bwd) * inv).astype(BF16)


def _hy_filters(emb, win, fw1, fb1, fw2, fb2, fw3, fb3):
    l = emb.shape[0]
    hid = fw2.shape[0]
    nblk = HY_CH // LANE
    const = lambda g: (0, 0)
    fcol = lambda g: (0, (g // nblk) * 2 * nblk + g % nblk)
    bcol = lambda g: (0, (g // nblk) * 2 * nblk + nblk + g % nblk)
    return pl.pallas_call(
        _hyfilt_kernel,
        grid=(HY_ORDER * nblk,),
        in_specs=[pl.BlockSpec(emb.shape, const),
                  pl.BlockSpec((l, LANE), lambda g: (0, g % nblk)),
                  pl.BlockSpec(fw1.shape, const), pl.BlockSpec(fb1.shape, const),
                  pl.BlockSpec(fw2.shape, const), pl.BlockSpec(fb2.shape, const),
                  pl.BlockSpec((hid, LANE), fcol), pl.BlockSpec((1, LANE), fcol),
                  pl.BlockSpec((hid, LANE), bcol), pl.BlockSpec((1, LANE), bcol)],
        out_specs=pl.BlockSpec((2, None, l, LANE), lambda g: (0, g // nblk, 0, g % nblk)),
        out_shape=jax.ShapeDtypeStruct((2, HY_ORDER, l, HY_CH), BF16),
        scratch_shapes=[pltpu.VMEM((l, hid), F32)],
        compiler_params=_cparams(("arbitrary",)),
        name="hyena_filters",
    )(emb, win, fw1, fb1, fw2, fb2, fw3, fb3, fw3, fb3)


def _hyspec_kernel(f_ref, s_ref, d_ref, o_ref):
    tf = f_ref.shape[0] // 2
    o_ref[0:tf, :] = _bdot(f_ref[0:tf, :], s_ref[...])
    o_ref[tf:2 * tf, :] = _bdot(f_ref[tf:2 * tf, :], d_ref[...])


def _hy_spectrum(fmat, filt, tf):
    rows, l = fmat.shape
    return pl.pallas_call(
        _hyspec_kernel,
        grid=(rows // (2 * tf), HY_ORDER),
        in_specs=[pl.BlockSpec((2 * tf, l), lambda j, o: (j, 0)),
                  pl.BlockSpec((None, None, l, HY_CH), lambda j, o: (0, o, 0, 0)),
                  pl.BlockSpec((None, None, l, HY_CH), lambda j, o: (1, o, 0, 0))],
        out_specs=pl.BlockSpec((2 * tf, HY_CH), lambda j, o: (j, o)),
        out_shape=jax.ShapeDtypeStruct((rows, HY_ORDER * HY_CH), F32),
        compiler_params=_cparams(("arbitrary", "arbitrary")),
        name="hyena_filter_spectrum",
    )(fmat, filt, filt)


def _shortconv_kernel(u_ref, w_ref, b_ref, o_ref):
    u = u_ref[0].astype(F32)
    l = u.shape[0]
    row = lax.broadcasted_iota(jnp.int32, u.shape, 0)
    up = jnp.where(row == 0, 0.0, pltpu.roll(u, 1, 0))
    un = jnp.where(row == l - 1, 0.0, pltpu.roll(u, l - 1, 0))
    w = w_ref[...]
    o_ref[0] = (up * w[0:1] + u * w[1:2] + un * w[2:3] + b_ref[...]).astype(BF16)


def _short_conv(u, w, bias):
    b, l, _ = u.shape
    g = HY_ORDER + 1
    return pl.pallas_call(
        _shortconv_kernel,
        grid=(b, g),
        in_specs=[pl.BlockSpec((1, l, HY_CH), lambda bb, gg: (bb, 0, gg)),
                  pl.BlockSpec((3, HY_CH), lambda bb, gg: (0, gg)),
                  pl.BlockSpec((1, HY_CH), lambda bb, gg: (0, gg))],
        out_specs=pl.BlockSpec((None, 1, l, HY_CH), lambda bb, gg: (gg, bb, 0, 0)),
        out_shape=jax.ShapeDtypeStruct((g, b, l, HY_CH), BF16),
        compiler_params=_cparams(("arbitrary", "arbitrary")),
        name="hyena_short_conv",
    )(u, w, bias)


def _hyfwd_kernel(f_ref, z_ref, k_ref, y_ref):
    tf = f_ref.shape[0] // 2
    s = _bdot(f_ref[...], z_ref[0])
    sr, si = s[0:tf], s[tf:2 * tf]
    kr, ki = k_ref[0:tf, :], k_ref[tf:2 * tf, :]
    y_ref[0, 0:tf, :] = (sr * kr - si * ki).astype(BF16)
    y_ref[0, tf:2 * tf, :] = (sr * ki + si * kr).astype(BF16)


def _hy_forward(fmat, z, kspec, order, tf):
    rows, l = fmat.shape
    b = z.shape[0]
    return pl.pallas_call(
        _hyfwd_kernel,
        grid=(rows // (2 * tf), b),
        in_specs=[pl.BlockSpec((2 * tf, l), lambda j, bb: (j, 0)),
                  pl.BlockSpec((1, l, HY_CH), lambda j, bb: (bb, 0, 0)),
                  pl.BlockSpec((2 * tf, HY_CH), lambda j, bb: (j, order))],
        out_specs=pl.BlockSpec((1, 2 * tf, HY_CH), lambda j, bb: (bb, j, 0)),
        out_shape=jax.ShapeDtypeStruct((b, rows, HY_CH), BF16),
        compiler_params=_cparams(("arbitrary", "arbitrary")),
        name="hyena_dft_forward",
    )(fmat, z, kspec)


def _hyinv_kernel(g_ref, y_ref, z_ref, x_ref, b_ref, o_ref):
    conv = _bdot(g_ref[...], y_ref[0])
    z = z_ref[0].astype(F32)
    o_ref[0] = (x_ref[0].astype(F32) * (conv + z * b_ref[...])).astype(BF16)


def _hy_inverse(gmat, y, z, gate, bias, tt):
    l, rows = gmat.shape
    b = y.shape[0]
    return pl.pallas_call(
        _hyinv_kernel,
        grid=(l // tt, b),
        in_specs=[pl.BlockSpec((tt, rows), lambda i, bb: (i, 0)),
                  pl.BlockSpec((1, rows, HY_CH), lambda i, bb: (bb, 0, 0)),
                  pl.BlockSpec((1, tt, HY_CH), lambda i, bb: (bb, i, 0)),
                  pl.BlockSpec((1, tt, HY_CH), lambda i, bb: (bb, i, 0)),
                  pl.BlockSpec((1, HY_CH), lambda i, bb: (0, 0))],
        out_specs=pl.BlockSpec((1, tt, HY_CH), lambda i, bb: (bb, i, 0)),
        out_shape=jax.ShapeDtypeStruct((b, l, HY_CH), BF16),
        compiler_params=_cparams(("arbitrary", "arbitrary")),
        name="hyena_dft_inverse",
    )(gmat, y, z, gate, bias)


def _dft_tables(l, tf):
    k = jnp.arange(l, dtype=jnp.int32)
    n = jnp.arange(l, dtype=jnp.int32)
    ph = ((2 * k + 1)[:, None] * n[None, :]) % (4 * l)
    ang = ph.astype(F32) * (2.0 * math.pi / (4 * l))
    c = jnp.cos(ang).reshape(l // tf, 1, tf, l)
    s = (-jnp.sin(ang)).reshape(l // tf, 1, tf, l)
    f = jnp.concatenate([c, s], axis=1).reshape(2 * l, l)
    return f.astype(BF16), (f.T * (1.0 / l)).astype(BF16)


def _hy_consts(l):
    t = jnp.arange(l, dtype=F32)
    bands = jnp.arange(1, HY_BANDS + 1, dtype=F32)
    ang = (2.0 * math.pi / l) * t[:, None] * bands[None, :]
    emb = jnp.concatenate([(t / l)[:, None], jnp.cos(ang), jnp.sin(ang)], -1)
    emb = jnp.pad(emb, ((0, 0), (0, LANE - HY_EMB)))
    rates = jnp.linspace(HY_MIN_RATE, HY_MAX_RATE, HY_CH, dtype=F32)
    win = jnp.exp(-(t / l)[:, None] * rates[None, :])
    return emb, win


HY_N1 = 16
HY_J = 16
HY_GROUPS = 4
HY_KG = 4


def _hy2_tables(l):
    n1h, jj = HY_N1, HY_J
    n2, k1n = l // n1h, 2 * n1h
    k2n = n2 // 2
    k1 = jnp.arange(k1n, dtype=jnp.int32)
    ph1 = ((2 * k1 + 1)[:, None] * jnp.arange(n1h, dtype=jnp.int32)[None, :]) % (4 * n1h)
    a1 = ph1.astype(F32) * (2.0 * math.pi / (4 * n1h))
    f1 = jnp.stack([jnp.cos(a1), -jnp.sin(a1)], axis=1)
    f1big = jnp.einsum('krn,ab->kranb', f1, jnp.eye(jj, dtype=F32)).reshape(k1n * 2 * jj, n1h * jj)
    kk = k1[:, None] + k1n * jnp.arange(k2n, dtype=jnp.int32)[None, :]
    ph2 = ((2 * kk + 1)[:, :, None] * jnp.arange(n2, dtype=jnp.int32)[None, None, :]) % (4 * l)
    a2 = ph2.astype(F32) * (2.0 * math.pi / (4 * l))
    mr, mi = jnp.cos(a2), -jnp.sin(a2)
    f2t = jnp.concatenate([jnp.concatenate([mr, -mi], axis=2), jnp.concatenate([mi, mr], axis=2)], axis=1)
    return (f1big.astype(BF16), (f1big.T * (1.0 / l)).astype(BF16), f2t.astype(BF16),
            jnp.swapaxes(f2t, 1, 2).astype(BF16))


def _hy2_s1_kernel(f_ref, z_ref, o_ref):
    n1h, _, c = z_ref.shape[1:]
    k1n = o_ref.shape[1]
    for g in range(HY_GROUPS):
        cols = slice(g * HY_J, (g + 1) * HY_J)
        z = z_ref[0, :, cols, :].reshape(n1h * HY_J, c)
        a = _bdot(f_ref[...], z)
        o_ref[0, :, :, cols, :] = a.astype(BF16).reshape(k1n, 2, HY_J, c)


def _hy2_stage1(f1big, vxx, which, z5):
    src = vxx if z5 is None else z5
    b, n1h, n2, c = src.shape[-4:]
    k1n = 2 * n1h
    w = HY_J * HY_GROUPS
    if z5 is None:
        zspec = pl.BlockSpec((None, 1, n1h, w, c), lambda bb, g: (which, bb, 0, g, 0))
    else:
        zspec = pl.BlockSpec((1, n1h, w, c), lambda bb, g: (bb, 0, g, 0))
    return pl.pallas_call(
        _hy2_s1_kernel,
        grid=(b, n2 // w),
        in_specs=[pl.BlockSpec(f1big.shape, lambda bb, g: (0, 0)), zspec],
        out_specs=pl.BlockSpec((1, k1n, 2, w, c), lambda bb, g: (bb, 0, 0, g, 0)),
        out_shape=jax.ShapeDtypeStruct((b, k1n, 2, n2, c), BF16),
        compiler_params=_cparams(("arbitrary", "arbitrary")),
        name="hyena_stage1",
    )(f1big, src)


def _hy2_fspec_kernel(f_ref, s_ref, d_ref, o_ref):
    h = o_ref.shape[1] // 2
    for i in range(HY_KG):
        o_ref[i, 0:h, :] = _bdot(f_ref[i], s_ref[0, i])[0:h]
        o_ref[i, h:2 * h, :] = _bdot(f_ref[i], d_ref[0, i])[h:2 * h]


def _hy2_filter_spectrum(f2t, a5):
    nsig, k1n, _, n2, c = a5.shape
    a4 = a5.reshape(nsig, k1n, 2 * n2, c)
    return pl.pallas_call(
        _hy2_fspec_kernel,
        grid=(k1n // HY_KG, HY_ORDER),
        in_specs=[pl.BlockSpec((HY_KG, n2, 2 * n2), lambda g, o: (g, 0, 0)),
                  pl.BlockSpec((1, HY_KG, 2 * n2, c), lambda g, o: (o, g, 0, 0)),
                  pl.BlockSpec((1, HY_KG, 2 * n2, c), lambda g, o: (HY_ORDER + o, g, 0, 0))],
        out_specs=pl.BlockSpec((HY_KG, n2, c), lambda g, o: (g, 0, o)),
        out_shape=jax.ShapeDtypeStruct((k1n, n2, HY_ORDER * c), F32),
        compiler_params=_cparams(("arbitrary", "arbitrary")),
        name="hyena_filter_spectrum2",
    )(f2t, a4, a4)


def _hy2_s2_kernel(f_ref, g_ref, a_ref, k_ref, o_ref):
    for i in range(HY_KG):
        y = _bdot(f_ref[i], a_ref[0, i])
        h = y.shape[0] // 2
        yr, yi = y[:h], y[h:]
        kr, ki = k_ref[i, 0:h, :], k_ref[i, h:2 * h, :]
        p = jnp.concatenate([yr * kr - yi * ki, yr * ki + yi * kr], axis=0).astype(BF16)
        o_ref[0, i] = _bdot(g_ref[i], p).astype(BF16)


def _hy2_stage2(f2t, g2t, a5, kspec, order):
    b, k1n, _, n2, c = a5.shape
    a4 = a5.reshape(b, k1n, 2 * n2, c)
    out = pl.pallas_call(
        _hy2_s2_kernel,
        grid=(k1n // HY_KG, b),
        in_specs=[pl.BlockSpec((HY_KG, n2, 2 * n2), lambda g, bb: (g, 0, 0)),
                  pl.BlockSpec((HY_KG, 2 * n2, n2), lambda g, bb: (g, 0, 0)),
                  pl.BlockSpec((1, HY_KG, 2 * n2, c), lambda g, bb: (bb, g, 0, 0)),
                  pl.BlockSpec((HY_KG, n2, c), lambda g, bb: (g, 0, order))],
        out_specs=pl.BlockSpec((1, HY_KG, 2 * n2, c), lambda g, bb: (bb, g, 0, 0)),
        out_shape=jax.ShapeDtypeStruct(a4.shape, BF16),
        compiler_params=_cparams(("arbitrary", "arbitrary")),
        name="hyena_stage2",
    )(f2t, g2t, a4, kspec)
    return out.reshape(a5.shape)


def _hy2_s3_kernel(g_ref, c_ref, z_ref, x_ref, b_ref, *rest, chain):
    if chain:
        f_ref, o_ref, a_ref = rest
    else:
        (o_ref,) = rest
    k1n = c_ref.shape[1]
    n1h, _, c = z_ref.shape[1:]
    for g in range(HY_GROUPS):
        cols = slice(g * HY_J, (g + 1) * HY_J)
        cc = c_ref[0, :, :, cols, :].reshape(k1n * 2 * HY_J, c)
        conv = _bdot(g_ref[...], cc)
        z = z_ref[0, :, cols, :].reshape(n1h * HY_J, c).astype(F32)
        x = x_ref[0, :, cols, :].reshape(n1h * HY_J, c).astype(F32)
        o = (x * (conv + z * b_ref[...])).astype(BF16)
        o_ref[0, :, cols, :] = o.reshape(n1h, HY_J, c)
        if chain:
            a_ref[0, :, :, cols, :] = _bdot(f_ref[...], o).astype(BF16).reshape(k1n, 2, HY_J, c)


def _hy2_stage3(g1big, c5, vxx, zwhich, z5, xwhich, bias, f1big=None):
    b, k1n, _, n2, c = c5.shape
    n1h = k1n // 2
    w = HY_J * HY_GROUPS
    chain = f1big is not None
    const = lambda bb, g: (0, 0)
    sel = lambda which: pl.BlockSpec((None, 1, n1h, w, c), lambda bb, g: (which, bb, 0, g, 0))
    row = pl.BlockSpec((1, n1h, w, c), lambda bb, g: (bb, 0, g, 0))
    spec5 = pl.BlockSpec((1, k1n, 2, w, c), lambda bb, g: (bb, 0, 0, g, 0))
    out5 = jax.ShapeDtypeStruct((b, n1h, n2, c), BF16)
    in_specs = [pl.BlockSpec(g1big.shape, const), spec5, sel(zwhich) if z5 is None else row, sel(xwhich),
                pl.BlockSpec((1, c), const)]
    args = [g1big, c5, vxx if z5 is None else z5, vxx, bias]
    if chain:
        in_specs.append(pl.BlockSpec(f1big.shape, const))
        args.append(f1big)
    return pl.pallas_call(
        functools.partial(_hy2_s3_kernel, chain=chain),
        grid=(b, n2 // w),
        in_specs=in_specs,
        out_specs=[row, spec5] if chain else row,
        out_shape=[out5, jax.ShapeDtypeStruct(c5.shape, BF16)] if chain else out5,
        compiler_params=_cparams(("arbitrary", "arbitrary")),
        name="hyena_stage3",
    )(*args)


def _hyena(u, consts, conv_w, conv_b, fw1, fb1, fw2, fb2, fw3, fb3, hbias):
    emb, win, tabs = consts
    filt = _hy_filters(emb, win, fw1, fb1, fw2, fb2, fw3, fb3)
    vxx = _short_conv(u, conv_w, conv_b)
    g, b, l, c = vxx.shape
    if len(tabs) == 3:
        fmat, gmat, tf = tabs
        kspec = _hy_spectrum(fmat, filt, tf)
        v, x1, x2 = vxx[0], vxx[1], vxx[2]
        z = _hy_inverse(gmat, _hy_forward(fmat, v, kspec, 0, tf), v, x1, hbias[0:1], tf)
        return _hy_inverse(gmat, _hy_forward(fmat, z, kspec, 1, tf), z, x2, hbias[1:2], tf)
    f1big, g1big, f2t, g2t = tabs
    n2 = l // HY_N1
    filt5 = filt.reshape(2 * HY_ORDER, HY_N1, n2, c)
    kspec = _hy2_filter_spectrum(f2t, _hy2_stage1(f1big, None, 0, filt5))
    vxx5 = vxx.reshape(g, b, HY_N1, n2, c)
    c5 = _hy2_stage2(f2t, g2t, _hy2_stage1(f1big, vxx5, 0, None), kspec, 0)
    z5, a5 = _hy2_stage3(g1big, c5, vxx5, 0, None, 1, hbias[0:1], f1big)
    c5 = _hy2_stage2(f2t, g2t, a5, kspec, 1)
    return _hy2_stage3(g1big, c5, vxx5, 0, z5, 2, hbias[1:2]).reshape(b, l, c)


def _outproj_kernel(om_ref, od_ref, oh_ref, wm_ref, wd_ref, wh_ref, bo_ref, x_ref, g_ref, lg_ref, lb_ref,
                    o_ref, *, alpha):
    y = _bdot(om_ref[0], wm_ref[...]) + _bdot(od_ref[0], wd_ref[...]) + _bdot(oh_ref[0], wh_ref[...])
    z = alpha * x_ref[0] + g_ref[0] * (y + bo_ref[...])
    o_ref[0] = _ln(z) * lg_ref[...] + lb_ref[...]


def _out_proj(om, od, oh, wm, wd, wh, bo, x, gate, lg, lb, alpha, tm):
    b, t, d = x.shape
    per_batch = gate.shape[0] == b and b > 1
    gmap = (lambda bb, i: (bb, 0, 0)) if per_batch else (lambda bb, i: (0, 0, 0))
    const = lambda bb, i: (0, 0)
    row = lambda bb, i: (bb, i, 0)
    return pl.pallas_call(
        functools.partial(_outproj_kernel, alpha=alpha),
        grid=(b, t // tm),
        in_specs=[pl.BlockSpec((1, tm, om.shape[2]), row), pl.BlockSpec((1, tm, od.shape[2]), row),
                  pl.BlockSpec((1, tm, oh.shape[2]), row),
                  pl.BlockSpec(wm.shape, const), pl.BlockSpec(wd.shape, const), pl.BlockSpec(wh.shape, const),
                  pl.BlockSpec((1, d), const), pl.BlockSpec((1, tm, d), row), pl.BlockSpec((1, 1, d), gmap),
                  pl.BlockSpec((1, d), const), pl.BlockSpec((1, d), const)],
        out_specs=pl.BlockSpec((1, tm, d), row),
        out_shape=jax.ShapeDtypeStruct((b, t, d), F32),
        compiler_params=_cparams(("arbitrary", "arbitrary")),
        name="out_proj",
    )(om, od, oh, wm, wd, wh, bo, x, gate, lg, lb)


def _ffn_kernel(x_ref, sh_ref, sc_ref, g_ref, w1_ref, b1_ref, w2_ref, b2_ref, lg_ref, lb_ref, o_ref, *,
                alpha, tm, tf):
    nt = x_ref.shape[1] // tm
    rows = [slice(t * tm, (t + 1) * tm) for t in range(nt)]
    xs = [x_ref[0, r, :] for r in rows]
    hs = [(_ln(x) * (1.0 + sc_ref[0]) + sh_ref[0]).astype(BF16) for x in xs]
    accs = [None] * nt
    for c in range(w1_ref.shape[1] // tf):
        cols = slice(c * tf, (c + 1) * tf)
        for t in range(nt):
            a = jnp.maximum(_bdot(hs[t], w1_ref[:, cols]) + b1_ref[:, cols], 0.0)
            y = _bdot((a * a).astype(BF16), w2_ref[cols, :])
            accs[t] = y if accs[t] is None else accs[t] + y
    for t in range(nt):
        z = alpha * xs[t] + g_ref[0] * (accs[t] + b2_ref[...])
        o_ref[0, rows[t], :] = _ln(z) * lg_ref[...] + lb_ref[...]


def _ffn(x, sh, sc, gate, w1, b1, w2, b2, lg, lb, alpha, tm, nt, tf):
    b, t, d = x.shape
    per_batch = gate.shape[0] == b and b > 1
    gmap = (lambda bb, i: (bb, 0, 0)) if per_batch else (lambda bb, i: (0, 0, 0))
    const = lambda bb, i: (0, 0)
    row = lambda bb, i: (bb, i, 0)
    resident = pl.Buffered(1)
    return pl.pallas_call(
        functools.partial(_ffn_kernel, alpha=alpha, tm=tm, tf=tf),
        grid=(b, t // (tm * nt)),
        in_specs=[pl.BlockSpec((1, tm * nt, d), row),
                  pl.BlockSpec((1, 1, d), gmap), pl.BlockSpec((1, 1, d), gmap), pl.BlockSpec((1, 1, d), gmap),
                  pl.BlockSpec(w1.shape, const, pipeline_mode=resident), pl.BlockSpec(b1.shape, const),
                  pl.BlockSpec(w2.shape, const, pipeline_mode=resident), pl.BlockSpec((1, d), const),
                  pl.BlockSpec((1, d), const), pl.BlockSpec((1, d), const)],
        out_specs=pl.BlockSpec((1, tm * nt, d), row),
        out_shape=jax.ShapeDtypeStruct((b, t, d), F32),
        compiler_params=_cparams(("arbitrary", "arbitrary")),
        name="ffn",
    )(x, sh, sc, gate, w1, b1, w2, b2, lg, lb)


def _rope_lane_plan(starts, dims):
    h = dims // 2
    inv = ROPE_BASE ** (-(np.arange(h // 2, dtype=np.float64) * (2.0 / h)))
    fr, fc, lo, hi = (np.zeros(LANE, np.float32) for _ in range(4))
    for st in starts:
        for axis_i, f in enumerate((fr, fc)):
            base = st + axis_i * h
            f[base:base + h // 2] = inv
            f[base + h // 2:base + h] = inv
            lo[base:base + h // 2] = 1.0
            hi[base + h // 2:base + h] = 1.0
    return fr, fc, lo, hi


def _rope_tables(t_len, roped):
    pos_t = jnp.arange(t_len)
    scale = 1.0 if roped else 0.0
    row = (pos_t // GRID_W).astype(F32)[:, None] * scale
    col = (pos_t % GRID_W).astype(F32)[:, None] * scale
    out = []
    for starts, dims in (([MLA_NOPE], MLA_ROPE), ([0, LANE // 2], DIFF_DIM)):
        fr, fc, lo, hi = _rope_lane_plan(starts, dims)
        ang = row * fr[None, :] + col * fc[None, :]
        sin = jnp.sin(ang)
        out += [jnp.cos(ang), -sin * lo[None, :], sin * hi[None, :]]
    return jnp.concatenate(out, axis=1)


def _prep_weights(w_in, w_q_up, w_kv_up, w_out, diff_lambda, diff_subln):
    depth, d, _ = w_in.shape
    s1, s2 = IN_MLA, IN_MLA + IN_DIFF
    pad_last = lambda a, n: jnp.pad(a, [(0, 0)] * (a.ndim - 1) + [(0, n - a.shape[-1])])
    kr = jnp.pad(w_in[..., C_KR:s1], ((0, 0), (0, 0), (MLA_NOPE, LANE - MLA_NOPE - MLA_ROPE)))
    dq = pad_last(w_in[..., s1:s1 + N_DQ].reshape(depth, d, DIFF_HEADS, 2, DIFF_DIM), LANE // 2)
    dk = pad_last(w_in[..., s1 + N_DQ:s1 + 2 * N_DQ].reshape(depth, d, DIFF_HEADS, 2, DIFF_DIM), LANE // 2)
    dv = pad_last(w_in[..., s1 + 2 * N_DQ:s2].reshape(depth, d, DIFF_HEADS, DIFF_V), LANE)
    win = jnp.concatenate([w_in[..., :C_KR], kr, dq.reshape(depth, d, W_D), dk.reshape(depth, d, W_D),
                           dv.reshape(depth, d, W_D), w_in[..., s2:]], axis=-1).astype(BF16)
    wq = pad_last(w_q_up.reshape(depth, MLA_Q_RANK, MLA_HEADS, MLA_NOPE + MLA_ROPE), LANE)
    wq = wq.reshape(depth, MLA_Q_RANK, W_QM).astype(BF16)
    kv = w_kv_up.reshape(depth, MLA_KV_RANK, MLA_HEADS, MLA_NOPE + MLA_V)
    wk = pad_last(kv[..., :MLA_NOPE], LANE).reshape(depth, MLA_KV_RANK, W_QM)
    wv = kv[..., MLA_NOPE:].reshape(depth, MLA_KV_RANK, W_VM)
    wkv = jnp.concatenate([wk, wv], axis=-1).astype(BF16)
    wo_m = w_out[:, :W_VM].astype(BF16)
    wo_d = w_out[:, W_VM:W_VM + DIFF_HEADS * DIFF_V].reshape(depth, DIFF_HEADS, DIFF_V, -1)
    wo_d = jnp.pad(wo_d, ((0, 0), (0, 0), (0, LANE - DIFF_V), (0, 0))).reshape(depth, W_D, -1).astype(BF16)
    wo_h = w_out[:, W_VM + DIFF_HEADS * DIFF_V:].astype(BF16)
    lam = pad_last(diff_lambda.astype(F32), LANE)
    subln = pad_last(diff_subln.astype(F32), LANE)[:, None, :]
    return win, wq, wkv, wo_m, wo_d, wo_h, lam, subln


def kernel(x, c, ctx, c_ctx, w_mod, b_mod, w_in, mla_q_norm, w_q_up, mla_kv_norm, w_kv_up, diff_lambda, diff_subln, hy_conv_w, hy_conv_b, hy_fw1, hy_fb1, hy_fw2, hy_fb2, hy_fw3, hy_fb3, hy_bias, w_out, b_out, ln1_g, ln1_b, w_ff1, b_ff1, w_ff2, b_ff2, ln2_g, ln2_b):
    bsz, seq, d = x.shape
    n_ctx = ctx.shape[1]
    depth = w_in.shape[0]
    alpha = (2.0 * depth) ** 0.25

    rpad = -(bsz + 1) % 8
    cc = jnp.concatenate([c, c_ctx[None, :], jnp.zeros((rpad, d), F32)], axis=0)
    mod_all = _modulation(cc, w_mod, b_mod)

    win, wq, wkv, wo_m, wo_d, wo_h, lam_p, subln = _prep_weights(w_in, w_q_up, w_kv_up, w_out, diff_lambda, diff_subln)
    w1 = w_ff1.astype(BF16)
    w2 = w_ff2.astype(BF16)
    fw1 = jnp.pad(hy_fw1, ((0, 0), (0, LANE - HY_EMB), (0, 0)))
    tab_x = _rope_tables(seq, True)
    tab_c = _rope_tables(n_ctx, False)
    def hy_tables(l):
        if l >= 1024 and l % (HY_N1 * HY_J * HY_GROUPS) == 0:
            return _hy2_tables(l)
        return _dft_tables(l, min(512, l)) + (min(512, l),)

    hyc_x = _hy_consts(seq) + (hy_tables(seq),)
    hyc_c = _hy_consts(n_ctx) + (hy_tables(n_ctx),)
    tm_x, tm_c = min(512, seq), min(512, n_ctx)
    ntf_x = 2 if seq % (2 * tm_x) == 0 else 1
    tq_x, tq_c = min(256, seq), min(256, n_ctx)
    tff = min(1024, w1.shape[2])
    row2 = lambda a: a.reshape(1, -1)

    for layer in range(depth):
        need_ctx = layer < depth - 1
        lam_init = 0.8 - 0.6 * math.exp(-0.3 * layer)
        mod = mod_all[layer, :bsz].reshape(bsz, 1, N_MOD, d)
        modc = mod_all[layer, bsz:bsz + 1].reshape(1, 1, N_MOD, d)
        sh1, sc1, g1, sh2, sc2, g2 = [mod[:, :, i] for i in range(N_MOD)]
        csh1, csc1, cg1, csh2, csc2, cg2 = [modc[:, :, i] for i in range(N_MOD)]
        gq, gkv = row2(mla_q_norm[layer]), row2(mla_kv_norm[layer])
        hy_args = (hy_conv_w[layer], row2(hy_conv_b[layer]), fw1[layer], row2(hy_fb1[layer]), hy_fw2[layer],
                   row2(hy_fb2[layer]), hy_fw3[layer], row2(hy_fb3[layer]), hy_bias[layer])

        qm, km, vm, dq, dk, dv, uh = _in_proj(x, sh1, sc1, win[layer], gq, wq[layer], gkv, wkv[layer], tab_x, tm_x)
        cqm, ckm, cvm, cdq, cdk, cdv, cuh = _in_proj(ctx, csh1, csc1, win[layer], gq, wq[layer], gkv, wkv[layer],
                                                     tab_c, tm_c)
        om = _mla_attention(qm, [ckm, km], [cvm, vm], tq_x, MLA_HEADS // 2)
        od = _diff_attention(dq, [cdk, dk], [cdv, dv], lam_p[layer], subln[layer], lam_init, tq_x, DIFF_HEADS)
        oh = _hyena(uh, hyc_x, *hy_args)
        x = _out_proj(om, od, oh, wo_m[layer], wo_d[layer], wo_h[layer], row2(b_out[layer]), x, g1,
                      row2(ln1_g[layer]), row2(ln1_b[layer]), alpha, tm_x)
        x = _ffn(x, sh2, sc2, g2, w1[layer], row2(b_ff1[layer]), w2[layer], row2(b_ff2[layer]),
                 row2(ln2_g[layer]), row2(ln2_b[layer]), alpha, tm_x, ntf_x, tff)
        if need_ctx:
            com = _mla_attention(cqm, [ckm], [cvm], tq_c, MLA_HEADS // 2)
            cod = _diff_attention(cdq, [cdk], [cdv], lam_p[layer], subln[layer], lam_init, tq_c, DIFF_HEADS)
            coh = _hyena(cuh, hyc_c, *hy_args)
            ctx = _out_proj(com, cod, coh, wo_m[layer], wo_d[layer], wo_h[layer], row2(b_out[layer]), ctx, cg1,
                            row2(ln1_g[layer]), row2(ln1_b[layer]), alpha, tm_c)
            ctx = _ffn(ctx, csh2, csc2, cg2, w1[layer], row2(b_ff1[layer]), w2[layer], row2(b_ff2[layer]),
                       row2(ln2_g[layer]), row2(ln2_b[layer]), alpha, tm_c, 1, tff)
    return x
```

```python
import functools
import math

import jax
import numpy as np
import jax.numpy as jnp
from jax import lax
from jax.experimental import pallas as pl
from jax.experimental.pallas import tpu as pltpu

F32 = jnp.float32
BF16 = jnp.bfloat16
HI = lax.Precision.HIGHEST

GRID_W = 64
ROPE_BASE = 10000.0
LN_EPS = 1e-6
MLA_HEADS, MLA_NOPE, MLA_ROPE, MLA_V = 6, 64, 32, 64
MLA_Q_RANK, MLA_KV_RANK = 256, 128
DIFF_HEADS, DIFF_DIM = 4, 48
DIFF_V = 2 * DIFF_DIM
HY_CH, HY_ORDER, HY_BANDS = 256, 2, 16
HY_EMB = 1 + 2 * HY_BANDS
HY_TARGET, HY_FAST_DECAY_PCT, HY_SLOW_DECAY_PCT = 1e-2, 0.3, 1.5
HY_MIN_RATE = -math.log(HY_TARGET) / HY_SLOW_DECAY_PCT
HY_MAX_RATE = -math.log(HY_TARGET) / HY_FAST_DECAY_PCT
N_MOD = 6

LOG2E = 1.0 / math.log(2.0)
LANE = 128
VMEM_LIMIT = 56 * 1024 * 1024

IN_MLA = MLA_Q_RANK + MLA_KV_RANK + MLA_ROPE
N_DQ = DIFF_HEADS * 2 * DIFF_DIM
IN_DIFF = 2 * N_DQ + DIFF_HEADS * DIFF_V
W_QM = MLA_HEADS * LANE
W_VM = MLA_HEADS * MLA_V
W_D = DIFF_HEADS * LANE
W_HY = (HY_ORDER + 1) * HY_CH
C_CQ, C_CKV, C_KR = 0, MLA_Q_RANK, MLA_Q_RANK + MLA_KV_RANK
C_DQ = C_KR + LANE
C_DK = C_DQ + W_D
C_DV = C_DK + W_D
C_HY = C_DV + W_D
W_IN = C_HY + W_HY


def _cparams(sem):
    return pltpu.CompilerParams(dimension_semantics=sem, vmem_limit_bytes=VMEM_LIMIT)


def _ln(x):
    mu = jnp.mean(x, -1, keepdims=True)
    xc = x - mu
    var = jnp.mean(xc * xc, -1, keepdims=True)
    return xc * lax.rsqrt(var + LN_EPS)


def _bdot(a, b):
    return jnp.dot(a, b, preferred_element_type=F32)


def _mod_kernel(c_ref, w_ref, b_ref, o_ref):
    c = c_ref[...]
    s = c / (1.0 + jnp.exp(-c))
    o_ref[...] = jnp.dot(s, w_ref[...], preferred_element_type=F32, precision=HI) + b_ref[...]


def _modulation(cc, w_mod, b_mod):
    depth, d, n = w_mod.shape
    r = cc.shape[0]
    tn = 1024
    return pl.pallas_call(
        _mod_kernel,
        grid=(depth, n // tn),
        in_specs=[pl.BlockSpec((r, d), lambda l, j: (0, 0)),
                  pl.BlockSpec((None, d, tn), lambda l, j: (l, 0, j)),
                  pl.BlockSpec((None, 1, tn), lambda l, j: (l, 0, j))],
        out_specs=pl.BlockSpec((None, r, tn), lambda l, j: (l, 0, j)),
        out_shape=jax.ShapeDtypeStruct((depth, r, n), F32),
        compiler_params=_cparams(("arbitrary", "arbitrary")),
        name="modulation",
    )(cc, w_mod, b_mod.reshape(depth, 1, n))


def _rope(x, c, sa, sb, half):
    return x * c + pltpu.roll(x, LANE - half, 1) * sa + pltpu.roll(x, half, 1) * sb


def _inproj_kernel(x_ref, xp_ref, xn_ref, sh_ref, sc_ref, win_ref, gq_ref, wq_ref, gkv_ref, wkv_ref, tab_ref,
                   cw_ref, cb_ref, qm_ref, km_ref, vm_ref, dq_ref, dk_ref, dv_ref, hy_ref):
    x = jnp.concatenate([x_ref[0], xp_ref[0], xn_ref[0]], axis=0)
    h = _ln(x) * (1.0 + sc_ref[0]) + sh_ref[0]
    p_all = _bdot(h.astype(BF16), win_ref[...])
    tm = x_ref.shape[1]
    p = p_all[0:tm]
    tab = tab_ref[...]
    cm, sam, sbm = tab[:, 0:LANE], tab[:, LANE:2 * LANE], tab[:, 2 * LANE:3 * LANE]
    cd, sad, sbd = tab[:, 3 * LANE:4 * LANE], tab[:, 4 * LANE:5 * LANE], tab[:, 5 * LANE:6 * LANE]

    cq = p[:, C_CQ:C_CQ + MLA_Q_RANK]
    qn = cq * lax.rsqrt(jnp.mean(cq * cq, -1, keepdims=True) + LN_EPS) * gq_ref[...]
    q = _bdot(qn.astype(BF16), wq_ref[...])
    ckv = p[:, C_CKV:C_CKV + MLA_KV_RANK]
    kvn = ckv * lax.rsqrt(jnp.mean(ckv * ckv, -1, keepdims=True) + LN_EPS) * gkv_ref[...]
    kv = _bdot(kvn.astype(BF16), wkv_ref[...])
    kr = _rope(p[:, C_KR:C_KR + LANE], cm, sam, sbm, MLA_ROPE // 4)
    scale_m = LOG2E * (MLA_NOPE + MLA_ROPE) ** -0.5
    for hd in range(MLA_HEADS):
        sl = slice(hd * LANE, (hd + 1) * LANE)
        qm_ref[0, :, sl] = (_rope(q[:, sl], cm, sam, sbm, MLA_ROPE // 4) * scale_m).astype(BF16)
        km_ref[0, :, sl] = (kv[:, sl] + kr).astype(BF16)
    vm_ref[0] = kv[:, W_QM:W_QM + W_VM].astype(BF16)
    scale_d = LOG2E * DIFF_DIM ** -0.5
    for hd in range(DIFF_HEADS):
        sl = slice(hd * LANE, (hd + 1) * LANE)
        dq = p[:, C_DQ + hd * LANE:C_DQ + (hd + 1) * LANE]
        dk = p[:, C_DK + hd * LANE:C_DK + (hd + 1) * LANE]
        dq_ref[0, :, sl] = (_rope(dq, cd, sad, sbd, DIFF_DIM // 4) * scale_d).astype(BF16)
        dk_ref[0, :, sl] = _rope(dk, cd, sad, sbd, DIFF_DIM // 4).astype(BF16)
    dv_ref[0] = p[:, C_DV:C_DV + W_D].astype(BF16)
    u = p[:, C_HY:C_HY + W_HY]
    i = pl.program_id(0)
    u_prev = jnp.where(i == 0, 0.0, p_all[tm + 7:tm + 8, C_HY:C_HY + W_HY])
    u_next = jnp.where(i == pl.num_programs(0) - 1, 0.0, p_all[tm + 8:tm + 9, C_HY:C_HY + W_HY])
    row = lax.broadcasted_iota(jnp.int32, u.shape, 0)
    up = jnp.where(row == 0, u_prev, pltpu.roll(u, 1, 0))
    un = jnp.where(row == tm - 1, u_next, pltpu.roll(u, tm - 1, 0))
    w = cw_ref[...]
    hy_ref[0] = (up * w[0:1] + u * w[1:2] + un * w[2:3] + cb_ref[...]).astype(BF16)


def _in_proj(x, sh, sc, win, gq, wq, gkv, wkv, tab, conv_w, conv_b, tm):
    b, t, d = x.shape
    hb = tm // 8
    per_batch = sh.shape[0] == b and b > 1
    mod_map = (lambda i, bb: (bb, 0, 0)) if per_batch else (lambda i, bb: (0, 0, 0))
    const = lambda i, bb: (0, 0)
    widths = (W_QM, W_QM, W_VM, W_D, W_D, W_D, W_HY)
    return pl.pallas_call(
        _inproj_kernel,
        grid=(t // tm, b),
        in_specs=[pl.BlockSpec((1, tm, d), lambda i, bb: (bb, i, 0)),
                  pl.BlockSpec((1, 8, d), lambda i, bb: (bb, jnp.maximum(i * hb - 1, 0), 0)),
                  pl.BlockSpec((1, 8, d), lambda i, bb: (bb, jnp.minimum((i + 1) * hb, t // 8 - 1), 0)),
                  pl.BlockSpec((1, 1, d), mod_map),
                  pl.BlockSpec((1, 1, d), mod_map),
                  pl.BlockSpec(win.shape, const),
                  pl.BlockSpec(gq.shape, const),
                  pl.BlockSpec(wq.shape, const),
                  pl.BlockSpec(gkv.shape, const),
                  pl.BlockSpec(wkv.shape, const),
                  pl.BlockSpec((tm, 6 * LANE), lambda i, bb: (i, 0)),
                  pl.BlockSpec(conv_w.shape, const), pl.BlockSpec(conv_b.shape, const)],
        out_specs=[pl.BlockSpec((1, tm, w), lambda i, bb: (bb, i, 0)) for w in widths],
        out_shape=[jax.ShapeDtypeStruct((b, t, w), BF16) for w in widths],
        compiler_params=_cparams(("arbitrary", "arbitrary")),
        name="in_proj",
    )(x, x, x, sh, sc, win, gq, wq, gkv, wkv, tab, conv_w, conv_b)


NEG_BIG = -1e30


def _lane_fold(x, op):
    r = x[:, 0:LANE]
    for i in range(1, x.shape[1] // LANE):
        r = op(r, x[:, i * LANE:(i + 1) * LANE])
    return r


def _score_pass(q, parts, ksl, s_refs):
    mx = jnp.full((q.shape[0], LANE), NEG_BIG, F32)
    for (k_ref, _, n, tk), s_ref in zip(parts, s_refs):
        for j in range(n // tk):
            k = k_ref[0, j * tk:(j + 1) * tk, ksl]
            s = lax.dot_general(q, k, (((1,), (1,)), ((), ())), preferred_element_type=F32)
            s_ref[j] = s
            mx = jnp.maximum(mx, _lane_fold(s, jnp.maximum))
    return jnp.max(mx, axis=-1, keepdims=True)


def _softmax_pv(q, parts, ksl, vsl, s_refs):
    m = _score_pass(q, parts, ksl, s_refs)
    ls = jnp.zeros((q.shape[0], LANE), F32)
    acc = jnp.zeros((q.shape[0], LANE), F32)
    for (_, v_ref, n, tk), s_ref in zip(parts, s_refs):
        for j in range(n // tk):
            p = jnp.exp2(s_ref[j] - m)
            ls = ls + _lane_fold(p, jnp.add)
            acc = acc + _bdot(p.astype(BF16), v_ref[0, j * tk:(j + 1) * tk, vsl])
    return acc * (1.0 / jnp.sum(ls, axis=-1, keepdims=True))


def _mla_attn_kernel(*refs, part_shapes, n_pairs):
    np_ = len(part_shapes)
    q_ref = refs[0]
    k_refs = refs[1:1 + np_]
    v_refs = refs[1 + np_:1 + 2 * np_]
    o_ref = refs[1 + 2 * np_]
    s_sets = (refs[2 + 2 * np_:2 + 3 * np_], refs[2 + 3 * np_:])
    parts = [(k_refs[i], v_refs[i]) + part_shapes[i] for i in range(np_)]
    for pr in range(n_pairs):
        outs = []
        for hh in range(2):
            hd = 2 * pr + hh
            sl = slice(hd * LANE, (hd + 1) * LANE)
            outs.append(_softmax_pv(q_ref[0, :, sl], parts, sl, slice(pr * LANE, (pr + 1) * LANE), s_sets[hh]))
        lane = lax.broadcasted_iota(jnp.int32, outs[0].shape, 1)
        o_ref[0, :, pr * LANE:(pr + 1) * LANE] = jnp.where(lane < MLA_V, outs[0], outs[1]).astype(BF16)


def _chunk(n):
    return n if n <= 512 else 512


def _mla_attention(q, ks, vs, tq, n_pairs):
    b, t, _ = q.shape
    part_shapes = tuple((k.shape[1], _chunk(k.shape[1])) for k in ks)
    in_specs = [pl.BlockSpec((1, tq, 2 * n_pairs * LANE), lambda bb, g, i: (bb, i, g))]
    in_specs += [pl.BlockSpec((1, k.shape[1], 2 * n_pairs * LANE), lambda bb, g, i: (bb, 0, g)) for k in ks]
    in_specs += [pl.BlockSpec((1, v.shape[1], n_pairs * LANE), lambda bb, g, i: (bb, 0, g)) for v in vs]
    scratch = [pltpu.VMEM((n // tk, tq, tk), F32) for n, tk in part_shapes] * 2
    return pl.pallas_call(
        functools.partial(_mla_attn_kernel, part_shapes=part_shapes, n_pairs=n_pairs),
        grid=(b, MLA_HEADS // (2 * n_pairs), t // tq),
        in_specs=in_specs,
        out_specs=pl.BlockSpec((1, tq, n_pairs * LANE), lambda bb, g, i: (bb, i, g)),
        out_shape=jax.ShapeDtypeStruct((b, t, W_VM), BF16),
        scratch_shapes=scratch,
        compiler_params=_cparams(("arbitrary", "arbitrary", "arbitrary")),
        name="mla_attention",
    )(q, *ks, *vs)


def _diff_attn_kernel(*refs, part_shapes, lam_init, n_heads):
    np_ = len(part_shapes)
    q_ref = refs[0]
    k_refs = refs[1:1 + np_]
    v_refs = refs[1 + np_:1 + 2 * np_]
    lam_ref, g_ref, o_ref = refs[1 + 2 * np_:4 + 2 * np_]
    s_sets = (refs[4 + 2 * np_:4 + 3 * np_], refs[4 + 3 * np_:])
    parts = [(k_refs[i], v_refs[i]) + part_shapes[i] for i in range(np_)]
    lp = lam_ref[...]
    lam = (jnp.exp(jnp.sum(lp[0:1] * lp[1:2], axis=-1, keepdims=True))
           - jnp.exp(jnp.sum(lp[2:3] * lp[3:4], axis=-1, keepdims=True)) + lam_init)
    for hd in range(n_heads):
        sl = slice(hd * LANE, (hd + 1) * LANE)
        q = q_ref[0, :, sl]
        lane = lax.broadcasted_iota(jnp.int32, q.shape, 1)
        zero = jnp.zeros_like(q)
        o1 = _softmax_pv(jnp.where(lane < LANE // 2, q, zero), parts, sl, sl, s_sets[0])
        o2 = _softmax_pv(jnp.where(lane >= LANE // 2, q, zero), parts, sl, sl, s_sets[1])
        o = o1 - lam * o2
        ms = jnp.sum(o * o, axis=-1, keepdims=True) * (1.0 / DIFF_V)
        o_ref[0, :, sl] = (o * lax.rsqrt(ms + LN_EPS) * g_ref[...] * (1.0 - lam_init)).astype(BF16)


def _diff_attention(q, ks, vs, lam_p, subln, lam_init, tq, n_heads):
    b, t, _ = q.shape
    part_shapes = tuple((k.shape[1], _chunk(k.shape[1])) for k in ks)
    hmap = lambda bb, g, i: (bb, 0, g)
    in_specs = [pl.BlockSpec((1, tq, n_heads * LANE), lambda bb, g, i: (bb, i, g))]
    in_specs += [pl.BlockSpec((1, k.shape[1], n_heads * LANE), hmap) for k in ks]
    in_specs += [pl.BlockSpec((1, v.shape[1], n_heads * LANE), hmap) for v in vs]
    in_specs += [pl.BlockSpec(lam_p.shape, lambda bb, g, i: (0, 0)),
                 pl.BlockSpec(subln.shape, lambda bb, g, i: (0, 0))]
    scratch = [pltpu.VMEM((n // tk, tq, tk), F32) for n, tk in part_shapes] * 2
    return pl.pallas_call(
        functools.partial(_diff_attn_kernel, part_shapes=part_shapes, lam_init=lam_init, n_heads=n_heads),
        grid=(b, DIFF_HEADS // n_heads, t // tq),
        in_specs=in_specs,
        out_specs=pl.BlockSpec((1, tq, n_heads * LANE), lambda bb, g, i: (bb, i, g)),
        out_shape=jax.ShapeDtypeStruct((b, t, W_D), BF16),
        scratch_shapes=scratch,
        compiler_params=_cparams(("arbitrary", "arbitrary", "arbitrary")),
        name="diff_attention",
    )(q, *ks, *vs, lam_p, subln)


def _hyfilt_kernel(emb_ref, win_ref, fw1_ref, fb1_ref, fw2_ref, fb2_ref, fwf_ref, fbf_ref, fwb_ref, fbb_ref,
                   d_ref, h_scr):
    @pl.when(pl.program_id(0) == 0)
    def _():
        h1 = jnp.sin(jnp.dot(emb_ref[...], fw1_ref[...], preferred_element_type=F32, precision=HI) + fb1_ref[...])
        h_scr[...] = jnp.sin(jnp.dot(h1, fw2_ref[...], preferred_element_type=F32, precision=HI) + fb2_ref[...])

    h = h_scr[...]
    w = win_ref[...]
    fwd = (jnp.dot(h, fwf_ref[...], preferred_element_type=F32, precision=HI) + fbf_ref[...]) * w
    bwd = (jnp.dot(h, fwb_ref[...], preferred_element_type=F32, precision=HI) + fbb_ref[...]) * w
    row = lax.broadcasted_iota(jnp.int32, bwd.shape, 0)
    bwd = jnp.where(row == 0, 0.0, bwd)
    norm = jnp.sum(jnp.abs(fwd), axis=0, keepdims=True) + jnp.sum(jnp.abs(bwd), axis=0, keepdims=True)
    inv = 1.0 / norm
    d_ref[0] = ((fwd + bwd) * inv).astype(BF16)
    d_ref[1] = ((fwd - bwd) * inv).astype(BF16)


def _hy_filters(emb, win, fw1, fb1, fw2, fb2, fw3, fb3):
    l = emb.shape[0]
    hid = fw2.shape[0]
    nblk = HY_CH // LANE
    const = lambda g: (0, 0)
    fcol = lambda g: (0, (g // nblk) * 2 * nblk + g % nblk)
    bcol = lambda g: (0, (g // nblk) * 2 * nblk + nblk + g % nblk)
    return pl.pallas_call(
        _hyfilt_kernel,
        grid=(HY_ORDER * nblk,),
        in_specs=[pl.BlockSpec(emb.shape, const),
                  pl.BlockSpec((l, LANE), lambda g: (0, g % nblk)),
                  pl.BlockSpec(fw1.shape, const), pl.BlockSpec(fb1.shape, const),
                  pl.BlockSpec(fw2.shape, const), pl.BlockSpec(fb2.shape, const),
                  pl.BlockSpec((hid, LANE), fcol), pl.BlockSpec((1, LANE), fcol),
                  pl.BlockSpec((hid, LANE), bcol), pl.BlockSpec((1, LANE), bcol)],
        out_specs=pl.BlockSpec((2, None, l, LANE), lambda g: (0, g // nblk, 0, g % nblk)),
        out_shape=jax.ShapeDtypeStruct((2, HY_ORDER, l, HY_CH), BF16),
        scratch_shapes=[pltpu.VMEM((l, hid), F32)],
        compiler_params=_cparams(("arbitrary",)),
        name="hyena_filters",
    )(emb, win, fw1, fb1, fw2, fb2, fw3, fb3, fw3, fb3)


def _hyspec_kernel(f_ref, s_ref, d_ref, o_ref):
    tf = f_ref.shape[0] // 2
    o_ref[0:tf, :] = _bdot(f_ref[0:tf, :], s_ref[...])
    o_ref[tf:2 * tf, :] = _bdot(f_ref[tf:2 * tf, :], d_ref[...])


def _hy_spectrum(fmat, filt, tf):
    rows, l = fmat.shape
    return pl.pallas_call(
        _hyspec_kernel,
        grid=(rows // (2 * tf), HY_ORDER),
        in_specs=[pl.BlockSpec((2 * tf, l), lambda j, o: (j, 0)),
                  pl.BlockSpec((None, None, l, HY_CH), lambda j, o: (0, o, 0, 0)),
                  pl.BlockSpec((None, None, l, HY_CH), lambda j, o: (1, o, 0, 0))],
        out_specs=pl.BlockSpec((2 * tf, HY_CH), lambda j, o: (j, o)),
        out_shape=jax.ShapeDtypeStruct((rows, HY_ORDER * HY_CH), F32),
        compiler_params=_cparams(("arbitrary", "arbitrary")),
        name="hyena_filter_spectrum",
    )(fmat, filt, filt)


def _hyfwd_kernel(f_ref, z_ref, k_ref, y_ref):
    tf = f_ref.shape[0] // 2
    s = _bdot(f_ref[...], z_ref[0])
    sr, si = s[0:tf], s[tf:2 * tf]
    kr, ki = k_ref[0:tf, :], k_ref[tf:2 * tf, :]
    y_ref[0, 0:tf, :] = (sr * kr - si * ki).astype(BF16)
    y_ref[0, tf:2 * tf, :] = (sr * ki + si * kr).astype(BF16)


def _hy_forward(fmat, z, kspec, order, tf):
    rows, l = fmat.shape
    b = z.shape[0]
    return pl.pallas_call(
        _hyfwd_kernel,
        grid=(rows // (2 * tf), b),
        in_specs=[pl.BlockSpec((2 * tf, l), lambda j, bb: (j, 0)),
                  pl.BlockSpec((1, l, HY_CH), lambda j, bb: (bb, 0, 0)),
                  pl.BlockSpec((2 * tf, HY_CH), lambda j, bb: (j, order))],
        out_specs=pl.BlockSpec((1, 2 * tf, HY_CH), lambda j, bb: (bb, j, 0)),
        out_shape=jax.ShapeDtypeStruct((b, rows, HY_CH), BF16),
        compiler_params=_cparams(("arbitrary", "arbitrary")),
        name="hyena_dft_forward",
    )(fmat, z, kspec)


def _hyinv_kernel(g_ref, y_ref, z_ref, x_ref, b_ref, o_ref):
    conv = _bdot(g_ref[...], y_ref[0])
    z = z_ref[0].astype(F32)
    o_ref[0] = (x_ref[0].astype(F32) * (conv + z * b_ref[...])).astype(BF16)


def _hy_inverse(gmat, y, z, gate, bias, tt):
    l, rows = gmat.shape
    b = y.shape[0]
    return pl.pallas_call(
        _hyinv_kernel,
        grid=(l // tt, b),
        in_specs=[pl.BlockSpec((tt, rows), lambda i, bb: (i, 0)),
                  pl.BlockSpec((1, rows, HY_CH), lambda i, bb: (bb, 0, 0)),
                  pl.BlockSpec((1, tt, HY_CH), lambda i, bb: (bb, i, 0)),
                  pl.BlockSpec((1, tt, HY_CH), lambda i, bb: (bb, i, 0)),
                  pl.BlockSpec((1, HY_CH), lambda i, bb: (0, 0))],
        out_specs=pl.BlockSpec((1, tt, HY_CH), lambda i, bb: (bb, i, 0)),
        out_shape=jax.ShapeDtypeStruct((b, l, HY_CH), BF16),
        compiler_params=_cparams(("arbitrary", "arbitrary")),
        name="hyena_dft_inverse",
    )(gmat, y, z, gate, bias)


def _dft_tables(l, tf):
    k = jnp.arange(l, dtype=jnp.int32)
    n = jnp.arange(l, dtype=jnp.int32)
    ph = ((2 * k + 1)[:, None] * n[None, :]) % (4 * l)
    ang = ph.astype(F32) * (2.0 * math.pi / (4 * l))
    c = jnp.cos(ang).reshape(l // tf, 1, tf, l)
    s = (-jnp.sin(ang)).reshape(l // tf, 1, tf, l)
    f = jnp.concatenate([c, s], axis=1).reshape(2 * l, l)
    return f.astype(BF16), (f.T * (1.0 / l)).astype(BF16)


def _hy_consts(l):
    t = jnp.arange(l, dtype=F32)
    bands = jnp.arange(1, HY_BANDS + 1, dtype=F32)
    ang = (2.0 * math.pi / l) * t[:, None] * bands[None, :]
    emb = jnp.concatenate([(t / l)[:, None], jnp.cos(ang), jnp.sin(ang)], -1)
    emb = jnp.pad(emb, ((0, 0), (0, LANE - HY_EMB)))
    rates = jnp.linspace(HY_MIN_RATE, HY_MAX_RATE, HY_CH, dtype=F32)
    win = jnp.exp(-(t / l)[:, None] * rates[None, :])
    return emb, win


HY_N1 = 16
HY_J = 16
HY_GROUPS = 4
HY_KG = 4


def _hy2_tables(l):
    n1h, jj = HY_N1, HY_J
    n2, k1n = l // n1h, 2 * n1h
    k2n = n2 // 2
    k1 = jnp.arange(k1n, dtype=jnp.int32)
    ph1 = ((2 * k1 + 1)[:, None] * jnp.arange(n1h, dtype=jnp.int32)[None, :]) % (4 * n1h)
    a1 = ph1.astype(F32) * (2.0 * math.pi / (4 * n1h))
    f1 = jnp.stack([jnp.cos(a1), -jnp.sin(a1)], axis=1)
    f1big = jnp.einsum('krn,ab->kranb', f1, jnp.eye(jj, dtype=F32)).reshape(k1n * 2 * jj, n1h * jj)
    kk = k1[:, None] + k1n * jnp.arange(k2n, dtype=jnp.int32)[None, :]
    ph2 = ((2 * kk + 1)[:, :, None] * jnp.arange(n2, dtype=jnp.int32)[None, None, :]) % (4 * l)
    a2 = ph2.astype(F32) * (2.0 * math.pi / (4 * l))
    mr, mi = jnp.cos(a2), -jnp.sin(a2)
    f2t = jnp.concatenate([jnp.concatenate([mr, -mi], axis=2), jnp.concatenate([mi, mr], axis=2)], axis=1)
    return (f1big.astype(BF16), (f1big.T * (1.0 / l)).astype(BF16), f2t.astype(BF16),
            jnp.swapaxes(f2t, 1, 2).astype(BF16))


def _hy2_s1_kernel(f_ref, z_ref, o_ref):
    n1h, _, c = z_ref.shape[1:]
    k1n = o_ref.shape[1]
    for g in range(HY_GROUPS):
        cols = slice(g * HY_J, (g + 1) * HY_J)
        z = z_ref[0, :, cols, :].reshape(n1h * HY_J, c)
        a = _bdot(f_ref[...], z)
        o_ref[0, :, :, cols, :] = a.astype(BF16).reshape(k1n, 2, HY_J, c)


def _hy2_stage1(f1big, vxx, which, z5):
    src = vxx if z5 is None else z5
    b, n1h, n2, _ = src.shape
    c = HY_CH
    k1n = 2 * n1h
    w = HY_J * HY_GROUPS
    zspec = pl.BlockSpec((1, n1h, w, c), lambda bb, g: (bb, 0, g, which if z5 is None else 0))
    return pl.pallas_call(
        _hy2_s1_kernel,
        grid=(b, n2 // w),
        in_specs=[pl.BlockSpec(f1big.shape, lambda bb, g: (0, 0)), zspec],
        out_specs=pl.BlockSpec((1, k1n, 2, w, c), lambda bb, g: (bb, 0, 0, g, 0)),
        out_shape=jax.ShapeDtypeStruct((b, k1n, 2, n2, c), BF16),
        compiler_params=_cparams(("arbitrary", "arbitrary")),
        name="hyena_stage1",
    )(f1big, src)


def _hy2_fspec_kernel(f_ref, s_ref, d_ref, o_ref):
    h = o_ref.shape[1] // 2
    for i in range(HY_KG):
        o_ref[i, 0:h, :] = _bdot(f_ref[i], s_ref[0, i])[0:h]
        o_ref[i, h:2 * h, :] = _bdot(f_ref[i], d_ref[0, i])[h:2 * h]


def _hy2_filter_spectrum(f2t, a5):
    nsig, k1n, _, n2, c = a5.shape
    a4 = a5.reshape(nsig, k1n, 2 * n2, c)
    return pl.pallas_call(
        _hy2_fspec_kernel,
        grid=(k1n // HY_KG, HY_ORDER),
        in_specs=[pl.BlockSpec((HY_KG, n2, 2 * n2), lambda g, o: (g, 0, 0)),
                  pl.BlockSpec((1, HY_KG, 2 * n2, c), lambda g, o: (o, g, 0, 0)),
                  pl.BlockSpec((1, HY_KG, 2 * n2, c), lambda g, o: (HY_ORDER + o, g, 0, 0))],
        out_specs=pl.BlockSpec((HY_KG, n2, c), lambda g, o: (g, 0, o)),
        out_shape=jax.ShapeDtypeStruct((k1n, n2, HY_ORDER * c), F32),
        compiler_params=_cparams(("arbitrary", "arbitrary")),
        name="hyena_filter_spectrum2",
    )(f2t, a4, a4)


def _hy2_s2_kernel(f_ref, g_ref, a_ref, k_ref, o_ref):
    for i in range(HY_KG):
        y = _bdot(f_ref[i], a_ref[0, i])
        h = y.shape[0] // 2
        yr, yi = y[:h], y[h:]
        kr, ki = k_ref[i, 0:h, :], k_ref[i, h:2 * h, :]
        p = jnp.concatenate([yr * kr - yi * ki, yr * ki + yi * kr], axis=0).astype(BF16)
        o_ref[0, i] = _bdot(g_ref[i], p).astype(BF16)


def _hy2_stage2(f2t, g2t, a5, kspec, order):
    b, k1n, _, n2, c = a5.shape
    a4 = a5.reshape(b, k1n, 2 * n2, c)
    out = pl.pallas_call(
        _hy2_s2_kernel,
        grid=(k1n // HY_KG, b),
        in_specs=[pl.BlockSpec((HY_KG, n2, 2 * n2), lambda g, bb: (g, 0, 0)),
                  pl.BlockSpec((HY_KG, 2 * n2, n2), lambda g, bb: (g, 0, 0)),
                  pl.BlockSpec((1, HY_KG, 2 * n2, c), lambda g, bb: (bb, g, 0, 0)),
                  pl.BlockSpec((HY_KG, n2, c), lambda g, bb: (g, 0, order))],
        out_specs=pl.BlockSpec((1, HY_KG, 2 * n2, c), lambda g, bb: (bb, g, 0, 0)),
        out_shape=jax.ShapeDtypeStruct(a4.shape, BF16),
        compiler_params=_cparams(("arbitrary", "arbitrary")),
        name="hyena_stage2",
    )(f2t, g2t, a4, kspec)
    return out.reshape(a5.shape)


def _hy2_s3_kernel(g_ref, c_ref, z_ref, x_ref, b_ref, *rest, chain):
    if chain:
        f_ref, o_ref, a_ref = rest
    else:
        (o_ref,) = rest
    k1n = c_ref.shape[1]
    n1h, _, c = z_ref.shape[1:]
    for g in range(HY_GROUPS):
        cols = slice(g * HY_J, (g + 1) * HY_J)
        cc = c_ref[0, :, :, cols, :].reshape(k1n * 2 * HY_J, c)
        conv = _bdot(g_ref[...], cc)
        z = z_ref[0, :, cols, :].reshape(n1h * HY_J, c).astype(F32)
        x = x_ref[0, :, cols, :].reshape(n1h * HY_J, c).astype(F32)
        o = (x * (conv + z * b_ref[...])).astype(BF16)
        o_ref[0, :, cols, :] = o.reshape(n1h, HY_J, c)
        if chain:
            a_ref[0, :, :, cols, :] = _bdot(f_ref[...], o).astype(BF16).reshape(k1n, 2, HY_J, c)


def _hy2_stage3(g1big, c5, vxx, zwhich, z5, xwhich, bias, f1big=None):
    b, k1n, _, n2, c = c5.shape
    n1h = k1n // 2
    w = HY_J * HY_GROUPS
    chain = f1big is not None
    const = lambda bb, g: (0, 0)
    sel = lambda which: pl.BlockSpec((1, n1h, w, c), lambda bb, g: (bb, 0, g, which))
    row = pl.BlockSpec((1, n1h, w, c), lambda bb, g: (bb, 0, g, 0))
    spec5 = pl.BlockSpec((1, k1n, 2, w, c), lambda bb, g: (bb, 0, 0, g, 0))
    out5 = jax.ShapeDtypeStruct((b, n1h, n2, c), BF16)
    in_specs = [pl.BlockSpec(g1big.shape, const), spec5, sel(zwhich) if z5 is None else row, sel(xwhich),
                pl.BlockSpec((1, c), const)]
    args = [g1big, c5, vxx if z5 is None else z5, vxx, bias]
    if chain:
        in_specs.append(pl.BlockSpec(f1big.shape, const))
        args.append(f1big)
    return pl.pallas_call(
        functools.partial(_hy2_s3_kernel, chain=chain),
        grid=(b, n2 // w),
        in_specs=in_specs,
        out_specs=[row, spec5] if chain else row,
        out_shape=[out5, jax.ShapeDtypeStruct(c5.shape, BF16)] if chain else out5,
        compiler_params=_cparams(("arbitrary", "arbitrary")),
        name="hyena_stage3",
    )(*args)


def _hyena(vxx, consts, fw1, fb1, fw2, fb2, fw3, fb3, hbias):
    emb, win, tabs = consts
    filt = _hy_filters(emb, win, fw1, fb1, fw2, fb2, fw3, fb3)
    b, l, _ = vxx.shape
    c = HY_CH
    if len(tabs) == 3:
        fmat, gmat, tf = tabs
        kspec = _hy_spectrum(fmat, filt, tf)
        v, x1, x2 = vxx[..., 0:c], vxx[..., c:2 * c], vxx[..., 2 * c:3 * c]
        z = _hy_inverse(gmat, _hy_forward(fmat, v, kspec, 0, tf), v, x1, hbias[0:1], tf)
        return _hy_inverse(gmat, _hy_forward(fmat, z, kspec, 1, tf), z, x2, hbias[1:2], tf)
    f1big, g1big, f2t, g2t = tabs
    n2 = l // HY_N1
    filt5 = filt.reshape(2 * HY_ORDER, HY_N1, n2, c)
    kspec = _hy2_filter_spectrum(f2t, _hy2_stage1(f1big, None, 0, filt5))
    vxx5 = vxx.reshape(b, HY_N1, n2, 3 * c)
    c5 = _hy2_stage2(f2t, g2t, _hy2_stage1(f1big, vxx5, 0, None), kspec, 0)
    z5, a5 = _hy2_stage3(g1big, c5, vxx5, 0, None, 1, hbias[0:1], f1big)
    c5 = _hy2_stage2(f2t, g2t, a5, kspec, 1)
    return _hy2_stage3(g1big, c5, vxx5, 0, z5, 2, hbias[1:2]).reshape(b, l, c)


def _outproj_kernel(om_ref, od_ref, oh_ref, wm_ref, wd_ref, wh_ref, bo_ref, x_ref, g_ref, lg_ref, lb_ref,
                    o_ref, *, alpha):
    y = _bdot(om_ref[0], wm_ref[...]) + _bdot(od_ref[0], wd_ref[...]) + _bdot(oh_ref[0], wh_ref[...])
    z = alpha * x_ref[0] + g_ref[0] * (y + bo_ref[...])
    o_ref[0] = _ln(z) * lg_ref[...] + lb_ref[...]


def _out_proj(om, od, oh, wm, wd, wh, bo, x, gate, lg, lb, alpha, tm):
    b, t, d = x.shape
    per_batch = gate.shape[0] == b and b > 1
    gmap = (lambda bb, i: (bb, 0, 0)) if per_batch else (lambda bb, i: (0, 0, 0))
    const = lambda bb, i: (0, 0)
    row = lambda bb, i: (bb, i, 0)
    return pl.pallas_call(
        functools.partial(_outproj_kernel, alpha=alpha),
        grid=(b, t // tm),
        in_specs=[pl.BlockSpec((1, tm, om.shape[2]), row), pl.BlockSpec((1, tm, od.shape[2]), row),
                  pl.BlockSpec((1, tm, oh.shape[2]), row),
                  pl.BlockSpec(wm.shape, const), pl.BlockSpec(wd.shape, const), pl.BlockSpec(wh.shape, const),
                  pl.BlockSpec((1, d), const), pl.BlockSpec((1, tm, d), row), pl.BlockSpec((1, 1, d), gmap),
                  pl.BlockSpec((1, d), const), pl.BlockSpec((1, d), const)],
        out_specs=pl.BlockSpec((1, tm, d), row),
        out_shape=jax.ShapeDtypeStruct((b, t, d), F32),
        compiler_params=_cparams(("arbitrary", "arbitrary")),
        name="out_proj",
    )(om, od, oh, wm, wd, wh, bo, x, gate, lg, lb)


def _ffn_kernel(x_ref, sh_ref, sc_ref, g_ref, w1_ref, b1_ref, w2_ref, b2_ref, lg_ref, lb_ref, o_ref, *,
                alpha, tm, tf):
    nt = x_ref.shape[1] // tm
    rows = [slice(t * tm, (t + 1) * tm) for t in range(nt)]
    xs = [x_ref[0, r, :] for r in rows]
    hs = [(_ln(x) * (1.0 + sc_ref[0]) + sh_ref[0]).astype(BF16) for x in xs]
    accs = [None] * nt
    for c in range(w1_ref.shape[1] // tf):
        cols = slice(c * tf, (c + 1) * tf)
        for t in range(nt):
            a = jnp.maximum(_bdot(hs[t], w1_ref[:, cols]) + b1_ref[:, cols], 0.0)
            y = _bdot((a * a).astype(BF16), w2_ref[cols, :])
            accs[t] = y if accs[t] is None else accs[t] + y
    for t in range(nt):
        z = alpha * xs[t] + g_ref[0] * (accs[t] + b2_ref[...])
        o_ref[0, rows[t], :] = _ln(z) * lg_ref[...] + lb_ref[...]


def _ffn(x, sh, sc, gate, w1, b1, w2, b2, lg, lb, alpha, tm, nt, tf):
    b, t, d = x.shape
    per_batch = gate.shape[0] == b and b > 1
    gmap = (lambda bb, i: (bb, 0, 0)) if per_batch else (lambda bb, i: (0, 0, 0))
    const = lambda bb, i: (0, 0)
    row = lambda bb, i: (bb, i, 0)
    resident = pl.Buffered(1)
    return pl.pallas_call(
        functools.partial(_ffn_kernel, alpha=alpha, tm=tm, tf=tf),
        grid=(b, t // (tm * nt)),
        in_specs=[pl.BlockSpec((1, tm * nt, d), row),
                  pl.BlockSpec((1, 1, d), gmap), pl.BlockSpec((1, 1, d), gmap), pl.BlockSpec((1, 1, d), gmap),
                  pl.BlockSpec(w1.shape, const, pipeline_mode=resident), pl.BlockSpec(b1.shape, const),
                  pl.BlockSpec(w2.shape, const, pipeline_mode=resident), pl.BlockSpec((1, d), const),
                  pl.BlockSpec((1, d), const), pl.BlockSpec((1, d), const)],
        out_specs=pl.BlockSpec((1, tm * nt, d), row),
        out_shape=jax.ShapeDtypeStruct((b, t, d), F32),
        compiler_params=_cparams(("arbitrary", "arbitrary")),
        name="ffn",
    )(x, sh, sc, gate, w1, b1, w2, b2, lg, lb)


def _rope_lane_plan(starts, dims):
    h = dims // 2
    inv = ROPE_BASE ** (-(np.arange(h // 2, dtype=np.float64) * (2.0 / h)))
    fr, fc, lo, hi = (np.zeros(LANE, np.float32) for _ in range(4))
    for st in starts:
        for axis_i, f in enumerate((fr, fc)):
            base = st + axis_i * h
            f[base:base + h // 2] = inv
            f[base + h // 2:base + h] = inv
            lo[base:base + h // 2] = 1.0
            hi[base + h // 2:base + h] = 1.0
    return fr, fc, lo, hi


def _rope_tables(t_len, roped):
    pos_t = jnp.arange(t_len)
    scale = 1.0 if roped else 0.0
    row = (pos_t // GRID_W).astype(F32)[:, None] * scale
    col = (pos_t % GRID_W).astype(F32)[:, None] * scale
    out = []
    for starts, dims in (([MLA_NOPE], MLA_ROPE), ([0, LANE // 2], DIFF_DIM)):
        fr, fc, lo, hi = _rope_lane_plan(starts, dims)
        ang = row * fr[None, :] + col * fc[None, :]
        sin = jnp.sin(ang)
        out += [jnp.cos(ang), -sin * lo[None, :], sin * hi[None, :]]
    return jnp.concatenate(out, axis=1)


def _prep_weights(w_in, w_q_up, w_kv_up, w_out, diff_lambda, diff_subln):
    depth, d, _ = w_in.shape
    s1, s2 = IN_MLA, IN_MLA + IN_DIFF
    pad_last = lambda a, n: jnp.pad(a, [(0, 0)] * (a.ndim - 1) + [(0, n - a.shape[-1])])
    kr = jnp.pad(w_in[..., C_KR:s1], ((0, 0), (0, 0), (MLA_NOPE, LANE - MLA_NOPE - MLA_ROPE)))
    dq = pad_last(w_in[..., s1:s1 + N_DQ].reshape(depth, d, DIFF_HEADS, 2, DIFF_DIM), LANE // 2)
    dk = pad_last(w_in[..., s1 + N_DQ:s1 + 2 * N_DQ].reshape(depth, d, DIFF_HEADS, 2, DIFF_DIM), LANE // 2)
    dv = pad_last(w_in[..., s1 + 2 * N_DQ:s2].reshape(depth, d, DIFF_HEADS, DIFF_V), LANE)
    win = jnp.concatenate([w_in[..., :C_KR], kr, dq.reshape(depth, d, W_D), dk.reshape(depth, d, W_D),
                           dv.reshape(depth, d, W_D), w_in[..., s2:]], axis=-1).astype(BF16)
    wq = pad_last(w_q_up.reshape(depth, MLA_Q_RANK, MLA_HEADS, MLA_NOPE + MLA_ROPE), LANE)
    wq = wq.reshape(depth, MLA_Q_RANK, W_QM).astype(BF16)
    kv = w_kv_up.reshape(depth, MLA_KV_RANK, MLA_HEADS, MLA_NOPE + MLA_V)
    wk = pad_last(kv[..., :MLA_NOPE], LANE).reshape(depth, MLA_KV_RANK, W_QM)
    wv = kv[..., MLA_NOPE:].reshape(depth, MLA_KV_RANK, W_VM)
    wkv = jnp.concatenate([wk, wv], axis=-1).astype(BF16)
    wo_m = w_out[:, :W_VM].astype(BF16)
    wo_d = w_out[:, W_VM:W_VM + DIFF_HEADS * DIFF_V].reshape(depth, DIFF_HEADS, DIFF_V, -1)
    wo_d = jnp.pad(wo_d, ((0, 0), (0, 0), (0, LANE - DIFF_V), (0, 0))).reshape(depth, W_D, -1).astype(BF16)
    wo_h = w_out[:, W_VM + DIFF_HEADS * DIFF_V:].astype(BF16)
    lam = pad_last(diff_lambda.astype(F32), LANE)
    subln = pad_last(diff_subln.astype(F32), LANE)[:, None, :]
    return win, wq, wkv, wo_m, wo_d, wo_h, lam, subln


def kernel(x, c, ctx, c_ctx, w_mod, b_mod, w_in, mla_q_norm, w_q_up, mla_kv_norm, w_kv_up, diff_lambda, diff_subln, hy_conv_w, hy_conv_b, hy_fw1, hy_fb1, hy_fw2, hy_fb2, hy_fw3, hy_fb3, hy_bias, w_out, b_out, ln1_g, ln1_b, w_ff1, b_ff1, w_ff2, b_ff2, ln2_g, ln2_b):
    bsz, seq, d = x.shape
    n_ctx = ctx.shape[1]
    depth = w_in.shape[0]
    alpha = (2.0 * depth) ** 0.25

    rpad = -(bsz + 1) % 8
    cc = jnp.concatenate([c, c_ctx[None, :], jnp.zeros((rpad, d), F32)], axis=0)
    mod_all = _modulation(cc, w_mod, b_mod)

    win, wq, wkv, wo_m, wo_d, wo_h, lam_p, subln = _prep_weights(w_in, w_q_up, w_kv_up, w_out, diff_lambda, diff_subln)
    w1 = w_ff1.astype(BF16)
    w2 = w_ff2.astype(BF16)
    fw1 = jnp.pad(hy_fw1, ((0, 0), (0, LANE - HY_EMB), (0, 0)))
    tab_x = _rope_tables(seq, True)
    tab_c = _rope_tables(n_ctx, False)
    def hy_tables(l):
        if l >= 1024 and l % (HY_N1 * HY_J * HY_GROUPS) == 0:
            return _hy2_tables(l)
        return _dft_tables(l, min(512, l)) + (min(512, l),)

    hyc_x = _hy_consts(seq) + (hy_tables(seq),)
    hyc_c = _hy_consts(n_ctx) + (hy_tables(n_ctx),)
    tm_x, tm_c = min(512, seq), min(512, n_ctx)
    ntf_x = 2 if seq % (2 * tm_x) == 0 else 1
    tq_x, tq_c = min(256, seq), min(256, n_ctx)
    tff = min(1024, w1.shape[2])
    row2 = lambda a: a.reshape(1, -1)

    for layer in range(depth):
        need_ctx = layer < depth - 1
        lam_init = 0.8 - 0.6 * math.exp(-0.3 * layer)
        mod = mod_all[layer, :bsz].reshape(bsz, 1, N_MOD, d)
        modc = mod_all[layer, bsz:bsz + 1].reshape(1, 1, N_MOD, d)
        sh1, sc1, g1, sh2, sc2, g2 = [mod[:, :, i] for i in range(N_MOD)]
        csh1, csc1, cg1, csh2, csc2, cg2 = [modc[:, :, i] for i in range(N_MOD)]
        gq, gkv = row2(mla_q_norm[layer]), row2(mla_kv_norm[layer])
        hy_args = (fw1[layer], row2(hy_fb1[layer]), hy_fw2[layer], row2(hy_fb2[layer]), hy_fw3[layer],
                   row2(hy_fb3[layer]), hy_bias[layer])
        conv = (hy_conv_w[layer], row2(hy_conv_b[layer]))

        qm, km, vm, dq, dk, dv, uh = _in_proj(x, sh1, sc1, win[layer], gq, wq[layer], gkv, wkv[layer], tab_x, *conv, tm_x)
        cqm, ckm, cvm, cdq, cdk, cdv, cuh = _in_proj(ctx, csh1, csc1, win[layer], gq, wq[layer], gkv, wkv[layer],
                                                     tab_c, *conv, tm_c)
        om = _mla_attention(qm, [ckm, km], [cvm, vm], tq_x, MLA_HEADS // 2)
        od = _diff_attention(dq, [cdk, dk], [cdv, dv], lam_p[layer], subln[layer], lam_init, tq_x, DIFF_HEADS)
        oh = _hyena(uh, hyc_x, *hy_args)
        x = _out_proj(om, od, oh, wo_m[layer], wo_d[layer], wo_h[layer], row2(b_out[layer]), x, g1,
                      row2(ln1_g[layer]), row2(ln1_b[layer]), alpha, tm_x)
        x = _ffn(x, sh2, sc2, g2, w1[layer], row2(b_ff1[layer]), w2[layer], row2(b_ff2[layer]),
                 row2(ln2_g[layer]), row2(ln2_b[layer]), alpha, tm_x, ntf_x, tff)
        if need_ctx:
            com = _mla_attention(cqm, [ckm], [cvm], tq_c, MLA_HEADS // 2)
            cod = _diff_attention(cdq, [cdk], [cdv], lam_p[layer], subln[layer], lam_init, tq_c, DIFF_HEADS)
            coh = _hyena(cuh, hyc_c, *hy_args)
            ctx = _out_proj(com, cod, coh, wo_m[layer], wo_d[layer], wo_h[layer], row2(b_out[layer]), ctx, cg1,
                            row2(ln1_g[layer]), row2(ln1_b[layer]), alpha, tm_c)
            ctx = _ffn(ctx, csh2, csc2, cg2, w1[layer], row2(b_ff1[layer]), w2[layer], row2(b_ff2[layer]),
                       row2(ln2_g[layer]), row2(ln2_b[layer]), alpha, tm_c, 1, tff)
    return x
```

```python
import functools
import math

import jax
import numpy as np
import jax.numpy as jnp
from jax import lax
from jax.experimental import pallas as pl
from jax.experimental.pallas import tpu as pltpu

F32 = jnp.float32
BF16 = jnp.bfloat16
HI = lax.Precision.HIGHEST

GRID_W = 64
ROPE_BASE = 10000.0
LN_EPS = 1e-6
MLA_HEADS, MLA_NOPE, MLA_ROPE, MLA_V = 6, 64, 32, 64
MLA_Q_RANK, MLA_KV_RANK = 256, 128
DIFF_HEADS, DIFF_DIM = 4, 48
DIFF_V = 2 * DIFF_DIM
HY_CH, HY_ORDER, HY_BANDS = 256, 2, 16
HY_EMB = 1 + 2 * HY_BANDS
HY_TARGET, HY_FAST_DECAY_PCT, HY_SLOW_DECAY_PCT = 1e-2, 0.3, 1.5
HY_MIN_RATE = -math.log(HY_TARGET) / HY_SLOW_DECAY_PCT
HY_MAX_RATE = -math.log(HY_TARGET) / HY_FAST_DECAY_PCT
N_MOD = 6

LOG2E = 1.0 / math.log(2.0)
LANE = 128
SUBLANE = 8
VMEM_LIMIT = 56 * 1024 * 1024

IN_MLA = MLA_Q_RANK + MLA_KV_RANK + MLA_ROPE
N_DQ = DIFF_HEADS * 2 * DIFF_DIM
IN_DIFF = 2 * N_DQ + DIFF_HEADS * DIFF_V
W_QM = MLA_HEADS * LANE
W_VM = MLA_HEADS * MLA_V
W_D = DIFF_HEADS * LANE
W_HY = (HY_ORDER + 1) * HY_CH
C_CQ, C_CKV, C_KR = 0, MLA_Q_RANK, MLA_Q_RANK + MLA_KV_RANK
C_DQ = C_KR + LANE
C_DK = C_DQ + W_D
C_DV = C_DK + W_D
C_HY = C_DV + W_D
W_IN = C_HY + W_HY


def _cparams(sem):
    return pltpu.CompilerParams(dimension_semantics=sem, vmem_limit_bytes=VMEM_LIMIT)


def _ln(x):
    mu = jnp.mean(x, -1, keepdims=True)
    xc = x - mu
    var = jnp.mean(xc * xc, -1, keepdims=True)
    return xc * lax.rsqrt(var + LN_EPS)


def _bdot(a, b):
    return jnp.dot(a, b, preferred_element_type=F32)


def _mod_kernel(c_ref, w_ref, b_ref, o_ref):
    c = c_ref[...]
    s = c / (1.0 + jnp.exp(-c))
    o_ref[...] = jnp.dot(s, w_ref[...], preferred_element_type=F32, precision=HI) + b_ref[...]


def _modulation(cc, w_mod, b_mod):
    depth, d, n = w_mod.shape
    r = cc.shape[0]
    tn = 1024
    return pl.pallas_call(
        _mod_kernel,
        grid=(depth, n // tn),
        in_specs=[pl.BlockSpec((r, d), lambda l, j: (0, 0)),
                  pl.BlockSpec((None, d, tn), lambda l, j: (l, 0, j)),
                  pl.BlockSpec((None, 1, tn), lambda l, j: (l, 0, j))],
        out_specs=pl.BlockSpec((None, r, tn), lambda l, j: (l, 0, j)),
        out_shape=jax.ShapeDtypeStruct((depth, r, n), F32),
        compiler_params=_cparams(("arbitrary", "arbitrary")),
        name="modulation",
    )(cc, w_mod, b_mod.reshape(depth, 1, n))


def _rope(x, c, sa, sb, half):
    return x * c + pltpu.roll(x, LANE - half, 1) * sa + pltpu.roll(x, half, 1) * sb


def _inproj_kernel(x_ref, xp_ref, xn_ref, sh_ref, sc_ref, win_ref, gq_ref, wq_ref, gkv_ref, wkv_ref, tab_ref,
                   cw_ref, cb_ref, qm_ref, km_ref, vm_ref, dq_ref, dk_ref, dv_ref, hy_ref):
    x = jnp.concatenate([x_ref[0], xp_ref[0], xn_ref[0]], axis=0)
    h = _ln(x) * (1.0 + sc_ref[0]) + sh_ref[0]
    p_all = _bdot(h.astype(BF16), win_ref[...])
    tm = x_ref.shape[1]
    p = p_all[0:tm]
    tab = tab_ref[...]
    cm, sam, sbm = tab[:, 0:LANE], tab[:, LANE:2 * LANE], tab[:, 2 * LANE:3 * LANE]
    cd, sad, sbd = tab[:, 3 * LANE:4 * LANE], tab[:, 4 * LANE:5 * LANE], tab[:, 5 * LANE:6 * LANE]

    cq = p[:, C_CQ:C_CQ + MLA_Q_RANK]
    qn = cq * lax.rsqrt(jnp.mean(cq * cq, -1, keepdims=True) + LN_EPS) * gq_ref[...]
    q = _bdot(qn.astype(BF16), wq_ref[...])
    ckv = p[:, C_CKV:C_CKV + MLA_KV_RANK]
    kvn = ckv * lax.rsqrt(jnp.mean(ckv * ckv, -1, keepdims=True) + LN_EPS) * gkv_ref[...]
    kv = _bdot(kvn.astype(BF16), wkv_ref[...])
    kr = _rope(p[:, C_KR:C_KR + LANE], cm, sam, sbm, MLA_ROPE // 4)
    scale_m = LOG2E * (MLA_NOPE + MLA_ROPE) ** -0.5
    for hd in range(MLA_HEADS):
        sl = slice(hd * LANE, (hd + 1) * LANE)
        qm_ref[0, :, sl] = (_rope(q[:, sl], cm, sam, sbm, MLA_ROPE // 4) * scale_m).astype(BF16)
        km_ref[0, :, sl] = (kv[:, sl] + kr).astype(BF16)
    vm_ref[0] = kv[:, W_QM:W_QM + W_VM].astype(BF16)
    scale_d = LOG2E * DIFF_DIM ** -0.5
    for hd in range(DIFF_HEADS):
        sl = slice(hd * LANE, (hd + 1) * LANE)
        dq = p[:, C_DQ + hd * LANE:C_DQ + (hd + 1) * LANE]
        dk = p[:, C_DK + hd * LANE:C_DK + (hd + 1) * LANE]
        dq_ref[0, :, sl] = (_rope(dq, cd, sad, sbd, DIFF_DIM // 4) * scale_d).astype(BF16)
        dk_ref[0, :, sl] = _rope(dk, cd, sad, sbd, DIFF_DIM // 4).astype(BF16)
    dv_ref[0] = p[:, C_DV:C_DV + W_D].astype(BF16)
    u = p[:, C_HY:C_HY + W_HY]
    i = pl.program_id(0)
    u_prev = jnp.where(i == 0, 0.0, p_all[tm + SUBLANE - 1:tm + SUBLANE, C_HY:C_HY + W_HY])
    u_next = jnp.where(i == pl.num_programs(0) - 1, 0.0, p_all[tm + SUBLANE:tm + SUBLANE + 1, C_HY:C_HY + W_HY])
    row = lax.broadcasted_iota(jnp.int32, u.shape, 0)
    up = jnp.where(row == 0, u_prev, pltpu.roll(u, 1, 0))
    un = jnp.where(row == tm - 1, u_next, pltpu.roll(u, tm - 1, 0))
    w = cw_ref[...]
    hy_ref[0] = (up * w[0:1] + u * w[1:2] + un * w[2:3] + cb_ref[...]).astype(BF16)


def _in_proj(x, sh, sc, win, gq, wq, gkv, wkv, tab, conv_w, conv_b, tm):
    b, t, d = x.shape
    hb = tm // SUBLANE
    halo = (1, SUBLANE, d)
    per_batch = sh.shape[0] == b and b > 1
    mod_map = (lambda i, bb: (bb, 0, 0)) if per_batch else (lambda i, bb: (0, 0, 0))
    const = lambda i, bb: (0, 0)
    widths = (W_QM, W_QM, W_VM, W_D, W_D, W_D, W_HY)
    return pl.pallas_call(
        _inproj_kernel,
        grid=(t // tm, b),
        in_specs=[pl.BlockSpec((1, tm, d), lambda i, bb: (bb, i, 0)),
                  pl.BlockSpec(halo, lambda i, bb: (bb, jnp.maximum(i * hb - 1, 0), 0)),
                  pl.BlockSpec(halo, lambda i, bb: (bb, jnp.minimum((i + 1) * hb, t // SUBLANE - 1), 0)),
                  pl.BlockSpec((1, 1, d), mod_map),
                  pl.BlockSpec((1, 1, d), mod_map),
                  pl.BlockSpec(win.shape, const),
                  pl.BlockSpec(gq.shape, const),
                  pl.BlockSpec(wq.shape, const),
                  pl.BlockSpec(gkv.shape, const),
                  pl.BlockSpec(wkv.shape, const),
                  pl.BlockSpec((tm, 6 * LANE), lambda i, bb: (i, 0)),
                  pl.BlockSpec(conv_w.shape, const), pl.BlockSpec(conv_b.shape, const)],
        out_specs=[pl.BlockSpec((1, tm, w), lambda i, bb: (bb, i, 0)) for w in widths],
        out_shape=[jax.ShapeDtypeStruct((b, t, w), BF16) for w in widths],
        compiler_params=_cparams(("arbitrary", "arbitrary")),
        name="in_proj",
    )(x, x, x, sh, sc, win, gq, wq, gkv, wkv, tab, conv_w, conv_b)


NEG_BIG = -1e30


def _lane_fold(x, op):
    r = x[:, 0:LANE]
    for i in range(1, x.shape[1] // LANE):
        r = op(r, x[:, i * LANE:(i + 1) * LANE])
    return r


def _score_pass(q, parts, ksl, s_refs):
    mx = jnp.full((q.shape[0], LANE), NEG_BIG, F32)
    for (k_ref, _, n, tk), s_ref in zip(parts, s_refs):
        for j in range(n // tk):
            k = k_ref[0, j * tk:(j + 1) * tk, ksl]
            s = lax.dot_general(q, k, (((1,), (1,)), ((), ())), preferred_element_type=F32)
            s_ref[j] = s
            mx = jnp.maximum(mx, _lane_fold(s, jnp.maximum))
    return jnp.max(mx, axis=-1, keepdims=True)


def _softmax_pv(q, parts, ksl, vsl, s_refs):
    m = _score_pass(q, parts, ksl, s_refs)
    ls = jnp.zeros((q.shape[0], LANE), F32)
    acc = jnp.zeros((q.shape[0], LANE), F32)
    for (_, v_ref, n, tk), s_ref in zip(parts, s_refs):
        for j in range(n // tk):
            p = jnp.exp2(s_ref[j] - m)
            ls = ls + _lane_fold(p, jnp.add)
            acc = acc + _bdot(p.astype(BF16), v_ref[0, j * tk:(j + 1) * tk, vsl])
    return acc * (1.0 / jnp.sum(ls, axis=-1, keepdims=True))


def _mla_attn_kernel(*refs, part_shapes, n_pairs):
    np_ = len(part_shapes)
    q_ref = refs[0]
    k_refs = refs[1:1 + np_]
    v_refs = refs[1 + np_:1 + 2 * np_]
    o_ref = refs[1 + 2 * np_]
    s_sets = (refs[2 + 2 * np_:2 + 3 * np_], refs[2 + 3 * np_:])
    parts = [(k_refs[i], v_refs[i]) + part_shapes[i] for i in range(np_)]
    for pr in range(n_pairs):
        outs = []
        for hh in range(2):
            hd = 2 * pr + hh
            sl = slice(hd * LANE, (hd + 1) * LANE)
            outs.append(_softmax_pv(q_ref[0, :, sl], parts, sl, slice(pr * LANE, (pr + 1) * LANE), s_sets[hh]))
        lane = lax.broadcasted_iota(jnp.int32, outs[0].shape, 1)
        o_ref[0, :, pr * LANE:(pr + 1) * LANE] = jnp.where(lane < MLA_V, outs[0], outs[1]).astype(BF16)


def _chunk(n):
    return n if n <= 512 else 512


def _mla_attention(q, ks, vs, tq, n_pairs):
    b, t, _ = q.shape
    part_shapes = tuple((k.shape[1], _chunk(k.shape[1])) for k in ks)
    in_specs = [pl.BlockSpec((1, tq, 2 * n_pairs * LANE), lambda bb, g, i: (bb, i, g))]
    in_specs += [pl.BlockSpec((1, k.shape[1], 2 * n_pairs * LANE), lambda bb, g, i: (bb, 0, g)) for k in ks]
    in_specs += [pl.BlockSpec((1, v.shape[1], n_pairs * LANE), lambda bb, g, i: (bb, 0, g)) for v in vs]
    scratch = [pltpu.VMEM((n // tk, tq, tk), F32) for n, tk in part_shapes] * 2
    return pl.pallas_call(
        functools.partial(_mla_attn_kernel, part_shapes=part_shapes, n_pairs=n_pairs),
        grid=(b, MLA_HEADS // (2 * n_pairs), t // tq),
        in_specs=in_specs,
        out_specs=pl.BlockSpec((1, tq, n_pairs * LANE), lambda bb, g, i: (bb, i, g)),
        out_shape=jax.ShapeDtypeStruct((b, t, W_VM), BF16),
        scratch_shapes=scratch,
        compiler_params=_cparams(("arbitrary", "arbitrary", "arbitrary")),
        name="mla_attention",
    )(q, *ks, *vs)


def _diff_attn_kernel(*refs, part_shapes, lam_init, n_heads):
    np_ = len(part_shapes)
    q_ref = refs[0]
    k_refs = refs[1:1 + np_]
    v_refs = refs[1 + np_:1 + 2 * np_]
    lam_ref, g_ref, o_ref = refs[1 + 2 * np_:4 + 2 * np_]
    s_sets = (refs[4 + 2 * np_:4 + 3 * np_], refs[4 + 3 * np_:])
    parts = [(k_refs[i], v_refs[i]) + part_shapes[i] for i in range(np_)]
    lp = lam_ref[...]
    lam = (jnp.exp(jnp.sum(lp[0:1] * lp[1:2], axis=-1, keepdims=True))
           - jnp.exp(jnp.sum(lp[2:3] * lp[3:4], axis=-1, keepdims=True)) + lam_init)
    for hd in range(n_heads):
        sl = slice(hd * LANE, (hd + 1) * LANE)
        q = q_ref[0, :, sl]
        lane = lax.broadcasted_iota(jnp.int32, q.shape, 1)
        zero = jnp.zeros_like(q)
        o1 = _softmax_pv(jnp.where(lane < LANE // 2, q, zero), parts, sl, sl, s_sets[0])
        o2 = _softmax_pv(jnp.where(lane >= LANE // 2, q, zero), parts, sl, sl, s_sets[1])
        o = o1 - lam * o2
        ms = jnp.sum(o * o, axis=-1, keepdims=True) * (1.0 / DIFF_V)
        o_ref[0, :, sl] = (o * lax.rsqrt(ms + LN_EPS) * g_ref[...] * (1.0 - lam_init)).astype(BF16)


def _diff_attention(q, ks, vs, lam_p, subln, lam_init, tq, n_heads):
    b, t, _ = q.shape
    part_shapes = tuple((k.shape[1], _chunk(k.shape[1])) for k in ks)
    hmap = lambda bb, g, i: (bb, 0, g)
    in_specs = [pl.BlockSpec((1, tq, n_heads * LANE), lambda bb, g, i: (bb, i, g))]
    in_specs += [pl.BlockSpec((1, k.shape[1], n_heads * LANE), hmap) for k in ks]
    in_specs += [pl.BlockSpec((1, v.shape[1], n_heads * LANE), hmap) for v in vs]
    in_specs += [pl.BlockSpec(lam_p.shape, lambda bb, g, i: (0, 0)),
                 pl.BlockSpec(subln.shape, lambda bb, g, i: (0, 0))]
    scratch = [pltpu.VMEM((n // tk, tq, tk), F32) for n, tk in part_shapes] * 2
    return pl.pallas_call(
        functools.partial(_diff_attn_kernel, part_shapes=part_shapes, lam_init=lam_init, n_heads=n_heads),
        grid=(b, DIFF_HEADS // n_heads, t // tq),
        in_specs=in_specs,
        out_specs=pl.BlockSpec((1, tq, n_heads * LANE), lambda bb, g, i: (bb, i, g)),
        out_shape=jax.ShapeDtypeStruct((b, t, W_D), BF16),
        scratch_shapes=scratch,
        compiler_params=_cparams(("arbitrary", "arbitrary", "arbitrary")),
        name="diff_attention",
    )(q, *ks, *vs, lam_p, subln)


def _hyfilt_kernel(emb_ref, win_ref, fw1_ref, fb1_ref, fw2_ref, fb2_ref, fwf_ref, fbf_ref, fwb_ref, fbb_ref,
                   d_ref, h_scr):
    @pl.when(pl.program_id(0) == 0)
    def _():
        h1 = jnp.sin(jnp.dot(emb_ref[...], fw1_ref[...], preferred_element_type=F32, precision=HI) + fb1_ref[...])
        h_scr[...] = jnp.sin(jnp.dot(h1, fw2_ref[...], preferred_element_type=F32, precision=HI) + fb2_ref[...])

    h = h_scr[...]
    w = win_ref[...]
    fwd = (jnp.dot(h, fwf_ref[...], preferred_element_type=F32, precision=HI) + fbf_ref[...]) * w
    bwd = (jnp.dot(h, fwb_ref[...], preferred_element_type=F32, precision=HI) + fbb_ref[...]) * w
    row = lax.broadcasted_iota(jnp.int32, bwd.shape, 0)
    bwd = jnp.where(row == 0, 0.0, bwd)
    norm = jnp.sum(jnp.abs(fwd), axis=0, keepdims=True) + jnp.sum(jnp.abs(bwd), axis=0, keepdims=True)
    inv = 1.0 / norm
    d_ref[0] = ((fwd + bwd) * inv).astype(BF16)
    d_ref[1] = ((fwd - bwd) * inv).astype(BF16)


def _hy_filters(emb, win, fw1, fb1, fw2, fb2, fw3, fb3):
    l = emb.shape[0]
    hid = fw2.shape[0]
    nblk = HY_CH // LANE
    const = lambda g: (0, 0)
    fcol = lambda g: (0, (g // nblk) * 2 * nblk + g % nblk)
    bcol = lambda g: (0, (g // nblk) * 2 * nblk + nblk + g % nblk)
    return pl.pallas_call(
        _hyfilt_kernel,
        grid=(HY_ORDER * nblk,),
        in_specs=[pl.BlockSpec(emb.shape, const),
                  pl.BlockSpec((l, LANE), lambda g: (0, g % nblk)),
                  pl.BlockSpec(fw1.shape, const), pl.BlockSpec(fb1.shape, const),
                  pl.BlockSpec(fw2.shape, const), pl.BlockSpec(fb2.shape, const),
                  pl.BlockSpec((hid, LANE), fcol), pl.BlockSpec((1, LANE), fcol),
                  pl.BlockSpec((hid, LANE), bcol), pl.BlockSpec((1, LANE), bcol)],
        out_specs=pl.BlockSpec((2, None, l, LANE), lambda g: (0, g // nblk, 0, g % nblk)),
        out_shape=jax.ShapeDtypeStruct((2, HY_ORDER, l, HY_CH), BF16),
        scratch_shapes=[pltpu.VMEM((l, hid), F32)],
        compiler_params=_cparams(("arbitrary",)),
        name="hyena_filters",
    )(emb, win, fw1, fb1, fw2, fb2, fw3, fb3, fw3, fb3)


def _hyspec_kernel(f_ref, s_ref, d_ref, o_ref):
    tf = f_ref.shape[0] // 2
    o_ref[0:tf, :] = _bdot(f_ref[0:tf, :], s_ref[...])
    o_ref[tf:2 * tf, :] = _bdot(f_ref[tf:2 * tf, :], d_ref[...])


def _hy_spectrum(fmat, filt, tf):
    rows, l = fmat.shape
    return pl.pallas_call(
        _hyspec_kernel,
        grid=(rows // (2 * tf), HY_ORDER),
        in_specs=[pl.BlockSpec((2 * tf, l), lambda j, o: (j, 0)),
                  pl.BlockSpec((None, None, l, HY_CH), lambda j, o: (0, o, 0, 0)),
                  pl.BlockSpec((None, None, l, HY_CH), lambda j, o: (1, o, 0, 0))],
        out_specs=pl.BlockSpec((2 * tf, HY_CH), lambda j, o: (j, o)),
        out_shape=jax.ShapeDtypeStruct((rows, HY_ORDER * HY_CH), F32),
        compiler_params=_cparams(("arbitrary", "arbitrary")),
        name="hyena_filter_spectrum",
    )(fmat, filt, filt)


def _hyfwd_kernel(f_ref, z_ref, k_ref, y_ref):
    tf = f_ref.shape[0] // 2
    s = _bdot(f_ref[...], z_ref[0])
    sr, si = s[0:tf], s[tf:2 * tf]
    kr, ki = k_ref[0:tf, :], k_ref[tf:2 * tf, :]
    y_ref[0, 0:tf, :] = (sr * kr - si * ki).astype(BF16)
    y_ref[0, tf:2 * tf, :] = (sr * ki + si * kr).astype(BF16)


def _hy_forward(fmat, z, kspec, order, tf):
    rows, l = fmat.shape
    b = z.shape[0]
    return pl.pallas_call(
        _hyfwd_kernel,
        grid=(rows // (2 * tf), b),
        in_specs=[pl.BlockSpec((2 * tf, l), lambda j, bb: (j, 0)),
                  pl.BlockSpec((1, l, HY_CH), lambda j, bb: (bb, 0, 0)),
                  pl.BlockSpec((2 * tf, HY_CH), lambda j, bb: (j, order))],
        out_specs=pl.BlockSpec((1, 2 * tf, HY_CH), lambda j, bb: (bb, j, 0)),
        out_shape=jax.ShapeDtypeStruct((b, rows, HY_CH), BF16),
        compiler_params=_cparams(("arbitrary", "arbitrary")),
        name="hyena_dft_forward",
    )(fmat, z, kspec)


def _hyinv_kernel(g_ref, y_ref, z_ref, x_ref, b_ref, o_ref):
    conv = _bdot(g_ref[...], y_ref[0])
    z = z_ref[0].astype(F32)
    o_ref[0] = (x_ref[0].astype(F32) * (conv + z * b_ref[...])).astype(BF16)


def _hy_inverse(gmat, y, z, gate, bias, tt):
    l, rows = gmat.shape
    b = y.shape[0]
    return pl.pallas_call(
        _hyinv_kernel,
        grid=(l // tt, b),
        in_specs=[pl.BlockSpec((tt, rows), lambda i, bb: (i, 0)),
                  pl.BlockSpec((1, rows, HY_CH), lambda i, bb: (bb, 0, 0)),
                  pl.BlockSpec((1, tt, HY_CH), lambda i, bb: (bb, i, 0)),
                  pl.BlockSpec((1, tt, HY_CH), lambda i, bb: (bb, i, 0)),
                  pl.BlockSpec((1, HY_CH), lambda i, bb: (0, 0))],
        out_specs=pl.BlockSpec((1, tt, HY_CH), lambda i, bb: (bb, i, 0)),
        out_shape=jax.ShapeDtypeStruct((b, l, HY_CH), BF16),
        compiler_params=_cparams(("arbitrary", "arbitrary")),
        name="hyena_dft_inverse",
    )(gmat, y, z, gate, bias)


def _dft_tables(l, tf):
    k = jnp.arange(l, dtype=jnp.int32)
    n = jnp.arange(l, dtype=jnp.int32)
    ph = ((2 * k + 1)[:, None] * n[None, :]) % (4 * l)
    ang = ph.astype(F32) * (2.0 * math.pi / (4 * l))
    c = jnp.cos(ang).reshape(l // tf, 1, tf, l)
    s = (-jnp.sin(ang)).reshape(l // tf, 1, tf, l)
    f = jnp.concatenate([c, s], axis=1).reshape(2 * l, l)
    return f.astype(BF16), (f.T * (1.0 / l)).astype(BF16)


def _hy_consts(l):
    t = jnp.arange(l, dtype=F32)
    bands = jnp.arange(1, HY_BANDS + 1, dtype=F32)
    ang = (2.0 * math.pi / l) * t[:, None] * bands[None, :]
    emb = jnp.concatenate([(t / l)[:, None], jnp.cos(ang), jnp.sin(ang)], -1)
    emb = jnp.pad(emb, ((0, 0), (0, LANE - HY_EMB)))
    rates = jnp.linspace(HY_MIN_RATE, HY_MAX_RATE, HY_CH, dtype=F32)
    win = jnp.exp(-(t / l)[:, None] * rates[None, :])
    return emb, win


HY_N1 = 16
HY_J = 16
HY_GROUPS = 8
HY_KG = 8


def _hy2_tables(l):
    n1h, jj = HY_N1, HY_J
    n2, k1n = l // n1h, 2 * n1h
    k2n = n2 // 2
    k1 = jnp.arange(k1n, dtype=jnp.int32)
    ph1 = ((2 * k1 + 1)[:, None] * jnp.arange(n1h, dtype=jnp.int32)[None, :]) % (4 * n1h)
    a1 = ph1.astype(F32) * (2.0 * math.pi / (4 * n1h))
    f1 = jnp.stack([jnp.cos(a1), -jnp.sin(a1)], axis=1)
    f1big = jnp.einsum('krn,ab->kranb', f1, jnp.eye(jj, dtype=F32)).reshape(k1n * 2 * jj, n1h * jj)
    kk = k1[:, None] + k1n * jnp.arange(k2n, dtype=jnp.int32)[None, :]
    ph2 = ((2 * kk + 1)[:, :, None] * jnp.arange(n2, dtype=jnp.int32)[None, None, :]) % (4 * l)
    a2 = ph2.astype(F32) * (2.0 * math.pi / (4 * l))
    mr, mi = jnp.cos(a2), -jnp.sin(a2)
    f2t = jnp.concatenate([jnp.concatenate([mr, -mi], axis=2), jnp.concatenate([mi, mr], axis=2)], axis=1)
    return (f1big.astype(BF16), (f1big.T * (1.0 / l)).astype(BF16), f2t.astype(BF16),
            jnp.swapaxes(f2t, 1, 2).astype(BF16))


def _hy2_s1_kernel(f_ref, z_ref, o_ref):
    n1h, _, c = z_ref.shape[1:]
    k1n = o_ref.shape[1]
    for g in range(HY_GROUPS):
        cols = slice(g * HY_J, (g + 1) * HY_J)
        z = z_ref[0, :, cols, :].reshape(n1h * HY_J, c)
        a = _bdot(f_ref[...], z)
        o_ref[0, :, :, cols, :] = a.astype(BF16).reshape(k1n, 2, HY_J, c)


def _hy2_stage1(f1big, vxx, which, z5):
    src = vxx if z5 is None else z5
    b, n1h, n2, _ = src.shape
    c = HY_CH
    k1n = 2 * n1h
    w = HY_J * HY_GROUPS
    zspec = pl.BlockSpec((1, n1h, w, c), lambda bb, g: (bb, 0, g, which if z5 is None else 0))
    return pl.pallas_call(
        _hy2_s1_kernel,
        grid=(b, n2 // w),
        in_specs=[pl.BlockSpec(f1big.shape, lambda bb, g: (0, 0)), zspec],
        out_specs=pl.BlockSpec((1, k1n, 2, w, c), lambda bb, g: (bb, 0, 0, g, 0)),
        out_shape=jax.ShapeDtypeStruct((b, k1n, 2, n2, c), BF16),
        compiler_params=_cparams(("arbitrary", "arbitrary")),
        name="hyena_stage1",
    )(f1big, src)


def _hy2_fspec_kernel(f_ref, s_ref, d_ref, o_ref):
    h = o_ref.shape[1] // 2
    for i in range(HY_KG):
        o_ref[i, 0:h, :] = _bdot(f_ref[i], s_ref[0, i])[0:h]
        o_ref[i, h:2 * h, :] = _bdot(f_ref[i], d_ref[0, i])[h:2 * h]


def _hy2_filter_spectrum(f2t, a5):
    nsig, k1n, _, n2, c = a5.shape
    a4 = a5.reshape(nsig, k1n, 2 * n2, c)
    return pl.pallas_call(
        _hy2_fspec_kernel,
        grid=(k1n // HY_KG, HY_ORDER),
        in_specs=[pl.BlockSpec((HY_KG, n2, 2 * n2), lambda g, o: (g, 0, 0)),
                  pl.BlockSpec((1, HY_KG, 2 * n2, c), lambda g, o: (o, g, 0, 0)),
                  pl.BlockSpec((1, HY_KG, 2 * n2, c), lambda g, o: (HY_ORDER + o, g, 0, 0))],
        out_specs=pl.BlockSpec((HY_KG, n2, c), lambda g, o: (g, 0, o)),
        out_shape=jax.ShapeDtypeStruct((k1n, n2, HY_ORDER * c), F32),
        compiler_params=_cparams(("arbitrary", "arbitrary")),
        name="hyena_filter_spectrum2",
    )(f2t, a4, a4)


def _hy2_s2_kernel(f_ref, g_ref, a_ref, k_ref, o_ref):
    for i in range(HY_KG):
        y = _bdot(f_ref[i], a_ref[0, i])
        h = y.shape[0] // 2
        yr, yi = y[:h], y[h:]
        kr, ki = k_ref[i, 0:h, :], k_ref[i, h:2 * h, :]
        p = jnp.concatenate([yr * kr - yi * ki, yr * ki + yi * kr], axis=0).astype(BF16)
        o_ref[0, i] = _bdot(g_ref[i], p).astype(BF16)


def _hy2_stage2(f2t, g2t, a5, kspec, order):
    b, k1n, _, n2, c = a5.shape
    a4 = a5.reshape(b, k1n, 2 * n2, c)
    out = pl.pallas_call(
        _hy2_s2_kernel,
        grid=(k1n // HY_KG, b),
        in_specs=[pl.BlockSpec((HY_KG, n2, 2 * n2), lambda g, bb: (g, 0, 0)),
                  pl.BlockSpec((HY_KG, 2 * n2, n2), lambda g, bb: (g, 0, 0)),
                  pl.BlockSpec((1, HY_KG, 2 * n2, c), lambda g, bb: (bb, g, 0, 0)),
                  pl.BlockSpec((HY_KG, n2, c), lambda g, bb: (g, 0, order))],
        out_specs=pl.BlockSpec((1, HY_KG, 2 * n2, c), lambda g, bb: (bb, g, 0, 0)),
        out_shape=jax.ShapeDtypeStruct(a4.shape, BF16),
        compiler_params=_cparams(("arbitrary", "arbitrary")),
        name="hyena_stage2",
    )(f2t, g2t, a4, kspec)
    return out.reshape(a5.shape)


def _hy2_s3_kernel(g_ref, c_ref, z_ref, x_ref, b_ref, *rest, chain):
    if chain:
        f_ref, o_ref, a_ref = rest
    else:
        (o_ref,) = rest
    k1n = c_ref.shape[1]
    n1h, _, c = z_ref.shape[1:]
    for g in range(HY_GROUPS):
        cols = slice(g * HY_J, (g + 1) * HY_J)
        cc = c_ref[0, :, :, cols, :].reshape(k1n * 2 * HY_J, c)
        conv = _bdot(g_ref[...], cc)
        z = z_ref[0, :, cols, :].reshape(n1h * HY_J, c).astype(F32)
        x = x_ref[0, :, cols, :].reshape(n1h * HY_J, c).astype(F32)
        o = (x * (conv + z * b_ref[...])).astype(BF16)
        o_ref[0, :, cols, :] = o.reshape(n1h, HY_J, c)
        if chain:
            a_ref[0, :, :, cols, :] = _bdot(f_ref[...], o).astype(BF16).reshape(k1n, 2, HY_J, c)


def _hy2_stage3(g1big, c5, vxx, zwhich, z5, xwhich, bias, f1big=None):
    b, k1n, _, n2, c = c5.shape
    n1h = k1n // 2
    w = HY_J * HY_GROUPS
    chain = f1big is not None
    const = lambda bb, g: (0, 0)
    sel = lambda which: pl.BlockSpec((1, n1h, w, c), lambda bb, g: (bb, 0, g, which))
    row = pl.BlockSpec((1, n1h, w, c), lambda bb, g: (bb, 0, g, 0))
    spec5 = pl.BlockSpec((1, k1n, 2, w, c), lambda bb, g: (bb, 0, 0, g, 0))
    out5 = jax.ShapeDtypeStruct((b, n1h, n2, c), BF16)
    in_specs = [pl.BlockSpec(g1big.shape, const), spec5, sel(zwhich) if z5 is None else row, sel(xwhich),
                pl.BlockSpec((1, c), const)]
    args = [g1big, c5, vxx if z5 is None else z5, vxx, bias]
    if chain:
        in_specs.append(pl.BlockSpec(f1big.shape, const))
        args.append(f1big)
    return pl.pallas_call(
        functools.partial(_hy2_s3_kernel, chain=chain),
        grid=(b, n2 // w),
        in_specs=in_specs,
        out_specs=[row, spec5] if chain else row,
        out_shape=[out5, jax.ShapeDtypeStruct(c5.shape, BF16)] if chain else out5,
        compiler_params=_cparams(("arbitrary", "arbitrary")),
        name="hyena_stage3",
    )(*args)


def _hyena(vxx, consts, fw1, fb1, fw2, fb2, fw3, fb3, hbias):
    emb, win, tabs = consts
    filt = _hy_filters(emb, win, fw1, fb1, fw2, fb2, fw3, fb3)
    b, l, _ = vxx.shape
    c = HY_CH
    if len(tabs) == 3:
        fmat, gmat, tf = tabs
        kspec = _hy_spectrum(fmat, filt, tf)
        v, x1, x2 = vxx[..., 0:c], vxx[..., c:2 * c], vxx[..., 2 * c:3 * c]
        z = _hy_inverse(gmat, _hy_forward(fmat, v, kspec, 0, tf), v, x1, hbias[0:1], tf)
        return _hy_inverse(gmat, _hy_forward(fmat, z, kspec, 1, tf), z, x2, hbias[1:2], tf)
    f1big, g1big, f2t, g2t = tabs
    n2 = l // HY_N1
    filt5 = filt.reshape(2 * HY_ORDER, HY_N1, n2, c)
    kspec = _hy2_filter_spectrum(f2t, _hy2_stage1(f1big, None, 0, filt5))
    vxx5 = vxx.reshape(b, HY_N1, n2, 3 * c)
    c5 = _hy2_stage2(f2t, g2t, _hy2_stage1(f1big, vxx5, 0, None), kspec, 0)
    z5, a5 = _hy2_stage3(g1big, c5, vxx5, 0, None, 1, hbias[0:1], f1big)
    c5 = _hy2_stage2(f2t, g2t, a5, kspec, 1)
    return _hy2_stage3(g1big, c5, vxx5, 0, z5, 2, hbias[1:2]).reshape(b, l, c)


def _outproj_kernel(om_ref, od_ref, oh_ref, wm_ref, wd_ref, wh_ref, bo_ref, x_ref, g_ref, lg_ref, lb_ref,
                    o_ref, *, alpha):
    y = _bdot(om_ref[0], wm_ref[...]) + _bdot(od_ref[0], wd_ref[...]) + _bdot(oh_ref[0], wh_ref[...])
    z = alpha * x_ref[0] + g_ref[0] * (y + bo_ref[...])
    o_ref[0] = _ln(z) * lg_ref[...] + lb_ref[...]


def _out_proj(om, od, oh, wm, wd, wh, bo, x, gate, lg, lb, alpha, tm):
    b, t, d = x.shape
    per_batch = gate.shape[0] == b and b > 1
    gmap = (lambda bb, i: (bb, 0, 0)) if per_batch else (lambda bb, i: (0, 0, 0))
    const = lambda bb, i: (0, 0)
    row = lambda bb, i: (bb, i, 0)
    return pl.pallas_call(
        functools.partial(_outproj_kernel, alpha=alpha),
        grid=(b, t // tm),
        in_specs=[pl.BlockSpec((1, tm, om.shape[2]), row), pl.BlockSpec((1, tm, od.shape[2]), row),
                  pl.BlockSpec((1, tm, oh.shape[2]), row),
                  pl.BlockSpec(wm.shape, const), pl.BlockSpec(wd.shape, const), pl.BlockSpec(wh.shape, const),
                  pl.BlockSpec((1, d), const), pl.BlockSpec((1, tm, d), row), pl.BlockSpec((1, 1, d), gmap),
                  pl.BlockSpec((1, d), const), pl.BlockSpec((1, d), const)],
        out_specs=pl.BlockSpec((1, tm, d), row),
        out_shape=jax.ShapeDtypeStruct((b, t, d), F32),
        compiler_params=_cparams(("arbitrary", "arbitrary")),
        name="out_proj",
    )(om, od, oh, wm, wd, wh, bo, x, gate, lg, lb)


def _ffn_kernel(x_ref, sh_ref, sc_ref, g_ref, w1_ref, b1_ref, w2_ref, b2_ref, lg_ref, lb_ref, o_ref, *,
                alpha, tm, tf):
    nt = x_ref.shape[1] // tm
    rows = [slice(t * tm, (t + 1) * tm) for t in range(nt)]
    xs = [x_ref[0, r, :] for r in rows]
    hs = [(_ln(x) * (1.0 + sc_ref[0]) + sh_ref[0]).astype(BF16) for x in xs]
    accs = [None] * nt
    for c in range(w1_ref.shape[1] // tf):
        cols = slice(c * tf, (c + 1) * tf)
        for t in range(nt):
            a = jnp.maximum(_bdot(hs[t], w1_ref[:, cols]) + b1_ref[:, cols], 0.0)
            y = _bdot((a * a).astype(BF16), w2_ref[cols, :])
            accs[t] = y if accs[t] is None else accs[t] + y
    for t in range(nt):
        z = alpha * xs[t] + g_ref[0] * (accs[t] + b2_ref[...])
        o_ref[0, rows[t], :] = _ln(z) * lg_ref[...] + lb_ref[...]


def _ffn(x, sh, sc, gate, w1, b1, w2, b2, lg, lb, alpha, tm, nt, tf):
    b, t, d = x.shape
    per_batch = gate.shape[0] == b and b > 1
    gmap = (lambda bb, i: (bb, 0, 0)) if per_batch else (lambda bb, i: (0, 0, 0))
    const = lambda bb, i: (0, 0)
    row = lambda bb, i: (bb, i, 0)
    resident = pl.Buffered(1)
    return pl.pallas_call(
        functools.partial(_ffn_kernel, alpha=alpha, tm=tm, tf=tf),
        grid=(b, t // (tm * nt)),
        in_specs=[pl.BlockSpec((1, tm * nt, d), row),
                  pl.BlockSpec((1, 1, d), gmap), pl.BlockSpec((1, 1, d), gmap), pl.BlockSpec((1, 1, d), gmap),
                  pl.BlockSpec(w1.shape, const, pipeline_mode=resident), pl.BlockSpec(b1.shape, const),
                  pl.BlockSpec(w2.shape, const, pipeline_mode=resident), pl.BlockSpec((1, d), const),
                  pl.BlockSpec((1, d), const), pl.BlockSpec((1, d), const)],
        out_specs=pl.BlockSpec((1, tm * nt, d), row),
        out_shape=jax.ShapeDtypeStruct((b, t, d), F32),
        compiler_params=_cparams(("arbitrary", "arbitrary")),
        name="ffn",
    )(x, sh, sc, gate, w1, b1, w2, b2, lg, lb)


def _rope_lane_plan(starts, dims):
    h = dims // 2
    inv = ROPE_BASE ** (-(np.arange(h // 2, dtype=np.float64) * (2.0 / h)))
    fr, fc, lo, hi = (np.zeros(LANE, np.float32) for _ in range(4))
    for st in starts:
        for axis_i, f in enumerate((fr, fc)):
            base = st + axis_i * h
            f[base:base + h // 2] = inv
            f[base + h // 2:base + h] = inv
            lo[base:base + h // 2] = 1.0
            hi[base + h // 2:base + h] = 1.0
    return fr, fc, lo, hi


def _rope_tables(t_len, roped):
    pos_t = jnp.arange(t_len)
    scale = 1.0 if roped else 0.0
    row = (pos_t // GRID_W).astype(F32)[:, None] * scale
    col = (pos_t % GRID_W).astype(F32)[:, None] * scale
    out = []
    for starts, dims in (([MLA_NOPE], MLA_ROPE), ([0, LANE // 2], DIFF_DIM)):
        fr, fc, lo, hi = _rope_lane_plan(starts, dims)
        ang = row * fr[None, :] + col * fc[None, :]
        sin = jnp.sin(ang)
        out += [jnp.cos(ang), -sin * lo[None, :], sin * hi[None, :]]
    return jnp.concatenate(out, axis=1)


def _prep_weights(w_in, w_q_up, w_kv_up, w_out, diff_lambda, diff_subln):
    depth, d, _ = w_in.shape
    s1, s2 = IN_MLA, IN_MLA + IN_DIFF
    pad_last = lambda a, n: jnp.pad(a, [(0, 0)] * (a.ndim - 1) + [(0, n - a.shape[-1])])
    kr = jnp.pad(w_in[..., C_KR:s1], ((0, 0), (0, 0), (MLA_NOPE, LANE - MLA_NOPE - MLA_ROPE)))
    dq = pad_last(w_in[..., s1:s1 + N_DQ].reshape(depth, d, DIFF_HEADS, 2, DIFF_DIM), LANE // 2)
    dk = pad_last(w_in[..., s1 + N_DQ:s1 + 2 * N_DQ].reshape(depth, d, DIFF_HEADS, 2, DIFF_DIM), LANE // 2)
    dv = pad_last(w_in[..., s1 + 2 * N_DQ:s2].reshape(depth, d, DIFF_HEADS, DIFF_V), LANE)
    win = jnp.concatenate([w_in[..., :C_KR], kr, dq.reshape(depth, d, W_D), dk.reshape(depth, d, W_D),
                           dv.reshape(depth, d, W_D), w_in[..., s2:]], axis=-1).astype(BF16)
    wq = pad_last(w_q_up.reshape(depth, MLA_Q_RANK, MLA_HEADS, MLA_NOPE + MLA_ROPE), LANE)
    wq = wq.reshape(depth, MLA_Q_RANK, W_QM).astype(BF16)
    kv = w_kv_up.reshape(depth, MLA_KV_RANK, MLA_HEADS, MLA_NOPE + MLA_V)
    wk = pad_last(kv[..., :MLA_NOPE], LANE).reshape(depth, MLA_KV_RANK, W_QM)
    wv = kv[..., MLA_NOPE:].reshape(depth, MLA_KV_RANK, W_VM)
    wkv = jnp.concatenate([wk, wv], axis=-1).astype(BF16)
    wo_m = w_out[:, :W_VM].astype(BF16)
    wo_d = w_out[:, W_VM:W_VM + DIFF_HEADS * DIFF_V].reshape(depth, DIFF_HEADS, DIFF_V, -1)
    wo_d = jnp.pad(wo_d, ((0, 0), (0, 0), (0, LANE - DIFF_V), (0, 0))).reshape(depth, W_D, -1).astype(BF16)
    wo_h = w_out[:, W_VM + DIFF_HEADS * DIFF_V:].astype(BF16)
    lam = pad_last(diff_lambda.astype(F32), LANE)
    subln = pad_last(diff_subln.astype(F32), LANE)[:, None, :]
    return win, wq, wkv, wo_m, wo_d, wo_h, lam, subln


def kernel(x, c, ctx, c_ctx, w_mod, b_mod, w_in, mla_q_norm, w_q_up, mla_kv_norm, w_kv_up, diff_lambda, diff_subln, hy_conv_w, hy_conv_b, hy_fw1, hy_fb1, hy_fw2, hy_fb2, hy_fw3, hy_fb3, hy_bias, w_out, b_out, ln1_g, ln1_b, w_ff1, b_ff1, w_ff2, b_ff2, ln2_g, ln2_b):
    bsz, seq, d = x.shape
    n_ctx = ctx.shape[1]
    depth = w_in.shape[0]
    alpha = (2.0 * depth) ** 0.25

    rpad = -(bsz + 1) % SUBLANE
    cc = jnp.concatenate([c, c_ctx[None, :], jnp.zeros((rpad, d), F32)], axis=0)
    mod_all = _modulation(cc, w_mod, b_mod)

    win, wq, wkv, wo_m, wo_d, wo_h, lam_p, subln = _prep_weights(w_in, w_q_up, w_kv_up, w_out, diff_lambda, diff_subln)
    w1 = w_ff1.astype(BF16)
    w2 = w_ff2.astype(BF16)
    fw1 = jnp.pad(hy_fw1, ((0, 0), (0, LANE - HY_EMB), (0, 0)))
    tab_x = _rope_tables(seq, True)
    tab_c = _rope_tables(n_ctx, False)
    def hy_tables(l):
        if l >= 1024 and l % (HY_N1 * HY_J * HY_GROUPS) == 0:
            return _hy2_tables(l)
        return _dft_tables(l, min(512, l)) + (min(512, l),)

    hyc_x = _hy_consts(seq) + (hy_tables(seq),)
    hyc_c = _hy_consts(n_ctx) + (hy_tables(n_ctx),)
    tm_x, tm_c = min(512, seq), min(512, n_ctx)
    ntf_x = 2 if seq % (2 * tm_x) == 0 else 1
    tq_x, tq_c = min(256, seq), min(256, n_ctx)
    tff = min(1024, w1.shape[2])
    row2 = lambda a: a.reshape(1, -1)

    for layer in range(depth):
        need_ctx = layer < depth - 1
        lam_init = 0.8 - 0.6 * math.exp(-0.3 * layer)
        mod = mod_all[layer, :bsz].reshape(bsz, 1, N_MOD, d)
        modc = mod_all[layer, bsz:bsz + 1].reshape(1, 1, N_MOD, d)
        sh1, sc1, g1, sh2, sc2, g2 = [mod[:, :, i] for i in range(N_MOD)]
        csh1, csc1, cg1, csh2, csc2, cg2 = [modc[:, :, i] for i in range(N_MOD)]
        gq, gkv = row2(mla_q_norm[layer]), row2(mla_kv_norm[layer])
        hy_args = (fw1[layer], row2(hy_fb1[layer]), hy_fw2[layer], row2(hy_fb2[layer]), hy_fw3[layer],
                   row2(hy_fb3[layer]), hy_bias[layer])
        conv = (hy_conv_w[layer], row2(hy_conv_b[layer]))

        qm, km, vm, dq, dk, dv, uh = _in_proj(x, sh1, sc1, win[layer], gq, wq[layer], gkv, wkv[layer], tab_x, *conv, tm_x)
        cqm, ckm, cvm, cdq, cdk, cdv, cuh = _in_proj(ctx, csh1, csc1, win[layer], gq, wq[layer], gkv, wkv[layer],
                                                     tab_c, *conv, tm_c)
        om = _mla_attention(qm, [ckm, km], [cvm, vm], tq_x, MLA_HEADS // 2)
        od = _diff_attention(dq, [cdk, dk], [cdv, dv], lam_p[layer], subln[layer], lam_init, tq_x, DIFF_HEADS)
        oh = _hyena(uh, hyc_x, *hy_args)
        x = _out_proj(om, od, oh, wo_m[layer], wo_d[layer], wo_h[layer], row2(b_out[layer]), x, g1,
                      row2(ln1_g[layer]), row2(ln1_b[layer]), alpha, tm_x)
        x = _ffn(x, sh2, sc2, g2, w1[layer], row2(b_ff1[layer]), w2[layer], row2(b_ff2[layer]),
                 row2(ln2_g[layer]), row2(ln2_b[layer]), alpha, tm_x, ntf_x, tff)
        if need_ctx:
            com = _mla_attention(cqm, [ckm], [cvm], tq_c, MLA_HEADS // 2)
            cod = _diff_attention(cdq, [cdk], [cdv], lam_p[layer], subln[layer], lam_init, tq_c, DIFF_HEADS)
            coh = _hyena(cuh, hyc_c, *hy_args)
            ctx = _out_proj(com, cod, coh, wo_m[layer], wo_d[layer], wo_h[layer], row2(b_out[layer]), ctx, cg1,
                            row2(ln1_g[layer]), row2(ln1_b[layer]), alpha, tm_c)
            ctx = _ffn(ctx, csh2, csc2, cg2, w1[layer], row2(b_ff1[layer]), w2[layer], row2(b_ff2[layer]),
                       row2(ln2_g[layer]), row2(ln2_b[layer]), alpha, tm_c, 1, tff)
    return x
```

```python
import functools
import math

import jax
import numpy as np
import jax.numpy as jnp
from jax import lax
from jax.experimental import pallas as pl
from jax.experimental.pallas import tpu as pltpu

F32 = jnp.float32
BF16 = jnp.bfloat16
HI = lax.Precision.HIGHEST

GRID_W = 64
ROPE_BASE = 10000.0
LN_EPS = 1e-6
MLA_HEADS, MLA_NOPE, MLA_ROPE, MLA_V = 6, 64, 32, 64
MLA_Q_RANK, MLA_KV_RANK = 256, 128
DIFF_HEADS, DIFF_DIM = 4, 48
DIFF_V = 2 * DIFF_DIM
HY_CH, HY_ORDER, HY_BANDS = 256, 2, 16
HY_EMB = 1 + 2 * HY_BANDS
HY_TARGET, HY_FAST_DECAY_PCT, HY_SLOW_DECAY_PCT = 1e-2, 0.3, 1.5
HY_MIN_RATE = -math.log(HY_TARGET) / HY_SLOW_DECAY_PCT
HY_MAX_RATE = -math.log(HY_TARGET) / HY_FAST_DECAY_PCT
N_MOD = 6

LOG2E = 1.0 / math.log(2.0)
LANE = 128
SUBLANE = 8
VMEM_LIMIT = 56 * 1024 * 1024

IN_MLA = MLA_Q_RANK + MLA_KV_RANK + MLA_ROPE
N_DQ = DIFF_HEADS * 2 * DIFF_DIM
IN_DIFF = 2 * N_DQ + DIFF_HEADS * DIFF_V
W_QM = MLA_HEADS * LANE
W_VM = MLA_HEADS * MLA_V
W_D = DIFF_HEADS * LANE
W_HY = (HY_ORDER + 1) * HY_CH
C_CQ, C_CKV, C_KR = 0, MLA_Q_RANK, MLA_Q_RANK + MLA_KV_RANK
C_DQ = C_KR + LANE
C_DK = C_DQ + W_D
C_DV = C_DK + W_D
C_HY = C_DV + W_D
W_IN = C_HY + W_HY


def _cparams(sem):
    return pltpu.CompilerParams(dimension_semantics=sem, vmem_limit_bytes=VMEM_LIMIT)


def _ln(x):
    mu = jnp.mean(x, -1, keepdims=True)
    xc = x - mu
    var = jnp.mean(xc * xc, -1, keepdims=True)
    return xc * lax.rsqrt(var + LN_EPS)


def _bdot(a, b):
    return jnp.dot(a, b, preferred_element_type=F32)


def _mod_kernel(c_ref, w_ref, b_ref, o_ref):
    c = c_ref[...]
    s = c / (1.0 + jnp.exp(-c))
    o_ref[...] = jnp.dot(s, w_ref[...], preferred_element_type=F32, precision=HI) + b_ref[...]


def _modulation(cc, w_mod, b_mod):
    depth, d, n = w_mod.shape
    r = cc.shape[0]
    tn = 1024
    return pl.pallas_call(
        _mod_kernel,
        grid=(depth, n // tn),
        in_specs=[pl.BlockSpec((r, d), lambda l, j: (0, 0)),
                  pl.BlockSpec((None, d, tn), lambda l, j: (l, 0, j)),
                  pl.BlockSpec((None, 1, tn), lambda l, j: (l, 0, j))],
        out_specs=pl.BlockSpec((None, r, tn), lambda l, j: (l, 0, j)),
        out_shape=jax.ShapeDtypeStruct((depth, r, n), F32),
        compiler_params=_cparams(("arbitrary", "arbitrary")),
        name="modulation",
    )(cc, w_mod, b_mod.reshape(depth, 1, n))


def _rope(x, c, sa, sb, half):
    return x * c + pltpu.roll(x, LANE - half, 1) * sa + pltpu.roll(x, half, 1) * sb


def _inproj_kernel(x_ref, xp_ref, xn_ref, sh_ref, sc_ref, win_ref, gq_ref, wq_ref, gkv_ref, wkv_ref, tab_ref,
                   cw_ref, cb_ref, qm_ref, km_ref, vm_ref, dq_ref, dk_ref, dv_ref, hy_ref):
    x = jnp.concatenate([x_ref[0], xp_ref[0], xn_ref[0]], axis=0)
    h = _ln(x) * (1.0 + sc_ref[0]) + sh_ref[0]
    p_all = _bdot(h.astype(BF16), win_ref[...])
    tm = x_ref.shape[1]
    p = p_all[0:tm]
    tab = tab_ref[...]
    cm, sam, sbm = tab[:, 0:LANE], tab[:, LANE:2 * LANE], tab[:, 2 * LANE:3 * LANE]
    cd, sad, sbd = tab[:, 3 * LANE:4 * LANE], tab[:, 4 * LANE:5 * LANE], tab[:, 5 * LANE:6 * LANE]

    cq = p[:, C_CQ:C_CQ + MLA_Q_RANK]
    qn = cq * lax.rsqrt(jnp.mean(cq * cq, -1, keepdims=True) + LN_EPS) * gq_ref[...]
    q = _bdot(qn.astype(BF16), wq_ref[...])
    ckv = p[:, C_CKV:C_CKV + MLA_KV_RANK]
    kvn = ckv * lax.rsqrt(jnp.mean(ckv * ckv, -1, keepdims=True) + LN_EPS) * gkv_ref[...]
    kv = _bdot(kvn.astype(BF16), wkv_ref[...])
    kr = _rope(p[:, C_KR:C_KR + LANE], cm, sam, sbm, MLA_ROPE // 4)
    scale_m = LOG2E * (MLA_NOPE + MLA_ROPE) ** -0.5
    for hd in range(MLA_HEADS):
        sl = slice(hd * LANE, (hd + 1) * LANE)
        qm_ref[0, :, sl] = (_rope(q[:, sl], cm, sam, sbm, MLA_ROPE // 4) * scale_m).astype(BF16)
        km_ref[0, :, sl] = (kv[:, sl] + kr).astype(BF16)
    vm_ref[0] = kv[:, W_QM:W_QM + W_VM].astype(BF16)
    scale_d = LOG2E * DIFF_DIM ** -0.5
    for hd in range(DIFF_HEADS):
        sl = slice(hd * LANE, (hd + 1) * LANE)
        dq = p[:, C_DQ + hd * LANE:C_DQ + (hd + 1) * LANE]
        dk = p[:, C_DK + hd * LANE:C_DK + (hd + 1) * LANE]
        dq_ref[0, :, sl] = (_rope(dq, cd, sad, sbd, DIFF_DIM // 4) * scale_d).astype(BF16)
        dk_ref[0, :, sl] = _rope(dk, cd, sad, sbd, DIFF_DIM // 4).astype(BF16)
    dv_ref[0] = p[:, C_DV:C_DV + W_D].astype(BF16)
    u = p[:, C_HY:C_HY + W_HY]
    i = pl.program_id(0)
    u_prev = jnp.where(i == 0, 0.0, p_all[tm + SUBLANE - 1:tm + SUBLANE, C_HY:C_HY + W_HY])
    u_next = jnp.where(i == pl.num_programs(0) - 1, 0.0, p_all[tm + SUBLANE:tm + SUBLANE + 1, C_HY:C_HY + W_HY])
    row = lax.broadcasted_iota(jnp.int32, u.shape, 0)
    up = jnp.where(row == 0, u_prev, pltpu.roll(u, 1, 0))
    un = jnp.where(row == tm - 1, u_next, pltpu.roll(u, tm - 1, 0))
    w = cw_ref[...]
    hy_ref[0] = (up * w[0:1] + u * w[1:2] + un * w[2:3] + cb_ref[...]).astype(BF16)


def _in_proj(x, sh, sc, win, gq, wq, gkv, wkv, tab, conv_w, conv_b, tm):
    b, t, d = x.shape
    hb = tm // SUBLANE
    halo = (1, SUBLANE, d)
    per_batch = sh.shape[0] == b and b > 1
    mod_map = (lambda i, bb: (bb, 0, 0)) if per_batch else (lambda i, bb: (0, 0, 0))
    const = lambda i, bb: (0, 0)
    widths = (W_QM, W_QM, W_VM, W_D, W_D, W_D, W_HY)
    return pl.pallas_call(
        _inproj_kernel,
        grid=(t // tm, b),
        in_specs=[pl.BlockSpec((1, tm, d), lambda i, bb: (bb, i, 0)),
                  pl.BlockSpec(halo, lambda i, bb: (bb, jnp.maximum(i * hb - 1, 0), 0)),
                  pl.BlockSpec(halo, lambda i, bb: (bb, jnp.minimum((i + 1) * hb, t // SUBLANE - 1), 0)),
                  pl.BlockSpec((1, 1, d), mod_map),
                  pl.BlockSpec((1, 1, d), mod_map),
                  pl.BlockSpec(win.shape, const),
                  pl.BlockSpec(gq.shape, const),
                  pl.BlockSpec(wq.shape, const),
                  pl.BlockSpec(gkv.shape, const),
                  pl.BlockSpec(wkv.shape, const),
                  pl.BlockSpec((tm, 6 * LANE), lambda i, bb: (i, 0)),
                  pl.BlockSpec(conv_w.shape, const), pl.BlockSpec(conv_b.shape, const)],
        out_specs=[pl.BlockSpec((1, tm, w), lambda i, bb: (bb, i, 0)) for w in widths],
        out_shape=[jax.ShapeDtypeStruct((b, t, w), BF16) for w in widths],
        compiler_params=_cparams(("arbitrary", "arbitrary")),
        name="in_proj",
    )(x, x, x, sh, sc, win, gq, wq, gkv, wkv, tab, conv_w, conv_b)


NEG_BIG = -1e30


def _lane_fold(x, op):
    r = x[:, 0:LANE]
    for i in range(1, x.shape[1] // LANE):
        r = op(r, x[:, i * LANE:(i + 1) * LANE])
    return r


def _score_pass(q, parts, ksl, s_refs):
    mx = jnp.full((q.shape[0], LANE), NEG_BIG, F32)
    for (k_ref, _, n, tk), s_ref in zip(parts, s_refs):
        for j in range(n // tk):
            k = k_ref[0, j * tk:(j + 1) * tk, ksl]
            s = lax.dot_general(q, k, (((1,), (1,)), ((), ())), preferred_element_type=F32)
            s_ref[j] = s
            mx = jnp.maximum(mx, _lane_fold(s, jnp.maximum))
    return jnp.max(mx, axis=-1, keepdims=True)


def _softmax_pv(q, parts, ksl, vsl, s_refs):
    m = _score_pass(q, parts, ksl, s_refs)
    ls = jnp.zeros((q.shape[0], LANE), F32)
    acc = jnp.zeros((q.shape[0], LANE), F32)
    for (_, v_ref, n, tk), s_ref in zip(parts, s_refs):
        for j in range(n // tk):
            p = jnp.exp2(s_ref[j] - m)
            ls = ls + _lane_fold(p, jnp.add)
            acc = acc + _bdot(p.astype(BF16), v_ref[0, j * tk:(j + 1) * tk, vsl])
    return acc * (1.0 / jnp.sum(ls, axis=-1, keepdims=True))


def _mla_attn_kernel(*refs, part_shapes, n_pairs):
    np_ = len(part_shapes)
    q_ref = refs[0]
    k_refs = refs[1:1 + np_]
    v_refs = refs[1 + np_:1 + 2 * np_]
    o_ref = refs[1 + 2 * np_]
    s_sets = (refs[2 + 2 * np_:2 + 3 * np_], refs[2 + 3 * np_:])
    parts = [(k_refs[i], v_refs[i]) + part_shapes[i] for i in range(np_)]
    for pr in range(n_pairs):
        outs = []
        for hh in range(2):
            hd = 2 * pr + hh
            sl = slice(hd * LANE, (hd + 1) * LANE)
            outs.append(_softmax_pv(q_ref[0, :, sl], parts, sl, slice(pr * LANE, (pr + 1) * LANE), s_sets[hh]))
        lane = lax.broadcasted_iota(jnp.int32, outs[0].shape, 1)
        o_ref[0, :, pr * LANE:(pr + 1) * LANE] = jnp.where(lane < MLA_V, outs[0], outs[1]).astype(BF16)


def _chunk(n):
    return n if n <= 512 else 512


def _mla_attention(q, ks, vs, tq, n_pairs):
    b, t, _ = q.shape
    part_shapes = tuple((k.shape[1], _chunk(k.shape[1])) for k in ks)
    in_specs = [pl.BlockSpec((1, tq, 2 * n_pairs * LANE), lambda bb, g, i: (bb, i, g))]
    in_specs += [pl.BlockSpec((1, k.shape[1], 2 * n_pairs * LANE), lambda bb, g, i: (bb, 0, g)) for k in ks]
    in_specs += [pl.BlockSpec((1, v.shape[1], n_pairs * LANE), lambda bb, g, i: (bb, 0, g)) for v in vs]
    scratch = [pltpu.VMEM((n // tk, tq, tk), F32) for n, tk in part_shapes] * 2
    return pl.pallas_call(
        functools.partial(_mla_attn_kernel, part_shapes=part_shapes, n_pairs=n_pairs),
        grid=(b, MLA_HEADS // (2 * n_pairs), t // tq),
        in_specs=in_specs,
        out_specs=pl.BlockSpec((1, tq, n_pairs * LANE), lambda bb, g, i: (bb, i, g)),
        out_shape=jax.ShapeDtypeStruct((b, t, W_VM), BF16),
        scratch_shapes=scratch,
        compiler_params=_cparams(("arbitrary", "arbitrary", "arbitrary")),
        name="mla_attention",
    )(q, *ks, *vs)


def _diff_attn_kernel(*refs, part_shapes, lam_init, n_heads):
    np_ = len(part_shapes)
    q_ref = refs[0]
    k_refs = refs[1:1 + np_]
    v_refs = refs[1 + np_:1 + 2 * np_]
    lam_ref, g_ref, o_ref = refs[1 + 2 * np_:4 + 2 * np_]
    s_sets = (refs[4 + 2 * np_:4 + 3 * np_], refs[4 + 3 * np_:])
    parts = [(k_refs[i], v_refs[i]) + part_shapes[i] for i in range(np_)]
    lp = lam_ref[...]
    lam = (jnp.exp(jnp.sum(lp[0:1] * lp[1:2], axis=-1, keepdims=True))
           - jnp.exp(jnp.sum(lp[2:3] * lp[3:4], axis=-1, keepdims=True)) + lam_init)
    for hd in range(n_heads):
        sl = slice(hd * LANE, (hd + 1) * LANE)
        q = q_ref[0, :, sl]
        lane = lax.broadcasted_iota(jnp.int32, q.shape, 1)
        zero = jnp.zeros_like(q)
        o1 = _softmax_pv(jnp.where(lane < LANE // 2, q, zero), parts, sl, sl, s_sets[0])
        o2 = _softmax_pv(jnp.where(lane >= LANE // 2, q, zero), parts, sl, sl, s_sets[1])
        o = o1 - lam * o2
        ms = jnp.sum(o * o, axis=-1, keepdims=True) * (1.0 / DIFF_V)
        o_ref[0, :, sl] = (o * lax.rsqrt(ms + LN_EPS) * g_ref[...] * (1.0 - lam_init)).astype(BF16)


def _diff_attention(q, ks, vs, lam_p, subln, lam_init, tq, n_heads):
    b, t, _ = q.shape
    part_shapes = tuple((k.shape[1], _chunk(k.shape[1])) for k in ks)
    hmap = lambda bb, g, i: (bb, 0, g)
    in_specs = [pl.BlockSpec((1, tq, n_heads * LANE), lambda bb, g, i: (bb, i, g))]
    in_specs += [pl.BlockSpec((1, k.shape[1], n_heads * LANE), hmap) for k in ks]
    in_specs += [pl.BlockSpec((1, v.shape[1], n_heads * LANE), hmap) for v in vs]
    in_specs += [pl.BlockSpec(lam_p.shape, lambda bb, g, i: (0, 0)),
                 pl.BlockSpec(subln.shape, lambda bb, g, i: (0, 0))]
    scratch = [pltpu.VMEM((n // tk, tq, tk), F32) for n, tk in part_shapes] * 2
    return pl.pallas_call(
        functools.partial(_diff_attn_kernel, part_shapes=part_shapes, lam_init=lam_init, n_heads=n_heads),
        grid=(b, DIFF_HEADS // n_heads, t // tq),
        in_specs=in_specs,
        out_specs=pl.BlockSpec((1, tq, n_heads * LANE), lambda bb, g, i: (bb, i, g)),
        out_shape=jax.ShapeDtypeStruct((b, t, W_D), BF16),
        scratch_shapes=scratch,
        compiler_params=_cparams(("arbitrary", "arbitrary", "arbitrary")),
        name="diff_attention",
    )(q, *ks, *vs, lam_p, subln)


def _hyfilt_kernel(emb_ref, win_ref, fw1_ref, fb1_ref, fw2_ref, fb2_ref, fwf_ref, fbf_ref, fwb_ref, fbb_ref,
                   d_ref, h_scr):
    @pl.when(pl.program_id(0) == 0)
    def _():
        h1 = jnp.sin(jnp.dot(emb_ref[...], fw1_ref[...], preferred_element_type=F32, precision=HI) + fb1_ref[...])
        h_scr[...] = jnp.sin(jnp.dot(h1, fw2_ref[...], preferred_element_type=F32, precision=HI) + fb2_ref[...])

    h = h_scr[...]
    w = win_ref[...]
    fwd = (jnp.dot(h, fwf_ref[...], preferred_element_type=F32, precision=HI) + fbf_ref[...]) * w
    bwd = (jnp.dot(h, fwb_ref[...], preferred_element_type=F32, precision=HI) + fbb_ref[...]) * w
    row = lax.broadcasted_iota(jnp.int32, bwd.shape, 0)
    bwd = jnp.where(row == 0, 0.0, bwd)
    norm = jnp.sum(jnp.abs(fwd), axis=0, keepdims=True) + jnp.sum(jnp.abs(bwd), axis=0, keepdims=True)
    inv = 1.0 / norm
    d_ref[0] = ((fwd + bwd) * inv).astype(BF16)
    d_ref[1] = ((fwd - bwd) * inv).astype(BF16)


def _hy_filters(emb, win, fw1, fb1, fw2, fb2, fw3, fb3):
    l = emb.shape[0]
    hid = fw2.shape[0]
    nblk = HY_CH // LANE
    const = lambda g: (0, 0)
    fcol = lambda g: (0, (g // nblk) * 2 * nblk + g % nblk)
    bcol = lambda g: (0, (g // nblk) * 2 * nblk + nblk + g % nblk)
    return pl.pallas_call(
        _hyfilt_kernel,
        grid=(HY_ORDER * nblk,),
        in_specs=[pl.BlockSpec(emb.shape, const),
                  pl.BlockSpec((l, LANE), lambda g: (0, g % nblk)),
                  pl.BlockSpec(fw1.shape, const), pl.BlockSpec(fb1.shape, const),
                  pl.BlockSpec(fw2.shape, const), pl.BlockSpec(fb2.shape, const),
                  pl.BlockSpec((hid, LANE), fcol), pl.BlockSpec((1, LANE), fcol),
                  pl.BlockSpec((hid, LANE), bcol), pl.BlockSpec((1, LANE), bcol)],
        out_specs=pl.BlockSpec((2, None, l, LANE), lambda g: (0, g // nblk, 0, g % nblk)),
        out_shape=jax.ShapeDtypeStruct((2, HY_ORDER, l, HY_CH), BF16),
        scratch_shapes=[pltpu.VMEM((l, hid), F32)],
        compiler_params=_cparams(("arbitrary",)),
        name="hyena_filters",
    )(emb, win, fw1, fb1, fw2, fb2, fw3, fb3, fw3, fb3)


def _hyspec_kernel(f_ref, s_ref, d_ref, o_ref):
    tf = f_ref.shape[0] // 2
    o_ref[0:tf, :] = _bdot(f_ref[0:tf, :], s_ref[...])
    o_ref[tf:2 * tf, :] = _bdot(f_ref[tf:2 * tf, :], d_ref[...])


def _hy_spectrum(fmat, filt, tf):
    rows, l = fmat.shape
    return pl.pallas_call(
        _hyspec_kernel,
        grid=(rows // (2 * tf), HY_ORDER),
        in_specs=[pl.BlockSpec((2 * tf, l), lambda j, o: (j, 0)),
                  pl.BlockSpec((None, None, l, HY_CH), lambda j, o: (0, o, 0, 0)),
                  pl.BlockSpec((None, None, l, HY_CH), lambda j, o: (1, o, 0, 0))],
        out_specs=pl.BlockSpec((2 * tf, HY_CH), lambda j, o: (j, o)),
        out_shape=jax.ShapeDtypeStruct((rows, HY_ORDER * HY_CH), F32),
        compiler_params=_cparams(("arbitrary", "arbitrary")),
        name="hyena_filter_spectrum",
    )(fmat, filt, filt)


def _hyfwd_kernel(f_ref, z_ref, k_ref, y_ref):
    tf = f_ref.shape[0] // 2
    s = _bdot(f_ref[...], z_ref[0])
    sr, si = s[0:tf], s[tf:2 * tf]
    kr, ki = k_ref[0:tf, :], k_ref[tf:2 * tf, :]
    y_ref[0, 0:tf, :] = (sr * kr - si * ki).astype(BF16)
    y_ref[0, tf:2 * tf, :] = (sr * ki + si * kr).astype(BF16)


def _hy_forward(fmat, z, kspec, order, tf):
    rows, l = fmat.shape
    b = z.shape[0]
    return pl.pallas_call(
        _hyfwd_kernel,
        grid=(rows // (2 * tf), b),
        in_specs=[pl.BlockSpec((2 * tf, l), lambda j, bb: (j, 0)),
                  pl.BlockSpec((1, l, HY_CH), lambda j, bb: (bb, 0, 0)),
                  pl.BlockSpec((2 * tf, HY_CH), lambda j, bb: (j, order))],
        out_specs=pl.BlockSpec((1, 2 * tf, HY_CH), lambda j, bb: (bb, j, 0)),
        out_shape=jax.ShapeDtypeStruct((b, rows, HY_CH), BF16),
        compiler_params=_cparams(("arbitrary", "arbitrary")),
        name="hyena_dft_forward",
    )(fmat, z, kspec)


def _hyinv_kernel(g_ref, y_ref, z_ref, x_ref, b_ref, o_ref):
    conv = _bdot(g_ref[...], y_ref[0])
    z = z_ref[0].astype(F32)
    o_ref[0] = (x_ref[0].astype(F32) * (conv + z * b_ref[...])).astype(BF16)


def _hy_inverse(gmat, y, z, gate, bias, tt):
    l, rows = gmat.shape
    b = y.shape[0]
    return pl.pallas_call(
        _hyinv_kernel,
        grid=(l // tt, b),
        in_specs=[pl.BlockSpec((tt, rows), lambda i, bb: (i, 0)),
                  pl.BlockSpec((1, rows, HY_CH), lambda i, bb: (bb, 0, 0)),
                  pl.BlockSpec((1, tt, HY_CH), lambda i, bb: (bb, i, 0)),
                  pl.BlockSpec((1, tt, HY_CH), lambda i, bb: (bb, i, 0)),
                  pl.BlockSpec((1, HY_CH), lambda i, bb: (0, 0))],
        out_specs=pl.BlockSpec((1, tt, HY_CH), lambda i, bb: (bb, i, 0)),
        out_shape=jax.ShapeDtypeStruct((b, l, HY_CH), BF16),
        compiler_params=_cparams(("arbitrary", "arbitrary")),
        name="hyena_dft_inverse",
    )(gmat, y, z, gate, bias)


def _dft_tables(l, tf):
    k = jnp.arange(l, dtype=jnp.int32)
    n = jnp.arange(l, dtype=jnp.int32)
    ph = ((2 * k + 1)[:, None] * n[None, :]) % (4 * l)
    ang = ph.astype(F32) * (2.0 * math.pi / (4 * l))
    c = jnp.cos(ang).reshape(l // tf, 1, tf, l)
    s = (-jnp.sin(ang)).reshape(l // tf, 1, tf, l)
    f = jnp.concatenate([c, s], axis=1).reshape(2 * l, l)
    return f.astype(BF16), (f.T * (1.0 / l)).astype(BF16)


def _hy_consts(l):
    t = jnp.arange(l, dtype=F32)
    bands = jnp.arange(1, HY_BANDS + 1, dtype=F32)
    ang = (2.0 * math.pi / l) * t[:, None] * bands[None, :]
    emb = jnp.concatenate([(t / l)[:, None], jnp.cos(ang), jnp.sin(ang)], -1)
    emb = jnp.pad(emb, ((0, 0), (0, LANE - HY_EMB)))
    rates = jnp.linspace(HY_MIN_RATE, HY_MAX_RATE, HY_CH, dtype=F32)
    win = jnp.exp(-(t / l)[:, None] * rates[None, :])
    return emb, win


HY_N1 = 16
HY_J = 16
HY_GROUPS = 8
HY_KG = 8


def _hy2_tables(l):
    n1h, jj = HY_N1, HY_J
    n2, k1n = l // n1h, 2 * n1h
    k2n = n2 // 2
    k1 = jnp.arange(k1n, dtype=jnp.int32)
    ph1 = ((2 * k1 + 1)[:, None] * jnp.arange(n1h, dtype=jnp.int32)[None, :]) % (4 * n1h)
    a1 = ph1.astype(F32) * (2.0 * math.pi / (4 * n1h))
    f1 = jnp.stack([jnp.cos(a1), -jnp.sin(a1)], axis=1)
    f1big = jnp.einsum('krn,ab->kranb', f1, jnp.eye(jj, dtype=F32)).reshape(k1n * 2 * jj, n1h * jj)
    kk = k1[:, None] + k1n * jnp.arange(k2n, dtype=jnp.int32)[None, :]
    ph2 = ((2 * kk + 1)[:, :, None] * jnp.arange(n2, dtype=jnp.int32)[None, None, :]) % (4 * l)
    a2 = ph2.astype(F32) * (2.0 * math.pi / (4 * l))
    mr, mi = jnp.cos(a2), -jnp.sin(a2)
    f2t = jnp.concatenate([jnp.concatenate([mr, -mi], axis=2), jnp.concatenate([mi, mr], axis=2)], axis=1)
    return (f1big.astype(BF16), (f1big.T * (1.0 / l)).astype(BF16), f2t.astype(BF16),
            jnp.swapaxes(f2t, 1, 2).astype(BF16))


def _hy2_s1_kernel(f_ref, z_ref, o_ref):
    n1h, _, c = z_ref.shape[1:]
    k1n = o_ref.shape[1]
    for g in range(HY_GROUPS):
        cols = slice(g * HY_J, (g + 1) * HY_J)
        z = z_ref[0, :, cols, :].reshape(n1h * HY_J, c)
        a = _bdot(f_ref[...], z)
        o_ref[0, :, :, cols, :] = a.astype(BF16).reshape(k1n, 2, HY_J, c)


def _hy2_stage1(f1big, vxx, which, z5):
    src = vxx if z5 is None else z5
    b, n1h, n2, _ = src.shape
    c = HY_CH
    k1n = 2 * n1h
    w = HY_J * HY_GROUPS
    zspec = pl.BlockSpec((1, n1h, w, c), lambda bb, g: (bb, 0, g, which if z5 is None else 0))
    return pl.pallas_call(
        _hy2_s1_kernel,
        grid=(b, n2 // w),
        in_specs=[pl.BlockSpec(f1big.shape, lambda bb, g: (0, 0)), zspec],
        out_specs=pl.BlockSpec((1, k1n, 2, w, c), lambda bb, g: (bb, 0, 0, g, 0)),
        out_shape=jax.ShapeDtypeStruct((b, k1n, 2, n2, c), BF16),
        compiler_params=_cparams(("arbitrary", "arbitrary")),
        name="hyena_stage1",
    )(f1big, src)


def _hy2_fspec_kernel(f_ref, s_ref, d_ref, o_ref):
    h = o_ref.shape[1] // 2
    for i in range(HY_KG):
        o_ref[i, 0:h, :] = _bdot(f_ref[i], s_ref[0, i])[0:h]
        o_ref[i, h:2 * h, :] = _bdot(f_ref[i], d_ref[0, i])[h:2 * h]


def _hy2_filter_spectrum(f2t, a5):
    nsig, k1n, _, n2, c = a5.shape
    a4 = a5.reshape(nsig, k1n, 2 * n2, c)
    return pl.pallas_call(
        _hy2_fspec_kernel,
        grid=(k1n // HY_KG, HY_ORDER),
        in_specs=[pl.BlockSpec((HY_KG, n2, 2 * n2), lambda g, o: (g, 0, 0)),
                  pl.BlockSpec((1, HY_KG, 2 * n2, c), lambda g, o: (o, g, 0, 0)),
                  pl.BlockSpec((1, HY_KG, 2 * n2, c), lambda g, o: (HY_ORDER + o, g, 0, 0))],
        out_specs=pl.BlockSpec((HY_KG, n2, c), lambda g, o: (g, 0, o)),
        out_shape=jax.ShapeDtypeStruct((k1n, n2, HY_ORDER * c), F32),
        compiler_params=_cparams(("arbitrary", "arbitrary")),
        name="hyena_filter_spectrum2",
    )(f2t, a4, a4)


def _hy2_s2_kernel(f_ref, g_ref, a_ref, k_ref, o_ref):
    for i in range(HY_KG):
        y = _bdot(f_ref[i], a_ref[0, i])
        h = y.shape[0] // 2
        yr, yi = y[:h], y[h:]
        kr, ki = k_ref[i, 0:h, :], k_ref[i, h:2 * h, :]
        p = jnp.concatenate([yr * kr - yi * ki, yr * ki + yi * kr], axis=0).astype(BF16)
        o_ref[0, i] = _bdot(g_ref[i], p).astype(BF16)


def _hy2_stage2(f2t, g2t, a5, kspec, order):
    b, k1n, _, n2, c = a5.shape
    a4 = a5.reshape(b, k1n, 2 * n2, c)
    out = pl.pallas_call(
        _hy2_s2_kernel,
        grid=(k1n // HY_KG, b),
        in_specs=[pl.BlockSpec((HY_KG, n2, 2 * n2), lambda g, bb: (g, 0, 0)),
                  pl.BlockSpec((HY_KG, 2 * n2, n2), lambda g, bb: (g, 0, 0)),
                  pl.BlockSpec((1, HY_KG, 2 * n2, c), lambda g, bb: (bb, g, 0, 0)),
                  pl.BlockSpec((HY_KG, n2, c), lambda g, bb: (g, 0, order))],
        out_specs=pl.BlockSpec((1, HY_KG, 2 * n2, c), lambda g, bb: (bb, g, 0, 0)),
        out_shape=jax.ShapeDtypeStruct(a4.shape, BF16),
        compiler_params=_cparams(("arbitrary", "arbitrary")),
        name="hyena_stage2",
    )(f2t, g2t, a4, kspec)
    return out.reshape(a5.shape)


def _hy2_s3_kernel(g_ref, c_ref, z_ref, x_ref, b_ref, *rest, chain):
    if chain:
        f_ref, o_ref, a_ref = rest
    else:
        (o_ref,) = rest
    k1n = c_ref.shape[1]
    n1h, _, c = z_ref.shape[1:]
    for g in range(HY_GROUPS):
        cols = slice(g * HY_J, (g + 1) * HY_J)
        cc = c_ref[0, :, :, cols, :].reshape(k1n * 2 * HY_J, c)
        conv = _bdot(g_ref[...], cc)
        z = z_ref[0, :, cols, :].reshape(n1h * HY_J, c).astype(F32)
        x = x_ref[0, :, cols, :].reshape(n1h * HY_J, c).astype(F32)
        o = (x * (conv + z * b_ref[...])).astype(BF16)
        o_ref[0, :, cols, :] = o.reshape(n1h, HY_J, c)
        if chain:
            a_ref[0, :, :, cols, :] = _bdot(f_ref[...], o).astype(BF16).reshape(k1n, 2, HY_J, c)


def _hy2_stage3(g1big, c5, vxx, zwhich, z5, xwhich, bias, f1big=None):
    b, k1n, _, n2, c = c5.shape
    n1h = k1n // 2
    w = HY_J * HY_GROUPS
    chain = f1big is not None
    const = lambda bb, g: (0, 0)
    sel = lambda which: pl.BlockSpec((1, n1h, w, c), lambda bb, g: (bb, 0, g, which))
    row = pl.BlockSpec((1, n1h, w, c), lambda bb, g: (bb, 0, g, 0))
    spec5 = pl.BlockSpec((1, k1n, 2, w, c), lambda bb, g: (bb, 0, 0, g, 0))
    out5 = jax.ShapeDtypeStruct((b, n1h, n2, c), BF16)
    in_specs = [pl.BlockSpec(g1big.shape, const), spec5, sel(zwhich) if z5 is None else row, sel(xwhich),
                pl.BlockSpec((1, c), const)]
    args = [g1big, c5, vxx if z5 is None else z5, vxx, bias]
    if chain:
        in_specs.append(pl.BlockSpec(f1big.shape, const))
        args.append(f1big)
    return pl.pallas_call(
        functools.partial(_hy2_s3_kernel, chain=chain),
        grid=(b, n2 // w),
        in_specs=in_specs,
        out_specs=[row, spec5] if chain else row,
        out_shape=[out5, jax.ShapeDtypeStruct(c5.shape, BF16)] if chain else out5,
        compiler_params=_cparams(("arbitrary", "arbitrary")),
        name="hyena_stage3",
    )(*args)


def _hyena(vxx, consts, fw1, fb1, fw2, fb2, fw3, fb3, hbias):
    emb, win, tabs = consts
    filt = _hy_filters(emb, win, fw1, fb1, fw2, fb2, fw3, fb3)
    b, l, _ = vxx.shape
    c = HY_CH
    if len(tabs) == 3:
        fmat, gmat, tf = tabs
        kspec = _hy_spectrum(fmat, filt, tf)
        v, x1, x2 = vxx[..., 0:c], vxx[..., c:2 * c], vxx[..., 2 * c:3 * c]
        z = _hy_inverse(gmat, _hy_forward(fmat, v, kspec, 0, tf), v, x1, hbias[0:1], tf)
        return _hy_inverse(gmat, _hy_forward(fmat, z, kspec, 1, tf), z, x2, hbias[1:2], tf)
    f1big, g1big, f2t, g2t = tabs
    n2 = l // HY_N1
    filt5 = filt.reshape(2 * HY_ORDER, HY_N1, n2, c)
    kspec = _hy2_filter_spectrum(f2t, _hy2_stage1(f1big, None, 0, filt5))
    vxx5 = vxx.reshape(b, HY_N1, n2, 3 * c)
    c5 = _hy2_stage2(f2t, g2t, _hy2_stage1(f1big, vxx5, 0, None), kspec, 0)
    z5, a5 = _hy2_stage3(g1big, c5, vxx5, 0, None, 1, hbias[0:1], f1big)
    c5 = _hy2_stage2(f2t, g2t, a5, kspec, 1)
    return _hy2_stage3(g1big, c5, vxx5, 0, z5, 2, hbias[1:2]).reshape(b, l, c)


def _mix_ffn_kernel(om_ref, od_ref, oh_ref, x_ref, mod_ref, wm_ref, wd_ref, wh_ref, w1_ref, b1_ref, w2_ref,
                    vec_ref, o_ref, *, alpha, tm, tf):
    g1, sh2, sc2, g2 = (mod_ref[0, i:i + 1, :] for i in range(4))
    bo, lg1, lb1, b2, lg2, lb2 = (vec_ref[i:i + 1, :] for i in range(6))
    nt = x_ref.shape[1] // tm
    rows = [slice(t * tm, (t + 1) * tm) for t in range(nt)]
    xs, hs = [], []
    for r in rows:
        y = _bdot(om_ref[0, r, :], wm_ref[...]) + _bdot(od_ref[0, r, :], wd_ref[...]) + _bdot(oh_ref[0, r, :], wh_ref[...])
        x1 = _ln(alpha * x_ref[0, r, :] + g1 * (y + bo)) * lg1 + lb1
        xs.append(x1)
        hs.append((_ln(x1) * (1.0 + sc2) + sh2).astype(BF16))
    accs = [None] * nt
    for c in range(w1_ref.shape[1] // tf):
        cols = slice(c * tf, (c + 1) * tf)
        for t in range(nt):
            a = jnp.maximum(_bdot(hs[t], w1_ref[:, cols]) + b1_ref[:, cols], 0.0)
            y = _bdot((a * a).astype(BF16), w2_ref[cols, :])
            accs[t] = y if accs[t] is None else accs[t] + y
    for t in range(nt):
        z = alpha * xs[t] + g2 * (accs[t] + b2)
        o_ref[0, rows[t], :] = _ln(z) * lg2 + lb2


def _mix_ffn(om, od, oh, x, mod, wm, wd, wh, w1, b1, w2, vec, alpha, tm, nt, tf):
    b, t, d = x.shape
    per_batch = mod.shape[0] == b and b > 1
    gmap = (lambda bb, i: (bb, 0, 0)) if per_batch else (lambda bb, i: (0, 0, 0))
    const = lambda bb, i: (0, 0)
    row = lambda bb, i: (bb, i, 0)
    resident = pl.Buffered(1)
    blk = lambda a: pl.BlockSpec((1, tm * nt, a.shape[2]), row)
    res = lambda a: pl.BlockSpec(a.shape, const, pipeline_mode=resident)
    return pl.pallas_call(
        functools.partial(_mix_ffn_kernel, alpha=alpha, tm=tm, tf=tf),
        grid=(b, t // (tm * nt)),
        in_specs=[blk(om), blk(od), blk(oh), blk(x), pl.BlockSpec((1,) + mod.shape[1:], gmap),
                  res(wm), res(wd), res(wh), res(w1), pl.BlockSpec(b1.shape, const), res(w2),
                  pl.BlockSpec(vec.shape, const)],
        out_specs=blk(x),
        out_shape=jax.ShapeDtypeStruct((b, t, d), F32),
        compiler_params=_cparams(("arbitrary", "arbitrary")),
        name="mix_ffn",
    )(om, od, oh, x, mod, wm, wd, wh, w1, b1, w2, vec)


def _rope_lane_plan(starts, dims):
    h = dims // 2
    inv = ROPE_BASE ** (-(np.arange(h // 2, dtype=np.float64) * (2.0 / h)))
    fr, fc, lo, hi = (np.zeros(LANE, np.float32) for _ in range(4))
    for st in starts:
        for axis_i, f in enumerate((fr, fc)):
            base = st + axis_i * h
            f[base:base + h // 2] = inv
            f[base + h // 2:base + h] = inv
            lo[base:base + h // 2] = 1.0
            hi[base + h // 2:base + h] = 1.0
    return fr, fc, lo, hi


def _rope_tables(t_len, roped):
    pos_t = jnp.arange(t_len)
    scale = 1.0 if roped else 0.0
    row = (pos_t // GRID_W).astype(F32)[:, None] * scale
    col = (pos_t % GRID_W).astype(F32)[:, None] * scale
    out = []
    for starts, dims in (([MLA_NOPE], MLA_ROPE), ([0, LANE // 2], DIFF_DIM)):
        fr, fc, lo, hi = _rope_lane_plan(starts, dims)
        ang = row * fr[None, :] + col * fc[None, :]
        sin = jnp.sin(ang)
        out += [jnp.cos(ang), -sin * lo[None, :], sin * hi[None, :]]
    return jnp.concatenate(out, axis=1)


def _prep_weights(w_in, w_q_up, w_kv_up, w_out, diff_lambda, diff_subln):
    depth, d, _ = w_in.shape
    s1, s2 = IN_MLA, IN_MLA + IN_DIFF
    pad_last = lambda a, n: jnp.pad(a, [(0, 0)] * (a.ndim - 1) + [(0, n - a.shape[-1])])
    kr = jnp.pad(w_in[..., C_KR:s1], ((0, 0), (0, 0), (MLA_NOPE, LANE - MLA_NOPE - MLA_ROPE)))
    dq = pad_last(w_in[..., s1:s1 + N_DQ].reshape(depth, d, DIFF_HEADS, 2, DIFF_DIM), LANE // 2)
    dk = pad_last(w_in[..., s1 + N_DQ:s1 + 2 * N_DQ].reshape(depth, d, DIFF_HEADS, 2, DIFF_DIM), LANE // 2)
    dv = pad_last(w_in[..., s1 + 2 * N_DQ:s2].reshape(depth, d, DIFF_HEADS, DIFF_V), LANE)
    win = jnp.concatenate([w_in[..., :C_KR], kr, dq.reshape(depth, d, W_D), dk.reshape(depth, d, W_D),
                           dv.reshape(depth, d, W_D), w_in[..., s2:]], axis=-1).astype(BF16)
    wq = pad_last(w_q_up.reshape(depth, MLA_Q_RANK, MLA_HEADS, MLA_NOPE + MLA_ROPE), LANE)
    wq = wq.reshape(depth, MLA_Q_RANK, W_QM).astype(BF16)
    kv = w_kv_up.reshape(depth, MLA_KV_RANK, MLA_HEADS, MLA_NOPE + MLA_V)
    wk = pad_last(kv[..., :MLA_NOPE], LANE).reshape(depth, MLA_KV_RANK, W_QM)
    wv = kv[..., MLA_NOPE:].reshape(depth, MLA_KV_RANK, W_VM)
    wkv = jnp.concatenate([wk, wv], axis=-1).astype(BF16)
    wo_m = w_out[:, :W_VM].astype(BF16)
    wo_d = w_out[:, W_VM:W_VM + DIFF_HEADS * DIFF_V].reshape(depth, DIFF_HEADS, DIFF_V, -1)
    wo_d = jnp.pad(wo_d, ((0, 0), (0, 0), (0, LANE - DIFF_V), (0, 0))).reshape(depth, W_D, -1).astype(BF16)
    wo_h = w_out[:, W_VM + DIFF_HEADS * DIFF_V:].astype(BF16)
    lam = pad_last(diff_lambda.astype(F32), LANE)
    subln = pad_last(diff_subln.astype(F32), LANE)[:, None, :]
    return win, wq, wkv, wo_m, wo_d, wo_h, lam, subln


def kernel(x, c, ctx, c_ctx, w_mod, b_mod, w_in, mla_q_norm, w_q_up, mla_kv_norm, w_kv_up, diff_lambda, diff_subln, hy_conv_w, hy_conv_b, hy_fw1, hy_fb1, hy_fw2, hy_fb2, hy_fw3, hy_fb3, hy_bias, w_out, b_out, ln1_g, ln1_b, w_ff1, b_ff1, w_ff2, b_ff2, ln2_g, ln2_b):
    bsz, seq, d = x.shape
    n_ctx = ctx.shape[1]
    depth = w_in.shape[0]
    alpha = (2.0 * depth) ** 0.25

    rpad = -(bsz + 1) % SUBLANE
    cc = jnp.concatenate([c, c_ctx[None, :], jnp.zeros((rpad, d), F32)], axis=0)
    mod_all = _modulation(cc, w_mod, b_mod)

    win, wq, wkv, wo_m, wo_d, wo_h, lam_p, subln = _prep_weights(w_in, w_q_up, w_kv_up, w_out, diff_lambda, diff_subln)
    w1 = w_ff1.astype(BF16)
    w2 = w_ff2.astype(BF16)
    fw1 = jnp.pad(hy_fw1, ((0, 0), (0, LANE - HY_EMB), (0, 0)))
    tab_x = _rope_tables(seq, True)
    tab_c = _rope_tables(n_ctx, False)
    def hy_tables(l):
        if l >= 1024 and l % (HY_N1 * HY_J * HY_GROUPS) == 0:
            return _hy2_tables(l)
        return _dft_tables(l, min(512, l)) + (min(512, l),)

    hyc_x = _hy_consts(seq) + (hy_tables(seq),)
    hyc_c = _hy_consts(n_ctx) + (hy_tables(n_ctx),)
    tm_x, tm_c = min(512, seq), min(512, n_ctx)
    ntf_x = 2 if seq % (2 * tm_x) == 0 else 1
    tq_x, tq_c = min(256, seq), min(256, n_ctx)
    tff = min(1024, w1.shape[2])
    row2 = lambda a: a.reshape(1, -1)

    for layer in range(depth):
        need_ctx = layer < depth - 1
        lam_init = 0.8 - 0.6 * math.exp(-0.3 * layer)
        mod = mod_all[layer, :bsz].reshape(bsz, 1, N_MOD, d)
        modc = mod_all[layer, bsz:bsz + 1].reshape(1, 1, N_MOD, d)
        sh1, sc1 = mod[:, :, 0], mod[:, :, 1]
        csh1, csc1 = modc[:, :, 0], modc[:, :, 1]
        gq, gkv = row2(mla_q_norm[layer]), row2(mla_kv_norm[layer])
        hy_args = (fw1[layer], row2(hy_fb1[layer]), hy_fw2[layer], row2(hy_fb2[layer]), hy_fw3[layer],
                   row2(hy_fb3[layer]), hy_bias[layer])
        conv = (hy_conv_w[layer], row2(hy_conv_b[layer]))

        qm, km, vm, dq, dk, dv, uh = _in_proj(x, sh1, sc1, win[layer], gq, wq[layer], gkv, wkv[layer], tab_x, *conv, tm_x)
        cqm, ckm, cvm, cdq, cdk, cdv, cuh = _in_proj(ctx, csh1, csc1, win[layer], gq, wq[layer], gkv, wkv[layer],
                                                     tab_c, *conv, tm_c)
        om = _mla_attention(qm, [ckm, km], [cvm, vm], tq_x, MLA_HEADS // 2)
        od = _diff_attention(dq, [cdk, dk], [cdv, dv], lam_p[layer], subln[layer], lam_init, tq_x, DIFF_HEADS)
        oh = _hyena(uh, hyc_x, *hy_args)
        vec = jnp.stack([b_out[layer], ln1_g[layer], ln1_b[layer], b_ff2[layer], ln2_g[layer], ln2_b[layer]])
        mix_w = (wo_m[layer], wo_d[layer], wo_h[layer], w1[layer], row2(b_ff1[layer]), w2[layer], vec)
        x = _mix_ffn(om, od, oh, x, mod[:, 0, 2:6], *mix_w, alpha, tm_x, ntf_x, tff)
        if need_ctx:
            com = _mla_attention(cqm, [ckm], [cvm], tq_c, MLA_HEADS // 2)
            cod = _diff_attention(cdq, [cdk], [cdv], lam_p[layer], subln[layer], lam_init, tq_c, DIFF_HEADS)
            coh = _hyena(cuh, hyc_c, *hy_args)
            ctx = _mix_ffn(com, cod, coh, ctx, modc[:, 0, 2:6], *mix_w, alpha, tm_c, 1, tff)
    return x
```

```python
import functools
import math

import jax
import numpy as np
import jax.numpy as jnp
from jax import lax
from jax.experimental import pallas as pl
from jax.experimental.pallas import tpu as pltpu

F32 = jnp.float32
BF16 = jnp.bfloat16
HI = lax.Precision.HIGHEST

GRID_W = 64
ROPE_BASE = 10000.0
LN_EPS = 1e-6
MLA_HEADS, MLA_NOPE, MLA_ROPE, MLA_V = 6, 64, 32, 64
MLA_Q_RANK, MLA_KV_RANK = 256, 128
DIFF_HEADS, DIFF_DIM = 4, 48
DIFF_V = 2 * DIFF_DIM
HY_CH, HY_ORDER, HY_BANDS = 256, 2, 16
HY_EMB = 1 + 2 * HY_BANDS
HY_TARGET, HY_FAST_DECAY_PCT, HY_SLOW_DECAY_PCT = 1e-2, 0.3, 1.5
HY_MIN_RATE = -math.log(HY_TARGET) / HY_SLOW_DECAY_PCT
HY_MAX_RATE = -math.log(HY_TARGET) / HY_FAST_DECAY_PCT
N_MOD = 6

LOG2E = 1.0 / math.log(2.0)
LANE = 128
SUBLANE = 8
VMEM_LIMIT = 56 * 1024 * 1024

IN_MLA = MLA_Q_RANK + MLA_KV_RANK + MLA_ROPE
N_DQ = DIFF_HEADS * 2 * DIFF_DIM
IN_DIFF = 2 * N_DQ + DIFF_HEADS * DIFF_V
W_QM = MLA_HEADS * LANE
W_VM = MLA_HEADS * MLA_V
W_D = DIFF_HEADS * LANE
W_HY = (HY_ORDER + 1) * HY_CH
C_CQ, C_CKV, C_KR = 0, MLA_Q_RANK, MLA_Q_RANK + MLA_KV_RANK
C_DQ = C_KR + LANE
C_DK = C_DQ + W_D
C_DV = C_DK + W_D
C_HY = C_DV + W_D
W_IN = C_HY + W_HY


def _cparams(sem):
    return pltpu.CompilerParams(dimension_semantics=sem, vmem_limit_bytes=VMEM_LIMIT)


def _ln(x):
    mu = jnp.mean(x, -1, keepdims=True)
    xc = x - mu
    var = jnp.mean(xc * xc, -1, keepdims=True)
    return xc * lax.rsqrt(var + LN_EPS)


def _bdot(a, b):
    return jnp.dot(a, b, preferred_element_type=F32)


def _mod_kernel(c_ref, w_ref, b_ref, o_ref):
    c = c_ref[...]
    s = c / (1.0 + jnp.exp(-c))
    o_ref[...] = jnp.dot(s, w_ref[...], preferred_element_type=F32, precision=HI) + b_ref[...]


def _modulation(cc, w_mod, b_mod):
    depth, d, n = w_mod.shape
    r = cc.shape[0]
    tn = 1024
    return pl.pallas_call(
        _mod_kernel,
        grid=(depth, n // tn),
        in_specs=[pl.BlockSpec((r, d), lambda l, j: (0, 0)),
                  pl.BlockSpec((None, d, tn), lambda l, j: (l, 0, j)),
                  pl.BlockSpec((None, 1, tn), lambda l, j: (l, 0, j))],
        out_specs=pl.BlockSpec((None, r, tn), lambda l, j: (l, 0, j)),
        out_shape=jax.ShapeDtypeStruct((depth, r, n), F32),
        compiler_params=_cparams(("arbitrary", "arbitrary")),
        name="modulation",
    )(cc, w_mod, b_mod.reshape(depth, 1, n))


def _rope(x, c, sa, sb, half):
    return x * c + pltpu.roll(x, LANE - half, 1) * sa + pltpu.roll(x, half, 1) * sb


def _inproj_kernel(x_ref, xp_ref, xn_ref, sh_ref, sc_ref, win_ref, gq_ref, wq_ref, gkv_ref, wkv_ref, tab_ref,
                   cw_ref, cb_ref, qm_ref, km_ref, vm_ref, dq_ref, dk_ref, dv_ref, hy_ref):
    x = jnp.concatenate([x_ref[0], xp_ref[0], xn_ref[0]], axis=0)
    h = _ln(x) * (1.0 + sc_ref[0]) + sh_ref[0]
    p_all = _bdot(h.astype(BF16), win_ref[...])
    tm = x_ref.shape[1]
    p = p_all[0:tm]
    tab = tab_ref[...]
    cm, sam, sbm = tab[:, 0:LANE], tab[:, LANE:2 * LANE], tab[:, 2 * LANE:3 * LANE]
    cd, sad, sbd = tab[:, 3 * LANE:4 * LANE], tab[:, 4 * LANE:5 * LANE], tab[:, 5 * LANE:6 * LANE]

    cq = p[:, C_CQ:C_CQ + MLA_Q_RANK]
    qn = cq * lax.rsqrt(jnp.mean(cq * cq, -1, keepdims=True) + LN_EPS) * gq_ref[...]
    q = _bdot(qn.astype(BF16), wq_ref[...])
    ckv = p[:, C_CKV:C_CKV + MLA_KV_RANK]
    kvn = ckv * lax.rsqrt(jnp.mean(ckv * ckv, -1, keepdims=True) + LN_EPS) * gkv_ref[...]
    kv = _bdot(kvn.astype(BF16), wkv_ref[...])
    kr = _rope(p[:, C_KR:C_KR + LANE], cm, sam, sbm, MLA_ROPE // 4)
    scale_m = LOG2E * (MLA_NOPE + MLA_ROPE) ** -0.5
    for hd in range(MLA_HEADS):
        sl = slice(hd * LANE, (hd + 1) * LANE)
        qm_ref[0, :, sl] = (_rope(q[:, sl], cm, sam, sbm, MLA_ROPE // 4) * scale_m).astype(BF16)
        km_ref[0, :, sl] = (kv[:, sl] + kr).astype(BF16)
    vm_ref[0] = kv[:, W_QM:W_QM + W_VM].astype(BF16)
    scale_d = LOG2E * DIFF_DIM ** -0.5
    for hd in range(DIFF_HEADS):
        sl = slice(hd * LANE, (hd + 1) * LANE)
        dq = p[:, C_DQ + hd * LANE:C_DQ + (hd + 1) * LANE]
        dk = p[:, C_DK + hd * LANE:C_DK + (hd + 1) * LANE]
        dq_ref[0, :, sl] = (_rope(dq, cd, sad, sbd, DIFF_DIM // 4) * scale_d).astype(BF16)
        dk_ref[0, :, sl] = _rope(dk, cd, sad, sbd, DIFF_DIM // 4).astype(BF16)
    dv_ref[0] = p[:, C_DV:C_DV + W_D].astype(BF16)
    u = p[:, C_HY:C_HY + W_HY]
    i = pl.program_id(0)
    u_prev = jnp.where(i == 0, 0.0, p_all[tm + SUBLANE - 1:tm + SUBLANE, C_HY:C_HY + W_HY])
    u_next = jnp.where(i == pl.num_programs(0) - 1, 0.0, p_all[tm + SUBLANE:tm + SUBLANE + 1, C_HY:C_HY + W_HY])
    row = lax.broadcasted_iota(jnp.int32, u.shape, 0)
    up = jnp.where(row == 0, u_prev, pltpu.roll(u, 1, 0))
    un = jnp.where(row == tm - 1, u_next, pltpu.roll(u, tm - 1, 0))
    w = cw_ref[...]
    hy_ref[0] = (up * w[0:1] + u * w[1:2] + un * w[2:3] + cb_ref[...]).astype(BF16)


def _in_proj(x, sh, sc, win, gq, wq, gkv, wkv, tab, conv_w, conv_b, tm):
    b, t, d = x.shape
    hb = tm // SUBLANE
    halo = (1, SUBLANE, d)
    per_batch = sh.shape[0] == b and b > 1
    mod_map = (lambda i, bb: (bb, 0, 0)) if per_batch else (lambda i, bb: (0, 0, 0))
    const = lambda i, bb: (0, 0)
    widths = (W_QM, W_QM, W_VM, W_D, W_D, W_D, W_HY)
    return pl.pallas_call(
        _inproj_kernel,
        grid=(t // tm, b),
        in_specs=[pl.BlockSpec((1, tm, d), lambda i, bb: (bb, i, 0)),
                  pl.BlockSpec(halo, lambda i, bb: (bb, jnp.maximum(i * hb - 1, 0), 0)),
                  pl.BlockSpec(halo, lambda i, bb: (bb, jnp.minimum((i + 1) * hb, t // SUBLANE - 1), 0)),
                  pl.BlockSpec((1, 1, d), mod_map),
                  pl.BlockSpec((1, 1, d), mod_map),
                  pl.BlockSpec(win.shape, const),
                  pl.BlockSpec(gq.shape, const),
                  pl.BlockSpec(wq.shape, const),
                  pl.BlockSpec(gkv.shape, const),
                  pl.BlockSpec(wkv.shape, const),
                  pl.BlockSpec((tm, 6 * LANE), lambda i, bb: (i, 0)),
                  pl.BlockSpec(conv_w.shape, const), pl.BlockSpec(conv_b.shape, const)],
        out_specs=[pl.BlockSpec((1, tm, w), lambda i, bb: (bb, i, 0)) for w in widths],
        out_shape=[jax.ShapeDtypeStruct((b, t, w), BF16) for w in widths],
        compiler_params=_cparams(("arbitrary", "arbitrary")),
        name="in_proj",
    )(x, x, x, sh, sc, win, gq, wq, gkv, wkv, tab, conv_w, conv_b)


NEG_BIG = -1e30


def _lane_fold(x, op):
    r = x[:, 0:LANE]
    for i in range(1, x.shape[1] // LANE):
        r = op(r, x[:, i * LANE:(i + 1) * LANE])
    return r


def _score_pass(q, parts, ksl, s_refs):
    mx = jnp.full((q.shape[0], LANE), NEG_BIG, F32)
    for (k_ref, _, n, tk), s_ref in zip(parts, s_refs):
        for j in range(n // tk):
            k = k_ref[0, j * tk:(j + 1) * tk, ksl]
            s = lax.dot_general(q, k, (((1,), (1,)), ((), ())), preferred_element_type=F32)
            s_ref[j] = s
            mx = jnp.maximum(mx, _lane_fold(s, jnp.maximum))
    return jnp.max(mx, axis=-1, keepdims=True)


def _softmax_pv(q, parts, ksl, vsl, s_refs):
    m = _score_pass(q, parts, ksl, s_refs)
    ls = jnp.zeros((q.shape[0], LANE), F32)
    acc = jnp.zeros((q.shape[0], LANE), F32)
    for (_, v_ref, n, tk), s_ref in zip(parts, s_refs):
        for j in range(n // tk):
            p = jnp.exp2(s_ref[j] - m)
            ls = ls + _lane_fold(p, jnp.add)
            acc = acc + _bdot(p.astype(BF16), v_ref[0, j * tk:(j + 1) * tk, vsl])
    return acc * (1.0 / jnp.sum(ls, axis=-1, keepdims=True))


def _mla_attn_kernel(*refs, part_shapes, n_pairs):
    np_ = len(part_shapes)
    q_ref = refs[0]
    k_refs = refs[1:1 + np_]
    v_refs = refs[1 + np_:1 + 2 * np_]
    o_ref = refs[1 + 2 * np_]
    s_sets = (refs[2 + 2 * np_:2 + 3 * np_], refs[2 + 3 * np_:])
    parts = [(k_refs[i], v_refs[i]) + part_shapes[i] for i in range(np_)]
    for pr in range(n_pairs):
        outs = []
        for hh in range(2):
            hd = 2 * pr + hh
            sl = slice(hd * LANE, (hd + 1) * LANE)
            outs.append(_softmax_pv(q_ref[0, :, sl], parts, sl, slice(pr * LANE, (pr + 1) * LANE), s_sets[hh]))
        lane = lax.broadcasted_iota(jnp.int32, outs[0].shape, 1)
        o_ref[0, :, pr * LANE:(pr + 1) * LANE] = jnp.where(lane < MLA_V, outs[0], outs[1]).astype(BF16)


def _chunk(n):
    return n if n <= 512 else 512


def _mla_attention(q, ks, vs, tq, n_pairs):
    b, t, _ = q.shape
    part_shapes = tuple((k.shape[1], _chunk(k.shape[1])) for k in ks)
    in_specs = [pl.BlockSpec((1, tq, 2 * n_pairs * LANE), lambda bb, g, i: (bb, i, g))]
    in_specs += [pl.BlockSpec((1, k.shape[1], 2 * n_pairs * LANE), lambda bb, g, i: (bb, 0, g)) for k in ks]
    in_specs += [pl.BlockSpec((1, v.shape[1], n_pairs * LANE), lambda bb, g, i: (bb, 0, g)) for v in vs]
    scratch = [pltpu.VMEM((n // tk, tq, tk), F32) for n, tk in part_shapes] * 2
    return pl.pallas_call(
        functools.partial(_mla_attn_kernel, part_shapes=part_shapes, n_pairs=n_pairs),
        grid=(b, MLA_HEADS // (2 * n_pairs), t // tq),
        in_specs=in_specs,
        out_specs=pl.BlockSpec((1, tq, n_pairs * LANE), lambda bb, g, i: (bb, i, g)),
        out_shape=jax.ShapeDtypeStruct((b, t, W_VM), BF16),
        scratch_shapes=scratch,
        compiler_params=_cparams(("arbitrary", "arbitrary", "arbitrary")),
        name="mla_attention",
    )(q, *ks, *vs)


def _diff_attn_kernel(*refs, part_shapes, lam_init, n_heads):
    np_ = len(part_shapes)
    q_ref = refs[0]
    k_refs = refs[1:1 + np_]
    v_refs = refs[1 + np_:1 + 2 * np_]
    lam_ref, g_ref, o_ref = refs[1 + 2 * np_:4 + 2 * np_]
    s_sets = (refs[4 + 2 * np_:4 + 3 * np_], refs[4 + 3 * np_:])
    parts = [(k_refs[i], v_refs[i]) + part_shapes[i] for i in range(np_)]
    lp = lam_ref[...]
    lam = (jnp.exp(jnp.sum(lp[0:1] * lp[1:2], axis=-1, keepdims=True))
           - jnp.exp(jnp.sum(lp[2:3] * lp[3:4], axis=-1, keepdims=True)) + lam_init)
    for hd in range(n_heads):
        sl = slice(hd * LANE, (hd + 1) * LANE)
        q = q_ref[0, :, sl]
        lane = lax.broadcasted_iota(jnp.int32, q.shape, 1)
        zero = jnp.zeros_like(q)
        o1 = _softmax_pv(jnp.where(lane < LANE // 2, q, zero), parts, sl, sl, s_sets[0])
        o2 = _softmax_pv(jnp.where(lane >= LANE // 2, q, zero), parts, sl, sl, s_sets[1])
        o = o1 - lam * o2
        ms = jnp.sum(o * o, axis=-1, keepdims=True) * (1.0 / DIFF_V)
        o_ref[0, :, sl] = (o * lax.rsqrt(ms + LN_EPS) * g_ref[...] * (1.0 - lam_init)).astype(BF16)


def _diff_attention(q, ks, vs, lam_p, subln, lam_init, tq, n_heads):
    b, t, _ = q.shape
    part_shapes = tuple((k.shape[1], _chunk(k.shape[1])) for k in ks)
    hmap = lambda bb, g, i: (bb, 0, g)
    in_specs = [pl.BlockSpec((1, tq, n_heads * LANE), lambda bb, g, i: (bb, i, g))]
    in_specs += [pl.BlockSpec((1, k.shape[1], n_heads * LANE), hmap) for k in ks]
    in_specs += [pl.BlockSpec((1, v.shape[1], n_heads * LANE), hmap) for v in vs]
    in_specs += [pl.BlockSpec(lam_p.shape, lambda bb, g, i: (0, 0)),
                 pl.BlockSpec(subln.shape, lambda bb, g, i: (0, 0))]
    scratch = [pltpu.VMEM((n // tk, tq, tk), F32) for n, tk in part_shapes] * 2
    return pl.pallas_call(
        functools.partial(_diff_attn_kernel, part_shapes=part_shapes, lam_init=lam_init, n_heads=n_heads),
        grid=(b, DIFF_HEADS // n_heads, t // tq),
        in_specs=in_specs,
        out_specs=pl.BlockSpec((1, tq, n_heads * LANE), lambda bb, g, i: (bb, i, g)),
        out_shape=jax.ShapeDtypeStruct((b, t, W_D), BF16),
        scratch_shapes=scratch,
        compiler_params=_cparams(("arbitrary", "arbitrary", "arbitrary")),
        name="diff_attention",
    )(q, *ks, *vs, lam_p, subln)


def _hyfilt_kernel(emb_ref, win_ref, fw1_ref, fb1_ref, fw2_ref, fb2_ref, fwf_ref, fbf_ref, fwb_ref, fbb_ref,
                   d_ref, h_scr):
    @pl.when(pl.program_id(0) == 0)
    def _():
        h1 = jnp.sin(jnp.dot(emb_ref[...], fw1_ref[...], preferred_element_type=F32, precision=HI) + fb1_ref[...])
        h_scr[...] = jnp.sin(jnp.dot(h1, fw2_ref[...], preferred_element_type=F32, precision=HI) + fb2_ref[...])

    h = h_scr[...]
    w = win_ref[...]
    fwd = (jnp.dot(h, fwf_ref[...], preferred_element_type=F32, precision=HI) + fbf_ref[...]) * w
    bwd = (jnp.dot(h, fwb_ref[...], preferred_element_type=F32, precision=HI) + fbb_ref[...]) * w
    row = lax.broadcasted_iota(jnp.int32, bwd.shape, 0)
    bwd = jnp.where(row == 0, 0.0, bwd)
    norm = jnp.sum(jnp.abs(fwd), axis=0, keepdims=True) + jnp.sum(jnp.abs(bwd), axis=0, keepdims=True)
    inv = 1.0 / norm
    d_ref[0] = ((fwd + bwd) * inv).astype(BF16)
    d_ref[1] = ((fwd - bwd) * inv).astype(BF16)


def _hy_filters(emb, win, fw1, fb1, fw2, fb2, fw3, fb3):
    l = emb.shape[0]
    hid = fw2.shape[0]
    nblk = HY_CH // LANE
    const = lambda g: (0, 0)
    fcol = lambda g: (0, (g // nblk) * 2 * nblk + g % nblk)
    bcol = lambda g: (0, (g // nblk) * 2 * nblk + nblk + g % nblk)
    return pl.pallas_call(
        _hyfilt_kernel,
        grid=(HY_ORDER * nblk,),
        in_specs=[pl.BlockSpec(emb.shape, const),
                  pl.BlockSpec((l, LANE), lambda g: (0, g % nblk)),
                  pl.BlockSpec(fw1.shape, const), pl.BlockSpec(fb1.shape, const),
                  pl.BlockSpec(fw2.shape, const), pl.BlockSpec(fb2.shape, const),
                  pl.BlockSpec((hid, LANE), fcol), pl.BlockSpec((1, LANE), fcol),
                  pl.BlockSpec((hid, LANE), bcol), pl.BlockSpec((1, LANE), bcol)],
        out_specs=pl.BlockSpec((2, None, l, LANE), lambda g: (0, g // nblk, 0, g % nblk)),
        out_shape=jax.ShapeDtypeStruct((2, HY_ORDER, l, HY_CH), BF16),
        scratch_shapes=[pltpu.VMEM((l, hid), F32)],
        compiler_params=_cparams(("arbitrary",)),
        name="hyena_filters",
    )(emb, win, fw1, fb1, fw2, fb2, fw3, fb3, fw3, fb3)


def _hyspec_kernel(f_ref, s_ref, d_ref, o_ref):
    tf = f_ref.shape[0] // 2
    o_ref[0:tf, :] = _bdot(f_ref[0:tf, :], s_ref[...])
    o_ref[tf:2 * tf, :] = _bdot(f_ref[tf:2 * tf, :], d_ref[...])


def _hy_spectrum(fmat, filt, tf):
    rows, l = fmat.shape
    return pl.pallas_call(
        _hyspec_kernel,
        grid=(rows // (2 * tf), HY_ORDER),
        in_specs=[pl.BlockSpec((2 * tf, l), lambda j, o: (j, 0)),
                  pl.BlockSpec((None, None, l, HY_CH), lambda j, o: (0, o, 0, 0)),
                  pl.BlockSpec((None, None, l, HY_CH), lambda j, o: (1, o, 0, 0))],
        out_specs=pl.BlockSpec((2 * tf, HY_CH), lambda j, o: (j, o)),
        out_shape=jax.ShapeDtypeStruct((rows, HY_ORDER * HY_CH), F32),
        compiler_params=_cparams(("arbitrary", "arbitrary")),
        name="hyena_filter_spectrum",
    )(fmat, filt, filt)


def _hyfwd_kernel(f_ref, z_ref, k_ref, y_ref):
    tf = f_ref.shape[0] // 2
    s = _bdot(f_ref[...], z_ref[0])
    sr, si = s[0:tf], s[tf:2 * tf]
    kr, ki = k_ref[0:tf, :], k_ref[tf:2 * tf, :]
    y_ref[0, 0:tf, :] = (sr * kr - si * ki).astype(BF16)
    y_ref[0, tf:2 * tf, :] = (sr * ki + si * kr).astype(BF16)


def _hy_forward(fmat, z, kspec, order, tf):
    rows, l = fmat.shape
    b = z.shape[0]
    return pl.pallas_call(
        _hyfwd_kernel,
        grid=(rows // (2 * tf), b),
        in_specs=[pl.BlockSpec((2 * tf, l), lambda j, bb: (j, 0)),
                  pl.BlockSpec((1, l, HY_CH), lambda j, bb: (bb, 0, 0)),
                  pl.BlockSpec((2 * tf, HY_CH), lambda j, bb: (j, order))],
        out_specs=pl.BlockSpec((1, 2 * tf, HY_CH), lambda j, bb: (bb, j, 0)),
        out_shape=jax.ShapeDtypeStruct((b, rows, HY_CH), BF16),
        compiler_params=_cparams(("arbitrary", "arbitrary")),
        name="hyena_dft_forward",
    )(fmat, z, kspec)


def _hyinv_kernel(g_ref, y_ref, z_ref, x_ref, b_ref, o_ref):
    conv = _bdot(g_ref[...], y_ref[0])
    z = z_ref[0].astype(F32)
    o_ref[0] = (x_ref[0].astype(F32) * (conv + z * b_ref[...])).astype(BF16)


def _hy_inverse(gmat, y, z, gate, bias, tt):
    l, rows = gmat.shape
    b = y.shape[0]
    return pl.pallas_call(
        _hyinv_kernel,
        grid=(l // tt, b),
        in_specs=[pl.BlockSpec((tt, rows), lambda i, bb: (i, 0)),
                  pl.BlockSpec((1, rows, HY_CH), lambda i, bb: (bb, 0, 0)),
                  pl.BlockSpec((1, tt, HY_CH), lambda i, bb: (bb, i, 0)),
                  pl.BlockSpec((1, tt, HY_CH), lambda i, bb: (bb, i, 0)),
                  pl.BlockSpec((1, HY_CH), lambda i, bb: (0, 0))],
        out_specs=pl.BlockSpec((1, tt, HY_CH), lambda i, bb: (bb, i, 0)),
        out_shape=jax.ShapeDtypeStruct((b, l, HY_CH), BF16),
        compiler_params=_cparams(("arbitrary", "arbitrary")),
        name="hyena_dft_inverse",
    )(gmat, y, z, gate, bias)


def _dft_tables(l, tf):
    k = jnp.arange(l, dtype=jnp.int32)
    n = jnp.arange(l, dtype=jnp.int32)
    ph = ((2 * k + 1)[:, None] * n[None, :]) % (4 * l)
    ang = ph.astype(F32) * (2.0 * math.pi / (4 * l))
    c = jnp.cos(ang).reshape(l // tf, 1, tf, l)
    s = (-jnp.sin(ang)).reshape(l // tf, 1, tf, l)
    f = jnp.concatenate([c, s], axis=1).reshape(2 * l, l)
    return f.astype(BF16), (f.T * (1.0 / l)).astype(BF16)


def _hy_consts(l):
    t = jnp.arange(l, dtype=F32)
    bands = jnp.arange(1, HY_BANDS + 1, dtype=F32)
    ang = (2.0 * math.pi / l) * t[:, None] * bands[None, :]
    emb = jnp.concatenate([(t / l)[:, None], jnp.cos(ang), jnp.sin(ang)], -1)
    emb = jnp.pad(emb, ((0, 0), (0, LANE - HY_EMB)))
    rates = jnp.linspace(HY_MIN_RATE, HY_MAX_RATE, HY_CH, dtype=F32)
    win = jnp.exp(-(t / l)[:, None] * rates[None, :])
    return emb, win


HY_N1 = 16
HY_J = 16
HY_GROUPS = 8
HY_KG = 16


def _hy2_tables(l):
    n1h, jj = HY_N1, HY_J
    n2, k1n = l // n1h, 2 * n1h
    k2n = n2 // 2
    k1 = jnp.arange(k1n, dtype=jnp.int32)
    ph1 = ((2 * k1 + 1)[:, None] * jnp.arange(n1h, dtype=jnp.int32)[None, :]) % (4 * n1h)
    a1 = ph1.astype(F32) * (2.0 * math.pi / (4 * n1h))
    f1 = jnp.stack([jnp.cos(a1), -jnp.sin(a1)], axis=1)
    f1big = jnp.einsum('krn,ab->kranb', f1, jnp.eye(jj, dtype=F32)).reshape(k1n * 2 * jj, n1h * jj)
    kk = k1[:, None] + k1n * jnp.arange(k2n, dtype=jnp.int32)[None, :]
    ph2 = ((2 * kk + 1)[:, :, None] * jnp.arange(n2, dtype=jnp.int32)[None, None, :]) % (4 * l)
    a2 = ph2.astype(F32) * (2.0 * math.pi / (4 * l))
    mr, mi = jnp.cos(a2), -jnp.sin(a2)
    f2t = jnp.concatenate([jnp.concatenate([mr, -mi], axis=2), jnp.concatenate([mi, mr], axis=2)], axis=1)
    return (f1big.astype(BF16), (f1big.T * (1.0 / l)).astype(BF16), f2t.astype(BF16),
            jnp.swapaxes(f2t, 1, 2).astype(BF16))


def _hy2_s1_kernel(f_ref, z_ref, o_ref):
    n1h, _, c = z_ref.shape[1:]
    k1n = o_ref.shape[1]
    for g in range(HY_GROUPS):
        cols = slice(g * HY_J, (g + 1) * HY_J)
        z = z_ref[0, :, cols, :].reshape(n1h * HY_J, c)
        a = _bdot(f_ref[...], z)
        o_ref[0, :, :, cols, :] = a.astype(BF16).reshape(k1n, 2, HY_J, c)


def _hy2_stage1(f1big, vxx, which, z5):
    src = vxx if z5 is None else z5
    b, n1h, n2, _ = src.shape
    c = HY_CH
    k1n = 2 * n1h
    w = HY_J * HY_GROUPS
    zspec = pl.BlockSpec((1, n1h, w, c), lambda bb, g: (bb, 0, g, which if z5 is None else 0))
    return pl.pallas_call(
        _hy2_s1_kernel,
        grid=(b, n2 // w),
        in_specs=[pl.BlockSpec(f1big.shape, lambda bb, g: (0, 0)), zspec],
        out_specs=pl.BlockSpec((1, k1n, 2, w, c), lambda bb, g: (bb, 0, 0, g, 0)),
        out_shape=jax.ShapeDtypeStruct((b, k1n, 2, n2, c), BF16),
        compiler_params=_cparams(("arbitrary", "arbitrary")),
        name="hyena_stage1",
    )(f1big, src)


def _hy2_fspec_kernel(f_ref, s_ref, d_ref, o_ref):
    h = o_ref.shape[1] // 2
    for i in range(HY_KG):
        o_ref[i, 0:h, :] = _bdot(f_ref[i], s_ref[0, i])[0:h]
        o_ref[i, h:2 * h, :] = _bdot(f_ref[i], d_ref[0, i])[h:2 * h]


def _hy2_filter_spectrum(f2t, a5):
    nsig, k1n, _, n2, c = a5.shape
    a4 = a5.reshape(nsig, k1n, 2 * n2, c)
    return pl.pallas_call(
        _hy2_fspec_kernel,
        grid=(k1n // HY_KG, HY_ORDER),
        in_specs=[pl.BlockSpec((HY_KG, n2, 2 * n2), lambda g, o: (g, 0, 0)),
                  pl.BlockSpec((1, HY_KG, 2 * n2, c), lambda g, o: (o, g, 0, 0)),
                  pl.BlockSpec((1, HY_KG, 2 * n2, c), lambda g, o: (HY_ORDER + o, g, 0, 0))],
        out_specs=pl.BlockSpec((HY_KG, n2, c), lambda g, o: (g, 0, o)),
        out_shape=jax.ShapeDtypeStruct((k1n, n2, HY_ORDER * c), F32),
        compiler_params=_cparams(("arbitrary", "arbitrary")),
        name="hyena_filter_spectrum2",
    )(f2t, a4, a4)


def _hy2_s2_kernel(f_ref, g_ref, a_ref, k_ref, o_ref):
    for i in range(HY_KG):
        y = _bdot(f_ref[i], a_ref[0, i])
        h = y.shape[0] // 2
        yr, yi = y[:h], y[h:]
        kr, ki = k_ref[i, 0:h, :], k_ref[i, h:2 * h, :]
        p = jnp.concatenate([yr * kr - yi * ki, yr * ki + yi * kr], axis=0).astype(BF16)
        o_ref[0, i] = _bdot(g_ref[i], p).astype(BF16)


def _hy2_stage2(f2t, g2t, a5, kspec, order):
    b, k1n, _, n2, c = a5.shape
    a4 = a5.reshape(b, k1n, 2 * n2, c)
    out = pl.pallas_call(
        _hy2_s2_kernel,
        grid=(k1n // HY_KG, b),
        in_specs=[pl.BlockSpec((HY_KG, n2, 2 * n2), lambda g, bb: (g, 0, 0)),
                  pl.BlockSpec((HY_KG, 2 * n2, n2), lambda g, bb: (g, 0, 0)),
                  pl.BlockSpec((1, HY_KG, 2 * n2, c), lambda g, bb: (bb, g, 0, 0)),
                  pl.BlockSpec((HY_KG, n2, c), lambda g, bb: (g, 0, order))],
        out_specs=pl.BlockSpec((1, HY_KG, 2 * n2, c), lambda g, bb: (bb, g, 0, 0)),
        out_shape=jax.ShapeDtypeStruct(a4.shape, BF16),
        compiler_params=_cparams(("arbitrary", "arbitrary")),
        name="hyena_stage2",
    )(f2t, g2t, a4, kspec)
    return out.reshape(a5.shape)


def _hy2_s3_kernel(g_ref, c_ref, z_ref, x_ref, b_ref, *rest, chain):
    if chain:
        f_ref, o_ref, a_ref = rest
    else:
        (o_ref,) = rest
    k1n = c_ref.shape[1]
    n1h, _, c = z_ref.shape[1:]
    for g in range(HY_GROUPS):
        cols = slice(g * HY_J, (g + 1) * HY_J)
        cc = c_ref[0, :, :, cols, :].reshape(k1n * 2 * HY_J, c)
        conv = _bdot(g_ref[...], cc)
        z = z_ref[0, :, cols, :].reshape(n1h * HY_J, c).astype(F32)
        x = x_ref[0, :, cols, :].reshape(n1h * HY_J, c).astype(F32)
        o = (x * (conv + z * b_ref[...])).astype(BF16)
        o_ref[0, :, cols, :] = o.reshape(n1h, HY_J, c)
        if chain:
            a_ref[0, :, :, cols, :] = _bdot(f_ref[...], o).astype(BF16).reshape(k1n, 2, HY_J, c)


def _hy2_stage3(g1big, c5, vxx, zwhich, z5, xwhich, bias, f1big=None):
    b, k1n, _, n2, c = c5.shape
    n1h = k1n // 2
    w = HY_J * HY_GROUPS
    chain = f1big is not None
    const = lambda bb, g: (0, 0)
    sel = lambda which: pl.BlockSpec((1, n1h, w, c), lambda bb, g: (bb, 0, g, which))
    row = pl.BlockSpec((1, n1h, w, c), lambda bb, g: (bb, 0, g, 0))
    spec5 = pl.BlockSpec((1, k1n, 2, w, c), lambda bb, g: (bb, 0, 0, g, 0))
    out5 = jax.ShapeDtypeStruct((b, n1h, n2, c), BF16)
    in_specs = [pl.BlockSpec(g1big.shape, const), spec5, sel(zwhich) if z5 is None else row, sel(xwhich),
                pl.BlockSpec((1, c), const)]
    args = [g1big, c5, vxx if z5 is None else z5, vxx, bias]
    if chain:
        in_specs.append(pl.BlockSpec(f1big.shape, const))
        args.append(f1big)
    return pl.pallas_call(
        functools.partial(_hy2_s3_kernel, chain=chain),
        grid=(b, n2 // w),
        in_specs=in_specs,
        out_specs=[row, spec5] if chain else row,
        out_shape=[out5, jax.ShapeDtypeStruct(c5.shape, BF16)] if chain else out5,
        compiler_params=_cparams(("arbitrary", "arbitrary")),
        name="hyena_stage3",
    )(*args)


def _hyena(vxx, consts, fw1, fb1, fw2, fb2, fw3, fb3, hbias):
    emb, win, tabs = consts
    filt = _hy_filters(emb, win, fw1, fb1, fw2, fb2, fw3, fb3)
    b, l, _ = vxx.shape
    c = HY_CH
    if len(tabs) == 3:
        fmat, gmat, tf = tabs
        kspec = _hy_spectrum(fmat, filt, tf)
        v, x1, x2 = vxx[..., 0:c], vxx[..., c:2 * c], vxx[..., 2 * c:3 * c]
        z = _hy_inverse(gmat, _hy_forward(fmat, v, kspec, 0, tf), v, x1, hbias[0:1], tf)
        return _hy_inverse(gmat, _hy_forward(fmat, z, kspec, 1, tf), z, x2, hbias[1:2], tf)
    f1big, g1big, f2t, g2t = tabs
    n2 = l // HY_N1
    filt5 = filt.reshape(2 * HY_ORDER, HY_N1, n2, c)
    kspec = _hy2_filter_spectrum(f2t, _hy2_stage1(f1big, None, 0, filt5))
    vxx5 = vxx.reshape(b, HY_N1, n2, 3 * c)
    c5 = _hy2_stage2(f2t, g2t, _hy2_stage1(f1big, vxx5, 0, None), kspec, 0)
    z5, a5 = _hy2_stage3(g1big, c5, vxx5, 0, None, 1, hbias[0:1], f1big)
    c5 = _hy2_stage2(f2t, g2t, a5, kspec, 1)
    return _hy2_stage3(g1big, c5, vxx5, 0, z5, 2, hbias[1:2]).reshape(b, l, c)


def _mix_ffn_kernel(om_ref, od_ref, oh_ref, x_ref, mod_ref, wm_ref, wd_ref, wh_ref, w1_ref, b1_ref, w2_ref,
                    vec_ref, o_ref, *, alpha, tm, tf):
    g1, sh2, sc2, g2 = (mod_ref[0, i:i + 1, :] for i in range(4))
    bo, lg1, lb1, b2, lg2, lb2 = (vec_ref[i:i + 1, :] for i in range(6))
    nt = x_ref.shape[1] // tm
    rows = [slice(t * tm, (t + 1) * tm) for t in range(nt)]
    xs, hs = [], []
    for r in rows:
        y = _bdot(om_ref[0, r, :], wm_ref[...]) + _bdot(od_ref[0, r, :], wd_ref[...]) + _bdot(oh_ref[0, r, :], wh_ref[...])
        x1 = _ln(alpha * x_ref[0, r, :] + g1 * (y + bo)) * lg1 + lb1
        xs.append(x1)
        hs.append((_ln(x1) * (1.0 + sc2) + sh2).astype(BF16))
    accs = [None] * nt
    for c in range(w1_ref.shape[1] // tf):
        cols = slice(c * tf, (c + 1) * tf)
        for t in range(nt):
            a = jnp.maximum(_bdot(hs[t], w1_ref[:, cols]) + b1_ref[:, cols], 0.0)
            y = _bdot((a * a).astype(BF16), w2_ref[cols, :])
            accs[t] = y if accs[t] is None else accs[t] + y
    for t in range(nt):
        z = alpha * xs[t] + g2 * (accs[t] + b2)
        o_ref[0, rows[t], :] = _ln(z) * lg2 + lb2


def _mix_ffn(om, od, oh, x, mod, wm, wd, wh, w1, b1, w2, vec, alpha, tm, nt, tf):
    b, t, d = x.shape
    per_batch = mod.shape[0] == b and b > 1
    gmap = (lambda bb, i: (bb, 0, 0)) if per_batch else (lambda bb, i: (0, 0, 0))
    const = lambda bb, i: (0, 0)
    row = lambda bb, i: (bb, i, 0)
    resident = pl.Buffered(1)
    blk = lambda a: pl.BlockSpec((1, tm * nt, a.shape[2]), row)
    res = lambda a: pl.BlockSpec(a.shape, const, pipeline_mode=resident)
    return pl.pallas_call(
        functools.partial(_mix_ffn_kernel, alpha=alpha, tm=tm, tf=tf),
        grid=(b, t // (tm * nt)),
        in_specs=[blk(om), blk(od), blk(oh), blk(x), pl.BlockSpec((1,) + mod.shape[1:], gmap),
                  res(wm), res(wd), res(wh), res(w1), pl.BlockSpec(b1.shape, const), res(w2),
                  pl.BlockSpec(vec.shape, const)],
        out_specs=blk(x),
        out_shape=jax.ShapeDtypeStruct((b, t, d), F32),
        compiler_params=_cparams(("arbitrary", "arbitrary")),
        name="mix_ffn",
    )(om, od, oh, x, mod, wm, wd, wh, w1, b1, w2, vec)


def _rope_lane_plan(starts, dims):
    h = dims // 2
    inv = ROPE_BASE ** (-(np.arange(h // 2, dtype=np.float64) * (2.0 / h)))
    fr, fc, lo, hi = (np.zeros(LANE, np.float32) for _ in range(4))
    for st in starts:
        for axis_i, f in enumerate((fr, fc)):
            base = st + axis_i * h
            f[base:base + h // 2] = inv
            f[base + h // 2:base + h] = inv
            lo[base:base + h // 2] = 1.0
            hi[base + h // 2:base + h] = 1.0
    return fr, fc, lo, hi


def _rope_tables(t_len, roped):
    pos_t = jnp.arange(t_len)
    scale = 1.0 if roped else 0.0
    row = (pos_t // GRID_W).astype(F32)[:, None] * scale
    col = (pos_t % GRID_W).astype(F32)[:, None] * scale
    out = []
    for starts, dims in (([MLA_NOPE], MLA_ROPE), ([0, LANE // 2], DIFF_DIM)):
        fr, fc, lo, hi = _rope_lane_plan(starts, dims)
        ang = row * fr[None, :] + col * fc[None, :]
        sin = jnp.sin(ang)
        out += [jnp.cos(ang), -sin * lo[None, :], sin * hi[None, :]]
    return jnp.concatenate(out, axis=1)


def _prep_weights(w_in, w_q_up, w_kv_up, w_out, diff_lambda, diff_subln):
    depth, d, _ = w_in.shape
    s1, s2 = IN_MLA, IN_MLA + IN_DIFF
    pad_last = lambda a, n: jnp.pad(a, [(0, 0)] * (a.ndim - 1) + [(0, n - a.shape[-1])])
    kr = jnp.pad(w_in[..., C_KR:s1], ((0, 0), (0, 0), (MLA_NOPE, LANE - MLA_NOPE - MLA_ROPE)))
    dq = pad_last(w_in[..., s1:s1 + N_DQ].reshape(depth, d, DIFF_HEADS, 2, DIFF_DIM), LANE // 2)
    dk = pad_last(w_in[..., s1 + N_DQ:s1 + 2 * N_DQ].reshape(depth, d, DIFF_HEADS, 2, DIFF_DIM), LANE // 2)
    dv = pad_last(w_in[..., s1 + 2 * N_DQ:s2].reshape(depth, d, DIFF_HEADS, DIFF_V), LANE)
    win = jnp.concatenate([w_in[..., :C_KR], kr, dq.reshape(depth, d, W_D), dk.reshape(depth, d, W_D),
                           dv.reshape(depth, d, W_D), w_in[..., s2:]], axis=-1).astype(BF16)
    wq = pad_last(w_q_up.reshape(depth, MLA_Q_RANK, MLA_HEADS, MLA_NOPE + MLA_ROPE), LANE)
    wq = wq.reshape(depth, MLA_Q_RANK, W_QM).astype(BF16)
    kv = w_kv_up.reshape(depth, MLA_KV_RANK, MLA_HEADS, MLA_NOPE + MLA_V)
    wk = pad_last(kv[..., :MLA_NOPE], LANE).reshape(depth, MLA_KV_RANK, W_QM)
    wv = kv[..., MLA_NOPE:].reshape(depth, MLA_KV_RANK, W_VM)
    wkv = jnp.concatenate([wk, wv], axis=-1).astype(BF16)
    wo_m = w_out[:, :W_VM].astype(BF16)
    wo_d = w_out[:, W_VM:W_VM + DIFF_HEADS * DIFF_V].reshape(depth, DIFF_HEADS, DIFF_V, -1)
    wo_d = jnp.pad(wo_d, ((0, 0), (0, 0), (0, LANE - DIFF_V), (0, 0))).reshape(depth, W_D, -1).astype(BF16)
    wo_h = w_out[:, W_VM + DIFF_HEADS * DIFF_V:].astype(BF16)
    lam = pad_last(diff_lambda.astype(F32), LANE)
    subln = pad_last(diff_subln.astype(F32), LANE)[:, None, :]
    return win, wq, wkv, wo_m, wo_d, wo_h, lam, subln


def kernel(x, c, ctx, c_ctx, w_mod, b_mod, w_in, mla_q_norm, w_q_up, mla_kv_norm, w_kv_up, diff_lambda, diff_subln, hy_conv_w, hy_conv_b, hy_fw1, hy_fb1, hy_fw2, hy_fb2, hy_fw3, hy_fb3, hy_bias, w_out, b_out, ln1_g, ln1_b, w_ff1, b_ff1, w_ff2, b_ff2, ln2_g, ln2_b):
    bsz, seq, d = x.shape
    n_ctx = ctx.shape[1]
    depth = w_in.shape[0]
    alpha = (2.0 * depth) ** 0.25

    rpad = -(bsz + 1) % SUBLANE
    cc = jnp.concatenate([c, c_ctx[None, :], jnp.zeros((rpad, d), F32)], axis=0)
    mod_all = _modulation(cc, w_mod, b_mod)

    win, wq, wkv, wo_m, wo_d, wo_h, lam_p, subln = _prep_weights(w_in, w_q_up, w_kv_up, w_out, diff_lambda, diff_subln)
    w1 = w_ff1.astype(BF16)
    w2 = w_ff2.astype(BF16)
    fw1 = jnp.pad(hy_fw1, ((0, 0), (0, LANE - HY_EMB), (0, 0)))
    tab_x = _rope_tables(seq, True)
    tab_c = _rope_tables(n_ctx, False)
    def hy_tables(l):
        if l >= 1024 and l % (HY_N1 * HY_J * HY_GROUPS) == 0:
            return _hy2_tables(l)
        return _dft_tables(l, min(512, l)) + (min(512, l),)

    hyc_x = _hy_consts(seq) + (hy_tables(seq),)
    hyc_c = _hy_consts(n_ctx) + (hy_tables(n_ctx),)
    tm_x, tm_c = min(512, seq), min(512, n_ctx)
    ntf_x = 2 if seq % (2 * tm_x) == 0 else 1
    tq_x, tq_c = min(256, seq), min(256, n_ctx)
    tff = min(1024, w1.shape[2])
    row2 = lambda a: a.reshape(1, -1)

    for layer in range(depth):
        need_ctx = layer < depth - 1
        lam_init = 0.8 - 0.6 * math.exp(-0.3 * layer)
        mod = mod_all[layer, :bsz].reshape(bsz, 1, N_MOD, d)
        modc = mod_all[layer, bsz:bsz + 1].reshape(1, 1, N_MOD, d)
        sh1, sc1 = mod[:, :, 0], mod[:, :, 1]
        csh1, csc1 = modc[:, :, 0], modc[:, :, 1]
        gq, gkv = row2(mla_q_norm[layer]), row2(mla_kv_norm[layer])
        hy_args = (fw1[layer], row2(hy_fb1[layer]), hy_fw2[layer], row2(hy_fb2[layer]), hy_fw3[layer],
                   row2(hy_fb3[layer]), hy_bias[layer])
        conv = (hy_conv_w[layer], row2(hy_conv_b[layer]))

        qm, km, vm, dq, dk, dv, uh = _in_proj(x, sh1, sc1, win[layer], gq, wq[layer], gkv, wkv[layer], tab_x, *conv, tm_x)
        cqm, ckm, cvm, cdq, cdk, cdv, cuh = _in_proj(ctx, csh1, csc1, win[layer], gq, wq[layer], gkv, wkv[layer],
                                                     tab_c, *conv, tm_c)
        om = _mla_attention(qm, [ckm, km], [cvm, vm], tq_x, MLA_HEADS // 2)
        od = _diff_attention(dq, [cdk, dk], [cdv, dv], lam_p[layer], subln[layer], lam_init, tq_x, DIFF_HEADS)
        oh = _hyena(uh, hyc_x, *hy_args)
        vec = jnp.stack([b_out[layer], ln1_g[layer], ln1_b[layer], b_ff2[layer], ln2_g[layer], ln2_b[layer]])
        mix_w = (wo_m[layer], wo_d[layer], wo_h[layer], w1[layer], row2(b_ff1[layer]), w2[layer], vec)
        x = _mix_ffn(om, od, oh, x, mod[:, 0, 2:6], *mix_w, alpha, tm_x, ntf_x, tff)
        if need_ctx:
            com = _mla_attention(cqm, [ckm], [cvm], tq_c, MLA_HEADS // 2)
            cod = _diff_attention(cdq, [cdk], [cdv], lam_p[layer], subln[layer], lam_init, tq_c, DIFF_HEADS)
            coh = _hyena(cuh, hyc_c, *hy_args)
            ctx = _mix_ffn(com, cod, coh, ctx, modc[:, 0, 2:6], *mix_w, alpha, tm_c, 1, tff)
    return x
```

```python
import functools
import math

import jax
import numpy as np
import jax.numpy as jnp
from jax import lax
from jax.experimental import pallas as pl
from jax.experimental.pallas import tpu as pltpu

F32 = jnp.float32
BF16 = jnp.bfloat16
HI = lax.Precision.HIGHEST

GRID_W = 64
ROPE_BASE = 10000.0
LN_EPS = 1e-6
MLA_HEADS, MLA_NOPE, MLA_ROPE, MLA_V = 6, 64, 32, 64
MLA_Q_RANK, MLA_KV_RANK = 256, 128
DIFF_HEADS, DIFF_DIM = 4, 48
DIFF_V = 2 * DIFF_DIM
HY_CH, HY_ORDER, HY_BANDS = 256, 2, 16
HY_EMB = 1 + 2 * HY_BANDS
HY_TARGET, HY_FAST_DECAY_PCT, HY_SLOW_DECAY_PCT = 1e-2, 0.3, 1.5
HY_MIN_RATE = -math.log(HY_TARGET) / HY_SLOW_DECAY_PCT
HY_MAX_RATE = -math.log(HY_TARGET) / HY_FAST_DECAY_PCT
N_MOD = 6

LOG2E = 1.0 / math.log(2.0)
LANE = 128
SUBLANE = 8
VMEM_LIMIT = 56 * 1024 * 1024

IN_MLA = MLA_Q_RANK + MLA_KV_RANK + MLA_ROPE
N_DQ = DIFF_HEADS * 2 * DIFF_DIM
IN_DIFF = 2 * N_DQ + DIFF_HEADS * DIFF_V
W_QM = MLA_HEADS * LANE
W_VM = MLA_HEADS * MLA_V
W_D = DIFF_HEADS * LANE
W_HY = (HY_ORDER + 1) * HY_CH
C_CQ, C_CKV, C_KR = 0, MLA_Q_RANK, MLA_Q_RANK + MLA_KV_RANK
C_DQ = C_KR + LANE
C_DK = C_DQ + W_D
C_DV = C_DK + W_D
C_HY = C_DV + W_D
W_IN = C_HY + W_HY


def _cparams(sem):
    return pltpu.CompilerParams(dimension_semantics=sem, vmem_limit_bytes=VMEM_LIMIT)


def _ln(x):
    mu = jnp.mean(x, -1, keepdims=True)
    xc = x - mu
    var = jnp.mean(xc * xc, -1, keepdims=True)
    return xc * lax.rsqrt(var + LN_EPS)


def _bdot(a, b):
    return jnp.dot(a, b, preferred_element_type=F32)


def _mod_kernel(c_ref, w_ref, b_ref, o_ref):
    c = c_ref[...]
    s = c / (1.0 + jnp.exp(-c))
    o_ref[...] = jnp.dot(s, w_ref[...], preferred_element_type=F32, precision=HI) + b_ref[...]


def _modulation(cc, w_mod, b_mod):
    depth, d, n = w_mod.shape
    r = cc.shape[0]
    tn = 1024
    return pl.pallas_call(
        _mod_kernel,
        grid=(depth, n // tn),
        in_specs=[pl.BlockSpec((r, d), lambda l, j: (0, 0)),
                  pl.BlockSpec((None, d, tn), lambda l, j: (l, 0, j)),
                  pl.BlockSpec((None, 1, tn), lambda l, j: (l, 0, j))],
        out_specs=pl.BlockSpec((None, r, tn), lambda l, j: (l, 0, j)),
        out_shape=jax.ShapeDtypeStruct((depth, r, n), F32),
        compiler_params=_cparams(("arbitrary", "arbitrary")),
        name="modulation",
    )(cc, w_mod, b_mod.reshape(depth, 1, n))


def _rope(x, c, sa, sb, half):
    return x * c + pltpu.roll(x, LANE - half, 1) * sa + pltpu.roll(x, half, 1) * sb


def _inproj_kernel(x_ref, xp_ref, xn_ref, sh_ref, sc_ref, win_ref, gq_ref, wq_ref, gkv_ref, wkv_ref, tab_ref,
                   cw_ref, cb_ref, qm_ref, km_ref, vm_ref, dq_ref, dk_ref, dv_ref, hy_ref):
    x = jnp.concatenate([x_ref[0], xp_ref[0], xn_ref[0]], axis=0)
    h = _ln(x) * (1.0 + sc_ref[0]) + sh_ref[0]
    p_all = _bdot(h.astype(BF16), win_ref[...])
    tm = x_ref.shape[1]
    p = p_all[0:tm]
    tab = tab_ref[...]
    cm, sam, sbm = tab[:, 0:LANE], tab[:, LANE:2 * LANE], tab[:, 2 * LANE:3 * LANE]
    cd, sad, sbd = tab[:, 3 * LANE:4 * LANE], tab[:, 4 * LANE:5 * LANE], tab[:, 5 * LANE:6 * LANE]

    cq = p[:, C_CQ:C_CQ + MLA_Q_RANK]
    qn = cq * lax.rsqrt(jnp.mean(cq * cq, -1, keepdims=True) + LN_EPS) * gq_ref[...]
    q = _bdot(qn.astype(BF16), wq_ref[...])
    ckv = p[:, C_CKV:C_CKV + MLA_KV_RANK]
    kvn = ckv * lax.rsqrt(jnp.mean(ckv * ckv, -1, keepdims=True) + LN_EPS) * gkv_ref[...]
    kv = _bdot(kvn.astype(BF16), wkv_ref[...])
    kr = _rope(p[:, C_KR:C_KR + LANE], cm, sam, sbm, MLA_ROPE // 4)
    scale_m = LOG2E * (MLA_NOPE + MLA_ROPE) ** -0.5
    for hd in range(MLA_HEADS):
        sl = slice(hd * LANE, (hd + 1) * LANE)
        qm_ref[0, :, sl] = (_rope(q[:, sl], cm, sam, sbm, MLA_ROPE // 4) * scale_m).astype(BF16)
        km_ref[0, :, sl] = (kv[:, sl] + kr).astype(BF16)
    vm_ref[0] = kv[:, W_QM:W_QM + W_VM].astype(BF16)
    scale_d = LOG2E * DIFF_DIM ** -0.5
    for hd in range(DIFF_HEADS):
        sl = slice(hd * LANE, (hd + 1) * LANE)
        dq = p[:, C_DQ + hd * LANE:C_DQ + (hd + 1) * LANE]
        dk = p[:, C_DK + hd * LANE:C_DK + (hd + 1) * LANE]
        dq_ref[0, :, sl] = (_rope(dq, cd, sad, sbd, DIFF_DIM // 4) * scale_d).astype(BF16)
        dk_ref[0, :, sl] = _rope(dk, cd, sad, sbd, DIFF_DIM // 4).astype(BF16)
    dv_ref[0] = p[:, C_DV:C_DV + W_D].astype(BF16)
    u = p[:, C_HY:C_HY + W_HY]
    i = pl.program_id(0)
    u_prev = jnp.where(i == 0, 0.0, p_all[tm + SUBLANE - 1:tm + SUBLANE, C_HY:C_HY + W_HY])
    u_next = jnp.where(i == pl.num_programs(0) - 1, 0.0, p_all[tm + SUBLANE:tm + SUBLANE + 1, C_HY:C_HY + W_HY])
    row = lax.broadcasted_iota(jnp.int32, u.shape, 0)
    up = jnp.where(row == 0, u_prev, pltpu.roll(u, 1, 0))
    un = jnp.where(row == tm - 1, u_next, pltpu.roll(u, tm - 1, 0))
    w = cw_ref[...]
    hy_ref[0] = (up * w[0:1] + u * w[1:2] + un * w[2:3] + cb_ref[...]).astype(BF16)


def _in_proj(x, sh, sc, win, gq, wq, gkv, wkv, tab, conv_w, conv_b, tm):
    b, t, d = x.shape
    hb = tm // SUBLANE
    halo = (1, SUBLANE, d)
    resident = pl.Buffered(1)
    per_batch = sh.shape[0] == b and b > 1
    mod_map = (lambda i, bb: (bb, 0, 0)) if per_batch else (lambda i, bb: (0, 0, 0))
    const = lambda i, bb: (0, 0)
    widths = (W_QM, W_QM, W_VM, W_D, W_D, W_D, W_HY)
    return pl.pallas_call(
        _inproj_kernel,
        grid=(t // tm, b),
        in_specs=[pl.BlockSpec((1, tm, d), lambda i, bb: (bb, i, 0)),
                  pl.BlockSpec(halo, lambda i, bb: (bb, jnp.maximum(i * hb - 1, 0), 0)),
                  pl.BlockSpec(halo, lambda i, bb: (bb, jnp.minimum((i + 1) * hb, t // SUBLANE - 1), 0)),
                  pl.BlockSpec((1, 1, d), mod_map),
                  pl.BlockSpec((1, 1, d), mod_map),
                  pl.BlockSpec(win.shape, const, pipeline_mode=resident),
                  pl.BlockSpec(gq.shape, const),
                  pl.BlockSpec(wq.shape, const, pipeline_mode=resident),
                  pl.BlockSpec(gkv.shape, const),
                  pl.BlockSpec(wkv.shape, const, pipeline_mode=resident),
                  pl.BlockSpec((tm, 6 * LANE), lambda i, bb: (i, 0)),
                  pl.BlockSpec(conv_w.shape, const), pl.BlockSpec(conv_b.shape, const)],
        out_specs=[pl.BlockSpec((1, tm, w), lambda i, bb: (bb, i, 0)) for w in widths],
        out_shape=[jax.ShapeDtypeStruct((b, t, w), BF16) for w in widths],
        compiler_params=_cparams(("arbitrary", "arbitrary")),
        name="in_proj",
    )(x, x, x, sh, sc, win, gq, wq, gkv, wkv, tab, conv_w, conv_b)


NEG_BIG = -1e30


def _lane_fold(x, op):
    r = x[:, 0:LANE]
    for i in range(1, x.shape[1] // LANE):
        r = op(r, x[:, i * LANE:(i + 1) * LANE])
    return r


def _score_pass(q, parts, ksl, s_refs):
    mx = jnp.full((q.shape[0], LANE), NEG_BIG, F32)
    for (k_ref, _, n, tk), s_ref in zip(parts, s_refs):
        for j in range(n // tk):
            k = k_ref[0, j * tk:(j + 1) * tk, ksl]
            s = lax.dot_general(q, k, (((1,), (1,)), ((), ())), preferred_element_type=F32)
            s_ref[j] = s
            mx = jnp.maximum(mx, _lane_fold(s, jnp.maximum))
    return jnp.max(mx, axis=-1, keepdims=True)


def _softmax_pv(q, parts, ksl, vsl, s_refs):
    m = _score_pass(q, parts, ksl, s_refs)
    ls = jnp.zeros((q.shape[0], LANE), F32)
    acc = jnp.zeros((q.shape[0], LANE), F32)
    for (_, v_ref, n, tk), s_ref in zip(parts, s_refs):
        for j in range(n // tk):
            p = jnp.exp2(s_ref[j] - m)
            ls = ls + _lane_fold(p, jnp.add)
            acc = acc + _bdot(p.astype(BF16), v_ref[0, j * tk:(j + 1) * tk, vsl])
    return acc * (1.0 / jnp.sum(ls, axis=-1, keepdims=True))


def _mla_attn_kernel(*refs, part_shapes, n_pairs):
    np_ = len(part_shapes)
    q_ref = refs[0]
    k_refs = refs[1:1 + np_]
    v_refs = refs[1 + np_:1 + 2 * np_]
    o_ref = refs[1 + 2 * np_]
    s_sets = (refs[2 + 2 * np_:2 + 3 * np_], refs[2 + 3 * np_:])
    parts = [(k_refs[i], v_refs[i]) + part_shapes[i] for i in range(np_)]
    for pr in range(n_pairs):
        outs = []
        for hh in range(2):
            hd = 2 * pr + hh
            sl = slice(hd * LANE, (hd + 1) * LANE)
            outs.append(_softmax_pv(q_ref[0, :, sl], parts, sl, slice(pr * LANE, (pr + 1) * LANE), s_sets[hh]))
        lane = lax.broadcasted_iota(jnp.int32, outs[0].shape, 1)
        o_ref[0, :, pr * LANE:(pr + 1) * LANE] = jnp.where(lane < MLA_V, outs[0], outs[1]).astype(BF16)


def _chunk(n):
    return n if n <= 512 else 512


def _mla_attention(q, ks, vs, tq, n_pairs):
    b, t, _ = q.shape
    part_shapes = tuple((k.shape[1], _chunk(k.shape[1])) for k in ks)
    in_specs = [pl.BlockSpec((1, tq, 2 * n_pairs * LANE), lambda bb, g, i: (bb, i, g))]
    in_specs += [pl.BlockSpec((1, k.shape[1], 2 * n_pairs * LANE), lambda bb, g, i: (bb, 0, g)) for k in ks]
    in_specs += [pl.BlockSpec((1, v.shape[1], n_pairs * LANE), lambda bb, g, i: (bb, 0, g)) for v in vs]
    scratch = [pltpu.VMEM((n // tk, tq, tk), F32) for n, tk in part_shapes] * 2
    return pl.pallas_call(
        functools.partial(_mla_attn_kernel, part_shapes=part_shapes, n_pairs=n_pairs),
        grid=(b, MLA_HEADS // (2 * n_pairs), t // tq),
        in_specs=in_specs,
        out_specs=pl.BlockSpec((1, tq, n_pairs * LANE), lambda bb, g, i: (bb, i, g)),
        out_shape=jax.ShapeDtypeStruct((b, t, W_VM), BF16),
        scratch_shapes=scratch,
        compiler_params=_cparams(("arbitrary", "arbitrary", "arbitrary")),
        name="mla_attention",
    )(q, *ks, *vs)


def _diff_attn_kernel(*refs, part_shapes, lam_init, n_heads):
    np_ = len(part_shapes)
    q_ref = refs[0]
    k_refs = refs[1:1 + np_]
    v_refs = refs[1 + np_:1 + 2 * np_]
    lam_ref, g_ref, o_ref = refs[1 + 2 * np_:4 + 2 * np_]
    s_sets = (refs[4 + 2 * np_:4 + 3 * np_], refs[4 + 3 * np_:])
    parts = [(k_refs[i], v_refs[i]) + part_shapes[i] for i in range(np_)]
    lp = lam_ref[...]
    lam = (jnp.exp(jnp.sum(lp[0:1] * lp[1:2], axis=-1, keepdims=True))
           - jnp.exp(jnp.sum(lp[2:3] * lp[3:4], axis=-1, keepdims=True)) + lam_init)
    for hd in range(n_heads):
        sl = slice(hd * LANE, (hd + 1) * LANE)
        q = q_ref[0, :, sl]
        lane = lax.broadcasted_iota(jnp.int32, q.shape, 1)
        zero = jnp.zeros_like(q)
        o1 = _softmax_pv(jnp.where(lane < LANE // 2, q, zero), parts, sl, sl, s_sets[0])
        o2 = _softmax_pv(jnp.where(lane >= LANE // 2, q, zero), parts, sl, sl, s_sets[1])
        o = o1 - lam * o2
        ms = jnp.sum(o * o, axis=-1, keepdims=True) * (1.0 / DIFF_V)
        o_ref[0, :, sl] = (o * lax.rsqrt(ms + LN_EPS) * g_ref[...] * (1.0 - lam_init)).astype(BF16)


def _diff_attention(q, ks, vs, lam_p, subln, lam_init, tq, n_heads):
    b, t, _ = q.shape
    part_shapes = tuple((k.shape[1], _chunk(k.shape[1])) for k in ks)
    hmap = lambda bb, g, i: (bb, 0, g)
    in_specs = [pl.BlockSpec((1, tq, n_heads * LANE), lambda bb, g, i: (bb, i, g))]
    in_specs += [pl.BlockSpec((1, k.shape[1], n_heads * LANE), hmap) for k in ks]
    in_specs += [pl.BlockSpec((1, v.shape[1], n_heads * LANE), hmap) for v in vs]
    in_specs += [pl.BlockSpec(lam_p.shape, lambda bb, g, i: (0, 0)),
                 pl.BlockSpec(subln.shape, lambda bb, g, i: (0, 0))]
    scratch = [pltpu.VMEM((n // tk, tq, tk), F32) for n, tk in part_shapes] * 2
    return pl.pallas_call(
        functools.partial(_diff_attn_kernel, part_shapes=part_shapes, lam_init=lam_init, n_heads=n_heads),
        grid=(b, DIFF_HEADS // n_heads, t // tq),
        in_specs=in_specs,
        out_specs=pl.BlockSpec((1, tq, n_heads * LANE), lambda bb, g, i: (bb, i, g)),
        out_shape=jax.ShapeDtypeStruct((b, t, W_D), BF16),
        scratch_shapes=scratch,
        compiler_params=_cparams(("arbitrary", "arbitrary", "arbitrary")),
        name="diff_attention",
    )(q, *ks, *vs, lam_p, subln)


def _hyfilt_kernel(emb_ref, win_ref, fw1_ref, fb1_ref, fw2_ref, fb2_ref, fwf_ref, fbf_ref, fwb_ref, fbb_ref,
                   d_ref, h_scr):
    @pl.when(pl.program_id(0) == 0)
    def _():
        h1 = jnp.sin(jnp.dot(emb_ref[...], fw1_ref[...], preferred_element_type=F32, precision=HI) + fb1_ref[...])
        h_scr[...] = jnp.sin(jnp.dot(h1, fw2_ref[...], preferred_element_type=F32, precision=HI) + fb2_ref[...])

    h = h_scr[...]
    w = win_ref[...]
    fwd = (jnp.dot(h, fwf_ref[...], preferred_element_type=F32, precision=HI) + fbf_ref[...]) * w
    bwd = (jnp.dot(h, fwb_ref[...], preferred_element_type=F32, precision=HI) + fbb_ref[...]) * w
    row = lax.broadcasted_iota(jnp.int32, bwd.shape, 0)
    bwd = jnp.where(row == 0, 0.0, bwd)
    norm = jnp.sum(jnp.abs(fwd), axis=0, keepdims=True) + jnp.sum(jnp.abs(bwd), axis=0, keepdims=True)
    inv = 1.0 / norm
    d_ref[0] = ((fwd + bwd) * inv).astype(BF16)
    d_ref[1] = ((fwd - bwd) * inv).astype(BF16)


def _hy_filters(emb, win, fw1, fb1, fw2, fb2, fw3, fb3):
    l = emb.shape[0]
    hid = fw2.shape[0]
    nblk = HY_CH // LANE
    const = lambda g: (0, 0)
    fcol = lambda g: (0, (g // nblk) * 2 * nblk + g % nblk)
    bcol = lambda g: (0, (g // nblk) * 2 * nblk + nblk + g % nblk)
    return pl.pallas_call(
        _hyfilt_kernel,
        grid=(HY_ORDER * nblk,),
        in_specs=[pl.BlockSpec(emb.shape, const),
                  pl.BlockSpec((l, LANE), lambda g: (0, g % nblk)),
                  pl.BlockSpec(fw1.shape, const), pl.BlockSpec(fb1.shape, const),
                  pl.BlockSpec(fw2.shape, const), pl.BlockSpec(fb2.shape, const),
                  pl.BlockSpec((hid, LANE), fcol), pl.BlockSpec((1, LANE), fcol),
                  pl.BlockSpec((hid, LANE), bcol), pl.BlockSpec((1, LANE), bcol)],
        out_specs=pl.BlockSpec((2, None, l, LANE), lambda g: (0, g // nblk, 0, g % nblk)),
        out_shape=jax.ShapeDtypeStruct((2, HY_ORDER, l, HY_CH), BF16),
        scratch_shapes=[pltpu.VMEM((l, hid), F32)],
        compiler_params=_cparams(("arbitrary",)),
        name="hyena_filters",
    )(emb, win, fw1, fb1, fw2, fb2, fw3, fb3, fw3, fb3)


def _hyspec_kernel(f_ref, s_ref, d_ref, o_ref):
    tf = f_ref.shape[0] // 2
    o_ref[0:tf, :] = _bdot(f_ref[0:tf, :], s_ref[...])
    o_ref[tf:2 * tf, :] = _bdot(f_ref[tf:2 * tf, :], d_ref[...])


def _hy_spectrum(fmat, filt, tf):
    rows, l = fmat.shape
    return pl.pallas_call(
        _hyspec_kernel,
        grid=(rows // (2 * tf), HY_ORDER),
        in_specs=[pl.BlockSpec((2 * tf, l), lambda j, o: (j, 0)),
                  pl.BlockSpec((None, None, l, HY_CH), lambda j, o: (0, o, 0, 0)),
                  pl.BlockSpec((None, None, l, HY_CH), lambda j, o: (1, o, 0, 0))],
        out_specs=pl.BlockSpec((2 * tf, HY_CH), lambda j, o: (j, o)),
        out_shape=jax.ShapeDtypeStruct((rows, HY_ORDER * HY_CH), F32),
        compiler_params=_cparams(("arbitrary", "arbitrary")),
        name="hyena_filter_spectrum",
    )(fmat, filt, filt)


def _hyfwd_kernel(f_ref, z_ref, k_ref, y_ref):
    tf = f_ref.shape[0] // 2
    s = _bdot(f_ref[...], z_ref[0])
    sr, si = s[0:tf], s[tf:2 * tf]
    kr, ki = k_ref[0:tf, :], k_ref[tf:2 * tf, :]
    y_ref[0, 0:tf, :] = (sr * kr - si * ki).astype(BF16)
    y_ref[0, tf:2 * tf, :] = (sr * ki + si * kr).astype(BF16)


def _hy_forward(fmat, z, kspec, order, tf):
    rows, l = fmat.shape
    b = z.shape[0]
    return pl.pallas_call(
        _hyfwd_kernel,
        grid=(rows // (2 * tf), b),
        in_specs=[pl.BlockSpec((2 * tf, l), lambda j, bb: (j, 0)),
                  pl.BlockSpec((1, l, HY_CH), lambda j, bb: (bb, 0, 0)),
                  pl.BlockSpec((2 * tf, HY_CH), lambda j, bb: (j, order))],
        out_specs=pl.BlockSpec((1, 2 * tf, HY_CH), lambda j, bb: (bb, j, 0)),
        out_shape=jax.ShapeDtypeStruct((b, rows, HY_CH), BF16),
        compiler_params=_cparams(("arbitrary", "arbitrary")),
        name="hyena_dft_forward",
    )(fmat, z, kspec)


def _hyinv_kernel(g_ref, y_ref, z_ref, x_ref, b_ref, o_ref):
    conv = _bdot(g_ref[...], y_ref[0])
    z = z_ref[0].astype(F32)
    o_ref[0] = (x_ref[0].astype(F32) * (conv + z * b_ref[...])).astype(BF16)


def _hy_inverse(gmat, y, z, gate, bias, tt):
    l, rows = gmat.shape
    b = y.shape[0]
    return pl.pallas_call(
        _hyinv_kernel,
        grid=(l // tt, b),
        in_specs=[pl.BlockSpec((tt, rows), lambda i, bb: (i, 0)),
                  pl.BlockSpec((1, rows, HY_CH), lambda i, bb: (bb, 0, 0)),
                  pl.BlockSpec((1, tt, HY_CH), lambda i, bb: (bb, i, 0)),
                  pl.BlockSpec((1, tt, HY_CH), lambda i, bb: (bb, i, 0)),
                  pl.BlockSpec((1, HY_CH), lambda i, bb: (0, 0))],
        out_specs=pl.BlockSpec((1, tt, HY_CH), lambda i, bb: (bb, i, 0)),
        out_shape=jax.ShapeDtypeStruct((b, l, HY_CH), BF16),
        compiler_params=_cparams(("arbitrary", "arbitrary")),
        name="hyena_dft_inverse",
    )(gmat, y, z, gate, bias)


def _dft_tables(l, tf):
    k = jnp.arange(l, dtype=jnp.int32)
    n = jnp.arange(l, dtype=jnp.int32)
    ph = ((2 * k + 1)[:, None] * n[None, :]) % (4 * l)
    ang = ph.astype(F32) * (2.0 * math.pi / (4 * l))
    c = jnp.cos(ang).reshape(l // tf, 1, tf, l)
    s = (-jnp.sin(ang)).reshape(l // tf, 1, tf, l)
    f = jnp.concatenate([c, s], axis=1).reshape(2 * l, l)
    return f.astype(BF16), (f.T * (1.0 / l)).astype(BF16)


def _hy_consts(l):
    t = jnp.arange(l, dtype=F32)
    bands = jnp.arange(1, HY_BANDS + 1, dtype=F32)
    ang = (2.0 * math.pi / l) * t[:, None] * bands[None, :]
    emb = jnp.concatenate([(t / l)[:, None], jnp.cos(ang), jnp.sin(ang)], -1)
    emb = jnp.pad(emb, ((0, 0), (0, LANE - HY_EMB)))
    rates = jnp.linspace(HY_MIN_RATE, HY_MAX_RATE, HY_CH, dtype=F32)
    win = jnp.exp(-(t / l)[:, None] * rates[None, :])
    return emb, win


HY_N1 = 16
HY_J = 16
HY_GROUPS = 8
HY_KG = 16


def _hy2_tables(l):
    n1h, jj = HY_N1, HY_J
    n2, k1n = l // n1h, 2 * n1h
    k2n = n2 // 2
    k1 = jnp.arange(k1n, dtype=jnp.int32)
    ph1 = ((2 * k1 + 1)[:, None] * jnp.arange(n1h, dtype=jnp.int32)[None, :]) % (4 * n1h)
    a1 = ph1.astype(F32) * (2.0 * math.pi / (4 * n1h))
    f1 = jnp.stack([jnp.cos(a1), -jnp.sin(a1)], axis=1)
    f1big = jnp.einsum('krn,ab->kranb', f1, jnp.eye(jj, dtype=F32)).reshape(k1n * 2 * jj, n1h * jj)
    kk = k1[:, None] + k1n * jnp.arange(k2n, dtype=jnp.int32)[None, :]
    ph2 = ((2 * kk + 1)[:, :, None] * jnp.arange(n2, dtype=jnp.int32)[None, None, :]) % (4 * l)
    a2 = ph2.astype(F32) * (2.0 * math.pi / (4 * l))
    mr, mi = jnp.cos(a2), -jnp.sin(a2)
    f2t = jnp.concatenate([jnp.concatenate([mr, -mi], axis=2), jnp.concatenate([mi, mr], axis=2)], axis=1)
    return (f1big.astype(BF16), (f1big.T * (1.0 / l)).astype(BF16), f2t.astype(BF16),
            jnp.swapaxes(f2t, 1, 2).astype(BF16))


def _hy2_s1_kernel(f_ref, z_ref, o_ref):
    n1h, _, c = z_ref.shape[1:]
    k1n = o_ref.shape[1]
    for g in range(HY_GROUPS):
        cols = slice(g * HY_J, (g + 1) * HY_J)
        z = z_ref[0, :, cols, :].reshape(n1h * HY_J, c)
        a = _bdot(f_ref[...], z)
        o_ref[0, :, :, cols, :] = a.astype(BF16).reshape(k1n, 2, HY_J, c)


def _hy2_stage1(f1big, vxx, which, z5):
    src = vxx if z5 is None else z5
    b, n1h, n2, _ = src.shape
    c = HY_CH
    k1n = 2 * n1h
    w = HY_J * HY_GROUPS
    zspec = pl.BlockSpec((1, n1h, w, c), lambda bb, g: (bb, 0, g, which if z5 is None else 0))
    return pl.pallas_call(
        _hy2_s1_kernel,
        grid=(b, n2 // w),
        in_specs=[pl.BlockSpec(f1big.shape, lambda bb, g: (0, 0)), zspec],
        out_specs=pl.BlockSpec((1, k1n, 2, w, c), lambda bb, g: (bb, 0, 0, g, 0)),
        out_shape=jax.ShapeDtypeStruct((b, k1n, 2, n2, c), BF16),
        compiler_params=_cparams(("arbitrary", "arbitrary")),
        name="hyena_stage1",
    )(f1big, src)


def _hy2_fspec_kernel(f_ref, s_ref, d_ref, o_ref):
    h = o_ref.shape[1] // 2
    for i in range(HY_KG):
        o_ref[i, 0:h, :] = _bdot(f_ref[i], s_ref[0, i])[0:h]
        o_ref[i, h:2 * h, :] = _bdot(f_ref[i], d_ref[0, i])[h:2 * h]


def _hy2_filter_spectrum(f2t, a5):
    nsig, k1n, _, n2, c = a5.shape
    a4 = a5.reshape(nsig, k1n, 2 * n2, c)
    return pl.pallas_call(
        _hy2_fspec_kernel,
        grid=(k1n // HY_KG, HY_ORDER),
        in_specs=[pl.BlockSpec((HY_KG, n2, 2 * n2), lambda g, o: (g, 0, 0)),
                  pl.BlockSpec((1, HY_KG, 2 * n2, c), lambda g, o: (o, g, 0, 0)),
                  pl.BlockSpec((1, HY_KG, 2 * n2, c), lambda g, o: (HY_ORDER + o, g, 0, 0))],
        out_specs=pl.BlockSpec((HY_KG, n2, c), lambda g, o: (g, 0, o)),
        out_shape=jax.ShapeDtypeStruct((k1n, n2, HY_ORDER * c), F32),
        compiler_params=_cparams(("arbitrary", "arbitrary")),
        name="hyena_filter_spectrum2",
    )(f2t, a4, a4)


def _hy2_s2_kernel(f_ref, g_ref, a_ref, k_ref, o_ref):
    for i in range(HY_KG):
        y = _bdot(f_ref[i], a_ref[0, i])
        h = y.shape[0] // 2
        yr, yi = y[:h], y[h:]
        kr, ki = k_ref[i, 0:h, :], k_ref[i, h:2 * h, :]
        p = jnp.concatenate([yr * kr - yi * ki, yr * ki + yi * kr], axis=0).astype(BF16)
        o_ref[0, i] = _bdot(g_ref[i], p).astype(BF16)


def _hy2_stage2(f2t, g2t, a5, kspec, order):
    b, k1n, _, n2, c = a5.shape
    a4 = a5.reshape(b, k1n, 2 * n2, c)
    out = pl.pallas_call(
        _hy2_s2_kernel,
        grid=(k1n // HY_KG, b),
        in_specs=[pl.BlockSpec((HY_KG, n2, 2 * n2), lambda g, bb: (g, 0, 0)),
                  pl.BlockSpec((HY_KG, 2 * n2, n2), lambda g, bb: (g, 0, 0)),
                  pl.BlockSpec((1, HY_KG, 2 * n2, c), lambda g, bb: (bb, g, 0, 0)),
                  pl.BlockSpec((HY_KG, n2, c), lambda g, bb: (g, 0, order))],
        out_specs=pl.BlockSpec((1, HY_KG, 2 * n2, c), lambda g, bb: (bb, g, 0, 0)),
        out_shape=jax.ShapeDtypeStruct(a4.shape, BF16),
        compiler_params=_cparams(("arbitrary", "arbitrary")),
        name="hyena_stage2",
    )(f2t, g2t, a4, kspec)
    return out.reshape(a5.shape)


def _hy2_s3_kernel(g_ref, c_ref, z_ref, x_ref, b_ref, *rest, chain):
    if chain:
        f_ref, o_ref, a_ref = rest
    else:
        (o_ref,) = rest
    k1n = c_ref.shape[1]
    n1h, _, c = z_ref.shape[1:]
    for g in range(HY_GROUPS):
        cols = slice(g * HY_J, (g + 1) * HY_J)
        cc = c_ref[0, :, :, cols, :].reshape(k1n * 2 * HY_J, c)
        conv = _bdot(g_ref[...], cc)
        z = z_ref[0, :, cols, :].reshape(n1h * HY_J, c).astype(F32)
        x = x_ref[0, :, cols, :].reshape(n1h * HY_J, c).astype(F32)
        o = (x * (conv + z * b_ref[...])).astype(BF16)
        o_ref[0, :, cols, :] = o.reshape(n1h, HY_J, c)
        if chain:
            a_ref[0, :, :, cols, :] = _bdot(f_ref[...], o).astype(BF16).reshape(k1n, 2, HY_J, c)


def _hy2_stage3(g1big, c5, vxx, zwhich, z5, xwhich, bias, f1big=None):
    b, k1n, _, n2, c = c5.shape
    n1h = k1n // 2
    w = HY_J * HY_GROUPS
    chain = f1big is not None
    const = lambda bb, g: (0, 0)
    sel = lambda which: pl.BlockSpec((1, n1h, w, c), lambda bb, g: (bb, 0, g, which))
    row = pl.BlockSpec((1, n1h, w, c), lambda bb, g: (bb, 0, g, 0))
    spec5 = pl.BlockSpec((1, k1n, 2, w, c), lambda bb, g: (bb, 0, 0, g, 0))
    out5 = jax.ShapeDtypeStruct((b, n1h, n2, c), BF16)
    in_specs = [pl.BlockSpec(g1big.shape, const), spec5, sel(zwhich) if z5 is None else row, sel(xwhich),
                pl.BlockSpec((1, c), const)]
    args = [g1big, c5, vxx if z5 is None else z5, vxx, bias]
    if chain:
        in_specs.append(pl.BlockSpec(f1big.shape, const))
        args.append(f1big)
    return pl.pallas_call(
        functools.partial(_hy2_s3_kernel, chain=chain),
        grid=(b, n2 // w),
        in_specs=in_specs,
        out_specs=[row, spec5] if chain else row,
        out_shape=[out5, jax.ShapeDtypeStruct(c5.shape, BF16)] if chain else out5,
        compiler_params=_cparams(("arbitrary", "arbitrary")),
        name="hyena_stage3",
    )(*args)


def _hyena(vxx, consts, fw1, fb1, fw2, fb2, fw3, fb3, hbias):
    emb, win, tabs = consts
    filt = _hy_filters(emb, win, fw1, fb1, fw2, fb2, fw3, fb3)
    b, l, _ = vxx.shape
    c = HY_CH
    if len(tabs) == 3:
        fmat, gmat, tf = tabs
        kspec = _hy_spectrum(fmat, filt, tf)
        v, x1, x2 = vxx[..., 0:c], vxx[..., c:2 * c], vxx[..., 2 * c:3 * c]
        z = _hy_inverse(gmat, _hy_forward(fmat, v, kspec, 0, tf), v, x1, hbias[0:1], tf)
        return _hy_inverse(gmat, _hy_forward(fmat, z, kspec, 1, tf), z, x2, hbias[1:2], tf)
    f1big, g1big, f2t, g2t = tabs
    n2 = l // HY_N1
    filt5 = filt.reshape(2 * HY_ORDER, HY_N1, n2, c)
    kspec = _hy2_filter_spectrum(f2t, _hy2_stage1(f1big, None, 0, filt5))
    vxx5 = vxx.reshape(b, HY_N1, n2, 3 * c)
    c5 = _hy2_stage2(f2t, g2t, _hy2_stage1(f1big, vxx5, 0, None), kspec, 0)
    z5, a5 = _hy2_stage3(g1big, c5, vxx5, 0, None, 1, hbias[0:1], f1big)
    c5 = _hy2_stage2(f2t, g2t, a5, kspec, 1)
    return _hy2_stage3(g1big, c5, vxx5, 0, z5, 2, hbias[1:2]).reshape(b, l, c)


def _mix_ffn_kernel(om_ref, od_ref, oh_ref, x_ref, mod_ref, wm_ref, wd_ref, wh_ref, w1_ref, b1_ref, w2_ref,
                    vec_ref, o_ref, *, alpha, tm, tf):
    g1, sh2, sc2, g2 = (mod_ref[0, i:i + 1, :] for i in range(4))
    bo, lg1, lb1, b2, lg2, lb2 = (vec_ref[i:i + 1, :] for i in range(6))
    nt = x_ref.shape[1] // tm
    rows = [slice(t * tm, (t + 1) * tm) for t in range(nt)]
    xs, hs = [], []
    for r in rows:
        y = _bdot(om_ref[0, r, :], wm_ref[...]) + _bdot(od_ref[0, r, :], wd_ref[...]) + _bdot(oh_ref[0, r, :], wh_ref[...])
        x1 = _ln(alpha * x_ref[0, r, :] + g1 * (y + bo)) * lg1 + lb1
        xs.append(x1)
        hs.append((_ln(x1) * (1.0 + sc2) + sh2).astype(BF16))
    accs = [None] * nt
    for c in range(w1_ref.shape[1] // tf):
        cols = slice(c * tf, (c + 1) * tf)
        for t in range(nt):
            a = jnp.maximum(_bdot(hs[t], w1_ref[:, cols]) + b1_ref[:, cols], 0.0)
            y = _bdot((a * a).astype(BF16), w2_ref[cols, :])
            accs[t] = y if accs[t] is None else accs[t] + y
    for t in range(nt):
        z = alpha * xs[t] + g2 * (accs[t] + b2)
        o_ref[0, rows[t], :] = _ln(z) * lg2 + lb2


def _mix_ffn(om, od, oh, x, mod, wm, wd, wh, w1, b1, w2, vec, alpha, tm, nt, tf):
    b, t, d = x.shape
    per_batch = mod.shape[0] == b and b > 1
    gmap = (lambda bb, i: (bb, 0, 0)) if per_batch else (lambda bb, i: (0, 0, 0))
    const = lambda bb, i: (0, 0)
    row = lambda bb, i: (bb, i, 0)
    resident = pl.Buffered(1)
    blk = lambda a: pl.BlockSpec((1, tm * nt, a.shape[2]), row)
    res = lambda a: pl.BlockSpec(a.shape, const, pipeline_mode=resident)
    return pl.pallas_call(
        functools.partial(_mix_ffn_kernel, alpha=alpha, tm=tm, tf=tf),
        grid=(b, t // (tm * nt)),
        in_specs=[blk(om), blk(od), blk(oh), blk(x), pl.BlockSpec((1,) + mod.shape[1:], gmap),
                  res(wm), res(wd), res(wh), res(w1), pl.BlockSpec(b1.shape, const), res(w2),
                  pl.BlockSpec(vec.shape, const)],
        out_specs=blk(x),
        out_shape=jax.ShapeDtypeStruct((b, t, d), F32),
        compiler_params=_cparams(("arbitrary", "arbitrary")),
        name="mix_ffn",
    )(om, od, oh, x, mod, wm, wd, wh, w1, b1, w2, vec)


def _rope_lane_plan(starts, dims):
    h = dims // 2
    inv = ROPE_BASE ** (-(np.arange(h // 2, dtype=np.float64) * (2.0 / h)))
    fr, fc, lo, hi = (np.zeros(LANE, np.float32) for _ in range(4))
    for st in starts:
        for axis_i, f in enumerate((fr, fc)):
            base = st + axis_i * h
            f[base:base + h // 2] = inv
            f[base + h // 2:base + h] = inv
            lo[base:base + h // 2] = 1.0
            hi[base + h // 2:base + h] = 1.0
    return fr, fc, lo, hi


def _rope_tables(t_len, roped):
    pos_t = jnp.arange(t_len)
    scale = 1.0 if roped else 0.0
    row = (pos_t // GRID_W).astype(F32)[:, None] * scale
    col = (pos_t % GRID_W).astype(F32)[:, None] * scale
    out = []
    for starts, dims in (([MLA_NOPE], MLA_ROPE), ([0, LANE // 2], DIFF_DIM)):
        fr, fc, lo, hi = _rope_lane_plan(starts, dims)
        ang = row * fr[None, :] + col * fc[None, :]
        sin = jnp.sin(ang)
        out += [jnp.cos(ang), -sin * lo[None, :], sin * hi[None, :]]
    return jnp.concatenate(out, axis=1)


def _prep_weights(w_in, w_q_up, w_kv_up, w_out, diff_lambda, diff_subln):
    depth, d, _ = w_in.shape
    s1, s2 = IN_MLA, IN_MLA + IN_DIFF
    pad_last = lambda a, n: jnp.pad(a, [(0, 0)] * (a.ndim - 1) + [(0, n - a.shape[-1])])
    kr = jnp.pad(w_in[..., C_KR:s1], ((0, 0), (0, 0), (MLA_NOPE, LANE - MLA_NOPE - MLA_ROPE)))
    dq = pad_last(w_in[..., s1:s1 + N_DQ].reshape(depth, d, DIFF_HEADS, 2, DIFF_DIM), LANE // 2)
    dk = pad_last(w_in[..., s1 + N_DQ:s1 + 2 * N_DQ].reshape(depth, d, DIFF_HEADS, 2, DIFF_DIM), LANE // 2)
    dv = pad_last(w_in[..., s1 + 2 * N_DQ:s2].reshape(depth, d, DIFF_HEADS, DIFF_V), LANE)
    win = jnp.concatenate([w_in[..., :C_KR], kr, dq.reshape(depth, d, W_D), dk.reshape(depth, d, W_D),
                           dv.reshape(depth, d, W_D), w_in[..., s2:]], axis=-1).astype(BF16)
    wq = pad_last(w_q_up.reshape(depth, MLA_Q_RANK, MLA_HEADS, MLA_NOPE + MLA_ROPE), LANE)
    wq = wq.reshape(depth, MLA_Q_RANK, W_QM).astype(BF16)
    kv = w_kv_up.reshape(depth, MLA_KV_RANK, MLA_HEADS, MLA_NOPE + MLA_V)
    wk = pad_last(kv[..., :MLA_NOPE], LANE).reshape(depth, MLA_KV_RANK, W_QM)
    wv = kv[..., MLA_NOPE:].reshape(depth, MLA_KV_RANK, W_VM)
    wkv = jnp.concatenate([wk, wv], axis=-1).astype(BF16)
    wo_m = w_out[:, :W_VM].astype(BF16)
    wo_d = w_out[:, W_VM:W_VM + DIFF_HEADS * DIFF_V].reshape(depth, DIFF_HEADS, DIFF_V, -1)
    wo_d = jnp.pad(wo_d, ((0, 0), (0, 0), (0, LANE - DIFF_V), (0, 0))).reshape(depth, W_D, -1).astype(BF16)
    wo_h = w_out[:, W_VM + DIFF_HEADS * DIFF_V:].astype(BF16)
    lam = pad_last(diff_lambda.astype(F32), LANE)
    subln = pad_last(diff_subln.astype(F32), LANE)[:, None, :]
    return win, wq, wkv, wo_m, wo_d, wo_h, lam, subln


def kernel(x, c, ctx, c_ctx, w_mod, b_mod, w_in, mla_q_norm, w_q_up, mla_kv_norm, w_kv_up, diff_lambda, diff_subln, hy_conv_w, hy_conv_b, hy_fw1, hy_fb1, hy_fw2, hy_fb2, hy_fw3, hy_fb3, hy_bias, w_out, b_out, ln1_g, ln1_b, w_ff1, b_ff1, w_ff2, b_ff2, ln2_g, ln2_b):
    bsz, seq, d = x.shape
    n_ctx = ctx.shape[1]
    depth = w_in.shape[0]
    alpha = (2.0 * depth) ** 0.25

    rpad = -(bsz + 1) % SUBLANE
    cc = jnp.concatenate([c, c_ctx[None, :], jnp.zeros((rpad, d), F32)], axis=0)
    mod_all = _modulation(cc, w_mod, b_mod)

    win, wq, wkv, wo_m, wo_d, wo_h, lam_p, subln = _prep_weights(w_in, w_q_up, w_kv_up, w_out, diff_lambda, diff_subln)
    w1 = w_ff1.astype(BF16)
    w2 = w_ff2.astype(BF16)
    fw1 = jnp.pad(hy_fw1, ((0, 0), (0, LANE - HY_EMB), (0, 0)))
    tab_x = _rope_tables(seq, True)
    tab_c = _rope_tables(n_ctx, False)
    def hy_tables(l):
        if l >= 1024 and l % (HY_N1 * HY_J * HY_GROUPS) == 0:
            return _hy2_tables(l)
        return _dft_tables(l, min(512, l)) + (min(512, l),)

    hyc_x = _hy_consts(seq) + (hy_tables(seq),)
    hyc_c = _hy_consts(n_ctx) + (hy_tables(n_ctx),)
    tm_x, tm_c = min(512, seq), min(512, n_ctx)
    tmi_x = min(1024, seq)
    ntf_x = 2 if seq % (2 * tm_x) == 0 else 1
    tq_x, tq_c = min(256, seq), min(256, n_ctx)
    tff = min(1024, w1.shape[2])
    row2 = lambda a: a.reshape(1, -1)

    for layer in range(depth):
        need_ctx = layer < depth - 1
        lam_init = 0.8 - 0.6 * math.exp(-0.3 * layer)
        mod = mod_all[layer, :bsz].reshape(bsz, 1, N_MOD, d)
        modc = mod_all[layer, bsz:bsz + 1].reshape(1, 1, N_MOD, d)
        sh1, sc1 = mod[:, :, 0], mod[:, :, 1]
        csh1, csc1 = modc[:, :, 0], modc[:, :, 1]
        gq, gkv = row2(mla_q_norm[layer]), row2(mla_kv_norm[layer])
        hy_args = (fw1[layer], row2(hy_fb1[layer]), hy_fw2[layer], row2(hy_fb2[layer]), hy_fw3[layer],
                   row2(hy_fb3[layer]), hy_bias[layer])
        conv = (hy_conv_w[layer], row2(hy_conv_b[layer]))

        qm, km, vm, dq, dk, dv, uh = _in_proj(x, sh1, sc1, win[layer], gq, wq[layer], gkv, wkv[layer], tab_x, *conv, tmi_x)
        cqm, ckm, cvm, cdq, cdk, cdv, cuh = _in_proj(ctx, csh1, csc1, win[layer], gq, wq[layer], gkv, wkv[layer],
                                                     tab_c, *conv, tm_c)
        om = _mla_attention(qm, [ckm, km], [cvm, vm], tq_x, MLA_HEADS // 2)
        od = _diff_attention(dq, [cdk, dk], [cdv, dv], lam_p[layer], subln[layer], lam_init, tq_x, DIFF_HEADS)
        oh = _hyena(uh, hyc_x, *hy_args)
        vec = jnp.stack([b_out[layer], ln1_g[layer], ln1_b[layer], b_ff2[layer], ln2_g[layer], ln2_b[layer]])
        mix_w = (wo_m[layer], wo_d[layer], wo_h[layer], w1[layer], row2(b_ff1[layer]), w2[layer], vec)
        x = _mix_ffn(om, od, oh, x, mod[:, 0, 2:6], *mix_w, alpha, tm_x, ntf_x, tff)
        if need_ctx:
            com = _mla_attention(cqm, [ckm], [cvm], tq_c, MLA_HEADS // 2)
            cod = _diff_attention(cdq, [cdk], [cdv], lam_p[layer], subln[layer], lam_init, tq_c, DIFF_HEADS)
            coh = _hyena(cuh, hyc_c, *hy_args)
            ctx = _mix_ffn(com, cod, coh, ctx, modc[:, 0, 2:6], *mix_w, alpha, tm_c, 1, tff)
    return x
```

```python
import functools
import math

import jax
import numpy as np
import jax.numpy as jnp
from jax import lax
from jax.experimental import pallas as pl
from jax.experimental.pallas import tpu as pltpu

F32 = jnp.float32
BF16 = jnp.bfloat16
HI = lax.Precision.HIGHEST

GRID_W = 64
ROPE_BASE = 10000.0
LN_EPS = 1e-6
MLA_HEADS, MLA_NOPE, MLA_ROPE, MLA_V = 6, 64, 32, 64
MLA_Q_RANK, MLA_KV_RANK = 256, 128
DIFF_HEADS, DIFF_DIM = 4, 48
DIFF_V = 2 * DIFF_DIM
HY_CH, HY_ORDER, HY_BANDS = 256, 2, 16
HY_EMB = 1 + 2 * HY_BANDS
HY_TARGET, HY_FAST_DECAY_PCT, HY_SLOW_DECAY_PCT = 1e-2, 0.3, 1.5
HY_MIN_RATE = -math.log(HY_TARGET) / HY_SLOW_DECAY_PCT
HY_MAX_RATE = -math.log(HY_TARGET) / HY_FAST_DECAY_PCT
N_MOD = 6

LOG2E = 1.0 / math.log(2.0)
LANE = 128
SUBLANE = 8
VMEM_LIMIT = 56 * 1024 * 1024

IN_MLA = MLA_Q_RANK + MLA_KV_RANK + MLA_ROPE
N_DQ = DIFF_HEADS * 2 * DIFF_DIM
IN_DIFF = 2 * N_DQ + DIFF_HEADS * DIFF_V
W_QM = MLA_HEADS * LANE
W_VM = MLA_HEADS * MLA_V
W_D = DIFF_HEADS * LANE
W_HY = (HY_ORDER + 1) * HY_CH
C_CQ, C_CKV, C_KR = 0, MLA_Q_RANK, MLA_Q_RANK + MLA_KV_RANK
C_DQ = C_KR + LANE
C_DK = C_DQ + W_D
C_DV = C_DK + W_D
C_HY = C_DV + W_D
W_IN = C_HY + W_HY


def _cparams(sem):
    return pltpu.CompilerParams(dimension_semantics=sem, vmem_limit_bytes=VMEM_LIMIT)


def _ln(x):
    mu = jnp.mean(x, -1, keepdims=True)
    xc = x - mu
    var = jnp.mean(xc * xc, -1, keepdims=True)
    return xc * lax.rsqrt(var + LN_EPS)


def _bdot(a, b):
    return jnp.dot(a, b, preferred_element_type=F32)


def _mod_kernel(c_ref, w_ref, b_ref, o_ref):
    c = c_ref[...]
    s = c / (1.0 + jnp.exp(-c))
    o_ref[...] = jnp.dot(s, w_ref[...], preferred_element_type=F32, precision=HI) + b_ref[...]


def _modulation(cc, w_mod, b_mod):
    depth, d, n = w_mod.shape
    r = cc.shape[0]
    tn = 1024
    return pl.pallas_call(
        _mod_kernel,
        grid=(depth, n // tn),
        in_specs=[pl.BlockSpec((r, d), lambda l, j: (0, 0)),
                  pl.BlockSpec((None, d, tn), lambda l, j: (l, 0, j)),
                  pl.BlockSpec((None, 1, tn), lambda l, j: (l, 0, j))],
        out_specs=pl.BlockSpec((None, r, tn), lambda l, j: (l, 0, j)),
        out_shape=jax.ShapeDtypeStruct((depth, r, n), F32),
        compiler_params=_cparams(("arbitrary", "arbitrary")),
        name="modulation",
    )(cc, w_mod, b_mod.reshape(depth, 1, n))


def _rope(x, c, sa, sb, half):
    return x * c + pltpu.roll(x, LANE - half, 1) * sa + pltpu.roll(x, half, 1) * sb


def _inproj_kernel(x_ref, xp_ref, xn_ref, sh_ref, sc_ref, win_ref, gq_ref, wq_ref, gkv_ref, wkv_ref, tab_ref,
                   cw_ref, cb_ref, qm_ref, km_ref, vm_ref, dq_ref, dk_ref, dv_ref, hy_ref):
    x = jnp.concatenate([x_ref[0], xp_ref[0], xn_ref[0]], axis=0)
    h = _ln(x) * (1.0 + sc_ref[0]) + sh_ref[0]
    p_all = _bdot(h.astype(BF16), win_ref[...])
    tm = x_ref.shape[1]
    p = p_all[0:tm]
    tab = tab_ref[...]
    cm, sam, sbm = tab[:, 0:LANE], tab[:, LANE:2 * LANE], tab[:, 2 * LANE:3 * LANE]
    cd, sad, sbd = tab[:, 3 * LANE:4 * LANE], tab[:, 4 * LANE:5 * LANE], tab[:, 5 * LANE:6 * LANE]

    cq = p[:, C_CQ:C_CQ + MLA_Q_RANK]
    qn = cq * lax.rsqrt(jnp.mean(cq * cq, -1, keepdims=True) + LN_EPS) * gq_ref[...]
    q = _bdot(qn.astype(BF16), wq_ref[...])
    ckv = p[:, C_CKV:C_CKV + MLA_KV_RANK]
    kvn = ckv * lax.rsqrt(jnp.mean(ckv * ckv, -1, keepdims=True) + LN_EPS) * gkv_ref[...]
    kv = _bdot(kvn.astype(BF16), wkv_ref[...])
    kr = _rope(p[:, C_KR:C_KR + LANE], cm, sam, sbm, MLA_ROPE // 4)
    scale_m = LOG2E * (MLA_NOPE + MLA_ROPE) ** -0.5
    for hd in range(MLA_HEADS):
        sl = slice(hd * LANE, (hd + 1) * LANE)
        qm_ref[0, :, sl] = (_rope(q[:, sl], cm, sam, sbm, MLA_ROPE // 4) * scale_m).astype(BF16)
        km_ref[0, :, sl] = (kv[:, sl] + kr).astype(BF16)
    vm_ref[0] = kv[:, W_QM:W_QM + W_VM].astype(BF16)
    scale_d = LOG2E * DIFF_DIM ** -0.5
    for hd in range(DIFF_HEADS):
        sl = slice(hd * LANE, (hd + 1) * LANE)
        dq = p[:, C_DQ + hd * LANE:C_DQ + (hd + 1) * LANE]
        dk = p[:, C_DK + hd * LANE:C_DK + (hd + 1) * LANE]
        dq_ref[0, :, sl] = (_rope(dq, cd, sad, sbd, DIFF_DIM // 4) * scale_d).astype(BF16)
        dk_ref[0, :, sl] = _rope(dk, cd, sad, sbd, DIFF_DIM // 4).astype(BF16)
    dv_ref[0] = p[:, C_DV:C_DV + W_D].astype(BF16)
    u = p[:, C_HY:C_HY + W_HY]
    i = pl.program_id(0)
    u_prev = jnp.where(i == 0, 0.0, p_all[tm + SUBLANE - 1:tm + SUBLANE, C_HY:C_HY + W_HY])
    u_next = jnp.where(i == pl.num_programs(0) - 1, 0.0, p_all[tm + SUBLANE:tm + SUBLANE + 1, C_HY:C_HY + W_HY])
    row = lax.broadcasted_iota(jnp.int32, u.shape, 0)
    up = jnp.where(row == 0, u_prev, pltpu.roll(u, 1, 0))
    un = jnp.where(row == tm - 1, u_next, pltpu.roll(u, tm - 1, 0))
    w = cw_ref[...]
    hy_ref[0] = (up * w[0:1] + u * w[1:2] + un * w[2:3] + cb_ref[...]).astype(BF16)


def _in_proj(x, sh, sc, win, gq, wq, gkv, wkv, tab, conv_w, conv_b, tm):
    b, t, d = x.shape
    hb = tm // SUBLANE
    halo = (1, SUBLANE, d)
    resident = pl.Buffered(1)
    per_batch = sh.shape[0] == b and b > 1
    mod_map = (lambda i, bb: (bb, 0, 0)) if per_batch else (lambda i, bb: (0, 0, 0))
    const = lambda i, bb: (0, 0)
    widths = (W_QM, W_QM, W_VM, W_D, W_D, W_D, W_HY)
    return pl.pallas_call(
        _inproj_kernel,
        grid=(t // tm, b),
        in_specs=[pl.BlockSpec((1, tm, d), lambda i, bb: (bb, i, 0)),
                  pl.BlockSpec(halo, lambda i, bb: (bb, jnp.maximum(i * hb - 1, 0), 0)),
                  pl.BlockSpec(halo, lambda i, bb: (bb, jnp.minimum((i + 1) * hb, t // SUBLANE - 1), 0)),
                  pl.BlockSpec((1, 1, d), mod_map),
                  pl.BlockSpec((1, 1, d), mod_map),
                  pl.BlockSpec(win.shape, const, pipeline_mode=resident),
                  pl.BlockSpec(gq.shape, const),
                  pl.BlockSpec(wq.shape, const, pipeline_mode=resident),
                  pl.BlockSpec(gkv.shape, const),
                  pl.BlockSpec(wkv.shape, const, pipeline_mode=resident),
                  pl.BlockSpec((tm, 6 * LANE), lambda i, bb: (i, 0)),
                  pl.BlockSpec(conv_w.shape, const), pl.BlockSpec(conv_b.shape, const)],
        out_specs=[pl.BlockSpec((1, tm, w), lambda i, bb: (bb, i, 0)) for w in widths],
        out_shape=[jax.ShapeDtypeStruct((b, t, w), BF16) for w in widths],
        compiler_params=_cparams(("arbitrary", "arbitrary")),
        name="in_proj",
    )(x, x, x, sh, sc, win, gq, wq, gkv, wkv, tab, conv_w, conv_b)


NEG_BIG = -1e30


def _lane_fold(x, op):
    r = x[:, 0:LANE]
    for i in range(1, x.shape[1] // LANE):
        r = op(r, x[:, i * LANE:(i + 1) * LANE])
    return r


def _score_pass(q, parts, ksl, s_refs):
    mx = jnp.full((q.shape[0], LANE), NEG_BIG, F32)
    for (k_ref, _, n, tk), s_ref in zip(parts, s_refs):
        for j in range(n // tk):
            k = k_ref[0, j * tk:(j + 1) * tk, ksl]
            s = lax.dot_general(q, k, (((1,), (1,)), ((), ())), preferred_element_type=F32)
            s_ref[j] = s
            mx = jnp.maximum(mx, _lane_fold(s, jnp.maximum))
    return jnp.max(mx, axis=-1, keepdims=True)


def _softmax_pv(q, parts, ksl, vsl, s_refs):
    m = _score_pass(q, parts, ksl, s_refs)
    ls = jnp.zeros((q.shape[0], LANE), F32)
    acc = jnp.zeros((q.shape[0], LANE), F32)
    for (_, v_ref, n, tk), s_ref in zip(parts, s_refs):
        for j in range(n // tk):
            p = jnp.exp2(s_ref[j] - m)
            ls = ls + _lane_fold(p, jnp.add)
            acc = acc + _bdot(p.astype(BF16), v_ref[0, j * tk:(j + 1) * tk, vsl])
    return acc * (1.0 / jnp.sum(ls, axis=-1, keepdims=True))


def _mla_attn_kernel(*refs, part_shapes, n_pairs):
    np_ = len(part_shapes)
    q_ref = refs[0]
    k_refs = refs[1:1 + np_]
    v_refs = refs[1 + np_:1 + 2 * np_]
    o_ref = refs[1 + 2 * np_]
    s_sets = (refs[2 + 2 * np_:2 + 3 * np_], refs[2 + 3 * np_:])
    parts = [(k_refs[i], v_refs[i]) + part_shapes[i] for i in range(np_)]
    for pr in range(n_pairs):
        outs = []
        for hh in range(2):
            hd = 2 * pr + hh
            sl = slice(hd * LANE, (hd + 1) * LANE)
            outs.append(_softmax_pv(q_ref[0, :, sl], parts, sl, slice(pr * LANE, (pr + 1) * LANE), s_sets[hh]))
        lane = lax.broadcasted_iota(jnp.int32, outs[0].shape, 1)
        o_ref[0, :, pr * LANE:(pr + 1) * LANE] = jnp.where(lane < MLA_V, outs[0], outs[1]).astype(BF16)


def _chunk(n):
    return n if n <= 512 else 512


def _mla_attention(q, ks, vs, tq, n_pairs):
    b, t, _ = q.shape
    part_shapes = tuple((k.shape[1], _chunk(k.shape[1])) for k in ks)
    in_specs = [pl.BlockSpec((1, tq, 2 * n_pairs * LANE), lambda bb, g, i: (bb, i, g))]
    in_specs += [pl.BlockSpec((1, k.shape[1], 2 * n_pairs * LANE), lambda bb, g, i: (bb, 0, g)) for k in ks]
    in_specs += [pl.BlockSpec((1, v.shape[1], n_pairs * LANE), lambda bb, g, i: (bb, 0, g)) for v in vs]
    scratch = [pltpu.VMEM((n // tk, tq, tk), F32) for n, tk in part_shapes] * 2
    return pl.pallas_call(
        functools.partial(_mla_attn_kernel, part_shapes=part_shapes, n_pairs=n_pairs),
        grid=(b, MLA_HEADS // (2 * n_pairs), t // tq),
        in_specs=in_specs,
        out_specs=pl.BlockSpec((1, tq, n_pairs * LANE), lambda bb, g, i: (bb, i, g)),
        out_shape=jax.ShapeDtypeStruct((b, t, W_VM), BF16),
        scratch_shapes=scratch,
        compiler_params=_cparams(("arbitrary", "arbitrary", "arbitrary")),
        name="mla_attention",
    )(q, *ks, *vs)


def _diff_attn_kernel(*refs, part_shapes, lam_init, n_heads):
    np_ = len(part_shapes)
    q_ref = refs[0]
    k_refs = refs[1:1 + np_]
    v_refs = refs[1 + np_:1 + 2 * np_]
    lam_ref, g_ref, o_ref = refs[1 + 2 * np_:4 + 2 * np_]
    s_sets = (refs[4 + 2 * np_:4 + 3 * np_], refs[4 + 3 * np_:])
    parts = [(k_refs[i], v_refs[i]) + part_shapes[i] for i in range(np_)]
    lp = lam_ref[...]
    lam = (jnp.exp(jnp.sum(lp[0:1] * lp[1:2], axis=-1, keepdims=True))
           - jnp.exp(jnp.sum(lp[2:3] * lp[3:4], axis=-1, keepdims=True)) + lam_init)
    for hd in range(n_heads):
        sl = slice(hd * LANE, (hd + 1) * LANE)
        q = q_ref[0, :, sl]
        lane = lax.broadcasted_iota(jnp.int32, q.shape, 1)
        zero = jnp.zeros_like(q)
        o1 = _softmax_pv(jnp.where(lane < LANE // 2, q, zero), parts, sl, sl, s_sets[0])
        o2 = _softmax_pv(jnp.where(lane >= LANE // 2, q, zero), parts, sl, sl, s_sets[1])
        o = o1 - lam * o2
        ms = jnp.sum(o * o, axis=-1, keepdims=True) * (1.0 / DIFF_V)
        o_ref[0, :, sl] = (o * lax.rsqrt(ms + LN_EPS) * g_ref[...] * (1.0 - lam_init)).astype(BF16)


def _diff_attention(q, ks, vs, lam_p, subln, lam_init, tq, n_heads):
    b, t, _ = q.shape
    part_shapes = tuple((k.shape[1], _chunk(k.shape[1])) for k in ks)
    hmap = lambda bb, g, i: (bb, 0, g)
    in_specs = [pl.BlockSpec((1, tq, n_heads * LANE), lambda bb, g, i: (bb, i, g))]
    in_specs += [pl.BlockSpec((1, k.shape[1], n_heads * LANE), hmap) for k in ks]
    in_specs += [pl.BlockSpec((1, v.shape[1], n_heads * LANE), hmap) for v in vs]
    in_specs += [pl.BlockSpec(lam_p.shape, lambda bb, g, i: (0, 0)),
                 pl.BlockSpec(subln.shape, lambda bb, g, i: (0, 0))]
    scratch = [pltpu.VMEM((n // tk, tq, tk), F32) for n, tk in part_shapes] * 2
    return pl.pallas_call(
        functools.partial(_diff_attn_kernel, part_shapes=part_shapes, lam_init=lam_init, n_heads=n_heads),
        grid=(b, DIFF_HEADS // n_heads, t // tq),
        in_specs=in_specs,
        out_specs=pl.BlockSpec((1, tq, n_heads * LANE), lambda bb, g, i: (bb, i, g)),
        out_shape=jax.ShapeDtypeStruct((b, t, W_D), BF16),
        scratch_shapes=scratch,
        compiler_params=_cparams(("arbitrary", "arbitrary", "arbitrary")),
        name="diff_attention",
    )(q, *ks, *vs, lam_p, subln)


def _hyfilt_kernel(emb_ref, win_ref, fw1_ref, fb1_ref, fw2_ref, fb2_ref, fwf_ref, fbf_ref, fwb_ref, fbb_ref,
                   d_ref, h_scr):
    @pl.when(pl.program_id(0) == 0)
    def _():
        h1 = jnp.sin(jnp.dot(emb_ref[...], fw1_ref[...], preferred_element_type=F32, precision=HI) + fb1_ref[...])
        h_scr[...] = jnp.sin(jnp.dot(h1, fw2_ref[...], preferred_element_type=F32, precision=HI) + fb2_ref[...])

    h = h_scr[...]
    w = win_ref[...]
    fwd = (jnp.dot(h, fwf_ref[...], preferred_element_type=F32, precision=HI) + fbf_ref[...]) * w
    bwd = (jnp.dot(h, fwb_ref[...], preferred_element_type=F32, precision=HI) + fbb_ref[...]) * w
    row = lax.broadcasted_iota(jnp.int32, bwd.shape, 0)
    bwd = jnp.where(row == 0, 0.0, bwd)
    norm = jnp.sum(jnp.abs(fwd), axis=0, keepdims=True) + jnp.sum(jnp.abs(bwd), axis=0, keepdims=True)
    inv = 1.0 / norm
    d_ref[0] = ((fwd + bwd) * inv).astype(BF16)
    d_ref[1] = ((fwd - bwd) * inv).astype(BF16)


def _hy_filters(emb, win, fw1, fb1, fw2, fb2, fw3, fb3):
    l = emb.shape[0]
    hid = fw2.shape[0]
    nblk = HY_CH // LANE
    const = lambda g: (0, 0)
    fcol = lambda g: (0, (g // nblk) * 2 * nblk + g % nblk)
    bcol = lambda g: (0, (g // nblk) * 2 * nblk + nblk + g % nblk)
    return pl.pallas_call(
        _hyfilt_kernel,
        grid=(HY_ORDER * nblk,),
        in_specs=[pl.BlockSpec(emb.shape, const),
                  pl.BlockSpec((l, LANE), lambda g: (0, g % nblk)),
                  pl.BlockSpec(fw1.shape, const), pl.BlockSpec(fb1.shape, const),
                  pl.BlockSpec(fw2.shape, const), pl.BlockSpec(fb2.shape, const),
                  pl.BlockSpec((hid, LANE), fcol), pl.BlockSpec((1, LANE), fcol),
                  pl.BlockSpec((hid, LANE), bcol), pl.BlockSpec((1, LANE), bcol)],
        out_specs=pl.BlockSpec((2, None, l, LANE), lambda g: (0, g // nblk, 0, g % nblk)),
        out_shape=jax.ShapeDtypeStruct((2, HY_ORDER, l, HY_CH), BF16),
        scratch_shapes=[pltpu.VMEM((l, hid), F32)],
        compiler_params=_cparams(("arbitrary",)),
        name="hyena_filters",
    )(emb, win, fw1, fb1, fw2, fb2, fw3, fb3, fw3, fb3)


def _hyspec_kernel(f_ref, s_ref, d_ref, o_ref):
    tf = f_ref.shape[0] // 2
    o_ref[0:tf, :] = _bdot(f_ref[0:tf, :], s_ref[...])
    o_ref[tf:2 * tf, :] = _bdot(f_ref[tf:2 * tf, :], d_ref[...])


def _hy_spectrum(fmat, filt, tf):
    rows, l = fmat.shape
    return pl.pallas_call(
        _hyspec_kernel,
        grid=(rows // (2 * tf), HY_ORDER),
        in_specs=[pl.BlockSpec((2 * tf, l), lambda j, o: (j, 0)),
                  pl.BlockSpec((None, None, l, HY_CH), lambda j, o: (0, o, 0, 0)),
                  pl.BlockSpec((None, None, l, HY_CH), lambda j, o: (1, o, 0, 0))],
        out_specs=pl.BlockSpec((2 * tf, HY_CH), lambda j, o: (j, o)),
        out_shape=jax.ShapeDtypeStruct((rows, HY_ORDER * HY_CH), F32),
        compiler_params=_cparams(("arbitrary", "arbitrary")),
        name="hyena_filter_spectrum",
    )(fmat, filt, filt)


def _hyfwd_kernel(f_ref, z_ref, k_ref, y_ref):
    tf = f_ref.shape[0] // 2
    s = _bdot(f_ref[...], z_ref[0])
    sr, si = s[0:tf], s[tf:2 * tf]
    kr, ki = k_ref[0:tf, :], k_ref[tf:2 * tf, :]
    y_ref[0, 0:tf, :] = (sr * kr - si * ki).astype(BF16)
    y_ref[0, tf:2 * tf, :] = (sr * ki + si * kr).astype(BF16)


def _hy_forward(fmat, z, kspec, order, tf):
    rows, l = fmat.shape
    b = z.shape[0]
    return pl.pallas_call(
        _hyfwd_kernel,
        grid=(rows // (2 * tf), b),
        in_specs=[pl.BlockSpec((2 * tf, l), lambda j, bb: (j, 0)),
                  pl.BlockSpec((1, l, HY_CH), lambda j, bb: (bb, 0, 0)),
                  pl.BlockSpec((2 * tf, HY_CH), lambda j, bb: (j, order))],
        out_specs=pl.BlockSpec((1, 2 * tf, HY_CH), lambda j, bb: (bb, j, 0)),
        out_shape=jax.ShapeDtypeStruct((b, rows, HY_CH), BF16),
        compiler_params=_cparams(("arbitrary", "arbitrary")),
        name="hyena_dft_forward",
    )(fmat, z, kspec)


def _hyinv_kernel(g_ref, y_ref, z_ref, x_ref, b_ref, o_ref):
    conv = _bdot(g_ref[...], y_ref[0])
    z = z_ref[0].astype(F32)
    o_ref[0] = (x_ref[0].astype(F32) * (conv + z * b_ref[...])).astype(BF16)


def _hy_inverse(gmat, y, z, gate, bias, tt):
    l, rows = gmat.shape
    b = y.shape[0]
    return pl.pallas_call(
        _hyinv_kernel,
        grid=(l // tt, b),
        in_specs=[pl.BlockSpec((tt, rows), lambda i, bb: (i, 0)),
                  pl.BlockSpec((1, rows, HY_CH), lambda i, bb: (bb, 0, 0)),
                  pl.BlockSpec((1, tt, HY_CH), lambda i, bb: (bb, i, 0)),
                  pl.BlockSpec((1, tt, HY_CH), lambda i, bb: (bb, i, 0)),
                  pl.BlockSpec((1, HY_CH), lambda i, bb: (0, 0))],
        out_specs=pl.BlockSpec((1, tt, HY_CH), lambda i, bb: (bb, i, 0)),
        out_shape=jax.ShapeDtypeStruct((b, l, HY_CH), BF16),
        compiler_params=_cparams(("arbitrary", "arbitrary")),
        name="hyena_dft_inverse",
    )(gmat, y, z, gate, bias)


def _dft_tables(l, tf):
    k = jnp.arange(l, dtype=jnp.int32)
    n = jnp.arange(l, dtype=jnp.int32)
    ph = ((2 * k + 1)[:, None] * n[None, :]) % (4 * l)
    ang = ph.astype(F32) * (2.0 * math.pi / (4 * l))
    c = jnp.cos(ang).reshape(l // tf, 1, tf, l)
    s = (-jnp.sin(ang)).reshape(l // tf, 1, tf, l)
    f = jnp.concatenate([c, s], axis=1).reshape(2 * l, l)
    return f.astype(BF16), (f.T * (1.0 / l)).astype(BF16)


def _hy_consts(l):
    t = jnp.arange(l, dtype=F32)
    bands = jnp.arange(1, HY_BANDS + 1, dtype=F32)
    ang = (2.0 * math.pi / l) * t[:, None] * bands[None, :]
    emb = jnp.concatenate([(t / l)[:, None], jnp.cos(ang), jnp.sin(ang)], -1)
    emb = jnp.pad(emb, ((0, 0), (0, LANE - HY_EMB)))
    rates = jnp.linspace(HY_MIN_RATE, HY_MAX_RATE, HY_CH, dtype=F32)
    win = jnp.exp(-(t / l)[:, None] * rates[None, :])
    return emb, win


HY_N1 = 16
HY_J = 16
HY_GROUPS = 8
HY_KG = 16


def _hy2_tables(l):
    n1h, jj = HY_N1, HY_J
    n2, k1n = l // n1h, 2 * n1h
    k2n = n2 // 2
    k1 = jnp.arange(k1n, dtype=jnp.int32)
    ph1 = ((2 * k1 + 1)[:, None] * jnp.arange(n1h, dtype=jnp.int32)[None, :]) % (4 * n1h)
    a1 = ph1.astype(F32) * (2.0 * math.pi / (4 * n1h))
    f1 = jnp.stack([jnp.cos(a1), -jnp.sin(a1)], axis=1)
    f1big = jnp.einsum('krn,ab->kranb', f1, jnp.eye(jj, dtype=F32)).reshape(k1n * 2 * jj, n1h * jj)
    kk = k1[:, None] + k1n * jnp.arange(k2n, dtype=jnp.int32)[None, :]
    ph2 = ((2 * kk + 1)[:, :, None] * jnp.arange(n2, dtype=jnp.int32)[None, None, :]) % (4 * l)
    a2 = ph2.astype(F32) * (2.0 * math.pi / (4 * l))
    mr, mi = jnp.cos(a2), -jnp.sin(a2)
    f2t = jnp.concatenate([jnp.concatenate([mr, -mi], axis=2), jnp.concatenate([mi, mr], axis=2)], axis=1)
    return (f1big.astype(BF16), (f1big.T * (1.0 / l)).astype(BF16), f2t.astype(BF16),
            jnp.swapaxes(f2t, 1, 2).astype(BF16))


def _hy2_s1_kernel(f_ref, z_ref, o_ref):
    n1h, _, c = z_ref.shape[1:]
    k1n = o_ref.shape[1]
    for g in range(HY_GROUPS):
        cols = slice(g * HY_J, (g + 1) * HY_J)
        z = z_ref[0, :, cols, :].reshape(n1h * HY_J, c)
        a = _bdot(f_ref[...], z)
        o_ref[0, :, :, cols, :] = a.astype(BF16).reshape(k1n, 2, HY_J, c)


def _hy2_stage1(f1big, z5):
    b, n1h, n2, c = z5.shape
    k1n = 2 * n1h
    w = HY_J * HY_GROUPS
    return pl.pallas_call(
        _hy2_s1_kernel,
        grid=(b, n2 // w),
        in_specs=[pl.BlockSpec(f1big.shape, lambda bb, g: (0, 0)),
                  pl.BlockSpec((1, n1h, w, c), lambda bb, g: (bb, 0, g, 0))],
        out_specs=pl.BlockSpec((1, k1n, 2, w, c), lambda bb, g: (bb, 0, 0, g, 0)),
        out_shape=jax.ShapeDtypeStruct((b, k1n, 2, n2, c), BF16),
        compiler_params=_cparams(("arbitrary", "arbitrary")),
        name="hyena_stage1",
    )(f1big, z5)


def _hy2_fspec_kernel(f_ref, s_ref, d_ref, o_ref):
    h = o_ref.shape[1] // 2
    for i in range(HY_KG):
        o_ref[i, 0:h, :] = _bdot(f_ref[i], s_ref[0, i])[0:h]
        o_ref[i, h:2 * h, :] = _bdot(f_ref[i], d_ref[0, i])[h:2 * h]


def _hy2_filter_spectrum(f2t, a5):
    nsig, k1n, _, n2, c = a5.shape
    a4 = a5.reshape(nsig, k1n, 2 * n2, c)
    return pl.pallas_call(
        _hy2_fspec_kernel,
        grid=(k1n // HY_KG, HY_ORDER),
        in_specs=[pl.BlockSpec((HY_KG, n2, 2 * n2), lambda g, o: (g, 0, 0)),
                  pl.BlockSpec((1, HY_KG, 2 * n2, c), lambda g, o: (o, g, 0, 0)),
                  pl.BlockSpec((1, HY_KG, 2 * n2, c), lambda g, o: (HY_ORDER + o, g, 0, 0))],
        out_specs=pl.BlockSpec((HY_KG, n2, c), lambda g, o: (g, 0, o)),
        out_shape=jax.ShapeDtypeStruct((k1n, n2, HY_ORDER * c), F32),
        compiler_params=_cparams(("arbitrary", "arbitrary")),
        name="hyena_filter_spectrum2",
    )(f2t, a4, a4)


def _hy2_conv_kernel(f1_ref, g1_ref, f2_ref, g2_ref, k_ref, z_ref, x_ref, b_ref, o_ref, a_scr):
    n1h, n2, c = z_ref.shape[1:]
    k1n = a_scr.shape[0]
    groups = [slice(g * HY_J, (g + 1) * HY_J) for g in range(n2 // HY_J)]
    for cols in groups:
        z = z_ref[0, :, cols, :].reshape(n1h * HY_J, c)
        a_scr[:, :, cols, :] = _bdot(f1_ref[...], z).astype(BF16).reshape(k1n, 2, HY_J, c)
    h = n2 // 2
    for i in range(k1n):
        y = _bdot(f2_ref[i], a_scr[i].reshape(2 * n2, c))
        yr, yi = y[:h], y[h:]
        kr, ki = k_ref[i, 0:h, :], k_ref[i, h:2 * h, :]
        p = jnp.concatenate([yr * kr - yi * ki, yr * ki + yi * kr], axis=0).astype(BF16)
        a_scr[i] = _bdot(g2_ref[i], p).astype(BF16).reshape(2, n2, c)
    for cols in groups:
        cc = a_scr[:, :, cols, :].reshape(k1n * 2 * HY_J, c)
        conv = _bdot(g1_ref[...], cc)
        z = z_ref[0, :, cols, :].reshape(n1h * HY_J, c).astype(F32)
        x = x_ref[0, :, cols, :].reshape(n1h * HY_J, c).astype(F32)
        o_ref[0, :, cols, :] = (x * (conv + z * b_ref[...])).astype(BF16).reshape(n1h, HY_J, c)


def _hy2_conv(tabs, kspec, order, zsrc, zwhich, vxx5, xwhich, bias):
    f1big, g1big, f2t, g2t = tabs
    b, n1h, n2, _ = zsrc.shape
    c = HY_CH
    k1n = 2 * n1h
    resident = pl.Buffered(1)
    res = lambda a: pl.BlockSpec(a.shape, lambda bb: (0,) * a.ndim, pipeline_mode=resident)
    col = lambda which: pl.BlockSpec((1, n1h, n2, c), lambda bb: (bb, 0, 0, which))
    return pl.pallas_call(
        _hy2_conv_kernel,
        grid=(b,),
        in_specs=[res(f1big), res(g1big), res(f2t), res(g2t),
                  pl.BlockSpec((k1n, n2, c), lambda bb: (0, 0, order), pipeline_mode=resident),
                  col(zwhich), col(xwhich), pl.BlockSpec((1, c), lambda bb: (0, 0))],
        out_specs=col(0),
        out_shape=jax.ShapeDtypeStruct((b, n1h, n2, c), BF16),
        scratch_shapes=[pltpu.VMEM((k1n, 2, n2, c), BF16)],
        compiler_params=_cparams(("arbitrary",)),
        name="hyena_conv",
    )(f1big, g1big, f2t, g2t, kspec, zsrc, vxx5, bias)


def _hyena(vxx, consts, fw1, fb1, fw2, fb2, fw3, fb3, hbias):
    emb, win, tabs = consts
    filt = _hy_filters(emb, win, fw1, fb1, fw2, fb2, fw3, fb3)
    b, l, _ = vxx.shape
    c = HY_CH
    if len(tabs) == 3:
        fmat, gmat, tf = tabs
        kspec = _hy_spectrum(fmat, filt, tf)
        v, x1, x2 = vxx[..., 0:c], vxx[..., c:2 * c], vxx[..., 2 * c:3 * c]
        z = _hy_inverse(gmat, _hy_forward(fmat, v, kspec, 0, tf), v, x1, hbias[0:1], tf)
        return _hy_inverse(gmat, _hy_forward(fmat, z, kspec, 1, tf), z, x2, hbias[1:2], tf)
    f1big, g1big, f2t, g2t = tabs
    n2 = l // HY_N1
    filt5 = filt.reshape(2 * HY_ORDER, HY_N1, n2, c)
    kspec = _hy2_filter_spectrum(f2t, _hy2_stage1(f1big, filt5))
    vxx5 = vxx.reshape(b, HY_N1, n2, 3 * c)
    z5 = _hy2_conv(tabs, kspec, 0, vxx5, 0, vxx5, 1, hbias[0:1])
    return _hy2_conv(tabs, kspec, 1, z5, 0, vxx5, 2, hbias[1:2]).reshape(b, l, c)


def _mix_ffn_kernel(om_ref, od_ref, oh_ref, x_ref, mod_ref, wm_ref, wd_ref, wh_ref, w1_ref, b1_ref, w2_ref,
                    vec_ref, o_ref, *, alpha, tm, tf):
    g1, sh2, sc2, g2 = (mod_ref[0, i:i + 1, :] for i in range(4))
    bo, lg1, lb1, b2, lg2, lb2 = (vec_ref[i:i + 1, :] for i in range(6))
    nt = x_ref.shape[1] // tm
    rows = [slice(t * tm, (t + 1) * tm) for t in range(nt)]
    xs, hs = [], []
    for r in rows:
        y = _bdot(om_ref[0, r, :], wm_ref[...]) + _bdot(od_ref[0, r, :], wd_ref[...]) + _bdot(oh_ref[0, r, :], wh_ref[...])
        x1 = _ln(alpha * x_ref[0, r, :] + g1 * (y + bo)) * lg1 + lb1
        xs.append(x1)
        hs.append((_ln(x1) * (1.0 + sc2) + sh2).astype(BF16))
    accs = [None] * nt
    for c in range(w1_ref.shape[1] // tf):
        cols = slice(c * tf, (c + 1) * tf)
        for t in range(nt):
            a = jnp.maximum(_bdot(hs[t], w1_ref[:, cols]) + b1_ref[:, cols], 0.0)
            y = _bdot((a * a).astype(BF16), w2_ref[cols, :])
            accs[t] = y if accs[t] is None else accs[t] + y
    for t in range(nt):
        z = alpha * xs[t] + g2 * (accs[t] + b2)
        o_ref[0, rows[t], :] = _ln(z) * lg2 + lb2


def _mix_ffn(om, od, oh, x, mod, wm, wd, wh, w1, b1, w2, vec, alpha, tm, nt, tf):
    b, t, d = x.shape
    per_batch = mod.shape[0] == b and b > 1
    gmap = (lambda bb, i: (bb, 0, 0)) if per_batch else (lambda bb, i: (0, 0, 0))
    const = lambda bb, i: (0, 0)
    row = lambda bb, i: (bb, i, 0)
    resident = pl.Buffered(1)
    blk = lambda a: pl.BlockSpec((1, tm * nt, a.shape[2]), row)
    res = lambda a: pl.BlockSpec(a.shape, const, pipeline_mode=resident)
    return pl.pallas_call(
        functools.partial(_mix_ffn_kernel, alpha=alpha, tm=tm, tf=tf),
        grid=(b, t // (tm * nt)),
        in_specs=[blk(om), blk(od), blk(oh), blk(x), pl.BlockSpec((1,) + mod.shape[1:], gmap),
                  res(wm), res(wd), res(wh), res(w1), pl.BlockSpec(b1.shape, const), res(w2),
                  pl.BlockSpec(vec.shape, const)],
        out_specs=blk(x),
        out_shape=jax.ShapeDtypeStruct((b, t, d), F32),
        compiler_params=_cparams(("arbitrary", "arbitrary")),
        name="mix_ffn",
    )(om, od, oh, x, mod, wm, wd, wh, w1, b1, w2, vec)


def _rope_lane_plan(starts, dims):
    h = dims // 2
    inv = ROPE_BASE ** (-(np.arange(h // 2, dtype=np.float64) * (2.0 / h)))
    fr, fc, lo, hi = (np.zeros(LANE, np.float32) for _ in range(4))
    for st in starts:
        for axis_i, f in enumerate((fr, fc)):
            base = st + axis_i * h
            f[base:base + h // 2] = inv
            f[base + h // 2:base + h] = inv
            lo[base:base + h // 2] = 1.0
            hi[base + h // 2:base + h] = 1.0
    return fr, fc, lo, hi


def _rope_tables(t_len, roped):
    pos_t = jnp.arange(t_len)
    scale = 1.0 if roped else 0.0
    row = (pos_t // GRID_W).astype(F32)[:, None] * scale
    col = (pos_t % GRID_W).astype(F32)[:, None] * scale
    out = []
    for starts, dims in (([MLA_NOPE], MLA_ROPE), ([0, LANE // 2], DIFF_DIM)):
        fr, fc, lo, hi = _rope_lane_plan(starts, dims)
        ang = row * fr[None, :] + col * fc[None, :]
        sin = jnp.sin(ang)
        out += [jnp.cos(ang), -sin * lo[None, :], sin * hi[None, :]]
    return jnp.concatenate(out, axis=1)


def _prep_weights(w_in, w_q_up, w_kv_up, w_out, diff_lambda, diff_subln):
    depth, d, _ = w_in.shape
    s1, s2 = IN_MLA, IN_MLA + IN_DIFF
    pad_last = lambda a, n: jnp.pad(a, [(0, 0)] * (a.ndim - 1) + [(0, n - a.shape[-1])])
    kr = jnp.pad(w_in[..., C_KR:s1], ((0, 0), (0, 0), (MLA_NOPE, LANE - MLA_NOPE - MLA_ROPE)))
    dq = pad_last(w_in[..., s1:s1 + N_DQ].reshape(depth, d, DIFF_HEADS, 2, DIFF_DIM), LANE // 2)
    dk = pad_last(w_in[..., s1 + N_DQ:s1 + 2 * N_DQ].reshape(depth, d, DIFF_HEADS, 2, DIFF_DIM), LANE // 2)
    dv = pad_last(w_in[..., s1 + 2 * N_DQ:s2].reshape(depth, d, DIFF_HEADS, DIFF_V), LANE)
    win = jnp.concatenate([w_in[..., :C_KR], kr, dq.reshape(depth, d, W_D), dk.reshape(depth, d, W_D),
                           dv.reshape(depth, d, W_D), w_in[..., s2:]], axis=-1).astype(BF16)
    wq = pad_last(w_q_up.reshape(depth, MLA_Q_RANK, MLA_HEADS, MLA_NOPE + MLA_ROPE), LANE)
    wq = wq.reshape(depth, MLA_Q_RANK, W_QM).astype(BF16)
    kv = w_kv_up.reshape(depth, MLA_KV_RANK, MLA_HEADS, MLA_NOPE + MLA_V)
    wk = pad_last(kv[..., :MLA_NOPE], LANE).reshape(depth, MLA_KV_RANK, W_QM)
    wv = kv[..., MLA_NOPE:].reshape(depth, MLA_KV_RANK, W_VM)
    wkv = jnp.concatenate([wk, wv], axis=-1).astype(BF16)
    wo_m = w_out[:, :W_VM].astype(BF16)
    wo_d = w_out[:, W_VM:W_VM + DIFF_HEADS * DIFF_V].reshape(depth, DIFF_HEADS, DIFF_V, -1)
    wo_d = jnp.pad(wo_d, ((0, 0), (0, 0), (0, LANE - DIFF_V), (0, 0))).reshape(depth, W_D, -1).astype(BF16)
    wo_h = w_out[:, W_VM + DIFF_HEADS * DIFF_V:].astype(BF16)
    lam = pad_last(diff_lambda.astype(F32), LANE)
    subln = pad_last(diff_subln.astype(F32), LANE)[:, None, :]
    return win, wq, wkv, wo_m, wo_d, wo_h, lam, subln


def kernel(x, c, ctx, c_ctx, w_mod, b_mod, w_in, mla_q_norm, w_q_up, mla_kv_norm, w_kv_up, diff_lambda, diff_subln, hy_conv_w, hy_conv_b, hy_fw1, hy_fb1, hy_fw2, hy_fb2, hy_fw3, hy_fb3, hy_bias, w_out, b_out, ln1_g, ln1_b, w_ff1, b_ff1, w_ff2, b_ff2, ln2_g, ln2_b):
    bsz, seq, d = x.shape
    n_ctx = ctx.shape[1]
    depth = w_in.shape[0]
    alpha = (2.0 * depth) ** 0.25

    rpad = -(bsz + 1) % SUBLANE
    cc = jnp.concatenate([c, c_ctx[None, :], jnp.zeros((rpad, d), F32)], axis=0)
    mod_all = _modulation(cc, w_mod, b_mod)

    win, wq, wkv, wo_m, wo_d, wo_h, lam_p, subln = _prep_weights(w_in, w_q_up, w_kv_up, w_out, diff_lambda, diff_subln)
    w1 = w_ff1.astype(BF16)
    w2 = w_ff2.astype(BF16)
    fw1 = jnp.pad(hy_fw1, ((0, 0), (0, LANE - HY_EMB), (0, 0)))
    tab_x = _rope_tables(seq, True)
    tab_c = _rope_tables(n_ctx, False)
    def hy_tables(l):
        if l >= 1024 and l % (HY_N1 * HY_J * HY_GROUPS) == 0:
            return _hy2_tables(l)
        return _dft_tables(l, min(512, l)) + (min(512, l),)

    hyc_x = _hy_consts(seq) + (hy_tables(seq),)
    hyc_c = _hy_consts(n_ctx) + (hy_tables(n_ctx),)
    tm_x, tm_c = min(512, seq), min(512, n_ctx)
    tmi_x = min(1024, seq)
    ntf_x = 2 if seq % (2 * tm_x) == 0 else 1
    tq_x, tq_c = min(256, seq), min(256, n_ctx)
    tff = min(1024, w1.shape[2])
    row2 = lambda a: a.reshape(1, -1)

    for layer in range(depth):
        need_ctx = layer < depth - 1
        lam_init = 0.8 - 0.6 * math.exp(-0.3 * layer)
        mod = mod_all[layer, :bsz].reshape(bsz, 1, N_MOD, d)
        modc = mod_all[layer, bsz:bsz + 1].reshape(1, 1, N_MOD, d)
        sh1, sc1 = mod[:, :, 0], mod[:, :, 1]
        csh1, csc1 = modc[:, :, 0], modc[:, :, 1]
        gq, gkv = row2(mla_q_norm[layer]), row2(mla_kv_norm[layer])
        hy_args = (fw1[layer], row2(hy_fb1[layer]), hy_fw2[layer], row2(hy_fb2[layer]), hy_fw3[layer],
                   row2(hy_fb3[layer]), hy_bias[layer])
        conv = (hy_conv_w[layer], row2(hy_conv_b[layer]))

        qm, km, vm, dq, dk, dv, uh = _in_proj(x, sh1, sc1, win[layer], gq, wq[layer], gkv, wkv[layer], tab_x, *conv, tmi_x)
        cqm, ckm, cvm, cdq, cdk, cdv, cuh = _in_proj(ctx, csh1, csc1, win[layer], gq, wq[layer], gkv, wkv[layer],
                                                     tab_c, *conv, tm_c)
        om = _mla_attention(qm, [ckm, km], [cvm, vm], tq_x, MLA_HEADS // 2)
        od = _diff_attention(dq, [cdk, dk], [cdv, dv], lam_p[layer], subln[layer], lam_init, tq_x, DIFF_HEADS)
        oh = _hyena(uh, hyc_x, *hy_args)
        vec = jnp.stack([b_out[layer], ln1_g[layer], ln1_b[layer], b_ff2[layer], ln2_g[layer], ln2_b[layer]])
        mix_w = (wo_m[layer], wo_d[layer], wo_h[layer], w1[layer], row2(b_ff1[layer]), w2[layer], vec)
        x = _mix_ffn(om, od, oh, x, mod[:, 0, 2:6], *mix_w, alpha, tm_x, ntf_x, tff)
        if need_ctx:
            com = _mla_attention(cqm, [ckm], [cvm], tq_c, MLA_HEADS // 2)
            cod = _diff_attention(cdq, [cdk], [cdv], lam_p[layer], subln[layer], lam_init, tq_c, DIFF_HEADS)
            coh = _hyena(cuh, hyc_c, *hy_args)
            ctx = _mix_ffn(com, cod, coh, ctx, modc[:, 0, 2:6], *mix_w, alpha, tm_c, 1, tff)
    return x
```

```python
import functools
import math

import jax
import numpy as np
import jax.numpy as jnp
from jax import lax
from jax.experimental import pallas as pl
from jax.experimental.pallas import tpu as pltpu

F32 = jnp.float32
BF16 = jnp.bfloat16
HI = lax.Precision.HIGHEST

GRID_W = 64
ROPE_BASE = 10000.0
LN_EPS = 1e-6
MLA_HEADS, MLA_NOPE, MLA_ROPE, MLA_V = 6, 64, 32, 64
MLA_Q_RANK, MLA_KV_RANK = 256, 128
DIFF_HEADS, DIFF_DIM = 4, 48
DIFF_V = 2 * DIFF_DIM
HY_CH, HY_ORDER, HY_BANDS = 256, 2, 16
HY_EMB = 1 + 2 * HY_BANDS
HY_TARGET, HY_FAST_DECAY_PCT, HY_SLOW_DECAY_PCT = 1e-2, 0.3, 1.5
HY_MIN_RATE = -math.log(HY_TARGET) / HY_SLOW_DECAY_PCT
HY_MAX_RATE = -math.log(HY_TARGET) / HY_FAST_DECAY_PCT
N_MOD = 6

LOG2E = 1.0 / math.log(2.0)
LANE = 128
SUBLANE = 8
VMEM_LIMIT = 56 * 1024 * 1024

IN_MLA = MLA_Q_RANK + MLA_KV_RANK + MLA_ROPE
N_DQ = DIFF_HEADS * 2 * DIFF_DIM
IN_DIFF = 2 * N_DQ + DIFF_HEADS * DIFF_V
W_QM = MLA_HEADS * LANE
W_VM = MLA_HEADS * MLA_V
W_D = DIFF_HEADS * LANE
W_HY = (HY_ORDER + 1) * HY_CH
C_CQ, C_CKV, C_KR = 0, MLA_Q_RANK, MLA_Q_RANK + MLA_KV_RANK
C_DQ = C_KR + LANE
C_DK = C_DQ + W_D
C_DV = C_DK + W_D
C_HY = C_DV + W_D
W_IN = C_HY + W_HY


def _cparams(sem):
    return pltpu.CompilerParams(dimension_semantics=sem, vmem_limit_bytes=VMEM_LIMIT)


def _ln(x):
    mu = jnp.mean(x, -1, keepdims=True)
    xc = x - mu
    var = jnp.mean(xc * xc, -1, keepdims=True)
    return xc * lax.rsqrt(var + LN_EPS)


def _bdot(a, b):
    return jnp.dot(a, b, preferred_element_type=F32)


def _mod_kernel(c_ref, w_ref, b_ref, o_ref):
    c = c_ref[...]
    s = c / (1.0 + jnp.exp(-c))
    o_ref[...] = jnp.dot(s, w_ref[...], preferred_element_type=F32, precision=HI) + b_ref[...]


def _modulation(cc, w_mod, b_mod):
    depth, d, n = w_mod.shape
    r = cc.shape[0]
    tn = 1024
    return pl.pallas_call(
        _mod_kernel,
        grid=(depth, n // tn),
        in_specs=[pl.BlockSpec((r, d), lambda l, j: (0, 0)),
                  pl.BlockSpec((None, d, tn), lambda l, j: (l, 0, j)),
                  pl.BlockSpec((None, 1, tn), lambda l, j: (l, 0, j))],
        out_specs=pl.BlockSpec((None, r, tn), lambda l, j: (l, 0, j)),
        out_shape=jax.ShapeDtypeStruct((depth, r, n), F32),
        compiler_params=_cparams(("arbitrary", "arbitrary")),
        name="modulation",
    )(cc, w_mod, b_mod.reshape(depth, 1, n))


def _rope(x, c, sa, sb, half):
    return x * c + pltpu.roll(x, LANE - half, 1) * sa + pltpu.roll(x, half, 1) * sb


def _inproj_kernel(x_ref, xp_ref, xn_ref, sh_ref, sc_ref, win_ref, gq_ref, wq_ref, gkv_ref, wkv_ref, tab_ref,
                   cw_ref, cb_ref, qm_ref, km_ref, vm_ref, dq_ref, dk_ref, dv_ref, hy_ref):
    x = jnp.concatenate([x_ref[0], xp_ref[0], xn_ref[0]], axis=0)
    h = _ln(x) * (1.0 + sc_ref[0]) + sh_ref[0]
    p_all = _bdot(h.astype(BF16), win_ref[...])
    tm = x_ref.shape[1]
    p = p_all[0:tm]
    tab = tab_ref[...]
    cm, sam, sbm = tab[:, 0:LANE], tab[:, LANE:2 * LANE], tab[:, 2 * LANE:3 * LANE]
    cd, sad, sbd = tab[:, 3 * LANE:4 * LANE], tab[:, 4 * LANE:5 * LANE], tab[:, 5 * LANE:6 * LANE]

    cq = p[:, C_CQ:C_CQ + MLA_Q_RANK]
    qn = cq * lax.rsqrt(jnp.mean(cq * cq, -1, keepdims=True) + LN_EPS) * gq_ref[...]
    q = _bdot(qn.astype(BF16), wq_ref[...])
    ckv = p[:, C_CKV:C_CKV + MLA_KV_RANK]
    kvn = ckv * lax.rsqrt(jnp.mean(ckv * ckv, -1, keepdims=True) + LN_EPS) * gkv_ref[...]
    kv = _bdot(kvn.astype(BF16), wkv_ref[...])
    kr = _rope(p[:, C_KR:C_KR + LANE], cm, sam, sbm, MLA_ROPE // 4)
    scale_m = LOG2E * (MLA_NOPE + MLA_ROPE) ** -0.5
    for hd in range(MLA_HEADS):
        sl = slice(hd * LANE, (hd + 1) * LANE)
        qm_ref[0, :, sl] = (_rope(q[:, sl], cm, sam, sbm, MLA_ROPE // 4) * scale_m).astype(BF16)
        km_ref[0, :, sl] = (kv[:, sl] + kr).astype(BF16)
    vm_ref[0] = kv[:, W_QM:W_QM + W_VM].astype(BF16)
    scale_d = LOG2E * DIFF_DIM ** -0.5
    for hd in range(DIFF_HEADS):
        sl = slice(hd * LANE, (hd + 1) * LANE)
        dq = p[:, C_DQ + hd * LANE:C_DQ + (hd + 1) * LANE]
        dk = p[:, C_DK + hd * LANE:C_DK + (hd + 1) * LANE]
        dq_ref[0, :, sl] = (_rope(dq, cd, sad, sbd, DIFF_DIM // 4) * scale_d).astype(BF16)
        dk_ref[0, :, sl] = _rope(dk, cd, sad, sbd, DIFF_DIM // 4).astype(BF16)
    dv_ref[0] = p[:, C_DV:C_DV + W_D].astype(BF16)
    u = p[:, C_HY:C_HY + W_HY]
    i = pl.program_id(0)
    u_prev = jnp.where(i == 0, 0.0, p_all[tm + SUBLANE - 1:tm + SUBLANE, C_HY:C_HY + W_HY])
    u_next = jnp.where(i == pl.num_programs(0) - 1, 0.0, p_all[tm + SUBLANE:tm + SUBLANE + 1, C_HY:C_HY + W_HY])
    row = lax.broadcasted_iota(jnp.int32, u.shape, 0)
    up = jnp.where(row == 0, u_prev, pltpu.roll(u, 1, 0))
    un = jnp.where(row == tm - 1, u_next, pltpu.roll(u, tm - 1, 0))
    w = cw_ref[...]
    hy_ref[0] = (up * w[0:1] + u * w[1:2] + un * w[2:3] + cb_ref[...]).astype(BF16)


def _in_proj(x, sh, sc, win, gq, wq, gkv, wkv, tab, conv_w, conv_b, tm):
    b, t, d = x.shape
    hb = tm // SUBLANE
    halo = (1, SUBLANE, d)
    resident = pl.Buffered(1)
    per_batch = sh.shape[0] == b and b > 1
    mod_map = (lambda i, bb: (bb, 0, 0)) if per_batch else (lambda i, bb: (0, 0, 0))
    const = lambda i, bb: (0, 0)
    widths = (W_QM, W_QM, W_VM, W_D, W_D, W_D, W_HY)
    return pl.pallas_call(
        _inproj_kernel,
        grid=(t // tm, b),
        in_specs=[pl.BlockSpec((1, tm, d), lambda i, bb: (bb, i, 0)),
                  pl.BlockSpec(halo, lambda i, bb: (bb, jnp.maximum(i * hb - 1, 0), 0)),
                  pl.BlockSpec(halo, lambda i, bb: (bb, jnp.minimum((i + 1) * hb, t // SUBLANE - 1), 0)),
                  pl.BlockSpec((1, 1, d), mod_map),
                  pl.BlockSpec((1, 1, d), mod_map),
                  pl.BlockSpec(win.shape, const, pipeline_mode=resident),
                  pl.BlockSpec(gq.shape, const),
                  pl.BlockSpec(wq.shape, const, pipeline_mode=resident),
                  pl.BlockSpec(gkv.shape, const),
                  pl.BlockSpec(wkv.shape, const, pipeline_mode=resident),
                  pl.BlockSpec((tm, 6 * LANE), lambda i, bb: (i, 0)),
                  pl.BlockSpec(conv_w.shape, const), pl.BlockSpec(conv_b.shape, const)],
        out_specs=[pl.BlockSpec((1, tm, w), lambda i, bb: (bb, i, 0)) for w in widths],
        out_shape=[jax.ShapeDtypeStruct((b, t, w), BF16) for w in widths],
        compiler_params=_cparams(("arbitrary", "arbitrary")),
        name="in_proj",
    )(x, x, x, sh, sc, win, gq, wq, gkv, wkv, tab, conv_w, conv_b)


NEG_BIG = -1e30


def _lane_fold(x, op):
    r = x[:, 0:LANE]
    for i in range(1, x.shape[1] // LANE):
        r = op(r, x[:, i * LANE:(i + 1) * LANE])
    return r


def _score_pass(q, parts, ksl, s_refs):
    mx = jnp.full((q.shape[0], LANE), NEG_BIG, F32)
    for (k_ref, _, n, tk), s_ref in zip(parts, s_refs):
        for j in range(n // tk):
            k = k_ref[0, j * tk:(j + 1) * tk, ksl]
            s = lax.dot_general(q, k, (((1,), (1,)), ((), ())), preferred_element_type=F32)
            s_ref[j] = s
            mx = jnp.maximum(mx, _lane_fold(s, jnp.maximum))
    return jnp.max(mx, axis=-1, keepdims=True)


def _softmax_pv(q, parts, ksl, vsl, s_refs):
    m = _score_pass(q, parts, ksl, s_refs)
    ls = jnp.zeros((q.shape[0], LANE), F32)
    acc = jnp.zeros((q.shape[0], LANE), F32)
    for (_, v_ref, n, tk), s_ref in zip(parts, s_refs):
        for j in range(n // tk):
            p = jnp.exp2(s_ref[j] - m)
            ls = ls + _lane_fold(p, jnp.add)
            acc = acc + _bdot(p.astype(BF16), v_ref[0, j * tk:(j + 1) * tk, vsl])
    return acc * (1.0 / jnp.sum(ls, axis=-1, keepdims=True))


def _mla_attn_kernel(*refs, part_shapes, n_pairs):
    np_ = len(part_shapes)
    q_ref = refs[0]
    k_refs = refs[1:1 + np_]
    v_refs = refs[1 + np_:1 + 2 * np_]
    o_ref = refs[1 + 2 * np_]
    s_sets = (refs[2 + 2 * np_:2 + 3 * np_], refs[2 + 3 * np_:])
    parts = [(k_refs[i], v_refs[i]) + part_shapes[i] for i in range(np_)]
    for pr in range(n_pairs):
        outs = []
        for hh in range(2):
            hd = 2 * pr + hh
            sl = slice(hd * LANE, (hd + 1) * LANE)
            outs.append(_softmax_pv(q_ref[0, :, sl], parts, sl, slice(pr * LANE, (pr + 1) * LANE), s_sets[hh]))
        lane = lax.broadcasted_iota(jnp.int32, outs[0].shape, 1)
        o_ref[0, :, pr * LANE:(pr + 1) * LANE] = jnp.where(lane < MLA_V, outs[0], outs[1]).astype(BF16)


def _chunk(n):
    return n if n <= 512 else 512


def _mla_attention(q, ks, vs, tq, n_pairs):
    b, t, _ = q.shape
    part_shapes = tuple((k.shape[1], _chunk(k.shape[1])) for k in ks)
    in_specs = [pl.BlockSpec((1, tq, 2 * n_pairs * LANE), lambda bb, g, i: (bb, i, g))]
    in_specs += [pl.BlockSpec((1, k.shape[1], 2 * n_pairs * LANE), lambda bb, g, i: (bb, 0, g)) for k in ks]
    in_specs += [pl.BlockSpec((1, v.shape[1], n_pairs * LANE), lambda bb, g, i: (bb, 0, g)) for v in vs]
    scratch = [pltpu.VMEM((n // tk, tq, tk), F32) for n, tk in part_shapes] * 2
    return pl.pallas_call(
        functools.partial(_mla_attn_kernel, part_shapes=part_shapes, n_pairs=n_pairs),
        grid=(b, MLA_HEADS // (2 * n_pairs), t // tq),
        in_specs=in_specs,
        out_specs=pl.BlockSpec((1, tq, n_pairs * LANE), lambda bb, g, i: (bb, i, g)),
        out_shape=jax.ShapeDtypeStruct((b, t, W_VM), BF16),
        scratch_shapes=scratch,
        compiler_params=_cparams(("arbitrary", "arbitrary", "arbitrary")),
        name="mla_attention",
    )(q, *ks, *vs)


def _diff_attn_kernel(*refs, part_shapes, lam_init, n_heads):
    np_ = len(part_shapes)
    q_ref = refs[0]
    k_refs = refs[1:1 + np_]
    v_refs = refs[1 + np_:1 + 2 * np_]
    lam_ref, g_ref, o_ref = refs[1 + 2 * np_:4 + 2 * np_]
    s_sets = (refs[4 + 2 * np_:4 + 3 * np_], refs[4 + 3 * np_:])
    parts = [(k_refs[i], v_refs[i]) + part_shapes[i] for i in range(np_)]
    lp = lam_ref[...]
    lam = (jnp.exp(jnp.sum(lp[0:1] * lp[1:2], axis=-1, keepdims=True))
           - jnp.exp(jnp.sum(lp[2:3] * lp[3:4], axis=-1, keepdims=True)) + lam_init)
    for hd in range(n_heads):
        sl = slice(hd * LANE, (hd + 1) * LANE)
        q = q_ref[0, :, sl]
        lane = lax.broadcasted_iota(jnp.int32, q.shape, 1)
        zero = jnp.zeros_like(q)
        o1 = _softmax_pv(jnp.where(lane < LANE // 2, q, zero), parts, sl, sl, s_sets[0])
        o2 = _softmax_pv(jnp.where(lane >= LANE // 2, q, zero), parts, sl, sl, s_sets[1])
        o = o1 - lam * o2
        ms = jnp.sum(o * o, axis=-1, keepdims=True) * (1.0 / DIFF_V)
        o_ref[0, :, sl] = (o * lax.rsqrt(ms + LN_EPS) * g_ref[...] * (1.0 - lam_init)).astype(BF16)


def _diff_attention(q, ks, vs, lam_p, subln, lam_init, tq, n_heads):
    b, t, _ = q.shape
    part_shapes = tuple((k.shape[1], _chunk(k.shape[1])) for k in ks)
    hmap = lambda bb, g, i: (bb, 0, g)
    in_specs = [pl.BlockSpec((1, tq, n_heads * LANE), lambda bb, g, i: (bb, i, g))]
    in_specs += [pl.BlockSpec((1, k.shape[1], n_heads * LANE), hmap) for k in ks]
    in_specs += [pl.BlockSpec((1, v.shape[1], n_heads * LANE), hmap) for v in vs]
    in_specs += [pl.BlockSpec(lam_p.shape, lambda bb, g, i: (0, 0)),
                 pl.BlockSpec(subln.shape, lambda bb, g, i: (0, 0))]
    scratch = [pltpu.VMEM((n // tk, tq, tk), F32) for n, tk in part_shapes] * 2
    return pl.pallas_call(
        functools.partial(_diff_attn_kernel, part_shapes=part_shapes, lam_init=lam_init, n_heads=n_heads),
        grid=(b, DIFF_HEADS // n_heads, t // tq),
        in_specs=in_specs,
        out_specs=pl.BlockSpec((1, tq, n_heads * LANE), lambda bb, g, i: (bb, i, g)),
        out_shape=jax.ShapeDtypeStruct((b, t, W_D), BF16),
        scratch_shapes=scratch,
        compiler_params=_cparams(("arbitrary", "arbitrary", "arbitrary")),
        name="diff_attention",
    )(q, *ks, *vs, lam_p, subln)


def _hyfilt_kernel(emb_ref, win_ref, fw1_ref, fb1_ref, fw2_ref, fb2_ref, fwf_ref, fbf_ref, fwb_ref, fbb_ref,
                   d_ref, h_scr):
    @pl.when(pl.program_id(0) == 0)
    def _():
        h1 = jnp.sin(jnp.dot(emb_ref[...], fw1_ref[...], preferred_element_type=F32, precision=HI) + fb1_ref[...])
        h_scr[...] = jnp.sin(jnp.dot(h1, fw2_ref[...], preferred_element_type=F32, precision=HI) + fb2_ref[...])

    h = h_scr[...]
    w = win_ref[...]
    fwd = (jnp.dot(h, fwf_ref[...], preferred_element_type=F32, precision=HI) + fbf_ref[...]) * w
    bwd = (jnp.dot(h, fwb_ref[...], preferred_element_type=F32, precision=HI) + fbb_ref[...]) * w
    row = lax.broadcasted_iota(jnp.int32, bwd.shape, 0)
    bwd = jnp.where(row == 0, 0.0, bwd)
    norm = jnp.sum(jnp.abs(fwd), axis=0, keepdims=True) + jnp.sum(jnp.abs(bwd), axis=0, keepdims=True)
    inv = 1.0 / norm
    d_ref[0] = ((fwd + bwd) * inv).astype(BF16)
    d_ref[1] = ((fwd - bwd) * inv).astype(BF16)


def _hy_filters(emb, win, fw1, fb1, fw2, fb2, fw3, fb3):
    l = emb.shape[0]
    hid = fw2.shape[0]
    nblk = HY_CH // LANE
    const = lambda g: (0, 0)
    fcol = lambda g: (0, (g // nblk) * 2 * nblk + g % nblk)
    bcol = lambda g: (0, (g // nblk) * 2 * nblk + nblk + g % nblk)
    return pl.pallas_call(
        _hyfilt_kernel,
        grid=(HY_ORDER * nblk,),
        in_specs=[pl.BlockSpec(emb.shape, const),
                  pl.BlockSpec((l, LANE), lambda g: (0, g % nblk)),
                  pl.BlockSpec(fw1.shape, const), pl.BlockSpec(fb1.shape, const),
                  pl.BlockSpec(fw2.shape, const), pl.BlockSpec(fb2.shape, const),
                  pl.BlockSpec((hid, LANE), fcol), pl.BlockSpec((1, LANE), fcol),
                  pl.BlockSpec((hid, LANE), bcol), pl.BlockSpec((1, LANE), bcol)],
        out_specs=pl.BlockSpec((2, None, l, LANE), lambda g: (0, g // nblk, 0, g % nblk)),
        out_shape=jax.ShapeDtypeStruct((2, HY_ORDER, l, HY_CH), BF16),
        scratch_shapes=[pltpu.VMEM((l, hid), F32)],
        compiler_params=_cparams(("arbitrary",)),
        name="hyena_filters",
    )(emb, win, fw1, fb1, fw2, fb2, fw3, fb3, fw3, fb3)


def _hyspec_kernel(f_ref, s_ref, d_ref, o_ref):
    tf = f_ref.shape[0] // 2
    o_ref[0:tf, :] = _bdot(f_ref[0:tf, :], s_ref[...])
    o_ref[tf:2 * tf, :] = _bdot(f_ref[tf:2 * tf, :], d_ref[...])


def _hy_spectrum(fmat, filt, tf):
    rows, l = fmat.shape
    return pl.pallas_call(
        _hyspec_kernel,
        grid=(rows // (2 * tf), HY_ORDER),
        in_specs=[pl.BlockSpec((2 * tf, l), lambda j, o: (j, 0)),
                  pl.BlockSpec((None, None, l, HY_CH), lambda j, o: (0, o, 0, 0)),
                  pl.BlockSpec((None, None, l, HY_CH), lambda j, o: (1, o, 0, 0))],
        out_specs=pl.BlockSpec((2 * tf, HY_CH), lambda j, o: (j, o)),
        out_shape=jax.ShapeDtypeStruct((rows, HY_ORDER * HY_CH), F32),
        compiler_params=_cparams(("arbitrary", "arbitrary")),
        name="hyena_filter_spectrum",
    )(fmat, filt, filt)


def _hyfwd_kernel(f_ref, z_ref, k_ref, y_ref):
    tf = f_ref.shape[0] // 2
    s = _bdot(f_ref[...], z_ref[0])
    sr, si = s[0:tf], s[tf:2 * tf]
    kr, ki = k_ref[0:tf, :], k_ref[tf:2 * tf, :]
    y_ref[0, 0:tf, :] = (sr * kr - si * ki).astype(BF16)
    y_ref[0, tf:2 * tf, :] = (sr * ki + si * kr).astype(BF16)


def _hy_forward(fmat, z, kspec, order, tf):
    rows, l = fmat.shape
    b = z.shape[0]
    return pl.pallas_call(
        _hyfwd_kernel,
        grid=(rows // (2 * tf), b),
        in_specs=[pl.BlockSpec((2 * tf, l), lambda j, bb: (j, 0)),
                  pl.BlockSpec((1, l, HY_CH), lambda j, bb: (bb, 0, 0)),
                  pl.BlockSpec((2 * tf, HY_CH), lambda j, bb: (j, order))],
        out_specs=pl.BlockSpec((1, 2 * tf, HY_CH), lambda j, bb: (bb, j, 0)),
        out_shape=jax.ShapeDtypeStruct((b, rows, HY_CH), BF16),
        compiler_params=_cparams(("arbitrary", "arbitrary")),
        name="hyena_dft_forward",
    )(fmat, z, kspec)


def _hyinv_kernel(g_ref, y_ref, z_ref, x_ref, b_ref, o_ref):
    conv = _bdot(g_ref[...], y_ref[0])
    z = z_ref[0].astype(F32)
    o_ref[0] = (x_ref[0].astype(F32) * (conv + z * b_ref[...])).astype(BF16)


def _hy_inverse(gmat, y, z, gate, bias, tt):
    l, rows = gmat.shape
    b = y.shape[0]
    return pl.pallas_call(
        _hyinv_kernel,
        grid=(l // tt, b),
        in_specs=[pl.BlockSpec((tt, rows), lambda i, bb: (i, 0)),
                  pl.BlockSpec((1, rows, HY_CH), lambda i, bb: (bb, 0, 0)),
                  pl.BlockSpec((1, tt, HY_CH), lambda i, bb: (bb, i, 0)),
                  pl.BlockSpec((1, tt, HY_CH), lambda i, bb: (bb, i, 0)),
                  pl.BlockSpec((1, HY_CH), lambda i, bb: (0, 0))],
        out_specs=pl.BlockSpec((1, tt, HY_CH), lambda i, bb: (bb, i, 0)),
        out_shape=jax.ShapeDtypeStruct((b, l, HY_CH), BF16),
        compiler_params=_cparams(("arbitrary", "arbitrary")),
        name="hyena_dft_inverse",
    )(gmat, y, z, gate, bias)


def _dft_tables(l, tf):
    k = jnp.arange(l, dtype=jnp.int32)
    n = jnp.arange(l, dtype=jnp.int32)
    ph = ((2 * k + 1)[:, None] * n[None, :]) % (4 * l)
    ang = ph.astype(F32) * (2.0 * math.pi / (4 * l))
    c = jnp.cos(ang).reshape(l // tf, 1, tf, l)
    s = (-jnp.sin(ang)).reshape(l // tf, 1, tf, l)
    f = jnp.concatenate([c, s], axis=1).reshape(2 * l, l)
    return f.astype(BF16), (f.T * (1.0 / l)).astype(BF16)


def _hy_consts(l):
    t = jnp.arange(l, dtype=F32)
    bands = jnp.arange(1, HY_BANDS + 1, dtype=F32)
    ang = (2.0 * math.pi / l) * t[:, None] * bands[None, :]
    emb = jnp.concatenate([(t / l)[:, None], jnp.cos(ang), jnp.sin(ang)], -1)
    emb = jnp.pad(emb, ((0, 0), (0, LANE - HY_EMB)))
    rates = jnp.linspace(HY_MIN_RATE, HY_MAX_RATE, HY_CH, dtype=F32)
    win = jnp.exp(-(t / l)[:, None] * rates[None, :])
    return emb, win


HY_N1 = 16
HY_J = 16
HY_GROUPS = 8
HY_KG = 16


def _hy2_tables(l):
    n1h, jj = HY_N1, HY_J
    n2, k1n = l // n1h, 2 * n1h
    k2n = n2 // 2
    k1 = jnp.arange(k1n, dtype=jnp.int32)
    ph1 = ((2 * k1 + 1)[:, None] * jnp.arange(n1h, dtype=jnp.int32)[None, :]) % (4 * n1h)
    a1 = ph1.astype(F32) * (2.0 * math.pi / (4 * n1h))
    f1 = jnp.stack([jnp.cos(a1), -jnp.sin(a1)], axis=1)
    f1big = jnp.einsum('krn,ab->kranb', f1, jnp.eye(jj, dtype=F32)).reshape(k1n * 2 * jj, n1h * jj)
    kk = k1[:, None] + k1n * jnp.arange(k2n, dtype=jnp.int32)[None, :]
    ph2 = ((2 * kk + 1)[:, :, None] * jnp.arange(n2, dtype=jnp.int32)[None, None, :]) % (4 * l)
    a2 = ph2.astype(F32) * (2.0 * math.pi / (4 * l))
    mr, mi = jnp.cos(a2), -jnp.sin(a2)
    f2t = jnp.concatenate([jnp.concatenate([mr, -mi], axis=2), jnp.concatenate([mi, mr], axis=2)], axis=1)
    return (f1big.astype(BF16), (f1big.T * (1.0 / l)).astype(BF16), f2t.astype(BF16),
            jnp.swapaxes(f2t, 1, 2).astype(BF16))


def _hy2_s1_kernel(f_ref, z_ref, o_ref):
    n1h, _, c = z_ref.shape[1:]
    k1n = o_ref.shape[1]
    for g in range(HY_GROUPS):
        cols = slice(g * HY_J, (g + 1) * HY_J)
        z = z_ref[0, :, cols, :].reshape(n1h * HY_J, c)
        a = _bdot(f_ref[...], z)
        o_ref[0, :, :, cols, :] = a.astype(BF16).reshape(k1n, 2, HY_J, c)


def _hy2_stage1(f1big, z5):
    b, n1h, n2, c = z5.shape
    k1n = 2 * n1h
    w = HY_J * HY_GROUPS
    return pl.pallas_call(
        _hy2_s1_kernel,
        grid=(b, n2 // w),
        in_specs=[pl.BlockSpec(f1big.shape, lambda bb, g: (0, 0)),
                  pl.BlockSpec((1, n1h, w, c), lambda bb, g: (bb, 0, g, 0))],
        out_specs=pl.BlockSpec((1, k1n, 2, w, c), lambda bb, g: (bb, 0, 0, g, 0)),
        out_shape=jax.ShapeDtypeStruct((b, k1n, 2, n2, c), BF16),
        compiler_params=_cparams(("arbitrary", "arbitrary")),
        name="hyena_stage1",
    )(f1big, z5)


def _hy2_fspec_kernel(f_ref, s_ref, d_ref, o_ref):
    h = o_ref.shape[1] // 2
    for i in range(HY_KG):
        o_ref[i, 0:h, :] = _bdot(f_ref[i], s_ref[0, i])[0:h]
        o_ref[i, h:2 * h, :] = _bdot(f_ref[i], d_ref[0, i])[h:2 * h]


def _hy2_filter_spectrum(f2t, a5):
    nsig, k1n, _, n2, c = a5.shape
    a4 = a5.reshape(nsig, k1n, 2 * n2, c)
    return pl.pallas_call(
        _hy2_fspec_kernel,
        grid=(k1n // HY_KG, HY_ORDER),
        in_specs=[pl.BlockSpec((HY_KG, n2, 2 * n2), lambda g, o: (g, 0, 0)),
                  pl.BlockSpec((1, HY_KG, 2 * n2, c), lambda g, o: (o, g, 0, 0)),
                  pl.BlockSpec((1, HY_KG, 2 * n2, c), lambda g, o: (HY_ORDER + o, g, 0, 0))],
        out_specs=pl.BlockSpec((HY_KG, n2, c), lambda g, o: (g, 0, o)),
        out_shape=jax.ShapeDtypeStruct((k1n, n2, HY_ORDER * c), F32),
        compiler_params=_cparams(("arbitrary", "arbitrary")),
        name="hyena_filter_spectrum2",
    )(f2t, a4, a4)


def _hy2_conv_kernel(f1_ref, g1_ref, f2_ref, g2_ref, k_ref, z_ref, x_ref, b_ref, o_ref, a_scr):
    n1h, n2, c = z_ref.shape[1:]
    k1n = a_scr.shape[0]
    groups = [slice(g * HY_J, (g + 1) * HY_J) for g in range(n2 // HY_J)]
    for cols in groups:
        z = z_ref[0, :, cols, :].reshape(n1h * HY_J, c)
        a_scr[:, :, cols, :] = _bdot(f1_ref[...], z).astype(BF16).reshape(k1n, 2, HY_J, c)
    h = n2 // 2
    for i in range(k1n):
        y = _bdot(f2_ref[i], a_scr[i].reshape(2 * n2, c))
        yr, yi = y[:h], y[h:]
        kr, ki = k_ref[i, 0:h, :], k_ref[i, h:2 * h, :]
        p = jnp.concatenate([yr * kr - yi * ki, yr * ki + yi * kr], axis=0).astype(BF16)
        a_scr[i] = _bdot(g2_ref[i], p).astype(BF16).reshape(2, n2, c)
    for cols in groups:
        cc = a_scr[:, :, cols, :].reshape(k1n * 2 * HY_J, c)
        conv = _bdot(g1_ref[...], cc)
        z = z_ref[0, :, cols, :].reshape(n1h * HY_J, c).astype(F32)
        x = x_ref[0, :, cols, :].reshape(n1h * HY_J, c).astype(F32)
        o_ref[0, :, cols, :] = (x * (conv + z * b_ref[...])).astype(BF16).reshape(n1h, HY_J, c)


def _hy2_conv(tabs, kspec, order, zsrc, zwhich, vxx5, xwhich, bias):
    f1big, g1big, f2t, g2t = tabs
    b, n1h, n2, _ = zsrc.shape
    c = HY_CH
    k1n = 2 * n1h
    resident = pl.Buffered(1)
    res = lambda a: pl.BlockSpec(a.shape, lambda bb: (0,) * a.ndim, pipeline_mode=resident)
    col = lambda which: pl.BlockSpec((1, n1h, n2, c), lambda bb: (bb, 0, 0, which))
    return pl.pallas_call(
        _hy2_conv_kernel,
        grid=(b,),
        in_specs=[res(f1big), res(g1big), res(f2t), res(g2t),
                  pl.BlockSpec((k1n, n2, c), lambda bb: (0, 0, order), pipeline_mode=resident),
                  col(zwhich), col(xwhich), pl.BlockSpec((1, c), lambda bb: (0, 0))],
        out_specs=col(0),
        out_shape=jax.ShapeDtypeStruct((b, n1h, n2, c), BF16),
        scratch_shapes=[pltpu.VMEM((k1n, 2, n2, c), BF16)],
        compiler_params=_cparams(("arbitrary",)),
        name="hyena_conv",
    )(f1big, g1big, f2t, g2t, kspec, zsrc, vxx5, bias)


def _hyena(vxx, consts, fw1, fb1, fw2, fb2, fw3, fb3, hbias):
    emb, win, tabs = consts
    filt = _hy_filters(emb, win, fw1, fb1, fw2, fb2, fw3, fb3)
    b, l, _ = vxx.shape
    c = HY_CH
    if len(tabs) == 3:
        fmat, gmat, tf = tabs
        kspec = _hy_spectrum(fmat, filt, tf)
        v, x1, x2 = vxx[..., 0:c], vxx[..., c:2 * c], vxx[..., 2 * c:3 * c]
        z = _hy_inverse(gmat, _hy_forward(fmat, v, kspec, 0, tf), v, x1, hbias[0:1], tf)
        return _hy_inverse(gmat, _hy_forward(fmat, z, kspec, 1, tf), z, x2, hbias[1:2], tf)
    f1big, g1big, f2t, g2t = tabs
    n2 = l // HY_N1
    filt5 = filt.reshape(2 * HY_ORDER, HY_N1, n2, c)
    kspec = _hy2_filter_spectrum(f2t, _hy2_stage1(f1big, filt5))
    vxx5 = vxx.reshape(b, HY_N1, n2, 3 * c)
    z5 = _hy2_conv(tabs, kspec, 0, vxx5, 0, vxx5, 1, hbias[0:1])
    return _hy2_conv(tabs, kspec, 1, z5, 0, vxx5, 2, hbias[1:2]).reshape(b, l, c)


def _mix_ffn_kernel(om_ref, od_ref, oh_ref, x_ref, mod_ref, wm_ref, wd_ref, wh_ref, w1_ref, b1_ref, w2_ref,
                    vec_ref, o_ref, *, alpha, tm, tf):
    g1, sh2, sc2, g2 = (mod_ref[0, i:i + 1, :] for i in range(4))
    bo, lg1, lb1, b2, lg2, lb2 = (vec_ref[i:i + 1, :] for i in range(6))
    nt = x_ref.shape[1] // tm
    rows = [slice(t * tm, (t + 1) * tm) for t in range(nt)]
    xs, hs = [], []
    for r in rows:
        y = _bdot(om_ref[0, r, :], wm_ref[...]) + _bdot(od_ref[0, r, :], wd_ref[...]) + _bdot(oh_ref[0, r, :], wh_ref[...])
        x1 = _ln(alpha * x_ref[0, r, :] + g1 * (y + bo)) * lg1 + lb1
        xs.append(x1)
        hs.append((_ln(x1) * (1.0 + sc2) + sh2).astype(BF16))
    accs = [None] * nt
    for c in range(w1_ref.shape[1] // tf):
        cols = slice(c * tf, (c + 1) * tf)
        for t in range(nt):
            a = jnp.maximum(_bdot(hs[t], w1_ref[:, cols]) + b1_ref[:, cols], 0.0)
            y = _bdot((a * a).astype(BF16), w2_ref[cols, :])
            accs[t] = y if accs[t] is None else accs[t] + y
    for t in range(nt):
        z = alpha * xs[t] + g2 * (accs[t] + b2)
        o_ref[0, rows[t], :] = _ln(z) * lg2 + lb2


def _mix_ffn(om, od, oh, x, mod, wm, wd, wh, w1, b1, w2, vec, alpha, tm, nt, tf):
    b, t, d = x.shape
    per_batch = mod.shape[0] == b and b > 1
    gmap = (lambda bb, i: (bb, 0, 0)) if per_batch else (lambda bb, i: (0, 0, 0))
    const = lambda bb, i: (0, 0)
    row = lambda bb, i: (bb, i, 0)
    resident = pl.Buffered(1)
    blk = lambda a: pl.BlockSpec((1, tm * nt, a.shape[2]), row)
    res = lambda a: pl.BlockSpec(a.shape, const, pipeline_mode=resident)
    return pl.pallas_call(
        functools.partial(_mix_ffn_kernel, alpha=alpha, tm=tm, tf=tf),
        grid=(b, t // (tm * nt)),
        in_specs=[blk(om), blk(od), blk(oh), blk(x), pl.BlockSpec((1,) + mod.shape[1:], gmap),
                  res(wm), res(wd), res(wh), res(w1), pl.BlockSpec(b1.shape, const), res(w2),
                  pl.BlockSpec(vec.shape, const)],
        out_specs=blk(x),
        out_shape=jax.ShapeDtypeStruct((b, t, d), F32),
        compiler_params=_cparams(("arbitrary", "arbitrary")),
        name="mix_ffn",
    )(om, od, oh, x, mod, wm, wd, wh, w1, b1, w2, vec)


def _rope_lane_plan(starts, dims):
    h = dims // 2
    inv = ROPE_BASE ** (-(np.arange(h // 2, dtype=np.float64) * (2.0 / h)))
    fr, fc, lo, hi = (np.zeros(LANE, np.float32) for _ in range(4))
    for st in starts:
        for axis_i, f in enumerate((fr, fc)):
            base = st + axis_i * h
            f[base:base + h // 2] = inv
            f[base + h // 2:base + h] = inv
            lo[base:base + h // 2] = 1.0
            hi[base + h // 2:base + h] = 1.0
    return fr, fc, lo, hi


def _rope_tables(t_len, roped):
    pos_t = jnp.arange(t_len)
    scale = 1.0 if roped else 0.0
    row = (pos_t // GRID_W).astype(F32)[:, None] * scale
    col = (pos_t % GRID_W).astype(F32)[:, None] * scale
    out = []
    for starts, dims in (([MLA_NOPE], MLA_ROPE), ([0, LANE // 2], DIFF_DIM)):
        fr, fc, lo, hi = _rope_lane_plan(starts, dims)
        ang = row * fr[None, :] + col * fc[None, :]
        sin = jnp.sin(ang)
        out += [jnp.cos(ang), -sin * lo[None, :], sin * hi[None, :]]
    return jnp.concatenate(out, axis=1)


def _prep_weights(w_in, w_q_up, w_kv_up, w_out, diff_lambda, diff_subln):
    depth, d, _ = w_in.shape
    s1, s2 = IN_MLA, IN_MLA + IN_DIFF
    pad_last = lambda a, n: jnp.pad(a, [(0, 0)] * (a.ndim - 1) + [(0, n - a.shape[-1])])
    kr = jnp.pad(w_in[..., C_KR:s1], ((0, 0), (0, 0), (MLA_NOPE, LANE - MLA_NOPE - MLA_ROPE)))
    dq = pad_last(w_in[..., s1:s1 + N_DQ].reshape(depth, d, DIFF_HEADS, 2, DIFF_DIM), LANE // 2)
    dk = pad_last(w_in[..., s1 + N_DQ:s1 + 2 * N_DQ].reshape(depth, d, DIFF_HEADS, 2, DIFF_DIM), LANE // 2)
    dv = pad_last(w_in[..., s1 + 2 * N_DQ:s2].reshape(depth, d, DIFF_HEADS, DIFF_V), LANE)
    win = jnp.concatenate([w_in[..., :C_KR], kr, dq.reshape(depth, d, W_D), dk.reshape(depth, d, W_D),
                           dv.reshape(depth, d, W_D), w_in[..., s2:]], axis=-1).astype(BF16)
    wq = pad_last(w_q_up.reshape(depth, MLA_Q_RANK, MLA_HEADS, MLA_NOPE + MLA_ROPE), LANE)
    wq = wq.reshape(depth, MLA_Q_RANK, W_QM).astype(BF16)
    kv = w_kv_up.reshape(depth, MLA_KV_RANK, MLA_HEADS, MLA_NOPE + MLA_V)
    wk = pad_last(kv[..., :MLA_NOPE], LANE).reshape(depth, MLA_KV_RANK, W_QM)
    wv = kv[..., MLA_NOPE:].reshape(depth, MLA_KV_RANK, W_VM)
    wkv = jnp.concatenate([wk, wv], axis=-1).astype(BF16)
    wo_m = w_out[:, :W_VM].astype(BF16)
    wo_d = w_out[:, W_VM:W_VM + DIFF_HEADS * DIFF_V].reshape(depth, DIFF_HEADS, DIFF_V, -1)
    wo_d = jnp.pad(wo_d, ((0, 0), (0, 0), (0, LANE - DIFF_V), (0, 0))).reshape(depth, W_D, -1).astype(BF16)
    wo_h = w_out[:, W_VM + DIFF_HEADS * DIFF_V:].astype(BF16)
    lam = pad_last(diff_lambda.astype(F32), LANE)
    subln = pad_last(diff_subln.astype(F32), LANE)[:, None, :]
    return win, wq, wkv, wo_m, wo_d, wo_h, lam, subln


def kernel(x, c, ctx, c_ctx, w_mod, b_mod, w_in, mla_q_norm, w_q_up, mla_kv_norm, w_kv_up, diff_lambda, diff_subln, hy_conv_w, hy_conv_b, hy_fw1, hy_fb1, hy_fw2, hy_fb2, hy_fw3, hy_fb3, hy_bias, w_out, b_out, ln1_g, ln1_b, w_ff1, b_ff1, w_ff2, b_ff2, ln2_g, ln2_b):
    bsz, seq, d = x.shape
    n_ctx = ctx.shape[1]
    depth = w_in.shape[0]
    alpha = (2.0 * depth) ** 0.25

    rpad = -(bsz + 1) % SUBLANE
    cc = jnp.concatenate([c, c_ctx[None, :], jnp.zeros((rpad, d), F32)], axis=0)
    mod_all = _modulation(cc, w_mod, b_mod)

    win, wq, wkv, wo_m, wo_d, wo_h, lam_p, subln = _prep_weights(w_in, w_q_up, w_kv_up, w_out, diff_lambda, diff_subln)
    w1 = w_ff1.astype(BF16)
    w2 = w_ff2.astype(BF16)
    fw1 = jnp.pad(hy_fw1, ((0, 0), (0, LANE - HY_EMB), (0, 0)))
    tab_x = _rope_tables(seq, True)
    tab_c = _rope_tables(n_ctx, False)
    def hy_tables(l):
        if l >= 1024 and l % (HY_N1 * HY_J * HY_GROUPS) == 0:
            return _hy2_tables(l)
        return _dft_tables(l, min(512, l)) + (min(512, l),)

    hyc_x = _hy_consts(seq) + (hy_tables(seq),)
    hyc_c = _hy_consts(n_ctx) + (hy_tables(n_ctx),)
    tm_x, tm_c = min(512, seq), min(512, n_ctx)
    tmi_x = min(1024, seq)
    ntf_x = 2 if seq % (2 * tm_x) == 0 else 1
    tq_x, tq_c = min(256, seq), min(256, n_ctx)
    tff = min(1024, w1.shape[2])
    row2 = lambda a: a.reshape(1, -1)

    for layer in range(depth):
        need_ctx = layer < depth - 1
        lam_init = 0.8 - 0.6 * math.exp(-0.3 * layer)
        mod = mod_all[layer, :bsz].reshape(bsz, 1, N_MOD, d)
        modc = mod_all[layer, bsz:bsz + 1].reshape(1, 1, N_MOD, d)
        sh1, sc1 = mod[:, :, 0], mod[:, :, 1]
        csh1, csc1 = modc[:, :, 0], modc[:, :, 1]
        gq, gkv = row2(mla_q_norm[layer]), row2(mla_kv_norm[layer])
        hy_args = (fw1[layer], row2(hy_fb1[layer]), hy_fw2[layer], row2(hy_fb2[layer]), hy_fw3[layer],
                   row2(hy_fb3[layer]), hy_bias[layer])
        conv = (hy_conv_w[layer], row2(hy_conv_b[layer]))

        qm, km, vm, dq, dk, dv, uh = _in_proj(x, sh1, sc1, win[layer], gq, wq[layer], gkv, wkv[layer], tab_x, *conv, tmi_x)
        cqm, ckm, cvm, cdq, cdk, cdv, cuh = _in_proj(ctx, csh1, csc1, win[layer], gq, wq[layer], gkv, wkv[layer],
                                                     tab_c, *conv, tm_c)
        om = _mla_attention(qm, [ckm, km], [cvm, vm], tq_x, MLA_HEADS // 2)
        od = _diff_attention(dq, [cdk, dk], [cdv, dv], lam_p[layer], subln[layer], lam_init, tq_x, DIFF_HEADS)
        oh = _hyena(uh, hyc_x, *hy_args)
        vec = jnp.stack([b_out[layer], ln1_g[layer], ln1_b[layer], b_ff2[layer], ln2_g[layer], ln2_b[layer]])
        mix_w = (wo_m[layer], wo_d[layer], wo_h[layer], w1[layer], row2(b_ff1[layer]), w2[layer], vec)
        x = _mix_ffn(om, od, oh, x, mod[:, 0, 2:6], *mix_w, alpha, tm_x, ntf_x, tff)
        if need_ctx:
            com = _mla_attention(cqm, [ckm], [cvm], tq_c, MLA_HEADS // 2)
            cod = _diff_attention(cdq, [cdk], [cdv], lam_p[layer], subln[layer], lam_init, tq_c, DIFF_HEADS)
            coh = _hyena(cuh, hyc_c, *hy_args)
            if (bsz * n_ctx) % (tm_x * ntf_x) == 0:
                flat = lambda a: a.reshape(1, bsz * n_ctx, a.shape[2])
                ctx = _mix_ffn(flat(com), flat(cod), flat(coh), flat(ctx), modc[:, 0, 2:6], *mix_w, alpha,
                               tm_x, ntf_x, tff).reshape(bsz, n_ctx, d)
            else:
                ctx = _mix_ffn(com, cod, coh, ctx, modc[:, 0, 2:6], *mix_w, alpha, tm_c, 1, tff)
    return x
```

```python
import functools
import math

import jax
import numpy as np
import jax.numpy as jnp
from jax import lax
from jax.experimental import pallas as pl
from jax.experimental.pallas import tpu as pltpu

F32 = jnp.float32
BF16 = jnp.bfloat16
HI = lax.Precision.HIGHEST

GRID_W = 64
ROPE_BASE = 10000.0
LN_EPS = 1e-6
MLA_HEADS, MLA_NOPE, MLA_ROPE, MLA_V = 6, 64, 32, 64
MLA_Q_RANK, MLA_KV_RANK = 256, 128
DIFF_HEADS, DIFF_DIM = 4, 48
DIFF_V = 2 * DIFF_DIM
HY_CH, HY_ORDER, HY_BANDS = 256, 2, 16
HY_EMB = 1 + 2 * HY_BANDS
HY_TARGET, HY_FAST_DECAY_PCT, HY_SLOW_DECAY_PCT = 1e-2, 0.3, 1.5
HY_MIN_RATE = -math.log(HY_TARGET) / HY_SLOW_DECAY_PCT
HY_MAX_RATE = -math.log(HY_TARGET) / HY_FAST_DECAY_PCT
N_MOD = 6

LOG2E = 1.0 / math.log(2.0)
LANE = 128
SUBLANE = 8
VMEM_LIMIT = 56 * 1024 * 1024

IN_MLA = MLA_Q_RANK + MLA_KV_RANK + MLA_ROPE
N_DQ = DIFF_HEADS * 2 * DIFF_DIM
IN_DIFF = 2 * N_DQ + DIFF_HEADS * DIFF_V
W_QM = MLA_HEADS * LANE
W_VM = MLA_HEADS * MLA_V
W_D = DIFF_HEADS * LANE
W_HY = (HY_ORDER + 1) * HY_CH
C_CQ, C_CKV, C_KR = 0, MLA_Q_RANK, MLA_Q_RANK + MLA_KV_RANK
C_DQ = C_KR + LANE
C_DK = C_DQ + W_D
C_DV = C_DK + W_D
C_HY = C_DV + W_D
W_IN = C_HY + W_HY


def _cparams(sem):
    return pltpu.CompilerParams(dimension_semantics=sem, vmem_limit_bytes=VMEM_LIMIT)


def _ln(x):
    mu = jnp.mean(x, -1, keepdims=True)
    xc = x - mu
    var = jnp.mean(xc * xc, -1, keepdims=True)
    return xc * lax.rsqrt(var + LN_EPS)


def _bdot(a, b):
    return jnp.dot(a, b, preferred_element_type=F32)


def _mod_kernel(c_ref, w_ref, b_ref, o_ref):
    c = c_ref[...]
    s = c / (1.0 + jnp.exp(-c))
    o_ref[...] = jnp.dot(s, w_ref[...], preferred_element_type=F32, precision=HI) + b_ref[...]


def _modulation(cc, w_mod, b_mod):
    depth, d, n = w_mod.shape
    r = cc.shape[0]
    tn = 1024
    return pl.pallas_call(
        _mod_kernel,
        grid=(depth, n // tn),
        in_specs=[pl.BlockSpec((r, d), lambda l, j: (0, 0)),
                  pl.BlockSpec((None, d, tn), lambda l, j: (l, 0, j)),
                  pl.BlockSpec((None, 1, tn), lambda l, j: (l, 0, j))],
        out_specs=pl.BlockSpec((None, r, tn), lambda l, j: (l, 0, j)),
        out_shape=jax.ShapeDtypeStruct((depth, r, n), F32),
        compiler_params=_cparams(("arbitrary", "arbitrary")),
        name="modulation",
    )(cc, w_mod, b_mod.reshape(depth, 1, n))


def _rope(x, c, sa, sb, half):
    return x * c + pltpu.roll(x, LANE - half, 1) * sa + pltpu.roll(x, half, 1) * sb


def _inproj_kernel(x_ref, xp_ref, xn_ref, sh_ref, sc_ref, win_ref, gq_ref, wq_ref, gkv_ref, wkv_ref, tab_ref,
                   cw_ref, cb_ref, qm_ref, km_ref, vm_ref, dq_ref, dk_ref, dv_ref, hy_ref):
    x = jnp.concatenate([x_ref[0], xp_ref[0], xn_ref[0]], axis=0)
    h = _ln(x) * (1.0 + sc_ref[0]) + sh_ref[0]
    p_all = _bdot(h.astype(BF16), win_ref[...])
    tm = x_ref.shape[1]
    p = p_all[0:tm]
    tab = tab_ref[...]
    cm, sam, sbm = tab[:, 0:LANE], tab[:, LANE:2 * LANE], tab[:, 2 * LANE:3 * LANE]
    cd, sad, sbd = tab[:, 3 * LANE:4 * LANE], tab[:, 4 * LANE:5 * LANE], tab[:, 5 * LANE:6 * LANE]

    cq = p[:, C_CQ:C_CQ + MLA_Q_RANK]
    qn = cq * lax.rsqrt(jnp.mean(cq * cq, -1, keepdims=True) + LN_EPS) * gq_ref[...]
    q = _bdot(qn.astype(BF16), wq_ref[...])
    ckv = p[:, C_CKV:C_CKV + MLA_KV_RANK]
    kvn = ckv * lax.rsqrt(jnp.mean(ckv * ckv, -1, keepdims=True) + LN_EPS) * gkv_ref[...]
    kv = _bdot(kvn.astype(BF16), wkv_ref[...])
    kr = _rope(p[:, C_KR:C_KR + LANE], cm, sam, sbm, MLA_ROPE // 4)
    scale_m = LOG2E * (MLA_NOPE + MLA_ROPE) ** -0.5
    for hd in range(MLA_HEADS):
        sl = slice(hd * LANE, (hd + 1) * LANE)
        qm_ref[0, :, sl] = (_rope(q[:, sl], cm, sam, sbm, MLA_ROPE // 4) * scale_m).astype(BF16)
        km_ref[0, :, sl] = (kv[:, sl] + kr).astype(BF16)
    vm_ref[0] = kv[:, W_QM:W_QM + W_VM].astype(BF16)
    scale_d = LOG2E * DIFF_DIM ** -0.5
    for hd in range(DIFF_HEADS):
        sl = slice(hd * LANE, (hd + 1) * LANE)
        dq = p[:, C_DQ + hd * LANE:C_DQ + (hd + 1) * LANE]
        dk = p[:, C_DK + hd * LANE:C_DK + (hd + 1) * LANE]
        dq_ref[0, :, sl] = (_rope(dq, cd, sad, sbd, DIFF_DIM // 4) * scale_d).astype(BF16)
        dk_ref[0, :, sl] = _rope(dk, cd, sad, sbd, DIFF_DIM // 4).astype(BF16)
    dv_ref[0] = p[:, C_DV:C_DV + W_D].astype(BF16)
    u = p[:, C_HY:C_HY + W_HY]
    i = pl.program_id(0)
    u_prev = jnp.where(i == 0, 0.0, p_all[tm + SUBLANE - 1:tm + SUBLANE, C_HY:C_HY + W_HY])
    u_next = jnp.where(i == pl.num_programs(0) - 1, 0.0, p_all[tm + SUBLANE:tm + SUBLANE + 1, C_HY:C_HY + W_HY])
    row = lax.broadcasted_iota(jnp.int32, u.shape, 0)
    up = jnp.where(row == 0, u_prev, pltpu.roll(u, 1, 0))
    un = jnp.where(row == tm - 1, u_next, pltpu.roll(u, tm - 1, 0))
    w = cw_ref[...]
    hy_ref[0] = (up * w[0:1] + u * w[1:2] + un * w[2:3] + cb_ref[...]).astype(BF16)


def _in_proj(x, sh, sc, win, gq, wq, gkv, wkv, tab, conv_w, conv_b, tm):
    b, t, d = x.shape
    hb = tm // SUBLANE
    halo = (1, SUBLANE, d)
    resident = pl.Buffered(1)
    per_batch = sh.shape[0] == b and b > 1
    mod_map = (lambda i, bb: (bb, 0, 0)) if per_batch else (lambda i, bb: (0, 0, 0))
    const = lambda i, bb: (0, 0)
    widths = (W_QM, W_QM, W_VM, W_D, W_D, W_D, W_HY)
    return pl.pallas_call(
        _inproj_kernel,
        grid=(t // tm, b),
        in_specs=[pl.BlockSpec((1, tm, d), lambda i, bb: (bb, i, 0)),
                  pl.BlockSpec(halo, lambda i, bb: (bb, jnp.maximum(i * hb - 1, 0), 0)),
                  pl.BlockSpec(halo, lambda i, bb: (bb, jnp.minimum((i + 1) * hb, t // SUBLANE - 1), 0)),
                  pl.BlockSpec((1, 1, d), mod_map),
                  pl.BlockSpec((1, 1, d), mod_map),
                  pl.BlockSpec(win.shape, const, pipeline_mode=resident),
                  pl.BlockSpec(gq.shape, const),
                  pl.BlockSpec(wq.shape, const, pipeline_mode=resident),
                  pl.BlockSpec(gkv.shape, const),
                  pl.BlockSpec(wkv.shape, const, pipeline_mode=resident),
                  pl.BlockSpec((tm, 6 * LANE), lambda i, bb: (i, 0)),
                  pl.BlockSpec(conv_w.shape, const), pl.BlockSpec(conv_b.shape, const)],
        out_specs=[pl.BlockSpec((1, tm, w), lambda i, bb: (bb, i, 0)) for w in widths],
        out_shape=[jax.ShapeDtypeStruct((b, t, w), BF16) for w in widths],
        compiler_params=_cparams(("arbitrary", "arbitrary")),
        name="in_proj",
    )(x, x, x, sh, sc, win, gq, wq, gkv, wkv, tab, conv_w, conv_b)


NEG_BIG = -1e30


def _lane_fold(x, op):
    r = x[:, 0:LANE]
    for i in range(1, x.shape[1] // LANE):
        r = op(r, x[:, i * LANE:(i + 1) * LANE])
    return r


def _score_pass(q, parts, ksl, s_refs):
    mx = jnp.full((q.shape[0], LANE), NEG_BIG, F32)
    for (k_ref, _, n, tk), s_ref in zip(parts, s_refs):
        for j in range(n // tk):
            k = k_ref[0, j * tk:(j + 1) * tk, ksl]
            s = lax.dot_general(q, k, (((1,), (1,)), ((), ())), preferred_element_type=F32)
            s_ref[j] = s
            mx = jnp.maximum(mx, _lane_fold(s, jnp.maximum))
    return jnp.max(mx, axis=-1, keepdims=True)


def _pv_pass(parts, vsl, s_refs, m):
    ls = jnp.zeros((m.shape[0], LANE), F32)
    acc = jnp.zeros((m.shape[0], LANE), F32)
    for (_, v_ref, n, tk), s_ref in zip(parts, s_refs):
        for j in range(n // tk):
            p = jnp.exp2(s_ref[j] - m)
            ls = ls + _lane_fold(p, jnp.add)
            acc = acc + _bdot(p.astype(BF16), v_ref[0, j * tk:(j + 1) * tk, vsl])
    return acc * (1.0 / jnp.sum(ls, axis=-1, keepdims=True))


def _softmax_pv(q, parts, ksl, vsl, s_refs):
    return _pv_pass(parts, vsl, s_refs, _score_pass(q, parts, ksl, s_refs))


def _score_pass2(qq, parts, ksl, s1_refs, s2_refs):
    tq = qq.shape[0] // 2
    mx1 = jnp.full((tq, LANE), NEG_BIG, F32)
    mx2 = jnp.full((tq, LANE), NEG_BIG, F32)
    for (k_ref, _, n, tk), s1_ref, s2_ref in zip(parts, s1_refs, s2_refs):
        for j in range(n // tk):
            k = k_ref[0, j * tk:(j + 1) * tk, ksl]
            s = lax.dot_general(qq, k, (((1,), (1,)), ((), ())), preferred_element_type=F32)
            s1_ref[j] = s[:tq]
            s2_ref[j] = s[tq:]
            mx1 = jnp.maximum(mx1, _lane_fold(s[:tq], jnp.maximum))
            mx2 = jnp.maximum(mx2, _lane_fold(s[tq:], jnp.maximum))
    return jnp.max(mx1, axis=-1, keepdims=True), jnp.max(mx2, axis=-1, keepdims=True)


def _mla_attn_kernel(*refs, part_shapes, n_pairs):
    np_ = len(part_shapes)
    q_ref = refs[0]
    k_refs = refs[1:1 + np_]
    v_refs = refs[1 + np_:1 + 2 * np_]
    o_ref = refs[1 + 2 * np_]
    s_sets = (refs[2 + 2 * np_:2 + 3 * np_], refs[2 + 3 * np_:])
    parts = [(k_refs[i], v_refs[i]) + part_shapes[i] for i in range(np_)]
    for pr in range(n_pairs):
        outs = []
        for hh in range(2):
            hd = 2 * pr + hh
            sl = slice(hd * LANE, (hd + 1) * LANE)
            outs.append(_softmax_pv(q_ref[0, :, sl], parts, sl, slice(pr * LANE, (pr + 1) * LANE), s_sets[hh]))
        lane = lax.broadcasted_iota(jnp.int32, outs[0].shape, 1)
        o_ref[0, :, pr * LANE:(pr + 1) * LANE] = jnp.where(lane < MLA_V, outs[0], outs[1]).astype(BF16)


def _chunk(n):
    return n if n <= 512 else 512


def _mla_attention(q, ks, vs, tq, n_pairs):
    b, t, _ = q.shape
    part_shapes = tuple((k.shape[1], _chunk(k.shape[1])) for k in ks)
    in_specs = [pl.BlockSpec((1, tq, 2 * n_pairs * LANE), lambda bb, g, i: (bb, i, g))]
    in_specs += [pl.BlockSpec((1, k.shape[1], 2 * n_pairs * LANE), lambda bb, g, i: (bb, 0, g)) for k in ks]
    in_specs += [pl.BlockSpec((1, v.shape[1], n_pairs * LANE), lambda bb, g, i: (bb, 0, g)) for v in vs]
    scratch = [pltpu.VMEM((n // tk, tq, tk), F32) for n, tk in part_shapes] * 2
    return pl.pallas_call(
        functools.partial(_mla_attn_kernel, part_shapes=part_shapes, n_pairs=n_pairs),
        grid=(b, MLA_HEADS // (2 * n_pairs), t // tq),
        in_specs=in_specs,
        out_specs=pl.BlockSpec((1, tq, n_pairs * LANE), lambda bb, g, i: (bb, i, g)),
        out_shape=jax.ShapeDtypeStruct((b, t, W_VM), BF16),
        scratch_shapes=scratch,
        compiler_params=_cparams(("arbitrary", "arbitrary", "arbitrary")),
        name="mla_attention",
    )(q, *ks, *vs)


def _diff_attn_kernel(*refs, part_shapes, lam_init, n_heads):
    np_ = len(part_shapes)
    q_ref = refs[0]
    k_refs = refs[1:1 + np_]
    v_refs = refs[1 + np_:1 + 2 * np_]
    lam_ref, g_ref, o_ref = refs[1 + 2 * np_:4 + 2 * np_]
    s_sets = (refs[4 + 2 * np_:4 + 3 * np_], refs[4 + 3 * np_:])
    parts = [(k_refs[i], v_refs[i]) + part_shapes[i] for i in range(np_)]
    lp = lam_ref[...]
    lam = (jnp.exp(jnp.sum(lp[0:1] * lp[1:2], axis=-1, keepdims=True))
           - jnp.exp(jnp.sum(lp[2:3] * lp[3:4], axis=-1, keepdims=True)) + lam_init)
    for hd in range(n_heads):
        sl = slice(hd * LANE, (hd + 1) * LANE)
        q = q_ref[0, :, sl]
        lane = lax.broadcasted_iota(jnp.int32, q.shape, 1)
        zero = jnp.zeros_like(q)
        qq = jnp.concatenate([jnp.where(lane < LANE // 2, q, zero), jnp.where(lane >= LANE // 2, q, zero)], axis=0)
        m1, m2 = _score_pass2(qq, parts, sl, s_sets[0], s_sets[1])
        o1 = _pv_pass(parts, sl, s_sets[0], m1)
        o2 = _pv_pass(parts, sl, s_sets[1], m2)
        o = o1 - lam * o2
        ms = jnp.sum(o * o, axis=-1, keepdims=True) * (1.0 / DIFF_V)
        o_ref[0, :, sl] = (o * lax.rsqrt(ms + LN_EPS) * g_ref[...] * (1.0 - lam_init)).astype(BF16)


def _diff_attention(q, ks, vs, lam_p, subln, lam_init, tq, n_heads):
    b, t, _ = q.shape
    part_shapes = tuple((k.shape[1], _chunk(k.shape[1])) for k in ks)
    hmap = lambda bb, g, i: (bb, 0, g)
    in_specs = [pl.BlockSpec((1, tq, n_heads * LANE), lambda bb, g, i: (bb, i, g))]
    in_specs += [pl.BlockSpec((1, k.shape[1], n_heads * LANE), hmap) for k in ks]
    in_specs += [pl.BlockSpec((1, v.shape[1], n_heads * LANE), hmap) for v in vs]
    in_specs += [pl.BlockSpec(lam_p.shape, lambda bb, g, i: (0, 0)),
                 pl.BlockSpec(subln.shape, lambda bb, g, i: (0, 0))]
    scratch = [pltpu.VMEM((n // tk, tq, tk), F32) for n, tk in part_shapes] * 2
    return pl.pallas_call(
        functools.partial(_diff_attn_kernel, part_shapes=part_shapes, lam_init=lam_init, n_heads=n_heads),
        grid=(b, DIFF_HEADS // n_heads, t // tq),
        in_specs=in_specs,
        out_specs=pl.BlockSpec((1, tq, n_heads * LANE), lambda bb, g, i: (bb, i, g)),
        out_shape=jax.ShapeDtypeStruct((b, t, W_D), BF16),
        scratch_shapes=scratch,
        compiler_params=_cparams(("arbitrary", "arbitrary", "arbitrary")),
        name="diff_attention",
    )(q, *ks, *vs, lam_p, subln)


def _hyfilt_kernel(emb_ref, win_ref, fw1_ref, fb1_ref, fw2_ref, fb2_ref, fwf_ref, fbf_ref, fwb_ref, fbb_ref,
                   d_ref, h_scr):
    @pl.when(pl.program_id(0) == 0)
    def _():
        h1 = jnp.sin(jnp.dot(emb_ref[...], fw1_ref[...], preferred_element_type=F32, precision=HI) + fb1_ref[...])
        h_scr[...] = jnp.sin(jnp.dot(h1, fw2_ref[...], preferred_element_type=F32, precision=HI) + fb2_ref[...])

    h = h_scr[...]
    w = win_ref[...]
    fwd = (jnp.dot(h, fwf_ref[...], preferred_element_type=F32, precision=HI) + fbf_ref[...]) * w
    bwd = (jnp.dot(h, fwb_ref[...], preferred_element_type=F32, precision=HI) + fbb_ref[...]) * w
    row = lax.broadcasted_iota(jnp.int32, bwd.shape, 0)
    bwd = jnp.where(row == 0, 0.0, bwd)
    norm = jnp.sum(jnp.abs(fwd), axis=0, keepdims=True) + jnp.sum(jnp.abs(bwd), axis=0, keepdims=True)
    inv = 1.0 / norm
    d_ref[0] = ((fwd + bwd) * inv).astype(BF16)
    d_ref[1] = ((fwd - bwd) * inv).astype(BF16)


def _hy_filters(emb, win, fw1, fb1, fw2, fb2, fw3, fb3):
    l = emb.shape[0]
    hid = fw2.shape[0]
    nblk = HY_CH // LANE
    const = lambda g: (0, 0)
    fcol = lambda g: (0, (g // nblk) * 2 * nblk + g % nblk)
    bcol = lambda g: (0, (g // nblk) * 2 * nblk + nblk + g % nblk)
    return pl.pallas_call(
        _hyfilt_kernel,
        grid=(HY_ORDER * nblk,),
        in_specs=[pl.BlockSpec(emb.shape, const),
                  pl.BlockSpec((l, LANE), lambda g: (0, g % nblk)),
                  pl.BlockSpec(fw1.shape, const), pl.BlockSpec(fb1.shape, const),
                  pl.BlockSpec(fw2.shape, const), pl.BlockSpec(fb2.shape, const),
                  pl.BlockSpec((hid, LANE), fcol), pl.BlockSpec((1, LANE), fcol),
                  pl.BlockSpec((hid, LANE), bcol), pl.BlockSpec((1, LANE), bcol)],
        out_specs=pl.BlockSpec((2, None, l, LANE), lambda g: (0, g // nblk, 0, g % nblk)),
        out_shape=jax.ShapeDtypeStruct((2, HY_ORDER, l, HY_CH), BF16),
        scratch_shapes=[pltpu.VMEM((l, hid), F32)],
        compiler_params=_cparams(("arbitrary",)),
        name="hyena_filters",
    )(emb, win, fw1, fb1, fw2, fb2, fw3, fb3, fw3, fb3)


def _hyspec_kernel(f_ref, s_ref, d_ref, o_ref):
    tf = f_ref.shape[0] // 2
    o_ref[0:tf, :] = _bdot(f_ref[0:tf, :], s_ref[...])
    o_ref[tf:2 * tf, :] = _bdot(f_ref[tf:2 * tf, :], d_ref[...])


def _hy_spectrum(fmat, filt, tf):
    rows, l = fmat.shape
    return pl.pallas_call(
        _hyspec_kernel,
        grid=(rows // (2 * tf), HY_ORDER),
        in_specs=[pl.BlockSpec((2 * tf, l), lambda j, o: (j, 0)),
                  pl.BlockSpec((None, None, l, HY_CH), lambda j, o: (0, o, 0, 0)),
                  pl.BlockSpec((None, None, l, HY_CH), lambda j, o: (1, o, 0, 0))],
        out_specs=pl.BlockSpec((2 * tf, HY_CH), lambda j, o: (j, o)),
        out_shape=jax.ShapeDtypeStruct((rows, HY_ORDER * HY_CH), F32),
        compiler_params=_cparams(("arbitrary", "arbitrary")),
        name="hyena_filter_spectrum",
    )(fmat, filt, filt)


def _hyfwd_kernel(f_ref, z_ref, k_ref, y_ref):
    tf = f_ref.shape[0] // 2
    s = _bdot(f_ref[...], z_ref[0])
    sr, si = s[0:tf], s[tf:2 * tf]
    kr, ki = k_ref[0:tf, :], k_ref[tf:2 * tf, :]
    y_ref[0, 0:tf, :] = (sr * kr - si * ki).astype(BF16)
    y_ref[0, tf:2 * tf, :] = (sr * ki + si * kr).astype(BF16)


def _hy_forward(fmat, z, kspec, order, tf):
    rows, l = fmat.shape
    b = z.shape[0]
    return pl.pallas_call(
        _hyfwd_kernel,
        grid=(rows // (2 * tf), b),
        in_specs=[pl.BlockSpec((2 * tf, l), lambda j, bb: (j, 0)),
                  pl.BlockSpec((1, l, HY_CH), lambda j, bb: (bb, 0, 0)),
                  pl.BlockSpec((2 * tf, HY_CH), lambda j, bb: (j, order))],
        out_specs=pl.BlockSpec((1, 2 * tf, HY_CH), lambda j, bb: (bb, j, 0)),
        out_shape=jax.ShapeDtypeStruct((b, rows, HY_CH), BF16),
        compiler_params=_cparams(("arbitrary", "arbitrary")),
        name="hyena_dft_forward",
    )(fmat, z, kspec)


def _hyinv_kernel(g_ref, y_ref, z_ref, x_ref, b_ref, o_ref):
    conv = _bdot(g_ref[...], y_ref[0])
    z = z_ref[0].astype(F32)
    o_ref[0] = (x_ref[0].astype(F32) * (conv + z * b_ref[...])).astype(BF16)


def _hy_inverse(gmat, y, z, gate, bias, tt):
    l, rows = gmat.shape
    b = y.shape[0]
    return pl.pallas_call(
        _hyinv_kernel,
        grid=(l // tt, b),
        in_specs=[pl.BlockSpec((tt, rows), lambda i, bb: (i, 0)),
                  pl.BlockSpec((1, rows, HY_CH), lambda i, bb: (bb, 0, 0)),
                  pl.BlockSpec((1, tt, HY_CH), lambda i, bb: (bb, i, 0)),
                  pl.BlockSpec((1, tt, HY_CH), lambda i, bb: (bb, i, 0)),
                  pl.BlockSpec((1, HY_CH), lambda i, bb: (0, 0))],
        out_specs=pl.BlockSpec((1, tt, HY_CH), lambda i, bb: (bb, i, 0)),
        out_shape=jax.ShapeDtypeStruct((b, l, HY_CH), BF16),
        compiler_params=_cparams(("arbitrary", "arbitrary")),
        name="hyena_dft_inverse",
    )(gmat, y, z, gate, bias)


def _dft_tables(l, tf):
    k = jnp.arange(l, dtype=jnp.int32)
    n = jnp.arange(l, dtype=jnp.int32)
    ph = ((2 * k + 1)[:, None] * n[None, :]) % (4 * l)
    ang = ph.astype(F32) * (2.0 * math.pi / (4 * l))
    c = jnp.cos(ang).reshape(l // tf, 1, tf, l)
    s = (-jnp.sin(ang)).reshape(l // tf, 1, tf, l)
    f = jnp.concatenate([c, s], axis=1).reshape(2 * l, l)
    return f.astype(BF16), (f.T * (1.0 / l)).astype(BF16)


def _hy_consts(l):
    t = jnp.arange(l, dtype=F32)
    bands = jnp.arange(1, HY_BANDS + 1, dtype=F32)
    ang = (2.0 * math.pi / l) * t[:, None] * bands[None, :]
    emb = jnp.concatenate([(t / l)[:, None], jnp.cos(ang), jnp.sin(ang)], -1)
    emb = jnp.pad(emb, ((0, 0), (0, LANE - HY_EMB)))
    rates = jnp.linspace(HY_MIN_RATE, HY_MAX_RATE, HY_CH, dtype=F32)
    win = jnp.exp(-(t / l)[:, None] * rates[None, :])
    return emb, win


HY_N1 = 16
HY_J = 16
HY_GROUPS = 8
HY_KG = 16


def _hy2_tables(l):
    n1h, jj = HY_N1, HY_J
    n2, k1n = l // n1h, 2 * n1h
    k2n = n2 // 2
    k1 = jnp.arange(k1n, dtype=jnp.int32)
    ph1 = ((2 * k1 + 1)[:, None] * jnp.arange(n1h, dtype=jnp.int32)[None, :]) % (4 * n1h)
    a1 = ph1.astype(F32) * (2.0 * math.pi / (4 * n1h))
    f1 = jnp.stack([jnp.cos(a1), -jnp.sin(a1)], axis=1)
    f1big = jnp.einsum('krn,ab->kranb', f1, jnp.eye(jj, dtype=F32)).reshape(k1n * 2 * jj, n1h * jj)
    kk = k1[:, None] + k1n * jnp.arange(k2n, dtype=jnp.int32)[None, :]
    ph2 = ((2 * kk + 1)[:, :, None] * jnp.arange(n2, dtype=jnp.int32)[None, None, :]) % (4 * l)
    a2 = ph2.astype(F32) * (2.0 * math.pi / (4 * l))
    mr, mi = jnp.cos(a2), -jnp.sin(a2)
    f2t = jnp.concatenate([jnp.concatenate([mr, -mi], axis=2), jnp.concatenate([mi, mr], axis=2)], axis=1)
    return (f1big.astype(BF16), (f1big.T * (1.0 / l)).astype(BF16), f2t.astype(BF16),
            jnp.swapaxes(f2t, 1, 2).astype(BF16))


def _hy2_s1_kernel(f_ref, z_ref, o_ref):
    n1h, _, c = z_ref.shape[1:]
    k1n = o_ref.shape[1]
    for g in range(HY_GROUPS):
        cols = slice(g * HY_J, (g + 1) * HY_J)
        z = z_ref[0, :, cols, :].reshape(n1h * HY_J, c)
        a = _bdot(f_ref[...], z)
        o_ref[0, :, :, cols, :] = a.astype(BF16).reshape(k1n, 2, HY_J, c)


def _hy2_stage1(f1big, z5):
    b, n1h, n2, c = z5.shape
    k1n = 2 * n1h
    w = HY_J * HY_GROUPS
    return pl.pallas_call(
        _hy2_s1_kernel,
        grid=(b, n2 // w),
        in_specs=[pl.BlockSpec(f1big.shape, lambda bb, g: (0, 0)),
                  pl.BlockSpec((1, n1h, w, c), lambda bb, g: (bb, 0, g, 0))],
        out_specs=pl.BlockSpec((1, k1n, 2, w, c), lambda bb, g: (bb, 0, 0, g, 0)),
        out_shape=jax.ShapeDtypeStruct((b, k1n, 2, n2, c), BF16),
        compiler_params=_cparams(("arbitrary", "arbitrary")),
        name="hyena_stage1",
    )(f1big, z5)


def _hy2_fspec_kernel(f_ref, s_ref, d_ref, o_ref):
    h = o_ref.shape[1] // 2
    for i in range(HY_KG):
        o_ref[i, 0:h, :] = _bdot(f_ref[i], s_ref[0, i])[0:h]
        o_ref[i, h:2 * h, :] = _bdot(f_ref[i], d_ref[0, i])[h:2 * h]


def _hy2_filter_spectrum(f2t, a5):
    nsig, k1n, _, n2, c = a5.shape
    a4 = a5.reshape(nsig, k1n, 2 * n2, c)
    return pl.pallas_call(
        _hy2_fspec_kernel,
        grid=(k1n // HY_KG, HY_ORDER),
        in_specs=[pl.BlockSpec((HY_KG, n2, 2 * n2), lambda g, o: (g, 0, 0)),
                  pl.BlockSpec((1, HY_KG, 2 * n2, c), lambda g, o: (o, g, 0, 0)),
                  pl.BlockSpec((1, HY_KG, 2 * n2, c), lambda g, o: (HY_ORDER + o, g, 0, 0))],
        out_specs=pl.BlockSpec((HY_KG, n2, c), lambda g, o: (g, 0, o)),
        out_shape=jax.ShapeDtypeStruct((k1n, n2, HY_ORDER * c), F32),
        compiler_params=_cparams(("arbitrary", "arbitrary")),
        name="hyena_filter_spectrum2",
    )(f2t, a4, a4)


def _hy2_conv_kernel(f1_ref, g1_ref, f2_ref, g2_ref, k_ref, z_ref, x_ref, b_ref, o_ref, a_scr):
    n1h, n2, c = z_ref.shape[1:]
    k1n = a_scr.shape[0]
    groups = [slice(g * HY_J, (g + 1) * HY_J) for g in range(n2 // HY_J)]
    for cols in groups:
        z = z_ref[0, :, cols, :].reshape(n1h * HY_J, c)
        a_scr[:, :, cols, :] = _bdot(f1_ref[...], z).astype(BF16).reshape(k1n, 2, HY_J, c)
    h = n2 // 2
    for i in range(k1n):
        y = _bdot(f2_ref[i], a_scr[i].reshape(2 * n2, c))
        yr, yi = y[:h], y[h:]
        kr, ki = k_ref[i, 0:h, :], k_ref[i, h:2 * h, :]
        p = jnp.concatenate([yr * kr - yi * ki, yr * ki + yi * kr], axis=0).astype(BF16)
        a_scr[i] = _bdot(g2_ref[i], p).astype(BF16).reshape(2, n2, c)
    for cols in groups:
        cc = a_scr[:, :, cols, :].reshape(k1n * 2 * HY_J, c)
        conv = _bdot(g1_ref[...], cc)
        z = z_ref[0, :, cols, :].reshape(n1h * HY_J, c).astype(F32)
        x = x_ref[0, :, cols, :].reshape(n1h * HY_J, c).astype(F32)
        o_ref[0, :, cols, :] = (x * (conv + z * b_ref[...])).astype(BF16).reshape(n1h, HY_J, c)


def _hy2_conv(tabs, kspec, order, zsrc, zwhich, vxx5, xwhich, bias):
    f1big, g1big, f2t, g2t = tabs
    b, n1h, n2, _ = zsrc.shape
    c = HY_CH
    k1n = 2 * n1h
    resident = pl.Buffered(1)
    res = lambda a: pl.BlockSpec(a.shape, lambda bb: (0,) * a.ndim, pipeline_mode=resident)
    col = lambda which: pl.BlockSpec((1, n1h, n2, c), lambda bb: (bb, 0, 0, which))
    return pl.pallas_call(
        _hy2_conv_kernel,
        grid=(b,),
        in_specs=[res(f1big), res(g1big), res(f2t), res(g2t),
                  pl.BlockSpec((k1n, n2, c), lambda bb: (0, 0, order), pipeline_mode=resident),
                  col(zwhich), col(xwhich), pl.BlockSpec((1, c), lambda bb: (0, 0))],
        out_specs=col(0),
        out_shape=jax.ShapeDtypeStruct((b, n1h, n2, c), BF16),
        scratch_shapes=[pltpu.VMEM((k1n, 2, n2, c), BF16)],
        compiler_params=_cparams(("arbitrary",)),
        name="hyena_conv",
    )(f1big, g1big, f2t, g2t, kspec, zsrc, vxx5, bias)


def _hyena(vxx, consts, fw1, fb1, fw2, fb2, fw3, fb3, hbias):
    emb, win, tabs = consts
    filt = _hy_filters(emb, win, fw1, fb1, fw2, fb2, fw3, fb3)
    b, l, _ = vxx.shape
    c = HY_CH
    if len(tabs) == 3:
        fmat, gmat, tf = tabs
        kspec = _hy_spectrum(fmat, filt, tf)
        v, x1, x2 = vxx[..., 0:c], vxx[..., c:2 * c], vxx[..., 2 * c:3 * c]
        z = _hy_inverse(gmat, _hy_forward(fmat, v, kspec, 0, tf), v, x1, hbias[0:1], tf)
        return _hy_inverse(gmat, _hy_forward(fmat, z, kspec, 1, tf), z, x2, hbias[1:2], tf)
    f1big, g1big, f2t, g2t = tabs
    n2 = l // HY_N1
    filt5 = filt.reshape(2 * HY_ORDER, HY_N1, n2, c)
    kspec = _hy2_filter_spectrum(f2t, _hy2_stage1(f1big, filt5))
    vxx5 = vxx.reshape(b, HY_N1, n2, 3 * c)
    z5 = _hy2_conv(tabs, kspec, 0, vxx5, 0, vxx5, 1, hbias[0:1])
    return _hy2_conv(tabs, kspec, 1, z5, 0, vxx5, 2, hbias[1:2]).reshape(b, l, c)


def _mix_ffn_kernel(om_ref, od_ref, oh_ref, x_ref, mod_ref, wm_ref, wd_ref, wh_ref, w1_ref, b1_ref, w2_ref,
                    vec_ref, o_ref, *, alpha, tm, tf):
    g1, sh2, sc2, g2 = (mod_ref[0, i:i + 1, :] for i in range(4))
    bo, lg1, lb1, b2, lg2, lb2 = (vec_ref[i:i + 1, :] for i in range(6))
    nt = x_ref.shape[1] // tm
    rows = [slice(t * tm, (t + 1) * tm) for t in range(nt)]
    xs, hs = [], []
    for r in rows:
        y = _bdot(om_ref[0, r, :], wm_ref[...]) + _bdot(od_ref[0, r, :], wd_ref[...]) + _bdot(oh_ref[0, r, :], wh_ref[...])
        x1 = _ln(alpha * x_ref[0, r, :] + g1 * (y + bo)) * lg1 + lb1
        xs.append(x1)
        hs.append((_ln(x1) * (1.0 + sc2) + sh2).astype(BF16))
    accs = [None] * nt
    for c in range(w1_ref.shape[1] // tf):
        cols = slice(c * tf, (c + 1) * tf)
        for t in range(nt):
            a = jnp.maximum(_bdot(hs[t], w1_ref[:, cols]) + b1_ref[:, cols], 0.0)
            y = _bdot((a * a).astype(BF16), w2_ref[cols, :])
            accs[t] = y if accs[t] is None else accs[t] + y
    for t in range(nt):
        z = alpha * xs[t] + g2 * (accs[t] + b2)
        o_ref[0, rows[t], :] = _ln(z) * lg2 + lb2


def _mix_ffn(om, od, oh, x, mod, wm, wd, wh, w1, b1, w2, vec, alpha, tm, nt, tf):
    b, t, d = x.shape
    per_batch = mod.shape[0] == b and b > 1
    gmap = (lambda bb, i: (bb, 0, 0)) if per_batch else (lambda bb, i: (0, 0, 0))
    const = lambda bb, i: (0, 0)
    row = lambda bb, i: (bb, i, 0)
    resident = pl.Buffered(1)
    blk = lambda a: pl.BlockSpec((1, tm * nt, a.shape[2]), row)
    res = lambda a: pl.BlockSpec(a.shape, const, pipeline_mode=resident)
    return pl.pallas_call(
        functools.partial(_mix_ffn_kernel, alpha=alpha, tm=tm, tf=tf),
        grid=(b, t // (tm * nt)),
        in_specs=[blk(om), blk(od), blk(oh), blk(x), pl.BlockSpec((1,) + mod.shape[1:], gmap),
                  res(wm), res(wd), res(wh), res(w1), pl.BlockSpec(b1.shape, const), res(w2),
                  pl.BlockSpec(vec.shape, const)],
        out_specs=blk(x),
        out_shape=jax.ShapeDtypeStruct((b, t, d), F32),
        compiler_params=_cparams(("arbitrary", "arbitrary")),
        name="mix_ffn",
    )(om, od, oh, x, mod, wm, wd, wh, w1, b1, w2, vec)


def _rope_lane_plan(starts, dims):
    h = dims // 2
    inv = ROPE_BASE ** (-(np.arange(h // 2, dtype=np.float64) * (2.0 / h)))
    fr, fc, lo, hi = (np.zeros(LANE, np.float32) for _ in range(4))
    for st in starts:
        for axis_i, f in enumerate((fr, fc)):
            base = st + axis_i * h
            f[base:base + h // 2] = inv
            f[base + h // 2:base + h] = inv
            lo[base:base + h // 2] = 1.0
            hi[base + h // 2:base + h] = 1.0
    return fr, fc, lo, hi


def _rope_tables(t_len, roped):
    pos_t = jnp.arange(t_len)
    scale = 1.0 if roped else 0.0
    row = (pos_t // GRID_W).astype(F32)[:, None] * scale
    col = (pos_t % GRID_W).astype(F32)[:, None] * scale
    out = []
    for starts, dims in (([MLA_NOPE], MLA_ROPE), ([0, LANE // 2], DIFF_DIM)):
        fr, fc, lo, hi = _rope_lane_plan(starts, dims)
        ang = row * fr[None, :] + col * fc[None, :]
        sin = jnp.sin(ang)
        out += [jnp.cos(ang), -sin * lo[None, :], sin * hi[None, :]]
    return jnp.concatenate(out, axis=1)


def _prep_weights(w_in, w_q_up, w_kv_up, w_out, diff_lambda, diff_subln):
    depth, d, _ = w_in.shape
    s1, s2 = IN_MLA, IN_MLA + IN_DIFF
    pad_last = lambda a, n: jnp.pad(a, [(0, 0)] * (a.ndim - 1) + [(0, n - a.shape[-1])])
    kr = jnp.pad(w_in[..., C_KR:s1], ((0, 0), (0, 0), (MLA_NOPE, LANE - MLA_NOPE - MLA_ROPE)))
    dq = pad_last(w_in[..., s1:s1 + N_DQ].reshape(depth, d, DIFF_HEADS, 2, DIFF_DIM), LANE // 2)
    dk = pad_last(w_in[..., s1 + N_DQ:s1 + 2 * N_DQ].reshape(depth, d, DIFF_HEADS, 2, DIFF_DIM), LANE // 2)
    dv = pad_last(w_in[..., s1 + 2 * N_DQ:s2].reshape(depth, d, DIFF_HEADS, DIFF_V), LANE)
    win = jnp.concatenate([w_in[..., :C_KR], kr, dq.reshape(depth, d, W_D), dk.reshape(depth, d, W_D),
                           dv.reshape(depth, d, W_D), w_in[..., s2:]], axis=-1).astype(BF16)
    wq = pad_last(w_q_up.reshape(depth, MLA_Q_RANK, MLA_HEADS, MLA_NOPE + MLA_ROPE), LANE)
    wq = wq.reshape(depth, MLA_Q_RANK, W_QM).astype(BF16)
    kv = w_kv_up.reshape(depth, MLA_KV_RANK, MLA_HEADS, MLA_NOPE + MLA_V)
    wk = pad_last(kv[..., :MLA_NOPE], LANE).reshape(depth, MLA_KV_RANK, W_QM)
    wv = kv[..., MLA_NOPE:].reshape(depth, MLA_KV_RANK, W_VM)
    wkv = jnp.concatenate([wk, wv], axis=-1).astype(BF16)
    wo_m = w_out[:, :W_VM].astype(BF16)
    wo_d = w_out[:, W_VM:W_VM + DIFF_HEADS * DIFF_V].reshape(depth, DIFF_HEADS, DIFF_V, -1)
    wo_d = jnp.pad(wo_d, ((0, 0), (0, 0), (0, LANE - DIFF_V), (0, 0))).reshape(depth, W_D, -1).astype(BF16)
    wo_h = w_out[:, W_VM + DIFF_HEADS * DIFF_V:].astype(BF16)
    lam = pad_last(diff_lambda.astype(F32), LANE)
    subln = pad_last(diff_subln.astype(F32), LANE)[:, None, :]
    return win, wq, wkv, wo_m, wo_d, wo_h, lam, subln


def kernel(x, c, ctx, c_ctx, w_mod, b_mod, w_in, mla_q_norm, w_q_up, mla_kv_norm, w_kv_up, diff_lambda, diff_subln, hy_conv_w, hy_conv_b, hy_fw1, hy_fb1, hy_fw2, hy_fb2, hy_fw3, hy_fb3, hy_bias, w_out, b_out, ln1_g, ln1_b, w_ff1, b_ff1, w_ff2, b_ff2, ln2_g, ln2_b):
    bsz, seq, d = x.shape
    n_ctx = ctx.shape[1]
    depth = w_in.shape[0]
    alpha = (2.0 * depth) ** 0.25

    rpad = -(bsz + 1) % SUBLANE
    cc = jnp.concatenate([c, c_ctx[None, :], jnp.zeros((rpad, d), F32)], axis=0)
    mod_all = _modulation(cc, w_mod, b_mod)

    win, wq, wkv, wo_m, wo_d, wo_h, lam_p, subln = _prep_weights(w_in, w_q_up, w_kv_up, w_out, diff_lambda, diff_subln)
    w1 = w_ff1.astype(BF16)
    w2 = w_ff2.astype(BF16)
    fw1 = jnp.pad(hy_fw1, ((0, 0), (0, LANE - HY_EMB), (0, 0)))
    tab_x = _rope_tables(seq, True)
    tab_c = _rope_tables(n_ctx, False)
    def hy_tables(l):
        if l >= 1024 and l % (HY_N1 * HY_J * HY_GROUPS) == 0:
            return _hy2_tables(l)
        return _dft_tables(l, min(512, l)) + (min(512, l),)

    hyc_x = _hy_consts(seq) + (hy_tables(seq),)
    hyc_c = _hy_consts(n_ctx) + (hy_tables(n_ctx),)
    tm_x, tm_c = min(512, seq), min(512, n_ctx)
    tmi_x = min(1024, seq)
    ntf_x = 2 if seq % (2 * tm_x) == 0 else 1
    tq_x, tq_c = min(256, seq), min(256, n_ctx)
    tff = min(1024, w1.shape[2])
    row2 = lambda a: a.reshape(1, -1)

    for layer in range(depth):
        need_ctx = layer < depth - 1
        lam_init = 0.8 - 0.6 * math.exp(-0.3 * layer)
        mod = mod_all[layer, :bsz].reshape(bsz, 1, N_MOD, d)
        modc = mod_all[layer, bsz:bsz + 1].reshape(1, 1, N_MOD, d)
        sh1, sc1 = mod[:, :, 0], mod[:, :, 1]
        csh1, csc1 = modc[:, :, 0], modc[:, :, 1]
        gq, gkv = row2(mla_q_norm[layer]), row2(mla_kv_norm[layer])
        hy_args = (fw1[layer], row2(hy_fb1[layer]), hy_fw2[layer], row2(hy_fb2[layer]), hy_fw3[layer],
                   row2(hy_fb3[layer]), hy_bias[layer])
        conv = (hy_conv_w[layer], row2(hy_conv_b[layer]))

        qm, km, vm, dq, dk, dv, uh = _in_proj(x, sh1, sc1, win[layer], gq, wq[layer], gkv, wkv[layer], tab_x, *conv, tmi_x)
        cqm, ckm, cvm, cdq, cdk, cdv, cuh = _in_proj(ctx, csh1, csc1, win[layer], gq, wq[layer], gkv, wkv[layer],
                                                     tab_c, *conv, tm_c)
        om = _mla_attention(qm, [ckm, km], [cvm, vm], tq_x, MLA_HEADS // 2)
        od = _diff_attention(dq, [cdk, dk], [cdv, dv], lam_p[layer], subln[layer], lam_init, tq_x, DIFF_HEADS)
        oh = _hyena(uh, hyc_x, *hy_args)
        vec = jnp.stack([b_out[layer], ln1_g[layer], ln1_b[layer], b_ff2[layer], ln2_g[layer], ln2_b[layer]])
        mix_w = (wo_m[layer], wo_d[layer], wo_h[layer], w1[layer], row2(b_ff1[layer]), w2[layer], vec)
        x = _mix_ffn(om, od, oh, x, mod[:, 0, 2:6], *mix_w, alpha, tm_x, ntf_x, tff)
        if need_ctx:
            com = _mla_attention(cqm, [ckm], [cvm], tq_c, MLA_HEADS // 2)
            cod = _diff_attention(cdq, [cdk], [cdv], lam_p[layer], subln[layer], lam_init, tq_c, DIFF_HEADS)
            coh = _hyena(cuh, hyc_c, *hy_args)
            if (bsz * n_ctx) % (tm_x * ntf_x) == 0:
                flat = lambda a: a.reshape(1, bsz * n_ctx, a.shape[2])
                ctx = _mix_ffn(flat(com), flat(cod), flat(coh), flat(ctx), modc[:, 0, 2:6], *mix_w, alpha,
                               tm_x, ntf_x, tff).reshape(bsz, n_ctx, d)
            else:
                ctx = _mix_ffn(com, cod, coh, ctx, modc[:, 0, 2:6], *mix_w, alpha, tm_c, 1, tff)
    return x
```

```python
import functools
import math

import jax
import numpy as np
import jax.numpy as jnp
from jax import lax
from jax.experimental import pallas as pl
from jax.experimental.pallas import tpu as pltpu

F32 = jnp.float32
BF16 = jnp.bfloat16
HI = lax.Precision.HIGHEST

GRID_W = 64
ROPE_BASE = 10000.0
LN_EPS = 1e-6
MLA_HEADS, MLA_NOPE, MLA_ROPE, MLA_V = 6, 64, 32, 64
MLA_Q_RANK, MLA_KV_RANK = 256, 128
DIFF_HEADS, DIFF_DIM = 4, 48
DIFF_V = 2 * DIFF_DIM
HY_CH, HY_ORDER, HY_BANDS = 256, 2, 16
HY_EMB = 1 + 2 * HY_BANDS
HY_TARGET, HY_FAST_DECAY_PCT, HY_SLOW_DECAY_PCT = 1e-2, 0.3, 1.5
HY_MIN_RATE = -math.log(HY_TARGET) / HY_SLOW_DECAY_PCT
HY_MAX_RATE = -math.log(HY_TARGET) / HY_FAST_DECAY_PCT
N_MOD = 6

LOG2E = 1.0 / math.log(2.0)
LANE = 128
SUBLANE = 8
VMEM_LIMIT = 56 * 1024 * 1024

IN_MLA = MLA_Q_RANK + MLA_KV_RANK + MLA_ROPE
N_DQ = DIFF_HEADS * 2 * DIFF_DIM
IN_DIFF = 2 * N_DQ + DIFF_HEADS * DIFF_V
W_QM = MLA_HEADS * LANE
W_VM = MLA_HEADS * MLA_V
W_D = DIFF_HEADS * LANE
W_HY = (HY_ORDER + 1) * HY_CH
C_CQ, C_CKV, C_KR = 0, MLA_Q_RANK, MLA_Q_RANK + MLA_KV_RANK
C_DQ = C_KR + LANE
C_DK = C_DQ + W_D
C_DV = C_DK + W_D
C_HY = C_DV + W_D
W_IN = C_HY + W_HY


def _cparams(sem):
    return pltpu.CompilerParams(dimension_semantics=sem, vmem_limit_bytes=VMEM_LIMIT)


def _ln(x):
    mu = jnp.mean(x, -1, keepdims=True)
    xc = x - mu
    var = jnp.mean(xc * xc, -1, keepdims=True)
    return xc * lax.rsqrt(var + LN_EPS)


def _bdot(a, b):
    return jnp.dot(a, b, preferred_element_type=F32)


def _mod_kernel(c_ref, w_ref, b_ref, o_ref):
    c = c_ref[...]
    s = c / (1.0 + jnp.exp(-c))
    o_ref[...] = jnp.dot(s, w_ref[...], preferred_element_type=F32, precision=HI) + b_ref[...]


def _modulation(cc, w_mod, b_mod):
    depth, d, n = w_mod.shape
    r = cc.shape[0]
    tn = 1024
    return pl.pallas_call(
        _mod_kernel,
        grid=(depth, n // tn),
        in_specs=[pl.BlockSpec((r, d), lambda l, j: (0, 0)),
                  pl.BlockSpec((None, d, tn), lambda l, j: (l, 0, j)),
                  pl.BlockSpec((None, 1, tn), lambda l, j: (l, 0, j))],
        out_specs=pl.BlockSpec((None, r, tn), lambda l, j: (l, 0, j)),
        out_shape=jax.ShapeDtypeStruct((depth, r, n), F32),
        compiler_params=_cparams(("arbitrary", "arbitrary")),
        name="modulation",
    )(cc, w_mod, b_mod.reshape(depth, 1, n))


def _rope(x, c, sa, sb, half):
    return x * c + pltpu.roll(x, LANE - half, 1) * sa + pltpu.roll(x, half, 1) * sb


def _inproj_kernel(x_ref, xp_ref, xn_ref, sh_ref, sc_ref, win_ref, gq_ref, wq_ref, gkv_ref, wkv_ref, tab_ref,
                   cw_ref, cb_ref, qm_ref, km_ref, vm_ref, dq_ref, dk_ref, dv_ref, hy_ref):
    x = jnp.concatenate([x_ref[0], xp_ref[0], xn_ref[0]], axis=0)
    h = _ln(x) * (1.0 + sc_ref[0]) + sh_ref[0]
    p_all = _bdot(h.astype(BF16), win_ref[...])
    tm = x_ref.shape[1]
    p = p_all[0:tm]
    tab = tab_ref[...]
    cm, sam, sbm = tab[:, 0:LANE], tab[:, LANE:2 * LANE], tab[:, 2 * LANE:3 * LANE]
    cd, sad, sbd = tab[:, 3 * LANE:4 * LANE], tab[:, 4 * LANE:5 * LANE], tab[:, 5 * LANE:6 * LANE]

    cq = p[:, C_CQ:C_CQ + MLA_Q_RANK]
    qn = cq * lax.rsqrt(jnp.mean(cq * cq, -1, keepdims=True) + LN_EPS) * gq_ref[...]
    q = _bdot(qn.astype(BF16), wq_ref[...])
    ckv = p[:, C_CKV:C_CKV + MLA_KV_RANK]
    kvn = ckv * lax.rsqrt(jnp.mean(ckv * ckv, -1, keepdims=True) + LN_EPS) * gkv_ref[...]
    kv = _bdot(kvn.astype(BF16), wkv_ref[...])
    kr = _rope(p[:, C_KR:C_KR + LANE], cm, sam, sbm, MLA_ROPE // 4)
    scale_m = LOG2E * (MLA_NOPE + MLA_ROPE) ** -0.5
    for hd in range(MLA_HEADS):
        sl = slice(hd * LANE, (hd + 1) * LANE)
        qm_ref[0, :, sl] = (_rope(q[:, sl], cm, sam, sbm, MLA_ROPE // 4) * scale_m).astype(BF16)
        km_ref[0, sl, :] = (kv[:, sl] + kr).T.astype(BF16)
    vm_ref[0] = kv[:, W_QM:W_QM + W_VM].astype(BF16)
    scale_d = LOG2E * DIFF_DIM ** -0.5
    for hd in range(DIFF_HEADS):
        sl = slice(hd * LANE, (hd + 1) * LANE)
        dq = p[:, C_DQ + hd * LANE:C_DQ + (hd + 1) * LANE]
        dk = p[:, C_DK + hd * LANE:C_DK + (hd + 1) * LANE]
        dq_ref[0, :, sl] = (_rope(dq, cd, sad, sbd, DIFF_DIM // 4) * scale_d).astype(BF16)
        dk_ref[0, sl, :] = _rope(dk, cd, sad, sbd, DIFF_DIM // 4).T.astype(BF16)
    dv_ref[0] = p[:, C_DV:C_DV + W_D].astype(BF16)
    u = p[:, C_HY:C_HY + W_HY]
    i = pl.program_id(0)
    u_prev = jnp.where(i == 0, 0.0, p_all[tm + SUBLANE - 1:tm + SUBLANE, C_HY:C_HY + W_HY])
    u_next = jnp.where(i == pl.num_programs(0) - 1, 0.0, p_all[tm + SUBLANE:tm + SUBLANE + 1, C_HY:C_HY + W_HY])
    row = lax.broadcasted_iota(jnp.int32, u.shape, 0)
    up = jnp.where(row == 0, u_prev, pltpu.roll(u, 1, 0))
    un = jnp.where(row == tm - 1, u_next, pltpu.roll(u, tm - 1, 0))
    w = cw_ref[...]
    hy_ref[0] = (up * w[0:1] + u * w[1:2] + un * w[2:3] + cb_ref[...]).astype(BF16)


def _in_proj(x, sh, sc, win, gq, wq, gkv, wkv, tab, conv_w, conv_b, tm):
    b, t, d = x.shape
    hb = tm // SUBLANE
    halo = (1, SUBLANE, d)
    resident = pl.Buffered(1)
    per_batch = sh.shape[0] == b and b > 1
    mod_map = (lambda i, bb: (bb, 0, 0)) if per_batch else (lambda i, bb: (0, 0, 0))
    const = lambda i, bb: (0, 0)
    widths = (W_QM, W_QM, W_VM, W_D, W_D, W_D, W_HY)
    transposed = (1, 4)
    return pl.pallas_call(
        _inproj_kernel,
        grid=(t // tm, b),
        in_specs=[pl.BlockSpec((1, tm, d), lambda i, bb: (bb, i, 0)),
                  pl.BlockSpec(halo, lambda i, bb: (bb, jnp.maximum(i * hb - 1, 0), 0)),
                  pl.BlockSpec(halo, lambda i, bb: (bb, jnp.minimum((i + 1) * hb, t // SUBLANE - 1), 0)),
                  pl.BlockSpec((1, 1, d), mod_map),
                  pl.BlockSpec((1, 1, d), mod_map),
                  pl.BlockSpec(win.shape, const, pipeline_mode=resident),
                  pl.BlockSpec(gq.shape, const),
                  pl.BlockSpec(wq.shape, const, pipeline_mode=resident),
                  pl.BlockSpec(gkv.shape, const),
                  pl.BlockSpec(wkv.shape, const, pipeline_mode=resident),
                  pl.BlockSpec((tm, 6 * LANE), lambda i, bb: (i, 0)),
                  pl.BlockSpec(conv_w.shape, const), pl.BlockSpec(conv_b.shape, const)],
        out_specs=[pl.BlockSpec((1, w, tm), lambda i, bb: (bb, 0, i)) if n in transposed else
                   pl.BlockSpec((1, tm, w), lambda i, bb: (bb, i, 0)) for n, w in enumerate(widths)],
        out_shape=[jax.ShapeDtypeStruct((b, w, t) if n in transposed else (b, t, w), BF16)
                   for n, w in enumerate(widths)],
        compiler_params=_cparams(("arbitrary", "arbitrary")),
        name="in_proj",
    )(x, x, x, sh, sc, win, gq, wq, gkv, wkv, tab, conv_w, conv_b)


NEG_BIG = -1e30


def _lane_fold(x, op):
    r = x[:, 0:LANE]
    for i in range(1, x.shape[1] // LANE):
        r = op(r, x[:, i * LANE:(i + 1) * LANE])
    return r


def _score_pass(q, parts, ksl, s_refs):
    mx = jnp.full((q.shape[0], LANE), NEG_BIG, F32)
    for (k_ref, _, n, tk), s_ref in zip(parts, s_refs):
        for j in range(n // tk):
            s = _bdot(q, k_ref[0, ksl, j * tk:(j + 1) * tk])
            s_ref[j] = s
            mx = jnp.maximum(mx, _lane_fold(s, jnp.maximum))
    return jnp.max(mx, axis=-1, keepdims=True)


def _pv_pass(parts, vsl, s_refs, m):
    ls = jnp.zeros((m.shape[0], LANE), F32)
    acc = jnp.zeros((m.shape[0], LANE), F32)
    for (_, v_ref, n, tk), s_ref in zip(parts, s_refs):
        for j in range(n // tk):
            p = jnp.exp2(s_ref[j] - m)
            ls = ls + _lane_fold(p, jnp.add)
            acc = acc + _bdot(p.astype(BF16), v_ref[0, j * tk:(j + 1) * tk, vsl])
    return acc * (1.0 / jnp.sum(ls, axis=-1, keepdims=True))


def _softmax_pv(q, parts, ksl, vsl, s_refs):
    return _pv_pass(parts, vsl, s_refs, _score_pass(q, parts, ksl, s_refs))


def _score_pass2(qq, parts, ksl, s1_refs, s2_refs):
    tq = qq.shape[0] // 2
    mx1 = jnp.full((tq, LANE), NEG_BIG, F32)
    mx2 = jnp.full((tq, LANE), NEG_BIG, F32)
    for (k_ref, _, n, tk), s1_ref, s2_ref in zip(parts, s1_refs, s2_refs):
        for j in range(n // tk):
            s = _bdot(qq, k_ref[0, ksl, j * tk:(j + 1) * tk])
            s1_ref[j] = s[:tq]
            s2_ref[j] = s[tq:]
            mx1 = jnp.maximum(mx1, _lane_fold(s[:tq], jnp.maximum))
            mx2 = jnp.maximum(mx2, _lane_fold(s[tq:], jnp.maximum))
    return jnp.max(mx1, axis=-1, keepdims=True), jnp.max(mx2, axis=-1, keepdims=True)


def _mla_attn_kernel(*refs, part_shapes, n_pairs):
    np_ = len(part_shapes)
    q_ref = refs[0]
    k_refs = refs[1:1 + np_]
    v_refs = refs[1 + np_:1 + 2 * np_]
    o_ref = refs[1 + 2 * np_]
    s_sets = (refs[2 + 2 * np_:2 + 3 * np_], refs[2 + 3 * np_:])
    parts = [(k_refs[i], v_refs[i]) + part_shapes[i] for i in range(np_)]
    for pr in range(n_pairs):
        outs = []
        for hh in range(2):
            hd = 2 * pr + hh
            sl = slice(hd * LANE, (hd + 1) * LANE)
            outs.append(_softmax_pv(q_ref[0, :, sl], parts, sl, slice(pr * LANE, (pr + 1) * LANE), s_sets[hh]))
        lane = lax.broadcasted_iota(jnp.int32, outs[0].shape, 1)
        o_ref[0, :, pr * LANE:(pr + 1) * LANE] = jnp.where(lane < MLA_V, outs[0], outs[1]).astype(BF16)


def _chunk(n):
    return n if n <= 512 else 512


def _mla_attention(q, ks, vs, tq, n_pairs):
    b, t, _ = q.shape
    part_shapes = tuple((k.shape[2], _chunk(k.shape[2])) for k in ks)
    in_specs = [pl.BlockSpec((1, tq, 2 * n_pairs * LANE), lambda bb, g, i: (bb, i, g))]
    in_specs += [pl.BlockSpec((1, 2 * n_pairs * LANE, k.shape[2]), lambda bb, g, i: (bb, g, 0)) for k in ks]
    in_specs += [pl.BlockSpec((1, v.shape[1], n_pairs * LANE), lambda bb, g, i: (bb, 0, g)) for v in vs]
    scratch = [pltpu.VMEM((n // tk, tq, tk), F32) for n, tk in part_shapes] * 2
    return pl.pallas_call(
        functools.partial(_mla_attn_kernel, part_shapes=part_shapes, n_pairs=n_pairs),
        grid=(b, MLA_HEADS // (2 * n_pairs), t // tq),
        in_specs=in_specs,
        out_specs=pl.BlockSpec((1, tq, n_pairs * LANE), lambda bb, g, i: (bb, i, g)),
        out_shape=jax.ShapeDtypeStruct((b, t, W_VM), BF16),
        scratch_shapes=scratch,
        compiler_params=_cparams(("arbitrary", "arbitrary", "arbitrary")),
        name="mla_attention",
    )(q, *ks, *vs)


def _diff_attn_kernel(*refs, part_shapes, lam_init, n_heads):
    np_ = len(part_shapes)
    q_ref = refs[0]
    k_refs = refs[1:1 + np_]
    v_refs = refs[1 + np_:1 + 2 * np_]
    lam_ref, g_ref, o_ref = refs[1 + 2 * np_:4 + 2 * np_]
    s_sets = (refs[4 + 2 * np_:4 + 3 * np_], refs[4 + 3 * np_:])
    parts = [(k_refs[i], v_refs[i]) + part_shapes[i] for i in range(np_)]
    lp = lam_ref[...]
    lam = (jnp.exp(jnp.sum(lp[0:1] * lp[1:2], axis=-1, keepdims=True))
           - jnp.exp(jnp.sum(lp[2:3] * lp[3:4], axis=-1, keepdims=True)) + lam_init)
    for hd in range(n_heads):
        sl = slice(hd * LANE, (hd + 1) * LANE)
        q = q_ref[0, :, sl]
        lane = lax.broadcasted_iota(jnp.int32, q.shape, 1)
        zero = jnp.zeros_like(q)
        qq = jnp.concatenate([jnp.where(lane < LANE // 2, q, zero), jnp.where(lane >= LANE // 2, q, zero)], axis=0)
        m1, m2 = _score_pass2(qq, parts, sl, s_sets[0], s_sets[1])
        o1 = _pv_pass(parts, sl, s_sets[0], m1)
        o2 = _pv_pass(parts, sl, s_sets[1], m2)
        o = o1 - lam * o2
        ms = jnp.sum(o * o, axis=-1, keepdims=True) * (1.0 / DIFF_V)
        o_ref[0, :, sl] = (o * lax.rsqrt(ms + LN_EPS) * g_ref[...] * (1.0 - lam_init)).astype(BF16)


def _diff_attention(q, ks, vs, lam_p, subln, lam_init, tq, n_heads):
    b, t, _ = q.shape
    part_shapes = tuple((k.shape[2], _chunk(k.shape[2])) for k in ks)
    hmap = lambda bb, g, i: (bb, 0, g)
    in_specs = [pl.BlockSpec((1, tq, n_heads * LANE), lambda bb, g, i: (bb, i, g))]
    in_specs += [pl.BlockSpec((1, n_heads * LANE, k.shape[2]), lambda bb, g, i: (bb, g, 0)) for k in ks]
    in_specs += [pl.BlockSpec((1, v.shape[1], n_heads * LANE), hmap) for v in vs]
    in_specs += [pl.BlockSpec(lam_p.shape, lambda bb, g, i: (0, 0)),
                 pl.BlockSpec(subln.shape, lambda bb, g, i: (0, 0))]
    scratch = [pltpu.VMEM((n // tk, tq, tk), F32) for n, tk in part_shapes] * 2
    return pl.pallas_call(
        functools.partial(_diff_attn_kernel, part_shapes=part_shapes, lam_init=lam_init, n_heads=n_heads),
        grid=(b, DIFF_HEADS // n_heads, t // tq),
        in_specs=in_specs,
        out_specs=pl.BlockSpec((1, tq, n_heads * LANE), lambda bb, g, i: (bb, i, g)),
        out_shape=jax.ShapeDtypeStruct((b, t, W_D), BF16),
        scratch_shapes=scratch,
        compiler_params=_cparams(("arbitrary", "arbitrary", "arbitrary")),
        name="diff_attention",
    )(q, *ks, *vs, lam_p, subln)


def _hyfilt_kernel(emb_ref, win_ref, fw1_ref, fb1_ref, fw2_ref, fb2_ref, fwf_ref, fbf_ref, fwb_ref, fbb_ref,
                   d_ref, h_scr):
    @pl.when(pl.program_id(0) == 0)
    def _():
        h1 = jnp.sin(jnp.dot(emb_ref[...], fw1_ref[...], preferred_element_type=F32, precision=HI) + fb1_ref[...])
        h_scr[...] = jnp.sin(jnp.dot(h1, fw2_ref[...], preferred_element_type=F32, precision=HI) + fb2_ref[...])

    h = h_scr[...]
    w = win_ref[...]
    fwd = (jnp.dot(h, fwf_ref[...], preferred_element_type=F32, precision=HI) + fbf_ref[...]) * w
    bwd = (jnp.dot(h, fwb_ref[...], preferred_element_type=F32, precision=HI) + fbb_ref[...]) * w
    row = lax.broadcasted_iota(jnp.int32, bwd.shape, 0)
    bwd = jnp.where(row == 0, 0.0, bwd)
    norm = jnp.sum(jnp.abs(fwd), axis=0, keepdims=True) + jnp.sum(jnp.abs(bwd), axis=0, keepdims=True)
    inv = 1.0 / norm
    d_ref[0] = ((fwd + bwd) * inv).astype(BF16)
    d_ref[1] = ((fwd - bwd) * inv).astype(BF16)


def _hy_filters(emb, win, fw1, fb1, fw2, fb2, fw3, fb3):
    l = emb.shape[0]
    hid = fw2.shape[0]
    nblk = HY_CH // LANE
    const = lambda g: (0, 0)
    fcol = lambda g: (0, (g // nblk) * 2 * nblk + g % nblk)
    bcol = lambda g: (0, (g // nblk) * 2 * nblk + nblk + g % nblk)
    return pl.pallas_call(
        _hyfilt_kernel,
        grid=(HY_ORDER * nblk,),
        in_specs=[pl.BlockSpec(emb.shape, const),
                  pl.BlockSpec((l, LANE), lambda g: (0, g % nblk)),
                  pl.BlockSpec(fw1.shape, const), pl.BlockSpec(fb1.shape, const),
                  pl.BlockSpec(fw2.shape, const), pl.BlockSpec(fb2.shape, const),
                  pl.BlockSpec((hid, LANE), fcol), pl.BlockSpec((1, LANE), fcol),
                  pl.BlockSpec((hid, LANE), bcol), pl.BlockSpec((1, LANE), bcol)],
        out_specs=pl.BlockSpec((2, None, l, LANE), lambda g: (0, g // nblk, 0, g % nblk)),
        out_shape=jax.ShapeDtypeStruct((2, HY_ORDER, l, HY_CH), BF16),
        scratch_shapes=[pltpu.VMEM((l, hid), F32)],
        compiler_params=_cparams(("arbitrary",)),
        name="hyena_filters",
    )(emb, win, fw1, fb1, fw2, fb2, fw3, fb3, fw3, fb3)


def _hyspec_kernel(f_ref, s_ref, d_ref, o_ref):
    tf = f_ref.shape[0] // 2
    o_ref[0:tf, :] = _bdot(f_ref[0:tf, :], s_ref[...])
    o_ref[tf:2 * tf, :] = _bdot(f_ref[tf:2 * tf, :], d_ref[...])


def _hy_spectrum(fmat, filt, tf):
    rows, l = fmat.shape
    return pl.pallas_call(
        _hyspec_kernel,
        grid=(rows // (2 * tf), HY_ORDER),
        in_specs=[pl.BlockSpec((2 * tf, l), lambda j, o: (j, 0)),
                  pl.BlockSpec((None, None, l, HY_CH), lambda j, o: (0, o, 0, 0)),
                  pl.BlockSpec((None, None, l, HY_CH), lambda j, o: (1, o, 0, 0))],
        out_specs=pl.BlockSpec((2 * tf, HY_CH), lambda j, o: (j, o)),
        out_shape=jax.ShapeDtypeStruct((rows, HY_ORDER * HY_CH), F32),
        compiler_params=_cparams(("arbitrary", "arbitrary")),
        name="hyena_filter_spectrum",
    )(fmat, filt, filt)


def _hyfwd_kernel(f_ref, z_ref, k_ref, y_ref):
    tf = f_ref.shape[0] // 2
    s = _bdot(f_ref[...], z_ref[0])
    sr, si = s[0:tf], s[tf:2 * tf]
    kr, ki = k_ref[0:tf, :], k_ref[tf:2 * tf, :]
    y_ref[0, 0:tf, :] = (sr * kr - si * ki).astype(BF16)
    y_ref[0, tf:2 * tf, :] = (sr * ki + si * kr).astype(BF16)


def _hy_forward(fmat, z, kspec, order, tf):
    rows, l = fmat.shape
    b = z.shape[0]
    return pl.pallas_call(
        _hyfwd_kernel,
        grid=(rows // (2 * tf), b),
        in_specs=[pl.BlockSpec((2 * tf, l), lambda j, bb: (j, 0)),
                  pl.BlockSpec((1, l, HY_CH), lambda j, bb: (bb, 0, 0)),
                  pl.BlockSpec((2 * tf, HY_CH), lambda j, bb: (j, order))],
        out_specs=pl.BlockSpec((1, 2 * tf, HY_CH), lambda j, bb: (bb, j, 0)),
        out_shape=jax.ShapeDtypeStruct((b, rows, HY_CH), BF16),
        compiler_params=_cparams(("arbitrary", "arbitrary")),
        name="hyena_dft_forward",
    )(fmat, z, kspec)


def _hyinv_kernel(g_ref, y_ref, z_ref, x_ref, b_ref, o_ref):
    conv = _bdot(g_ref[...], y_ref[0])
    z = z_ref[0].astype(F32)
    o_ref[0] = (x_ref[0].astype(F32) * (conv + z * b_ref[...])).astype(BF16)


def _hy_inverse(gmat, y, z, gate, bias, tt):
    l, rows = gmat.shape
    b = y.shape[0]
    return pl.pallas_call(
        _hyinv_kernel,
        grid=(l // tt, b),
        in_specs=[pl.BlockSpec((tt, rows), lambda i, bb: (i, 0)),
                  pl.BlockSpec((1, rows, HY_CH), lambda i, bb: (bb, 0, 0)),
                  pl.BlockSpec((1, tt, HY_CH), lambda i, bb: (bb, i, 0)),
                  pl.BlockSpec((1, tt, HY_CH), lambda i, bb: (bb, i, 0)),
                  pl.BlockSpec((1, HY_CH), lambda i, bb: (0, 0))],
        out_specs=pl.BlockSpec((1, tt, HY_CH), lambda i, bb: (bb, i, 0)),
        out_shape=jax.ShapeDtypeStruct((b, l, HY_CH), BF16),
        compiler_params=_cparams(("arbitrary", "arbitrary")),
        name="hyena_dft_inverse",
    )(gmat, y, z, gate, bias)


def _dft_tables(l, tf):
    k = jnp.arange(l, dtype=jnp.int32)
    n = jnp.arange(l, dtype=jnp.int32)
    ph = ((2 * k + 1)[:, None] * n[None, :]) % (4 * l)
    ang = ph.astype(F32) * (2.0 * math.pi / (4 * l))
    c = jnp.cos(ang).reshape(l // tf, 1, tf, l)
    s = (-jnp.sin(ang)).reshape(l // tf, 1, tf, l)
    f = jnp.concatenate([c, s], axis=1).reshape(2 * l, l)
    return f.astype(BF16), (f.T * (1.0 / l)).astype(BF16)


def _hy_consts(l):
    t = jnp.arange(l, dtype=F32)
    bands = jnp.arange(1, HY_BANDS + 1, dtype=F32)
    ang = (2.0 * math.pi / l) * t[:, None] * bands[None, :]
    emb = jnp.concatenate([(t / l)[:, None], jnp.cos(ang), jnp.sin(ang)], -1)
    emb = jnp.pad(emb, ((0, 0), (0, LANE - HY_EMB)))
    rates = jnp.linspace(HY_MIN_RATE, HY_MAX_RATE, HY_CH, dtype=F32)
    win = jnp.exp(-(t / l)[:, None] * rates[None, :])
    return emb, win


HY_N1 = 16
HY_J = 16
HY_GROUPS = 8
HY_KG = 16


def _hy2_tables(l):
    n1h, jj = HY_N1, HY_J
    n2, k1n = l // n1h, 2 * n1h
    k2n = n2 // 2
    k1 = jnp.arange(k1n, dtype=jnp.int32)
    ph1 = ((2 * k1 + 1)[:, None] * jnp.arange(n1h, dtype=jnp.int32)[None, :]) % (4 * n1h)
    a1 = ph1.astype(F32) * (2.0 * math.pi / (4 * n1h))
    f1 = jnp.stack([jnp.cos(a1), -jnp.sin(a1)], axis=1)
    f1big = jnp.einsum('krn,ab->kranb', f1, jnp.eye(jj, dtype=F32)).reshape(k1n * 2 * jj, n1h * jj)
    kk = k1[:, None] + k1n * jnp.arange(k2n, dtype=jnp.int32)[None, :]
    ph2 = ((2 * kk + 1)[:, :, None] * jnp.arange(n2, dtype=jnp.int32)[None, None, :]) % (4 * l)
    a2 = ph2.astype(F32) * (2.0 * math.pi / (4 * l))
    mr, mi = jnp.cos(a2), -jnp.sin(a2)
    f2t = jnp.concatenate([jnp.concatenate([mr, -mi], axis=2), jnp.concatenate([mi, mr], axis=2)], axis=1)
    return (f1big.astype(BF16), (f1big.T * (1.0 / l)).astype(BF16), f2t.astype(BF16),
            jnp.swapaxes(f2t, 1, 2).astype(BF16))


def _hy2_s1_kernel(f_ref, z_ref, o_ref):
    n1h, _, c = z_ref.shape[1:]
    k1n = o_ref.shape[1]
    for g in range(HY_GROUPS):
        cols = slice(g * HY_J, (g + 1) * HY_J)
        z = z_ref[0, :, cols, :].reshape(n1h * HY_J, c)
        a = _bdot(f_ref[...], z)
        o_ref[0, :, :, cols, :] = a.astype(BF16).reshape(k1n, 2, HY_J, c)


def _hy2_stage1(f1big, z5):
    b, n1h, n2, c = z5.shape
    k1n = 2 * n1h
    w = HY_J * HY_GROUPS
    return pl.pallas_call(
        _hy2_s1_kernel,
        grid=(b, n2 // w),
        in_specs=[pl.BlockSpec(f1big.shape, lambda bb, g: (0, 0)),
                  pl.BlockSpec((1, n1h, w, c), lambda bb, g: (bb, 0, g, 0))],
        out_specs=pl.BlockSpec((1, k1n, 2, w, c), lambda bb, g: (bb, 0, 0, g, 0)),
        out_shape=jax.ShapeDtypeStruct((b, k1n, 2, n2, c), BF16),
        compiler_params=_cparams(("arbitrary", "arbitrary")),
        name="hyena_stage1",
    )(f1big, z5)


def _hy2_fspec_kernel(f_ref, s_ref, d_ref, o_ref):
    h = o_ref.shape[1] // 2
    for i in range(HY_KG):
        o_ref[i, 0:h, :] = _bdot(f_ref[i], s_ref[0, i])[0:h]
        o_ref[i, h:2 * h, :] = _bdot(f_ref[i], d_ref[0, i])[h:2 * h]


def _hy2_filter_spectrum(f2t, a5):
    nsig, k1n, _, n2, c = a5.shape
    a4 = a5.reshape(nsig, k1n, 2 * n2, c)
    return pl.pallas_call(
        _hy2_fspec_kernel,
        grid=(k1n // HY_KG, HY_ORDER),
        in_specs=[pl.BlockSpec((HY_KG, n2, 2 * n2), lambda g, o: (g, 0, 0)),
                  pl.BlockSpec((1, HY_KG, 2 * n2, c), lambda g, o: (o, g, 0, 0)),
                  pl.BlockSpec((1, HY_KG, 2 * n2, c), lambda g, o: (HY_ORDER + o, g, 0, 0))],
        out_specs=pl.BlockSpec((HY_KG, n2, c), lambda g, o: (g, 0, o)),
        out_shape=jax.ShapeDtypeStruct((k1n, n2, HY_ORDER * c), F32),
        compiler_params=_cparams(("arbitrary", "arbitrary")),
        name="hyena_filter_spectrum2",
    )(f2t, a4, a4)


def _hy2_conv_kernel(f1_ref, g1_ref, f2_ref, g2_ref, k_ref, z_ref, x_ref, b_ref, o_ref, a_scr):
    n1h, n2, c = z_ref.shape[1:]
    k1n = a_scr.shape[0]
    groups = [slice(g * HY_J, (g + 1) * HY_J) for g in range(n2 // HY_J)]
    for cols in groups:
        z = z_ref[0, :, cols, :].reshape(n1h * HY_J, c)
        a_scr[:, :, cols, :] = _bdot(f1_ref[...], z).astype(BF16).reshape(k1n, 2, HY_J, c)
    h = n2 // 2
    for i in range(k1n):
        y = _bdot(f2_ref[i], a_scr[i].reshape(2 * n2, c))
        yr, yi = y[:h], y[h:]
        kr, ki = k_ref[i, 0:h, :], k_ref[i, h:2 * h, :]
        p = jnp.concatenate([yr * kr - yi * ki, yr * ki + yi * kr], axis=0).astype(BF16)
        a_scr[i] = _bdot(g2_ref[i], p).astype(BF16).reshape(2, n2, c)
    for cols in groups:
        cc = a_scr[:, :, cols, :].reshape(k1n * 2 * HY_J, c)
        conv = _bdot(g1_ref[...], cc)
        z = z_ref[0, :, cols, :].reshape(n1h * HY_J, c).astype(F32)
        x = x_ref[0, :, cols, :].reshape(n1h * HY_J, c).astype(F32)
        o_ref[0, :, cols, :] = (x * (conv + z * b_ref[...])).astype(BF16).reshape(n1h, HY_J, c)


def _hy2_conv(tabs, kspec, order, zsrc, zwhich, vxx5, xwhich, bias):
    f1big, g1big, f2t, g2t = tabs
    b, n1h, n2, _ = zsrc.shape
    c = HY_CH
    k1n = 2 * n1h
    resident = pl.Buffered(1)
    res = lambda a: pl.BlockSpec(a.shape, lambda bb: (0,) * a.ndim, pipeline_mode=resident)
    col = lambda which: pl.BlockSpec((1, n1h, n2, c), lambda bb: (bb, 0, 0, which))
    return pl.pallas_call(
        _hy2_conv_kernel,
        grid=(b,),
        in_specs=[res(f1big), res(g1big), res(f2t), res(g2t),
                  pl.BlockSpec((k1n, n2, c), lambda bb: (0, 0, order), pipeline_mode=resident),
                  col(zwhich), col(xwhich), pl.BlockSpec((1, c), lambda bb: (0, 0))],
        out_specs=col(0),
        out_shape=jax.ShapeDtypeStruct((b, n1h, n2, c), BF16),
        scratch_shapes=[pltpu.VMEM((k1n, 2, n2, c), BF16)],
        compiler_params=_cparams(("arbitrary",)),
        name="hyena_conv",
    )(f1big, g1big, f2t, g2t, kspec, zsrc, vxx5, bias)


def _hyena(vxx, consts, fw1, fb1, fw2, fb2, fw3, fb3, hbias):
    emb, win, tabs = consts
    filt = _hy_filters(emb, win, fw1, fb1, fw2, fb2, fw3, fb3)
    b, l, _ = vxx.shape
    c = HY_CH
    if len(tabs) == 3:
        fmat, gmat, tf = tabs
        kspec = _hy_spectrum(fmat, filt, tf)
        v, x1, x2 = vxx[..., 0:c], vxx[..., c:2 * c], vxx[..., 2 * c:3 * c]
        z = _hy_inverse(gmat, _hy_forward(fmat, v, kspec, 0, tf), v, x1, hbias[0:1], tf)
        return _hy_inverse(gmat, _hy_forward(fmat, z, kspec, 1, tf), z, x2, hbias[1:2], tf)
    f1big, g1big, f2t, g2t = tabs
    n2 = l // HY_N1
    filt5 = filt.reshape(2 * HY_ORDER, HY_N1, n2, c)
    kspec = _hy2_filter_spectrum(f2t, _hy2_stage1(f1big, filt5))
    vxx5 = vxx.reshape(b, HY_N1, n2, 3 * c)
    z5 = _hy2_conv(tabs, kspec, 0, vxx5, 0, vxx5, 1, hbias[0:1])
    return _hy2_conv(tabs, kspec, 1, z5, 0, vxx5, 2, hbias[1:2]).reshape(b, l, c)


def _mix_ffn_kernel(om_ref, od_ref, oh_ref, x_ref, mod_ref, wm_ref, wd_ref, wh_ref, w1_ref, b1_ref, w2_ref,
                    vec_ref, o_ref, *, alpha, tm, tf):
    g1, sh2, sc2, g2 = (mod_ref[0, i:i + 1, :] for i in range(4))
    bo, lg1, lb1, b2, lg2, lb2 = (vec_ref[i:i + 1, :] for i in range(6))
    nt = x_ref.shape[1] // tm
    rows = [slice(t * tm, (t + 1) * tm) for t in range(nt)]
    xs, hs = [], []
    for r in rows:
        y = _bdot(om_ref[0, r, :], wm_ref[...]) + _bdot(od_ref[0, r, :], wd_ref[...]) + _bdot(oh_ref[0, r, :], wh_ref[...])
        x1 = _ln(alpha * x_ref[0, r, :] + g1 * (y + bo)) * lg1 + lb1
        xs.append(x1)
        hs.append((_ln(x1) * (1.0 + sc2) + sh2).astype(BF16))
    accs = [None] * nt
    for c in range(w1_ref.shape[1] // tf):
        cols = slice(c * tf, (c + 1) * tf)
        for t in range(nt):
            a = jnp.maximum(_bdot(hs[t], w1_ref[:, cols]) + b1_ref[:, cols], 0.0)
            y = _bdot((a * a).astype(BF16), w2_ref[cols, :])
            accs[t] = y if accs[t] is None else accs[t] + y
    for t in range(nt):
        z = alpha * xs[t] + g2 * (accs[t] + b2)
        o_ref[0, rows[t], :] = _ln(z) * lg2 + lb2


def _mix_ffn(om, od, oh, x, mod, wm, wd, wh, w1, b1, w2, vec, alpha, tm, nt, tf):
    b, t, d = x.shape
    per_batch = mod.shape[0] == b and b > 1
    gmap = (lambda bb, i: (bb, 0, 0)) if per_batch else (lambda bb, i: (0, 0, 0))
    const = lambda bb, i: (0, 0)
    row = lambda bb, i: (bb, i, 0)
    resident = pl.Buffered(1)
    blk = lambda a: pl.BlockSpec((1, tm * nt, a.shape[2]), row)
    res = lambda a: pl.BlockSpec(a.shape, const, pipeline_mode=resident)
    return pl.pallas_call(
        functools.partial(_mix_ffn_kernel, alpha=alpha, tm=tm, tf=tf),
        grid=(b, t // (tm * nt)),
        in_specs=[blk(om), blk(od), blk(oh), blk(x), pl.BlockSpec((1,) + mod.shape[1:], gmap),
                  res(wm), res(wd), res(wh), res(w1), pl.BlockSpec(b1.shape, const), res(w2),
                  pl.BlockSpec(vec.shape, const)],
        out_specs=blk(x),
        out_shape=jax.ShapeDtypeStruct((b, t, d), F32),
        compiler_params=_cparams(("arbitrary", "arbitrary")),
        name="mix_ffn",
    )(om, od, oh, x, mod, wm, wd, wh, w1, b1, w2, vec)


def _rope_lane_plan(starts, dims):
    h = dims // 2
    inv = ROPE_BASE ** (-(np.arange(h // 2, dtype=np.float64) * (2.0 / h)))
    fr, fc, lo, hi = (np.zeros(LANE, np.float32) for _ in range(4))
    for st in starts:
        for axis_i, f in enumerate((fr, fc)):
            base = st + axis_i * h
            f[base:base + h // 2] = inv
            f[base + h // 2:base + h] = inv
            lo[base:base + h // 2] = 1.0
            hi[base + h // 2:base + h] = 1.0
    return fr, fc, lo, hi


def _rope_tables(t_len, roped):
    pos_t = jnp.arange(t_len)
    scale = 1.0 if roped else 0.0
    row = (pos_t // GRID_W).astype(F32)[:, None] * scale
    col = (pos_t % GRID_W).astype(F32)[:, None] * scale
    out = []
    for starts, dims in (([MLA_NOPE], MLA_ROPE), ([0, LANE // 2], DIFF_DIM)):
        fr, fc, lo, hi = _rope_lane_plan(starts, dims)
        ang = row * fr[None, :] + col * fc[None, :]
        sin = jnp.sin(ang)
        out += [jnp.cos(ang), -sin * lo[None, :], sin * hi[None, :]]
    return jnp.concatenate(out, axis=1)


def _prep_weights(w_in, w_q_up, w_kv_up, w_out, diff_lambda, diff_subln):
    depth, d, _ = w_in.shape
    s1, s2 = IN_MLA, IN_MLA + IN_DIFF
    pad_last = lambda a, n: jnp.pad(a, [(0, 0)] * (a.ndim - 1) + [(0, n - a.shape[-1])])
    kr = jnp.pad(w_in[..., C_KR:s1], ((0, 0), (0, 0), (MLA_NOPE, LANE - MLA_NOPE - MLA_ROPE)))
    dq = pad_last(w_in[..., s1:s1 + N_DQ].reshape(depth, d, DIFF_HEADS, 2, DIFF_DIM), LANE // 2)
    dk = pad_last(w_in[..., s1 + N_DQ:s1 + 2 * N_DQ].reshape(depth, d, DIFF_HEADS, 2, DIFF_DIM), LANE // 2)
    dv = pad_last(w_in[..., s1 + 2 * N_DQ:s2].reshape(depth, d, DIFF_HEADS, DIFF_V), LANE)
    win = jnp.concatenate([w_in[..., :C_KR], kr, dq.reshape(depth, d, W_D), dk.reshape(depth, d, W_D),
                           dv.reshape(depth, d, W_D), w_in[..., s2:]], axis=-1).astype(BF16)
    wq = pad_last(w_q_up.reshape(depth, MLA_Q_RANK, MLA_HEADS, MLA_NOPE + MLA_ROPE), LANE)
    wq = wq.reshape(depth, MLA_Q_RANK, W_QM).astype(BF16)
    kv = w_kv_up.reshape(depth, MLA_KV_RANK, MLA_HEADS, MLA_NOPE + MLA_V)
    wk = pad_last(kv[..., :MLA_NOPE], LANE).reshape(depth, MLA_KV_RANK, W_QM)
    wv = kv[..., MLA_NOPE:].reshape(depth, MLA_KV_RANK, W_VM)
    wkv = jnp.concatenate([wk, wv], axis=-1).astype(BF16)
    wo_m = w_out[:, :W_VM].astype(BF16)
    wo_d = w_out[:, W_VM:W_VM + DIFF_HEADS * DIFF_V].reshape(depth, DIFF_HEADS, DIFF_V, -1)
    wo_d = jnp.pad(wo_d, ((0, 0), (0, 0), (0, LANE - DIFF_V), (0, 0))).reshape(depth, W_D, -1).astype(BF16)
    wo_h = w_out[:, W_VM + DIFF_HEADS * DIFF_V:].astype(BF16)
    lam = pad_last(diff_lambda.astype(F32), LANE)
    subln = pad_last(diff_subln.astype(F32), LANE)[:, None, :]
    return win, wq, wkv, wo_m, wo_d, wo_h, lam, subln


def kernel(x, c, ctx, c_ctx, w_mod, b_mod, w_in, mla_q_norm, w_q_up, mla_kv_norm, w_kv_up, diff_lambda, diff_subln, hy_conv_w, hy_conv_b, hy_fw1, hy_fb1, hy_fw2, hy_fb2, hy_fw3, hy_fb3, hy_bias, w_out, b_out, ln1_g, ln1_b, w_ff1, b_ff1, w_ff2, b_ff2, ln2_g, ln2_b):
    bsz, seq, d = x.shape
    n_ctx = ctx.shape[1]
    depth = w_in.shape[0]
    alpha = (2.0 * depth) ** 0.25

    rpad = -(bsz + 1) % SUBLANE
    cc = jnp.concatenate([c, c_ctx[None, :], jnp.zeros((rpad, d), F32)], axis=0)
    mod_all = _modulation(cc, w_mod, b_mod)

    win, wq, wkv, wo_m, wo_d, wo_h, lam_p, subln = _prep_weights(w_in, w_q_up, w_kv_up, w_out, diff_lambda, diff_subln)
    w1 = w_ff1.astype(BF16)
    w2 = w_ff2.astype(BF16)
    fw1 = jnp.pad(hy_fw1, ((0, 0), (0, LANE - HY_EMB), (0, 0)))
    tab_x = _rope_tables(seq, True)
    tab_c = _rope_tables(n_ctx, False)
    def hy_tables(l):
        if l >= 1024 and l % (HY_N1 * HY_J * HY_GROUPS) == 0:
            return _hy2_tables(l)
        return _dft_tables(l, min(512, l)) + (min(512, l),)

    hyc_x = _hy_consts(seq) + (hy_tables(seq),)
    hyc_c = _hy_consts(n_ctx) + (hy_tables(n_ctx),)
    tm_x, tm_c = min(512, seq), min(512, n_ctx)
    tmi_x = min(1024, seq)
    ntf_x = 2 if seq % (2 * tm_x) == 0 else 1
    tq_x, tq_c = min(256, seq), min(256, n_ctx)
    tff = min(1024, w1.shape[2])
    row2 = lambda a: a.reshape(1, -1)

    for layer in range(depth):
        need_ctx = layer < depth - 1
        lam_init = 0.8 - 0.6 * math.exp(-0.3 * layer)
        mod = mod_all[layer, :bsz].reshape(bsz, 1, N_MOD, d)
        modc = mod_all[layer, bsz:bsz + 1].reshape(1, 1, N_MOD, d)
        sh1, sc1 = mod[:, :, 0], mod[:, :, 1]
        csh1, csc1 = modc[:, :, 0], modc[:, :, 1]
        gq, gkv = row2(mla_q_norm[layer]), row2(mla_kv_norm[layer])
        hy_args = (fw1[layer], row2(hy_fb1[layer]), hy_fw2[layer], row2(hy_fb2[layer]), hy_fw3[layer],
                   row2(hy_fb3[layer]), hy_bias[layer])
        conv = (hy_conv_w[layer], row2(hy_conv_b[layer]))

        qm, km, vm, dq, dk, dv, uh = _in_proj(x, sh1, sc1, win[layer], gq, wq[layer], gkv, wkv[layer], tab_x, *conv, tmi_x)
        cqm, ckm, cvm, cdq, cdk, cdv, cuh = _in_proj(ctx, csh1, csc1, win[layer], gq, wq[layer], gkv, wkv[layer],
                                                     tab_c, *conv, tm_c)
        om = _mla_attention(qm, [ckm, km], [cvm, vm], tq_x, MLA_HEADS // 2)
        od = _diff_attention(dq, [cdk, dk], [cdv, dv], lam_p[layer], subln[layer], lam_init, tq_x, DIFF_HEADS)
        oh = _hyena(uh, hyc_x, *hy_args)
        vec = jnp.stack([b_out[layer], ln1_g[layer], ln1_b[layer], b_ff2[layer], ln2_g[layer], ln2_b[layer]])
        mix_w = (wo_m[layer], wo_d[layer], wo_h[layer], w1[layer], row2(b_ff1[layer]), w2[layer], vec)
        x = _mix_ffn(om, od, oh, x, mod[:, 0, 2:6], *mix_w, alpha, tm_x, ntf_x, tff)
        if need_ctx:
            com = _mla_attention(cqm, [ckm], [cvm], tq_c, MLA_HEADS // 2)
            cod = _diff_attention(cdq, [cdk], [cdv], lam_p[layer], subln[layer], lam_init, tq_c, DIFF_HEADS)
            coh = _hyena(cuh, hyc_c, *hy_args)
            if (bsz * n_ctx) % (tm_x * ntf_x) == 0:
                flat = lambda a: a.reshape(1, bsz * n_ctx, a.shape[2])
                ctx = _mix_ffn(flat(com), flat(cod), flat(coh), flat(ctx), modc[:, 0, 2:6], *mix_w, alpha,
                               tm_x, ntf_x, tff).reshape(bsz, n_ctx, d)
            else:
                ctx = _mix_ffn(com, cod, coh, ctx, modc[:, 0, 2:6], *mix_w, alpha, tm_c, 1, tff)
    return x
```

```python
import functools
import math

import jax
import numpy as np
import jax.numpy as jnp
from jax import lax
from jax.experimental import pallas as pl
from jax.experimental.pallas import tpu as pltpu

F32 = jnp.float32
BF16 = jnp.bfloat16
HI = lax.Precision.HIGHEST

GRID_W = 64
ROPE_BASE = 10000.0
LN_EPS = 1e-6
MLA_HEADS, MLA_NOPE, MLA_ROPE, MLA_V = 6, 64, 32, 64
MLA_Q_RANK, MLA_KV_RANK = 256, 128
DIFF_HEADS, DIFF_DIM = 4, 48
DIFF_V = 2 * DIFF_DIM
HY_CH, HY_ORDER, HY_BANDS = 256, 2, 16
HY_EMB = 1 + 2 * HY_BANDS
HY_TARGET, HY_FAST_DECAY_PCT, HY_SLOW_DECAY_PCT = 1e-2, 0.3, 1.5
HY_MIN_RATE = -math.log(HY_TARGET) / HY_SLOW_DECAY_PCT
HY_MAX_RATE = -math.log(HY_TARGET) / HY_FAST_DECAY_PCT
N_MOD = 6

LOG2E = 1.0 / math.log(2.0)
LANE = 128
SUBLANE = 8
VMEM_LIMIT = 56 * 1024 * 1024

IN_MLA = MLA_Q_RANK + MLA_KV_RANK + MLA_ROPE
N_DQ = DIFF_HEADS * 2 * DIFF_DIM
IN_DIFF = 2 * N_DQ + DIFF_HEADS * DIFF_V
W_QM = MLA_HEADS * LANE
W_VM = MLA_HEADS * MLA_V
W_D = DIFF_HEADS * LANE
W_HY = (HY_ORDER + 1) * HY_CH
C_CQ, C_CKV, C_KR = 0, MLA_Q_RANK, MLA_Q_RANK + MLA_KV_RANK
C_DQ = C_KR + LANE
C_DK = C_DQ + W_D
C_DV = C_DK + W_D
C_HY = C_DV + W_D
W_IN = C_HY + W_HY


def _cparams(sem):
    return pltpu.CompilerParams(dimension_semantics=sem, vmem_limit_bytes=VMEM_LIMIT)


def _ln(x):
    mu = jnp.mean(x, -1, keepdims=True)
    xc = x - mu
    var = jnp.mean(xc * xc, -1, keepdims=True)
    return xc * lax.rsqrt(var + LN_EPS)


def _bdot(a, b):
    return jnp.dot(a, b, preferred_element_type=F32)


def _mod_kernel(c_ref, w_ref, b_ref, o_ref):
    c = c_ref[...]
    s = c / (1.0 + jnp.exp(-c))
    o_ref[...] = jnp.dot(s, w_ref[...], preferred_element_type=F32, precision=HI) + b_ref[...]


def _modulation(cc, w_mod, b_mod):
    depth, d, n = w_mod.shape
    r = cc.shape[0]
    tn = 1024
    return pl.pallas_call(
        _mod_kernel,
        grid=(depth, n // tn),
        in_specs=[pl.BlockSpec((r, d), lambda l, j: (0, 0)),
                  pl.BlockSpec((None, d, tn), lambda l, j: (l, 0, j)),
                  pl.BlockSpec((None, 1, tn), lambda l, j: (l, 0, j))],
        out_specs=pl.BlockSpec((None, r, tn), lambda l, j: (l, 0, j)),
        out_shape=jax.ShapeDtypeStruct((depth, r, n), F32),
        compiler_params=_cparams(("arbitrary", "arbitrary")),
        name="modulation",
    )(cc, w_mod, b_mod.reshape(depth, 1, n))


def _rope(x, c, sa, sb, half):
    return x * c + pltpu.roll(x, LANE - half, 1) * sa + pltpu.roll(x, half, 1) * sb


def _inproj_kernel(x_ref, xp_ref, xn_ref, sh_ref, sc_ref, win_ref, gq_ref, wq_ref, gkv_ref, wkv_ref, tab_ref,
                   cw_ref, cb_ref, qm_ref, km_ref, vm_ref, dq_ref, dk_ref, dv_ref, hy_ref):
    x = jnp.concatenate([x_ref[0], xp_ref[0], xn_ref[0]], axis=0)
    h = _ln(x) * (1.0 + sc_ref[0]) + sh_ref[0]
    p_all = _bdot(h.astype(BF16), win_ref[...])
    tm = x_ref.shape[1]
    p = p_all[0:tm]
    tab = tab_ref[...]
    cm, sam, sbm = tab[:, 0:LANE], tab[:, LANE:2 * LANE], tab[:, 2 * LANE:3 * LANE]
    cd, sad, sbd = tab[:, 3 * LANE:4 * LANE], tab[:, 4 * LANE:5 * LANE], tab[:, 5 * LANE:6 * LANE]

    cq = p[:, C_CQ:C_CQ + MLA_Q_RANK]
    qn = cq * lax.rsqrt(jnp.mean(cq * cq, -1, keepdims=True) + LN_EPS) * gq_ref[...]
    q = _bdot(qn.astype(BF16), wq_ref[...])
    ckv = p[:, C_CKV:C_CKV + MLA_KV_RANK]
    kvn = ckv * lax.rsqrt(jnp.mean(ckv * ckv, -1, keepdims=True) + LN_EPS) * gkv_ref[...]
    kv = _bdot(kvn.astype(BF16), wkv_ref[...])
    kr = _rope(p[:, C_KR:C_KR + LANE], cm, sam, sbm, MLA_ROPE // 4)
    scale_m = LOG2E * (MLA_NOPE + MLA_ROPE) ** -0.5
    for hd in range(MLA_HEADS):
        sl = slice(hd * LANE, (hd + 1) * LANE)
        qm_ref[0, :, sl] = (_rope(q[:, sl], cm, sam, sbm, MLA_ROPE // 4) * scale_m).astype(BF16)
        km_ref[0, sl, :] = (kv[:, sl] + kr).astype(BF16).T
    vm_ref[0] = kv[:, W_QM:W_QM + W_VM].astype(BF16)
    scale_d = LOG2E * DIFF_DIM ** -0.5
    for hd in range(DIFF_HEADS):
        sl = slice(hd * LANE, (hd + 1) * LANE)
        dq = p[:, C_DQ + hd * LANE:C_DQ + (hd + 1) * LANE]
        dk = p[:, C_DK + hd * LANE:C_DK + (hd + 1) * LANE]
        dq_ref[0, :, sl] = (_rope(dq, cd, sad, sbd, DIFF_DIM // 4) * scale_d).astype(BF16)
        dk_ref[0, sl, :] = _rope(dk, cd, sad, sbd, DIFF_DIM // 4).astype(BF16).T
    dv_ref[0] = p[:, C_DV:C_DV + W_D].astype(BF16)
    u = p[:, C_HY:C_HY + W_HY]
    i = pl.program_id(0)
    u_prev = jnp.where(i == 0, 0.0, p_all[tm + SUBLANE - 1:tm + SUBLANE, C_HY:C_HY + W_HY])
    u_next = jnp.where(i == pl.num_programs(0) - 1, 0.0, p_all[tm + SUBLANE:tm + SUBLANE + 1, C_HY:C_HY + W_HY])
    row = lax.broadcasted_iota(jnp.int32, u.shape, 0)
    up = jnp.where(row == 0, u_prev, pltpu.roll(u, 1, 0))
    un = jnp.where(row == tm - 1, u_next, pltpu.roll(u, tm - 1, 0))
    w = cw_ref[...]
    hy_ref[0] = (up * w[0:1] + u * w[1:2] + un * w[2:3] + cb_ref[...]).astype(BF16)


def _in_proj(x, sh, sc, win, gq, wq, gkv, wkv, tab, conv_w, conv_b, tm):
    b, t, d = x.shape
    hb = tm // SUBLANE
    halo = (1, SUBLANE, d)
    resident = pl.Buffered(1)
    per_batch = sh.shape[0] == b and b > 1
    mod_map = (lambda i, bb: (bb, 0, 0)) if per_batch else (lambda i, bb: (0, 0, 0))
    const = lambda i, bb: (0, 0)
    widths = (W_QM, W_QM, W_VM, W_D, W_D, W_D, W_HY)
    transposed = (1, 4)
    return pl.pallas_call(
        _inproj_kernel,
        grid=(t // tm, b),
        in_specs=[pl.BlockSpec((1, tm, d), lambda i, bb: (bb, i, 0)),
                  pl.BlockSpec(halo, lambda i, bb: (bb, jnp.maximum(i * hb - 1, 0), 0)),
                  pl.BlockSpec(halo, lambda i, bb: (bb, jnp.minimum((i + 1) * hb, t // SUBLANE - 1), 0)),
                  pl.BlockSpec((1, 1, d), mod_map),
                  pl.BlockSpec((1, 1, d), mod_map),
                  pl.BlockSpec(win.shape, const, pipeline_mode=resident),
                  pl.BlockSpec(gq.shape, const),
                  pl.BlockSpec(wq.shape, const, pipeline_mode=resident),
                  pl.BlockSpec(gkv.shape, const),
                  pl.BlockSpec(wkv.shape, const, pipeline_mode=resident),
                  pl.BlockSpec((tm, 6 * LANE), lambda i, bb: (i, 0)),
                  pl.BlockSpec(conv_w.shape, const), pl.BlockSpec(conv_b.shape, const)],
        out_specs=[pl.BlockSpec((1, w, tm), lambda i, bb: (bb, 0, i)) if n in transposed else
                   pl.BlockSpec((1, tm, w), lambda i, bb: (bb, i, 0)) for n, w in enumerate(widths)],
        out_shape=[jax.ShapeDtypeStruct((b, w, t) if n in transposed else (b, t, w), BF16)
                   for n, w in enumerate(widths)],
        compiler_params=_cparams(("arbitrary", "arbitrary")),
        name="in_proj",
    )(x, x, x, sh, sc, win, gq, wq, gkv, wkv, tab, conv_w, conv_b)


NEG_BIG = -1e30


def _lane_fold(x, op):
    r = x[:, 0:LANE]
    for i in range(1, x.shape[1] // LANE):
        r = op(r, x[:, i * LANE:(i + 1) * LANE])
    return r


def _score_pass(q, parts, ksl, s_refs):
    mx = jnp.full((q.shape[0], LANE), NEG_BIG, F32)
    for (k_ref, _, n, tk), s_ref in zip(parts, s_refs):
        for j in range(n // tk):
            s = _bdot(q, k_ref[0, ksl, j * tk:(j + 1) * tk])
            s_ref[j] = s
            mx = jnp.maximum(mx, _lane_fold(s, jnp.maximum))
    return jnp.max(mx, axis=-1, keepdims=True)


def _pv_pass(parts, vsl, s_refs, m):
    ls = jnp.zeros((m.shape[0], LANE), F32)
    acc = jnp.zeros((m.shape[0], LANE), F32)
    for (_, v_ref, n, tk), s_ref in zip(parts, s_refs):
        for j in range(n // tk):
            p = jnp.exp2(s_ref[j] - m)
            ls = ls + _lane_fold(p, jnp.add)
            acc = acc + _bdot(p.astype(BF16), v_ref[0, j * tk:(j + 1) * tk, vsl])
    return acc * (1.0 / jnp.sum(ls, axis=-1, keepdims=True))


def _softmax_pv(q, parts, ksl, vsl, s_refs):
    return _pv_pass(parts, vsl, s_refs, _score_pass(q, parts, ksl, s_refs))


def _score_pass2(qq, parts, ksl, s1_refs, s2_refs):
    tq = qq.shape[0] // 2
    mx1 = jnp.full((tq, LANE), NEG_BIG, F32)
    mx2 = jnp.full((tq, LANE), NEG_BIG, F32)
    for (k_ref, _, n, tk), s1_ref, s2_ref in zip(parts, s1_refs, s2_refs):
        for j in range(n // tk):
            s = _bdot(qq, k_ref[0, ksl, j * tk:(j + 1) * tk])
            s1_ref[j] = s[:tq]
            s2_ref[j] = s[tq:]
            mx1 = jnp.maximum(mx1, _lane_fold(s[:tq], jnp.maximum))
            mx2 = jnp.maximum(mx2, _lane_fold(s[tq:], jnp.maximum))
    return jnp.max(mx1, axis=-1, keepdims=True), jnp.max(mx2, axis=-1, keepdims=True)


def _mla_attn_kernel(*refs, part_shapes, n_pairs):
    np_ = len(part_shapes)
    q_ref = refs[0]
    k_refs = refs[1:1 + np_]
    v_refs = refs[1 + np_:1 + 2 * np_]
    o_ref = refs[1 + 2 * np_]
    s_sets = (refs[2 + 2 * np_:2 + 3 * np_], refs[2 + 3 * np_:])
    parts = [(k_refs[i], v_refs[i]) + part_shapes[i] for i in range(np_)]
    for pr in range(n_pairs):
        outs = []
        for hh in range(2):
            hd = 2 * pr + hh
            sl = slice(hd * LANE, (hd + 1) * LANE)
            outs.append(_softmax_pv(q_ref[0, :, sl], parts, sl, slice(pr * LANE, (pr + 1) * LANE), s_sets[hh]))
        lane = lax.broadcasted_iota(jnp.int32, outs[0].shape, 1)
        o_ref[0, :, pr * LANE:(pr + 1) * LANE] = jnp.where(lane < MLA_V, outs[0], outs[1]).astype(BF16)


def _chunk(n):
    return n if n <= 512 else 512


def _mla_attention(q, ks, vs, tq, n_pairs):
    b, t, _ = q.shape
    part_shapes = tuple((k.shape[2], _chunk(k.shape[2])) for k in ks)
    in_specs = [pl.BlockSpec((1, tq, 2 * n_pairs * LANE), lambda bb, g, i: (bb, i, g))]
    in_specs += [pl.BlockSpec((1, 2 * n_pairs * LANE, k.shape[2]), lambda bb, g, i: (bb, g, 0)) for k in ks]
    in_specs += [pl.BlockSpec((1, v.shape[1], n_pairs * LANE), lambda bb, g, i: (bb, 0, g)) for v in vs]
    scratch = [pltpu.VMEM((n // tk, tq, tk), F32) for n, tk in part_shapes] * 2
    return pl.pallas_call(
        functools.partial(_mla_attn_kernel, part_shapes=part_shapes, n_pairs=n_pairs),
        grid=(b, MLA_HEADS // (2 * n_pairs), t // tq),
        in_specs=in_specs,
        out_specs=pl.BlockSpec((1, tq, n_pairs * LANE), lambda bb, g, i: (bb, i, g)),
        out_shape=jax.ShapeDtypeStruct((b, t, W_VM), BF16),
        scratch_shapes=scratch,
        compiler_params=_cparams(("arbitrary", "arbitrary", "arbitrary")),
        name="mla_attention",
    )(q, *ks, *vs)


def _diff_attn_kernel(*refs, part_shapes, lam_init, n_heads):
    np_ = len(part_shapes)
    q_ref = refs[0]
    k_refs = refs[1:1 + np_]
    v_refs = refs[1 + np_:1 + 2 * np_]
    lam_ref, g_ref, o_ref = refs[1 + 2 * np_:4 + 2 * np_]
    s_sets = (refs[4 + 2 * np_:4 + 3 * np_], refs[4 + 3 * np_:])
    parts = [(k_refs[i], v_refs[i]) + part_shapes[i] for i in range(np_)]
    lp = lam_ref[...]
    lam = (jnp.exp(jnp.sum(lp[0:1] * lp[1:2], axis=-1, keepdims=True))
           - jnp.exp(jnp.sum(lp[2:3] * lp[3:4], axis=-1, keepdims=True)) + lam_init)
    for hd in range(n_heads):
        sl = slice(hd * LANE, (hd + 1) * LANE)
        q = q_ref[0, :, sl]
        lane = lax.broadcasted_iota(jnp.int32, q.shape, 1)
        zero = jnp.zeros_like(q)
        qq = jnp.concatenate([jnp.where(lane < LANE // 2, q, zero), jnp.where(lane >= LANE // 2, q, zero)], axis=0)
        m1, m2 = _score_pass2(qq, parts, sl, s_sets[0], s_sets[1])
        o1 = _pv_pass(parts, sl, s_sets[0], m1)
        o2 = _pv_pass(parts, sl, s_sets[1], m2)
        o = o1 - lam * o2
        ms = jnp.sum(o * o, axis=-1, keepdims=True) * (1.0 / DIFF_V)
        o_ref[0, :, sl] = (o * lax.rsqrt(ms + LN_EPS) * g_ref[...] * (1.0 - lam_init)).astype(BF16)


def _diff_attention(q, ks, vs, lam_p, subln, lam_init, tq, n_heads):
    b, t, _ = q.shape
    part_shapes = tuple((k.shape[2], _chunk(k.shape[2])) for k in ks)
    hmap = lambda bb, g, i: (bb, 0, g)
    in_specs = [pl.BlockSpec((1, tq, n_heads * LANE), lambda bb, g, i: (bb, i, g))]
    in_specs += [pl.BlockSpec((1, n_heads * LANE, k.shape[2]), lambda bb, g, i: (bb, g, 0)) for k in ks]
    in_specs += [pl.BlockSpec((1, v.shape[1], n_heads * LANE), hmap) for v in vs]
    in_specs += [pl.BlockSpec(lam_p.shape, lambda bb, g, i: (0, 0)),
                 pl.BlockSpec(subln.shape, lambda bb, g, i: (0, 0))]
    scratch = [pltpu.VMEM((n // tk, tq, tk), F32) for n, tk in part_shapes] * 2
    return pl.pallas_call(
        functools.partial(_diff_attn_kernel, part_shapes=part_shapes, lam_init=lam_init, n_heads=n_heads),
        grid=(b, DIFF_HEADS // n_heads, t // tq),
        in_specs=in_specs,
        out_specs=pl.BlockSpec((1, tq, n_heads * LANE), lambda bb, g, i: (bb, i, g)),
        out_shape=jax.ShapeDtypeStruct((b, t, W_D), BF16),
        scratch_shapes=scratch,
        compiler_params=_cparams(("arbitrary", "arbitrary", "arbitrary")),
        name="diff_attention",
    )(q, *ks, *vs, lam_p, subln)


def _hyfilt_kernel(emb_ref, win_ref, fw1_ref, fb1_ref, fw2_ref, fb2_ref, fwf_ref, fbf_ref, fwb_ref, fbb_ref,
                   d_ref, h_scr):
    @pl.when(pl.program_id(0) == 0)
    def _():
        h1 = jnp.sin(jnp.dot(emb_ref[...], fw1_ref[...], preferred_element_type=F32, precision=HI) + fb1_ref[...])
        h_scr[...] = jnp.sin(jnp.dot(h1, fw2_ref[...], preferred_element_type=F32, precision=HI) + fb2_ref[...])

    h = h_scr[...]
    w = win_ref[...]
    fwd = (jnp.dot(h, fwf_ref[...], preferred_element_type=F32, precision=HI) + fbf_ref[...]) * w
    bwd = (jnp.dot(h, fwb_ref[...], preferred_element_type=F32, precision=HI) + fbb_ref[...]) * w
    row = lax.broadcasted_iota(jnp.int32, bwd.shape, 0)
    bwd = jnp.where(row == 0, 0.0, bwd)
    norm = jnp.sum(jnp.abs(fwd), axis=0, keepdims=True) + jnp.sum(jnp.abs(bwd), axis=0, keepdims=True)
    inv = 1.0 / norm
    d_ref[0] = ((fwd + bwd) * inv).astype(BF16)
    d_ref[1] = ((fwd - bwd) * inv).astype(BF16)


def _hy_filters(emb, win, fw1, fb1, fw2, fb2, fw3, fb3):
    l = emb.shape[0]
    hid = fw2.shape[0]
    nblk = HY_CH // LANE
    const = lambda g: (0, 0)
    fcol = lambda g: (0, (g // nblk) * 2 * nblk + g % nblk)
    bcol = lambda g: (0, (g // nblk) * 2 * nblk + nblk + g % nblk)
    return pl.pallas_call(
        _hyfilt_kernel,
        grid=(HY_ORDER * nblk,),
        in_specs=[pl.BlockSpec(emb.shape, const),
                  pl.BlockSpec((l, LANE), lambda g: (0, g % nblk)),
                  pl.BlockSpec(fw1.shape, const), pl.BlockSpec(fb1.shape, const),
                  pl.BlockSpec(fw2.shape, const), pl.BlockSpec(fb2.shape, const),
                  pl.BlockSpec((hid, LANE), fcol), pl.BlockSpec((1, LANE), fcol),
                  pl.BlockSpec((hid, LANE), bcol), pl.BlockSpec((1, LANE), bcol)],
        out_specs=pl.BlockSpec((2, None, l, LANE), lambda g: (0, g // nblk, 0, g % nblk)),
        out_shape=jax.ShapeDtypeStruct((2, HY_ORDER, l, HY_CH), BF16),
        scratch_shapes=[pltpu.VMEM((l, hid), F32)],
        compiler_params=_cparams(("arbitrary",)),
        name="hyena_filters",
    )(emb, win, fw1, fb1, fw2, fb2, fw3, fb3, fw3, fb3)


def _hyspec_kernel(f_ref, s_ref, d_ref, o_ref):
    tf = f_ref.shape[0] // 2
    o_ref[0:tf, :] = _bdot(f_ref[0:tf, :], s_ref[...])
    o_ref[tf:2 * tf, :] = _bdot(f_ref[tf:2 * tf, :], d_ref[...])


def _hy_spectrum(fmat, filt, tf):
    rows, l = fmat.shape
    return pl.pallas_call(
        _hyspec_kernel,
        grid=(rows // (2 * tf), HY_ORDER),
        in_specs=[pl.BlockSpec((2 * tf, l), lambda j, o: (j, 0)),
                  pl.BlockSpec((None, None, l, HY_CH), lambda j, o: (0, o, 0, 0)),
                  pl.BlockSpec((None, None, l, HY_CH), lambda j, o: (1, o, 0, 0))],
        out_specs=pl.BlockSpec((2 * tf, HY_CH), lambda j, o: (j, o)),
        out_shape=jax.ShapeDtypeStruct((rows, HY_ORDER * HY_CH), F32),
        compiler_params=_cparams(("arbitrary", "arbitrary")),
        name="hyena_filter_spectrum",
    )(fmat, filt, filt)


def _hyfwd_kernel(f_ref, z_ref, k_ref, y_ref):
    tf = f_ref.shape[0] // 2
    s = _bdot(f_ref[...], z_ref[0])
    sr, si = s[0:tf], s[tf:2 * tf]
    kr, ki = k_ref[0:tf, :], k_ref[tf:2 * tf, :]
    y_ref[0, 0:tf, :] = (sr * kr - si * ki).astype(BF16)
    y_ref[0, tf:2 * tf, :] = (sr * ki + si * kr).astype(BF16)


def _hy_forward(fmat, z, kspec, order, tf):
    rows, l = fmat.shape
    b = z.shape[0]
    return pl.pallas_call(
        _hyfwd_kernel,
        grid=(rows // (2 * tf), b),
        in_specs=[pl.BlockSpec((2 * tf, l), lambda j, bb: (j, 0)),
                  pl.BlockSpec((1, l, HY_CH), lambda j, bb: (bb, 0, 0)),
                  pl.BlockSpec((2 * tf, HY_CH), lambda j, bb: (j, order))],
        out_specs=pl.BlockSpec((1, 2 * tf, HY_CH), lambda j, bb: (bb, j, 0)),
        out_shape=jax.ShapeDtypeStruct((b, rows, HY_CH), BF16),
        compiler_params=_cparams(("arbitrary", "arbitrary")),
        name="hyena_dft_forward",
    )(fmat, z, kspec)


def _hyinv_kernel(g_ref, y_ref, z_ref, x_ref, b_ref, o_ref):
    conv = _bdot(g_ref[...], y_ref[0])
    z = z_ref[0].astype(F32)
    o_ref[0] = (x_ref[0].astype(F32) * (conv + z * b_ref[...])).astype(BF16)


def _hy_inverse(gmat, y, z, gate, bias, tt):
    l, rows = gmat.shape
    b = y.shape[0]
    return pl.pallas_call(
        _hyinv_kernel,
        grid=(l // tt, b),
        in_specs=[pl.BlockSpec((tt, rows), lambda i, bb: (i, 0)),
                  pl.BlockSpec((1, rows, HY_CH), lambda i, bb: (bb, 0, 0)),
                  pl.BlockSpec((1, tt, HY_CH), lambda i, bb: (bb, i, 0)),
                  pl.BlockSpec((1, tt, HY_CH), lambda i, bb: (bb, i, 0)),
                  pl.BlockSpec((1, HY_CH), lambda i, bb: (0, 0))],
        out_specs=pl.BlockSpec((1, tt, HY_CH), lambda i, bb: (bb, i, 0)),
        out_shape=jax.ShapeDtypeStruct((b, l, HY_CH), BF16),
        compiler_params=_cparams(("arbitrary", "arbitrary")),
        name="hyena_dft_inverse",
    )(gmat, y, z, gate, bias)


def _dft_tables(l, tf):
    k = jnp.arange(l, dtype=jnp.int32)
    n = jnp.arange(l, dtype=jnp.int32)
    ph = ((2 * k + 1)[:, None] * n[None, :]) % (4 * l)
    ang = ph.astype(F32) * (2.0 * math.pi / (4 * l))
    c = jnp.cos(ang).reshape(l // tf, 1, tf, l)
    s = (-jnp.sin(ang)).reshape(l // tf, 1, tf, l)
    f = jnp.concatenate([c, s], axis=1).reshape(2 * l, l)
    return f.astype(BF16), (f.T * (1.0 / l)).astype(BF16)


def _hy_consts(l):
    t = jnp.arange(l, dtype=F32)
    bands = jnp.arange(1, HY_BANDS + 1, dtype=F32)
    ang = (2.0 * math.pi / l) * t[:, None] * bands[None, :]
    emb = jnp.concatenate([(t / l)[:, None], jnp.cos(ang), jnp.sin(ang)], -1)
    emb = jnp.pad(emb, ((0, 0), (0, LANE - HY_EMB)))
    rates = jnp.linspace(HY_MIN_RATE, HY_MAX_RATE, HY_CH, dtype=F32)
    win = jnp.exp(-(t / l)[:, None] * rates[None, :])
    return emb, win


HY_N1 = 16
HY_J = 16
HY_GROUPS = 8
HY_KG = 16


def _hy2_tables(l):
    n1h, jj = HY_N1, HY_J
    n2, k1n = l // n1h, 2 * n1h
    k2n = n2 // 2
    k1 = jnp.arange(k1n, dtype=jnp.int32)
    ph1 = ((2 * k1 + 1)[:, None] * jnp.arange(n1h, dtype=jnp.int32)[None, :]) % (4 * n1h)
    a1 = ph1.astype(F32) * (2.0 * math.pi / (4 * n1h))
    f1 = jnp.stack([jnp.cos(a1), -jnp.sin(a1)], axis=1)
    f1big = jnp.einsum('krn,ab->kranb', f1, jnp.eye(jj, dtype=F32)).reshape(k1n * 2 * jj, n1h * jj)
    kk = k1[:, None] + k1n * jnp.arange(k2n, dtype=jnp.int32)[None, :]
    ph2 = ((2 * kk + 1)[:, :, None] * jnp.arange(n2, dtype=jnp.int32)[None, None, :]) % (4 * l)
    a2 = ph2.astype(F32) * (2.0 * math.pi / (4 * l))
    mr, mi = jnp.cos(a2), -jnp.sin(a2)
    f2t = jnp.concatenate([jnp.concatenate([mr, -mi], axis=2), jnp.concatenate([mi, mr], axis=2)], axis=1)
    return (f1big.astype(BF16), (f1big.T * (1.0 / l)).astype(BF16), f2t.astype(BF16),
            jnp.swapaxes(f2t, 1, 2).astype(BF16))


def _hy2_s1_kernel(f_ref, z_ref, o_ref):
    n1h, _, c = z_ref.shape[1:]
    k1n = o_ref.shape[1]
    for g in range(HY_GROUPS):
        cols = slice(g * HY_J, (g + 1) * HY_J)
        z = z_ref[0, :, cols, :].reshape(n1h * HY_J, c)
        a = _bdot(f_ref[...], z)
        o_ref[0, :, :, cols, :] = a.astype(BF16).reshape(k1n, 2, HY_J, c)


def _hy2_stage1(f1big, z5):
    b, n1h, n2, c = z5.shape
    k1n = 2 * n1h
    w = HY_J * HY_GROUPS
    return pl.pallas_call(
        _hy2_s1_kernel,
        grid=(b, n2 // w),
        in_specs=[pl.BlockSpec(f1big.shape, lambda bb, g: (0, 0)),
                  pl.BlockSpec((1, n1h, w, c), lambda bb, g: (bb, 0, g, 0))],
        out_specs=pl.BlockSpec((1, k1n, 2, w, c), lambda bb, g: (bb, 0, 0, g, 0)),
        out_shape=jax.ShapeDtypeStruct((b, k1n, 2, n2, c), BF16),
        compiler_params=_cparams(("arbitrary", "arbitrary")),
        name="hyena_stage1",
    )(f1big, z5)


def _hy2_fspec_kernel(f_ref, s_ref, d_ref, o_ref):
    h = o_ref.shape[1] // 2
    for i in range(HY_KG):
        o_ref[i, 0:h, :] = _bdot(f_ref[i], s_ref[0, i])[0:h]
        o_ref[i, h:2 * h, :] = _bdot(f_ref[i], d_ref[0, i])[h:2 * h]


def _hy2_filter_spectrum(f2t, a5):
    nsig, k1n, _, n2, c = a5.shape
    a4 = a5.reshape(nsig, k1n, 2 * n2, c)
    return pl.pallas_call(
        _hy2_fspec_kernel,
        grid=(k1n // HY_KG, HY_ORDER),
        in_specs=[pl.BlockSpec((HY_KG, n2, 2 * n2), lambda g, o: (g, 0, 0)),
                  pl.BlockSpec((1, HY_KG, 2 * n2, c), lambda g, o: (o, g, 0, 0)),
                  pl.BlockSpec((1, HY_KG, 2 * n2, c), lambda g, o: (HY_ORDER + o, g, 0, 0))],
        out_specs=pl.BlockSpec((HY_KG, n2, c), lambda g, o: (g, 0, o)),
        out_shape=jax.ShapeDtypeStruct((k1n, n2, HY_ORDER * c), F32),
        compiler_params=_cparams(("arbitrary", "arbitrary")),
        name="hyena_filter_spectrum2",
    )(f2t, a4, a4)


def _hy2_conv_kernel(f1_ref, g1_ref, f2_ref, g2_ref, k_ref, z_ref, x_ref, b_ref, o_ref, a_scr):
    n1h, n2, c = z_ref.shape[1:]
    k1n = a_scr.shape[0]
    groups = [slice(g * HY_J, (g + 1) * HY_J) for g in range(n2 // HY_J)]
    for cols in groups:
        z = z_ref[0, :, cols, :].reshape(n1h * HY_J, c)
        a_scr[:, :, cols, :] = _bdot(f1_ref[...], z).astype(BF16).reshape(k1n, 2, HY_J, c)
    h = n2 // 2
    for i in range(k1n):
        y = _bdot(f2_ref[i], a_scr[i].reshape(2 * n2, c))
        yr, yi = y[:h], y[h:]
        kr, ki = k_ref[i, 0:h, :], k_ref[i, h:2 * h, :]
        p = jnp.concatenate([yr * kr - yi * ki, yr * ki + yi * kr], axis=0).astype(BF16)
        a_scr[i] = _bdot(g2_ref[i], p).astype(BF16).reshape(2, n2, c)
    for cols in groups:
        cc = a_scr[:, :, cols, :].reshape(k1n * 2 * HY_J, c)
        conv = _bdot(g1_ref[...], cc)
        z = z_ref[0, :, cols, :].reshape(n1h * HY_J, c).astype(F32)
        x = x_ref[0, :, cols, :].reshape(n1h * HY_J, c).astype(F32)
        o_ref[0, :, cols, :] = (x * (conv + z * b_ref[...])).astype(BF16).reshape(n1h, HY_J, c)


def _hy2_conv(tabs, kspec, order, zsrc, zwhich, vxx5, xwhich, bias):
    f1big, g1big, f2t, g2t = tabs
    b, n1h, n2, _ = zsrc.shape
    c = HY_CH
    k1n = 2 * n1h
    resident = pl.Buffered(1)
    res = lambda a: pl.BlockSpec(a.shape, lambda bb: (0,) * a.ndim, pipeline_mode=resident)
    col = lambda which: pl.BlockSpec((1, n1h, n2, c), lambda bb: (bb, 0, 0, which))
    return pl.pallas_call(
        _hy2_conv_kernel,
        grid=(b,),
        in_specs=[res(f1big), res(g1big), res(f2t), res(g2t),
                  pl.BlockSpec((k1n, n2, c), lambda bb: (0, 0, order), pipeline_mode=resident),
                  col(zwhich), col(xwhich), pl.BlockSpec((1, c), lambda bb: (0, 0))],
        out_specs=col(0),
        out_shape=jax.ShapeDtypeStruct((b, n1h, n2, c), BF16),
        scratch_shapes=[pltpu.VMEM((k1n, 2, n2, c), BF16)],
        compiler_params=_cparams(("arbitrary",)),
        name="hyena_conv",
    )(f1big, g1big, f2t, g2t, kspec, zsrc, vxx5, bias)


def _hyena(vxx, consts, fw1, fb1, fw2, fb2, fw3, fb3, hbias):
    emb, win, tabs = consts
    filt = _hy_filters(emb, win, fw1, fb1, fw2, fb2, fw3, fb3)
    b, l, _ = vxx.shape
    c = HY_CH
    if len(tabs) == 3:
        fmat, gmat, tf = tabs
        kspec = _hy_spectrum(fmat, filt, tf)
        v, x1, x2 = vxx[..., 0:c], vxx[..., c:2 * c], vxx[..., 2 * c:3 * c]
        z = _hy_inverse(gmat, _hy_forward(fmat, v, kspec, 0, tf), v, x1, hbias[0:1], tf)
        return _hy_inverse(gmat, _hy_forward(fmat, z, kspec, 1, tf), z, x2, hbias[1:2], tf)
    f1big, g1big, f2t, g2t = tabs
    n2 = l // HY_N1
    filt5 = filt.reshape(2 * HY_ORDER, HY_N1, n2, c)
    kspec = _hy2_filter_spectrum(f2t, _hy2_stage1(f1big, filt5))
    vxx5 = vxx.reshape(b, HY_N1, n2, 3 * c)
    z5 = _hy2_conv(tabs, kspec, 0, vxx5, 0, vxx5, 1, hbias[0:1])
    return _hy2_conv(tabs, kspec, 1, z5, 0, vxx5, 2, hbias[1:2]).reshape(b, l, c)


def _mix_ffn_kernel(om_ref, od_ref, oh_ref, x_ref, mod_ref, wm_ref, wd_ref, wh_ref, w1_ref, b1_ref, w2_ref,
                    vec_ref, o_ref, *, alpha, tm, tf):
    g1, sh2, sc2, g2 = (mod_ref[0, i:i + 1, :] for i in range(4))
    bo, lg1, lb1, b2, lg2, lb2 = (vec_ref[i:i + 1, :] for i in range(6))
    nt = x_ref.shape[1] // tm
    rows = [slice(t * tm, (t + 1) * tm) for t in range(nt)]
    xs, hs = [], []
    for r in rows:
        y = _bdot(om_ref[0, r, :], wm_ref[...]) + _bdot(od_ref[0, r, :], wd_ref[...]) + _bdot(oh_ref[0, r, :], wh_ref[...])
        x1 = _ln(alpha * x_ref[0, r, :] + g1 * (y + bo)) * lg1 + lb1
        xs.append(x1)
        hs.append((_ln(x1) * (1.0 + sc2) + sh2).astype(BF16))
    accs = [None] * nt
    for c in range(w1_ref.shape[1] // tf):
        cols = slice(c * tf, (c + 1) * tf)
        for t in range(nt):
            a = jnp.maximum(_bdot(hs[t], w1_ref[:, cols]) + b1_ref[:, cols], 0.0)
            y = _bdot((a * a).astype(BF16), w2_ref[cols, :])
            accs[t] = y if accs[t] is None else accs[t] + y
    for t in range(nt):
        z = alpha * xs[t] + g2 * (accs[t] + b2)
        o_ref[0, rows[t], :] = _ln(z) * lg2 + lb2


def _mix_ffn(om, od, oh, x, mod, wm, wd, wh, w1, b1, w2, vec, alpha, tm, nt, tf):
    b, t, d = x.shape
    per_batch = mod.shape[0] == b and b > 1
    gmap = (lambda bb, i: (bb, 0, 0)) if per_batch else (lambda bb, i: (0, 0, 0))
    const = lambda bb, i: (0, 0)
    row = lambda bb, i: (bb, i, 0)
    resident = pl.Buffered(1)
    blk = lambda a: pl.BlockSpec((1, tm * nt, a.shape[2]), row)
    res = lambda a: pl.BlockSpec(a.shape, const, pipeline_mode=resident)
    return pl.pallas_call(
        functools.partial(_mix_ffn_kernel, alpha=alpha, tm=tm, tf=tf),
        grid=(b, t // (tm * nt)),
        in_specs=[blk(om), blk(od), blk(oh), blk(x), pl.BlockSpec((1,) + mod.shape[1:], gmap),
                  res(wm), res(wd), res(wh), res(w1), pl.BlockSpec(b1.shape, const), res(w2),
                  pl.BlockSpec(vec.shape, const)],
        out_specs=blk(x),
        out_shape=jax.ShapeDtypeStruct((b, t, d), F32),
        compiler_params=_cparams(("arbitrary", "arbitrary")),
        name="mix_ffn",
    )(om, od, oh, x, mod, wm, wd, wh, w1, b1, w2, vec)


def _rope_lane_plan(starts, dims):
    h = dims // 2
    inv = ROPE_BASE ** (-(np.arange(h // 2, dtype=np.float64) * (2.0 / h)))
    fr, fc, lo, hi = (np.zeros(LANE, np.float32) for _ in range(4))
    for st in starts:
        for axis_i, f in enumerate((fr, fc)):
            base = st + axis_i * h
            f[base:base + h // 2] = inv
            f[base + h // 2:base + h] = inv
            lo[base:base + h // 2] = 1.0
            hi[base + h // 2:base + h] = 1.0
    return fr, fc, lo, hi


def _rope_tables(t_len, roped):
    pos_t = jnp.arange(t_len)
    scale = 1.0 if roped else 0.0
    row = (pos_t // GRID_W).astype(F32)[:, None] * scale
    col = (pos_t % GRID_W).astype(F32)[:, None] * scale
    out = []
    for starts, dims in (([MLA_NOPE], MLA_ROPE), ([0, LANE // 2], DIFF_DIM)):
        fr, fc, lo, hi = _rope_lane_plan(starts, dims)
        ang = row * fr[None, :] + col * fc[None, :]
        sin = jnp.sin(ang)
        out += [jnp.cos(ang), -sin * lo[None, :], sin * hi[None, :]]
    return jnp.concatenate(out, axis=1)


def _prep_weights(w_in, w_q_up, w_kv_up, w_out, diff_lambda, diff_subln):
    depth, d, _ = w_in.shape
    s1, s2 = IN_MLA, IN_MLA + IN_DIFF
    pad_last = lambda a, n: jnp.pad(a, [(0, 0)] * (a.ndim - 1) + [(0, n - a.shape[-1])])
    kr = jnp.pad(w_in[..., C_KR:s1], ((0, 0), (0, 0), (MLA_NOPE, LANE - MLA_NOPE - MLA_ROPE)))
    dq = pad_last(w_in[..., s1:s1 + N_DQ].reshape(depth, d, DIFF_HEADS, 2, DIFF_DIM), LANE // 2)
    dk = pad_last(w_in[..., s1 + N_DQ:s1 + 2 * N_DQ].reshape(depth, d, DIFF_HEADS, 2, DIFF_DIM), LANE // 2)
    dv = pad_last(w_in[..., s1 + 2 * N_DQ:s2].reshape(depth, d, DIFF_HEADS, DIFF_V), LANE)
    win = jnp.concatenate([w_in[..., :C_KR], kr, dq.reshape(depth, d, W_D), dk.reshape(depth, d, W_D),
                           dv.reshape(depth, d, W_D), w_in[..., s2:]], axis=-1).astype(BF16)
    wq = pad_last(w_q_up.reshape(depth, MLA_Q_RANK, MLA_HEADS, MLA_NOPE + MLA_ROPE), LANE)
    wq = wq.reshape(depth, MLA_Q_RANK, W_QM).astype(BF16)
    kv = w_kv_up.reshape(depth, MLA_KV_RANK, MLA_HEADS, MLA_NOPE + MLA_V)
    wk = pad_last(kv[..., :MLA_NOPE], LANE).reshape(depth, MLA_KV_RANK, W_QM)
    wv = kv[..., MLA_NOPE:].reshape(depth, MLA_KV_RANK, W_VM)
    wkv = jnp.concatenate([wk, wv], axis=-1).astype(BF16)
    wo_m = w_out[:, :W_VM].astype(BF16)
    wo_d = w_out[:, W_VM:W_VM + DIFF_HEADS * DIFF_V].reshape(depth, DIFF_HEADS, DIFF_V, -1)
    wo_d = jnp.pad(wo_d, ((0, 0), (0, 0), (0, LANE - DIFF_V), (0, 0))).reshape(depth, W_D, -1).astype(BF16)
    wo_h = w_out[:, W_VM + DIFF_HEADS * DIFF_V:].astype(BF16)
    lam = pad_last(diff_lambda.astype(F32), LANE)
    subln = pad_last(diff_subln.astype(F32), LANE)[:, None, :]
    return win, wq, wkv, wo_m, wo_d, wo_h, lam, subln


def kernel(x, c, ctx, c_ctx, w_mod, b_mod, w_in, mla_q_norm, w_q_up, mla_kv_norm, w_kv_up, diff_lambda, diff_subln, hy_conv_w, hy_conv_b, hy_fw1, hy_fb1, hy_fw2, hy_fb2, hy_fw3, hy_fb3, hy_bias, w_out, b_out, ln1_g, ln1_b, w_ff1, b_ff1, w_ff2, b_ff2, ln2_g, ln2_b):
    bsz, seq, d = x.shape
    n_ctx = ctx.shape[1]
    depth = w_in.shape[0]
    alpha = (2.0 * depth) ** 0.25

    rpad = -(bsz + 1) % SUBLANE
    cc = jnp.concatenate([c, c_ctx[None, :], jnp.zeros((rpad, d), F32)], axis=0)
    mod_all = _modulation(cc, w_mod, b_mod)

    win, wq, wkv, wo_m, wo_d, wo_h, lam_p, subln = _prep_weights(w_in, w_q_up, w_kv_up, w_out, diff_lambda, diff_subln)
    w1 = w_ff1.astype(BF16)
    w2 = w_ff2.astype(BF16)
    fw1 = jnp.pad(hy_fw1, ((0, 0), (0, LANE - HY_EMB), (0, 0)))
    tab_x = _rope_tables(seq, True)
    tab_c = _rope_tables(n_ctx, False)
    def hy_tables(l):
        if l >= 1024 and l % (HY_N1 * HY_J * HY_GROUPS) == 0:
            return _hy2_tables(l)
        return _dft_tables(l, min(512, l)) + (min(512, l),)

    hyc_x = _hy_consts(seq) + (hy_tables(seq),)
    hyc_c = _hy_consts(n_ctx) + (hy_tables(n_ctx),)
    tm_x, tm_c = min(512, seq), min(512, n_ctx)
    tmi_x = min(1024, seq)
    ntf_x = 2 if seq % (2 * tm_x) == 0 else 1
    tq_x, tq_c = min(256, seq), min(256, n_ctx)
    tff = min(1024, w1.shape[2])
    row2 = lambda a: a.reshape(1, -1)

    for layer in range(depth):
        need_ctx = layer < depth - 1
        lam_init = 0.8 - 0.6 * math.exp(-0.3 * layer)
        mod = mod_all[layer, :bsz].reshape(bsz, 1, N_MOD, d)
        modc = mod_all[layer, bsz:bsz + 1].reshape(1, 1, N_MOD, d)
        sh1, sc1 = mod[:, :, 0], mod[:, :, 1]
        csh1, csc1 = modc[:, :, 0], modc[:, :, 1]
        gq, gkv = row2(mla_q_norm[layer]), row2(mla_kv_norm[layer])
        hy_args = (fw1[layer], row2(hy_fb1[layer]), hy_fw2[layer], row2(hy_fb2[layer]), hy_fw3[layer],
                   row2(hy_fb3[layer]), hy_bias[layer])
        conv = (hy_conv_w[layer], row2(hy_conv_b[layer]))

        qm, km, vm, dq, dk, dv, uh = _in_proj(x, sh1, sc1, win[layer], gq, wq[layer], gkv, wkv[layer], tab_x, *conv, tmi_x)
        cqm, ckm, cvm, cdq, cdk, cdv, cuh = _in_proj(ctx, csh1, csc1, win[layer], gq, wq[layer], gkv, wkv[layer],
                                                     tab_c, *conv, tm_c)
        om = _mla_attention(qm, [ckm, km], [cvm, vm], tq_x, MLA_HEADS // 2)
        od = _diff_attention(dq, [cdk, dk], [cdv, dv], lam_p[layer], subln[layer], lam_init, tq_x, DIFF_HEADS)
        oh = _hyena(uh, hyc_x, *hy_args)
        vec = jnp.stack([b_out[layer], ln1_g[layer], ln1_b[layer], b_ff2[layer], ln2_g[layer], ln2_b[layer]])
        mix_w = (wo_m[layer], wo_d[layer], wo_h[layer], w1[layer], row2(b_ff1[layer]), w2[layer], vec)
        x = _mix_ffn(om, od, oh, x, mod[:, 0, 2:6], *mix_w, alpha, tm_x, ntf_x, tff)
        if need_ctx:
            com = _mla_attention(cqm, [ckm], [cvm], tq_c, MLA_HEADS // 2)
            cod = _diff_attention(cdq, [cdk], [cdv], lam_p[layer], subln[layer], lam_init, tq_c, DIFF_HEADS)
            coh = _hyena(cuh, hyc_c, *hy_args)
            if (bsz * n_ctx) % (tm_x * ntf_x) == 0:
                flat = lambda a: a.reshape(1, bsz * n_ctx, a.shape[2])
                ctx = _mix_ffn(flat(com), flat(cod), flat(coh), flat(ctx), modc[:, 0, 2:6], *mix_w, alpha,
                               tm_x, ntf_x, tff).reshape(bsz, n_ctx, d)
            else:
                ctx = _mix_ffn(com, cod, coh, ctx, modc[:, 0, 2:6], *mix_w, alpha, tm_c, 1, tff)
    return x
```
